```python
import math
import jax
import jax.numpy as jnp
from jax import lax
import numpy as np

D_MODEL = 4096
BATCH = 4
SEQ = 4096
DEPTH = 2

CTX_LEN = 256
GRID_W = 64
MIX_WIDTH = D_MODEL
BRANCH_W = MIX_WIDTH // 4
CHUNK = 64
EPS = 1e-6
F32 = jnp.float32

A_HEAD_DIM = 128
A_HEADS = BRANCH_W // A_HEAD_DIM
A_MIN_FORGET = 1e-6
B_HEADS = 4
B_KEY_W = BRANCH_W // 2
B_DK = B_KEY_W // B_HEADS
B_DV = BRANCH_W // B_HEADS
B_GATE_RANK = 16
B_GATE_NORM = 16.0
C_GROUP = 16
C_GROUPS = BRANCH_W // C_GROUP
C_STATE = 64
C_MAX_RE = -1e-4
DT_MIN = 1e-3
DT_MAX = 1e-1
D_HEADS = 4
D_KEY_W = BRANCH_W // 2
D_DK = D_KEY_W // D_HEADS
D_DV = BRANCH_W // D_HEADS
ROPE_BASE = 10000.0

IN_SPLITS = (
    BRANCH_W, BRANCH_W, BRANCH_W, BRANCH_W, BRANCH_W,
    B_KEY_W, B_KEY_W, BRANCH_W, B_GATE_RANK, B_GATE_RANK, BRANCH_W,
    BRANCH_W, BRANCH_W,
    D_KEY_W, D_KEY_W, BRANCH_W, BRANCH_W,
)
IN_WIDTH = sum(IN_SPLITS)

kernel_name = 'hybrid_hgrn2_gla_s5_retention_prefix_dit'


def rms_norm(x, g):
    xf = x.astype(F32)
    y = xf * lax.rsqrt(jnp.mean(xf * xf, axis=-1, keepdims=True) + EPS)
    return (y * g.astype(F32)).astype(x.dtype)


def head_layer_norm(x, g):
    xf = x.astype(F32)
    mu = jnp.mean(xf, axis=-1, keepdims=True)
    var = jnp.mean(jnp.square(xf - mu), axis=-1, keepdims=True)
    return ((xf - mu) * lax.rsqrt(var + EPS) * g.astype(F32)).astype(x.dtype)


def seg_flip(z, n_ctx):
    return jnp.concatenate([jnp.flip(z[:, :n_ctx], axis=1), jnp.flip(z[:, n_ctx:], axis=1)], axis=1)


def to_chunks(a):
    bsz, l = a.shape[:2]
    return jnp.moveaxis(a.reshape(bsz, l // CHUNK, CHUNK, *a.shape[2:]), 1, 0)


def from_chunks(a):
    a = jnp.moveaxis(a, 0, 1)
    return a.reshape(a.shape[0], -1, *a.shape[3:])


def split_columns(proj):
    offsets = np.cumsum(IN_SPLITS)[:-1].tolist()
    return jnp.split(proj, offsets, axis=-1)


def axial_rope(rows, n_ctx):
    quarter = D_DK // 4
    freqs = ROPE_BASE ** (-jnp.arange(quarter, dtype=F32) / quarter)
    t = jnp.arange(rows * GRID_W)
    r = (t // GRID_W).astype(F32)
    col = (t % GRID_W).astype(F32)
    ang = jnp.concatenate([r[:, None] * freqs, col[:, None] * freqs], axis=-1)
    ang = jnp.concatenate([jnp.zeros((n_ctx, D_DK // 2), F32), ang], axis=0)
    return jnp.cos(ang), jnp.sin(ang)


def apply_rope(x, cos, sin):
    half = x.shape[-1] // 2
    x1 = x[..., :half].astype(F32)
    x2 = x[..., half:].astype(F32)
    cs = cos[None, :, None, :]
    sn = sin[None, :, None, :]
    return jnp.concatenate([x1 * cs - x2 * sn, x1 * sn + x2 * cs], axis=-1).astype(x.dtype)


def gla_chunked(q, k, v, log_decay):
    bsz, _, h, dk = q.shape
    dv = v.shape[-1]
    lower = jnp.tril(jnp.ones((CHUNK, CHUNK), dtype=bool))[None, :, :, None, None]

    def step(state, inp):
        qi, ki, vi, gi = inp
        qi = qi.astype(F32)
        ki = ki.astype(F32)
        vi = vi.astype(F32)
        b = jnp.cumsum(gi, axis=1)
        diff = jnp.where(lower, b[:, :, None] - b[:, None, :], 0.0)
        rel = jnp.where(lower, jnp.exp(diff), 0.0)
        scores = jnp.einsum('bihd,bjhd,bijhd->bhij', qi, ki, rel)
        b_last = b[:, -1]
        o = (jnp.einsum('bhij,bjhe->bihe', scores, vi)
             + jnp.einsum('bihd,bhde->bihe', qi * jnp.exp(b), state))
        state = (state * jnp.exp(b_last)[..., None]
                 + jnp.einsum('bjhd,bjhe->bhde', ki * jnp.exp(b_last[:, None] - b), vi))
        return state, o

    s0 = jnp.zeros((bsz, h, dk, dv), F32)
    _, o = lax.scan(step, s0, (to_chunks(q), to_chunks(k), to_chunks(v), to_chunks(log_decay.astype(F32))))
    return from_chunks(o).astype(v.dtype)


def retention_chunked(q, k, v, log_gamma):
    bsz, _, h, dk = q.shape
    dv = v.shape[-1]
    pos = jnp.arange(CHUNK, dtype=F32)
    lg = log_gamma.astype(F32)
    rel = pos[:, None] - pos[None, :]
    dmat = jnp.where(rel[None] >= 0, jnp.exp(lg[:, None, None] * jnp.maximum(rel, 0.0)[None]), 0.0)
    xi = jnp.exp(lg[None, :] * (pos[:, None] + 1.0))[None, :, :, None]
    zeta = jnp.exp(lg[None, :] * (CHUNK - 1.0 - pos[:, None]))[None, :, :, None]
    chunk_decay = jnp.exp(lg * CHUNK)[None, :, None, None]

    def step(state, inp):
        qi, ki, vi = inp
        qi = qi.astype(F32)
        ki = ki.astype(F32)
        vi = vi.astype(F32)
        scores = jnp.einsum('bihd,bjhd->bhij', qi, ki) * dmat[None]
        o = (jnp.einsum('bhij,bjhe->bihe', scores, vi)
             + jnp.einsum('bihd,bhde->bihe', qi, state) * xi)
        state = state * chunk_decay + jnp.einsum('bjhd,bjhe->bhde', ki * zeta, vi)
        return state, o

    s0 = jnp.zeros((bsz, h, dk, dv), F32)
    _, o = lax.scan(step, s0, (to_chunks(q), to_chunks(k), to_chunks(v)))
    return from_chunks(o).astype(v.dtype)


def s5_scan(u, lam_re, lam_im, log_dt, b_re, b_im, c_re, c_im):
    lam = lax.complex(jnp.minimum(lam_re.astype(F32), C_MAX_RE), lam_im.astype(F32))
    dt = jnp.exp(log_dt.astype(F32))[:, None]
    lam_bar = jnp.exp(lam * dt)
    b_bar = ((lam_bar - 1.0) / lam)[..., None] * lax.complex(b_re.astype(F32), b_im.astype(F32))
    bu = jnp.einsum('gph,blgh->blgp', b_bar, u.astype(F32).astype(jnp.complex64))
    a = jnp.broadcast_to(lam_bar, bu.shape)

    def combine(e1, e2):
        a1, b1 = e1
        a2, b2 = e2
        return a1 * a2, a2 * b1 + b2

    _, states = lax.associative_scan(combine, (a, bu), axis=1)
    c_mat = lax.complex(c_re.astype(F32), c_im.astype(F32))
    return jnp.einsum('ghp,blgp->blgh', c_mat, states).real


def hgrn2_branch(q, f_fwd, f_bwd, i, gate, lower_bound, norm_g, n_ctx):
    bsz, l, _ = q.shape

    def heads(a):
        return a.reshape(bsz, l, A_HEADS, A_HEAD_DIM)

    qh, ih = heads(q), heads(i)

    def direction(qd, zd, idd, lb):
        lb = lb.reshape(A_HEADS, A_HEAD_DIM).astype(F32)
        z = heads(zd).astype(F32)
        f = lb + (1.0 - lb) * jax.nn.sigmoid(z)
        log_f = jnp.log(jnp.maximum(f, A_MIN_FORGET))
        key = (1.0 - lb) * jax.nn.sigmoid(-z)
        return gla_chunked(qd, key.astype(qd.dtype), idd, log_f)

    o_f = direction(qh, f_fwd, ih, lower_bound[0])
    o_b = seg_flip(direction(seg_flip(qh, n_ctx), seg_flip(f_bwd, n_ctx), seg_flip(ih, n_ctx), lower_bound[1]), n_ctx)
    o = rms_norm(o_f + o_b, norm_g).reshape(bsz, l, BRANCH_W)
    return o * jax.nn.silu(gate)


def gla_branch(q, k, v, lr_fwd, lr_bwd, gate, w_gk, b_gk, norm_g, n_ctx):
    bsz, l, _ = q.shape
    qh = q.reshape(bsz, l, B_HEADS, B_DK) * (B_DK ** -0.5)
    kh = k.reshape(bsz, l, B_HEADS, B_DK)
    vh = v.reshape(bsz, l, B_HEADS, B_DV)

    def log_decay(lr, w, b):
        g = jax.nn.log_sigmoid((lr @ w + b).astype(F32)) / B_GATE_NORM
        return g.reshape(bsz, l, B_HEADS, B_DK)

    o_f = gla_chunked(qh, kh, vh, log_decay(lr_fwd, w_gk[0], b_gk[0]))
    o_b = seg_flip(gla_chunked(seg_flip(qh, n_ctx), seg_flip(kh, n_ctx), seg_flip(vh, n_ctx),
                               log_decay(seg_flip(lr_bwd, n_ctx), w_gk[1], b_gk[1])), n_ctx)
    o = rms_norm(o_f + o_b, norm_g).reshape(bsz, l, BRANCH_W)
    return o * jax.nn.silu(gate)


def s5_branch(u, gate, lam_re, lam_im, log_dt, b_re, b_im, c_re, c_im, d, w_glu, b_glu, n_ctx):
    bsz, l, _ = u.shape
    ug = u.reshape(bsz, l, C_GROUPS, C_GROUP)
    y_f = s5_scan(ug, lam_re[0], lam_im[0], log_dt[0], b_re[0], b_im[0], c_re[0], c_im[0])
    y_b = seg_flip(s5_scan(seg_flip(ug, n_ctx), lam_re[1], lam_im[1], log_dt[1],
                           b_re[1], b_im[1], c_re[1], c_im[1]), n_ctx)
    y = (y_f + y_b + d.astype(F32) * ug.astype(F32)).reshape(bsz, l, BRANCH_W).astype(u.dtype)
    z = jax.nn.gelu(y)
    out = z * jax.nn.sigmoid(z @ w_glu + b_glu)
    return out * jax.nn.silu(gate)


def retention_branch(q, k, v, gate, log_gamma, norm_g, cos, sin, n_ctx):
    bsz, l, _ = q.shape
    qh = apply_rope(q.reshape(bsz, l, D_HEADS, D_DK), cos, sin) * (D_DK ** -0.5)
    kh = apply_rope(k.reshape(bsz, l, D_HEADS, D_DK), cos, sin)
    vh = v.reshape(bsz, l, D_HEADS, D_DV)
    o_f = retention_chunked(qh, kh, vh, log_gamma[0])
    o_b = seg_flip(retention_chunked(seg_flip(qh, n_ctx), seg_flip(kh, n_ctx), seg_flip(vh, n_ctx), log_gamma[1]), n_ctx)
    o = head_layer_norm(o_f + o_b, norm_g).reshape(bsz, l, BRANCH_W)
    return o * jax.nn.silu(gate)


def setup_inputs(seed: int = 0) -> dict:
    key = jax.random.key(seed)
    ks = jax.random.split(key, 27)

    def nrm(k, shape, scale):
        return jax.random.normal(k, shape, F32) * scale

    n_idx = jnp.arange(C_STATE, dtype=F32)
    gammas = 1.0 - 2.0 ** (-5.0 - jnp.arange(D_HEADS, dtype=F32))
    gamma_logit = jnp.log(gammas) - jnp.log1p(-gammas)
    return {
        'x': nrm(ks[0], (BATCH, SEQ, D_MODEL), 1.0),
        'c': nrm(ks[1], (BATCH, D_MODEL), 1.0),
        'ctx': nrm(ks[2], (BATCH, CTX_LEN, D_MODEL), 1.0),
        'c_ctx': nrm(ks[3], (D_MODEL,), 1.0),
        'norm_g': 1.0 + nrm(ks[4], (DEPTH, D_MODEL), 0.02),
        'w_ada': nrm(ks[5], (DEPTH, D_MODEL, 3 * D_MODEL), 0.5 * D_MODEL ** -0.5),
        'b_ada': nrm(ks[6], (DEPTH, 3 * D_MODEL), 0.02),
        'w_in': nrm(ks[7], (DEPTH, D_MODEL, IN_WIDTH), D_MODEL ** -0.5),
        'hgrn_lb_logits': nrm(ks[8], (DEPTH, 2, BRANCH_W), 0.1),
        'hgrn_norm_g': 1.0 + nrm(ks[9], (DEPTH, A_HEAD_DIM), 0.02),
        'gla_w_gk': nrm(ks[10], (DEPTH, 2, B_GATE_RANK, B_KEY_W), B_GATE_RANK ** -0.5),
        'gla_b_gk': nrm(ks[11], (DEPTH, 2, B_KEY_W), 0.1),
        'gla_norm_g': 1.0 + nrm(ks[12], (DEPTH, B_DV), 0.02),
        's5_lam_re': -0.5 + nrm(ks[13], (DEPTH, 2, C_GROUPS, C_STATE), 0.01),
        's5_lam_im': math.pi * n_idx + nrm(ks[14], (DEPTH, 2, C_GROUPS, C_STATE), 0.01),
        's5_log_dt': jax.random.uniform(ks[15], (DEPTH, 2, C_GROUPS), F32, math.log(DT_MIN), math.log(DT_MAX)),
        's5_b_re': nrm(ks[16], (DEPTH, 2, C_GROUPS, C_STATE, C_GROUP), (2 * C_GROUP) ** -0.5),
        's5_b_im': nrm(ks[17], (DEPTH, 2, C_GROUPS, C_STATE, C_GROUP), (2 * C_GROUP) ** -0.5),
        's5_c_re': nrm(ks[18], (DEPTH, 2, C_GROUPS, C_GROUP, C_STATE), (2 * C_STATE) ** -0.5),
        's5_c_im': nrm(ks[19], (DEPTH, 2, C_GROUPS, C_GROUP, C_STATE), (2 * C_STATE) ** -0.5),
        's5_d': nrm(ks[20], (DEPTH, C_GROUPS, C_GROUP), 0.5),
        's5_w_glu': nrm(ks[21], (DEPTH, BRANCH_W, BRANCH_W), BRANCH_W ** -0.5),
        's5_b_glu': nrm(ks[22], (DEPTH, BRANCH_W), 0.02),
        'ret_decay_logit': gamma_logit + nrm(ks[23], (DEPTH, 2, D_HEADS), 0.01),
        'ret_norm_g': 1.0 + nrm(ks[24], (DEPTH, D_DV), 0.02),
        'w_out': nrm(ks[25], (DEPTH, MIX_WIDTH, D_MODEL), MIX_WIDTH ** -0.5),
        'final_norm_g': 1.0 + nrm(ks[26], (D_MODEL,), 0.02),
    }


def reference(x, c, ctx, c_ctx, norm_g, w_ada, b_ada, w_in, hgrn_lb_logits, hgrn_norm_g,
              gla_w_gk, gla_b_gk, gla_norm_g, s5_lam_re, s5_lam_im, s5_log_dt, s5_b_re, s5_b_im,
              s5_c_re, s5_c_im, s5_d, s5_w_glu, s5_b_glu, ret_decay_logit, ret_norm_g, w_out,
              final_norm_g):
    n_ctx = ctx.shape[1]
    rows = x.shape[1] // GRID_W
    cos, sin = axial_rope(rows, n_ctx)
    lb_p = jax.nn.softmax(hgrn_lb_logits.astype(F32), axis=0)
    lower_bounds = jnp.cumsum(lb_p, axis=0) - lb_p[0:1]

    h_ctx, h_lat = ctx, x
    for layer in range(DEPTH):
        last = layer == DEPTH - 1
        mod_lat = jax.nn.silu(c) @ w_ada[layer] + b_ada[layer]
        mod_ctx = jax.nn.silu(c_ctx) @ w_ada[layer] + b_ada[layer]
        sh_l, sc_l, gt_l = jnp.split(mod_lat, 3, axis=-1)
        sh_c, sc_c, gt_c = jnp.split(mod_ctx, 3, axis=-1)
        hn = jnp.concatenate([
            rms_norm(h_ctx, norm_g[layer]) * (1.0 + sc_c) + sh_c,
            rms_norm(h_lat, norm_g[layer]) * (1.0 + sc_l[:, None]) + sh_l[:, None],
        ], axis=1)
        (a_q, a_ff, a_fb, a_i, a_g,
         b_q, b_k, b_v, b_lf, b_lb, b_g,
         c_u, c_g,
         d_q, d_k, d_v, d_g) = split_columns(hn @ w_in[layer])
        o_a = hgrn2_branch(a_q, a_ff, a_fb, a_i, a_g, lower_bounds[layer], hgrn_norm_g[layer], n_ctx)
        o_b = gla_branch(b_q, b_k, b_v, b_lf, b_lb, b_g, gla_w_gk[layer], gla_b_gk[layer], gla_norm_g[layer], n_ctx)
        o_c = s5_branch(c_u, c_g, s5_lam_re[layer], s5_lam_im[layer], s5_log_dt[layer], s5_b_re[layer],
                        s5_b_im[layer], s5_c_re[layer], s5_c_im[layer], s5_d[layer], s5_w_glu[layer],
                        s5_b_glu[layer], n_ctx)
        log_gamma = jax.nn.log_sigmoid(ret_decay_logit[layer].astype(F32))
        o_d = retention_branch(d_q, d_k, d_v, d_g, log_gamma, ret_norm_g[layer], cos, sin, n_ctx)
        o = jnp.concatenate([o_a, o_b, o_c, o_d], axis=-1)
        if last:
            h_lat = h_lat + gt_l[:, None] * (o[:, n_ctx:] @ w_out[layer])
        else:
            y = o @ w_out[layer]
            h_ctx = h_ctx + gt_c * y[:, :n_ctx]
            h_lat = h_lat + gt_l[:, None] * y[:, n_ctx:]
    return rms_norm(h_lat, final_norm_g)
```

```python
import functools
import math

import numpy as np
import jax
import jax.numpy as jnp
from jax import lax
from jax.experimental import pallas as pl
from jax.experimental.pallas import tpu as pltpu

F32 = jnp.float32
BF16 = jnp.bfloat16

EPS = 1e-6
A_HEAD_DIM = 128
A_MIN_FORGET = 1e-6
B_HEADS = 4
B_GATE_RANK = 16
B_GATE_NORM = 16.0
C_GROUP = 16
C_STATE = 64
C_MAX_RE = -1e-4
D_HEADS = 4
GRID_W = 64
ROPE_BASE = 10000.0

LANES = 128
SUBLANES = 8
BF16_ROWS = 16
VMEM_LIMIT = 56 * 1024 * 1024

CHUNK = 64
S5_CHUNK = 64
ROW_TILE = 256


def _cparams(sem):
    return pltpu.CompilerParams(dimension_semantics=sem, vmem_limit_bytes=VMEM_LIMIT)


def _dot(a, b):
    return jnp.dot(a, b, preferred_element_type=F32)


def _dot_nt(a, b):
    return lax.dot_general(a, b, (((1,), (1,)), ((), ())), preferred_element_type=F32)


def _dot_tn(a, b):
    return lax.dot_general(a, b, (((0,), (0,)), ((), ())), preferred_element_type=F32)


def _sigmoid(x):
    return 1.0 / (1.0 + jnp.exp(-x))


def _silu(x):
    return x * _sigmoid(x)


def _log_sigmoid(x):
    return jnp.minimum(x, 0.0) - jnp.log(1.0 + jnp.exp(-jnp.abs(x)))


def _ada_kernel(c_ref, w_ref, b_ref, o_ref):
    cv = _silu(c_ref[...]).astype(BF16)
    o_ref[0] = _dot(cv, w_ref[0].astype(BF16)) + b_ref[0]


def ada_modulation(cvec, w_ada, b_ada):
    depth, d, n3 = w_ada.shape
    tn = 512
    return pl.pallas_call(
        _ada_kernel,
        grid=(depth, n3 // tn),
        in_specs=[
            pl.BlockSpec((SUBLANES, d), lambda l, j: (0, 0)),
            pl.BlockSpec((1, d, tn), lambda l, j: (l, 0, j)),
            pl.BlockSpec((1, 1, tn), lambda l, j: (l, 0, j)),
        ],
        out_specs=pl.BlockSpec((1, SUBLANES, tn), lambda l, j: (l, 0, j)),
        out_shape=jax.ShapeDtypeStruct((depth, SUBLANES, n3), F32),
        compiler_params=_cparams(("parallel", "parallel")),
        name="ada_modulation",
    )(cvec, w_ada, b_ada.reshape(depth, 1, n3))


def _prenorm_kernel(x_ref, g_ref, sh_ref, sc_ref, o_ref):
    x = x_ref[0]
    y = x * lax.rsqrt(jnp.mean(x * x, axis=-1, keepdims=True) + EPS) * g_ref[...]
    o_ref[0] = (y * (1.0 + sc_ref[0]) + sh_ref[0]).astype(BF16)


def prenorm(h, g, mods_flat, layer, n_ctx):
    bsz, l, d = h.shape
    tr = ROW_TILE
    nct = n_ctx // tr

    def mod_map(part):
        def f(b, i):
            row = jnp.where(i < nct, bsz, b)
            return ((layer * SUBLANES + row) * 3 + part, 0, 0)
        return f

    return pl.pallas_call(
        _prenorm_kernel,
        grid=(bsz, l // tr),
        in_specs=[
            pl.BlockSpec((1, tr, d), lambda b, i: (b, i, 0)),
            pl.BlockSpec((1, d), lambda b, i: (0, 0)),
            pl.BlockSpec((1, 1, d), mod_map(0)),
            pl.BlockSpec((1, 1, d), mod_map(1)),
        ],
        out_specs=pl.BlockSpec((1, tr, d), lambda b, i: (b, i, 0)),
        out_shape=jax.ShapeDtypeStruct((bsz, l, d), BF16),
        compiler_params=_cparams(("parallel", "parallel")),
        name="prenorm",
    )(h, g.reshape(1, d), mods_flat, mods_flat)


def _matmul_kernel(x_ref, w_ref, o_ref):
    o_ref[...] = _dot(x_ref[...], w_ref[...])


def _pick_tile(n, candidates):
    for c in candidates:
        if n % c == 0:
            return c
    return n


def in_projection(xn, w):
    m, d = xn.shape
    n = w.shape[1]
    tm = _pick_tile(m, (1024, 512, 256, 128, 64, 32, 16, 8))
    tn = _pick_tile(n, (896, 640, 512, 384, 256, 128))
    return pl.pallas_call(
        _matmul_kernel,
        grid=(m // tm, n // tn),
        in_specs=[
            pl.BlockSpec((tm, d), lambda i, j: (i, 0)),
            pl.BlockSpec((d, tn), lambda i, j: (0, j)),
        ],
        out_specs=pl.BlockSpec((tm, tn), lambda i, j: (i, j)),
        out_shape=jax.ShapeDtypeStruct((m, n), F32),
        compiler_params=_cparams(("parallel", "parallel")),
        name="in_projection",
    )(xn, w)


def _decay_constants(c):
    nlev = int(math.log2(c))
    w = np.zeros((nlev + 2, c, c), np.float32)
    masks = np.zeros((nlev, c, c), np.float32)
    for lev in range(nlev):
        s = 1 << lev
        for r in range(c):
            pos = r % (2 * s)
            mid = r - pos + s
            if pos >= s:
                w[lev, r, mid:r + 1] = 1.0
            else:
                w[lev, r, r + 1:mid] = 1.0
        for i in range(c):
            for j in range(c):
                if i // (2 * s) == j // (2 * s) and i % (2 * s) >= s and j % (2 * s) < s:
                    masks[lev, i, j] = 1.0
    for r in range(c):
        w[nlev, r, :r + 1] = 1.0
        w[nlev + 1, r, r + 1:] = 1.0
    w = w.reshape((nlev + 2) * c, c)
    w = np.concatenate([w, np.ones((BF16_ROWS, c), np.float32)], axis=0)
    w_b = np.concatenate([w[:-BF16_ROWS].reshape(nlev + 2, c, c)[:, ::-1, ::-1].reshape(-1, c),
                          w[-BF16_ROWS:]], axis=0)
    masks_b = masks[:, ::-1, ::-1]
    return (np.stack([w, w_b]), np.stack([masks, masks_b]), nlev)


def _split3(x):
    hi = x.astype(BF16)
    r1 = x - hi.astype(F32)
    mid = r1.astype(BF16)
    lo = (r1 - mid.astype(F32)).astype(BF16)
    return hi, mid, lo


def _gated_core(q, k, v, g, w_ref, mask_ref, o_ref, state_ref, heads, dk, dv, nlev, c):
    w = w_ref[0]
    g_hi, g_mid, g_lo = _split3(g)
    e_all = jnp.exp(_dot(w, g_hi) + _dot(w, g_mid) + _dot(w, g_lo))
    for h in range(heads):
        ks = slice(h * dk, (h + 1) * dk)
        vs = slice(h * dv, (h + 1) * dv)
        qh, kh, vh = q[:, ks], k[:, ks], v[:, vs]
        vb = vh.astype(BF16)
        st = state_ref[h]
        e_in = e_all[nlev * c:(nlev + 1) * c, ks]
        e_out = e_all[(nlev + 1) * c:(nlev + 2) * c, ks]
        e_tot = e_all[(nlev + 2) * c:(nlev + 2) * c + 1, ks]
        o = _dot_nt((qh * e_in).astype(BF16), st.astype(BF16))
        scores = jnp.zeros((c, c), F32)
        for lev in range(nlev):
            f = e_all[lev * c:(lev + 1) * c, ks]
            scores = scores + mask_ref[0, lev] * _dot_nt((qh * f).astype(BF16), (kh * f).astype(BF16))
        o = o + _dot(scores.astype(BF16), vb) + jnp.sum(qh * kh, axis=-1, keepdims=True) * vh
        state_ref[h] = st * e_tot + _dot_tn(vb, (kh * e_out).astype(BF16))
        o_ref[0, 0, :, vs] = o


def _hgrn_kernel(q_ref, z_ref, v_ref, lbl_ref, w_ref, mask_ref, o_ref, state_ref, *,
                 layer, heads, dk, nlev, c):
    @pl.when(pl.program_id(2) == 0)
    def _():
        state_ref[...] = jnp.zeros_like(state_ref)

    logits = lbl_ref[0]
    ex = jnp.exp(logits - jnp.max(logits, axis=0, keepdims=True))
    p = ex / jnp.sum(ex, axis=0, keepdims=True)
    lb = jnp.sum(p[:layer + 1], axis=0, keepdims=True) - p[0:1]
    z = z_ref[0]
    e = jnp.exp(-jnp.abs(z))
    s_big = 1.0 / (1.0 + e)
    s_small = e / (1.0 + e)
    sig_pos = jnp.where(z >= 0, s_big, s_small)
    sig_neg = jnp.where(z >= 0, s_small, s_big)
    g = jnp.log(jnp.maximum(lb + (1.0 - lb) * sig_pos, A_MIN_FORGET))
    k = (1.0 - lb) * sig_neg
    _gated_core(q_ref[0], k, v_ref[0], g, w_ref, mask_ref, o_ref, state_ref, heads, dk, dk, nlev, c)


def _gla_kernel(q_ref, k_ref, v_ref, lr_ref, wgk_ref, bgk_ref, w_ref, mask_ref, o_ref, state_ref, *,
                heads, dk, dv, nlev, c):
    @pl.when(pl.program_id(2) == 0)
    def _():
        state_ref[...] = jnp.zeros_like(state_ref)

    logit = _dot(lr_ref[0].astype(BF16), wgk_ref[0]) + bgk_ref[0]
    g = _log_sigmoid(logit) / B_GATE_NORM
    q = q_ref[0] * (dk ** -0.5)
    _gated_core(q, k_ref[0], v_ref[0], g, w_ref, mask_ref, o_ref, state_ref, heads, dk, dv, nlev, c)


def _chunk_order(n_chunks_ctx, n_chunks):
    def chunk(d, n):
        bwd = jnp.where(n < n_chunks_ctx, n_chunks_ctx - 1 - n, n_chunks - 1 - n + n_chunks_ctx)
        return jnp.where(d == 0, n, bwd)
    return chunk


def hgrn_mixer(proj, col, lb_logits, layer, n_ctx):
    bsz, l, _ = proj.shape
    depth, _, width = lb_logits.shape
    c = CHUNK
    heads, dk = width // A_HEAD_DIM, A_HEAD_DIM
    w_np, m_np, nlev = _decay_constants(c)
    chunk = _chunk_order(n_ctx // c, l // c)
    cq, cf, ci = (x // width for x in col)
    kern = functools.partial(_hgrn_kernel, layer=layer, heads=heads, dk=dk, nlev=nlev, c=c)
    return pl.pallas_call(
        kern,
        grid=(2, bsz, l // c),
        in_specs=[
            pl.BlockSpec((1, c, width), lambda d, b, n: (b, chunk(d, n), cq)),
            pl.BlockSpec((1, c, width), lambda d, b, n: (b, chunk(d, n), cf + d)),
            pl.BlockSpec((1, c, width), lambda d, b, n: (b, chunk(d, n), ci)),
            pl.BlockSpec((1, depth, width), lambda d, b, n: (d, 0, 0)),
            pl.BlockSpec((1,) + w_np.shape[1:], lambda d, b, n: (d, 0, 0)),
            pl.BlockSpec((1,) + m_np.shape[1:], lambda d, b, n: (d, 0, 0, 0)),
        ],
        out_specs=pl.BlockSpec((1, 1, c, width), lambda d, b, n: (d, b, chunk(d, n), 0)),
        out_shape=jax.ShapeDtypeStruct((2, bsz, l, width), F32),
        scratch_shapes=[pltpu.VMEM((heads, dk, dk), F32)],
        compiler_params=_cparams(("parallel", "parallel", "arbitrary")),
        name="hgrn_mixer",
    )(proj, proj, proj, jnp.swapaxes(lb_logits, 0, 1), jnp.asarray(w_np, BF16), jnp.asarray(m_np))


def gla_mixer(proj, col, wgk_pad, b_gk, n_ctx, width):
    bsz, l, _ = proj.shape
    c = CHUNK
    key_w = wgk_pad.shape[-1]
    heads = B_HEADS
    dk, dv = key_w // heads, width // heads
    w_np, m_np, nlev = _decay_constants(c)
    chunk = _chunk_order(n_ctx // c, l // c)
    cq, ck, cv, clr = col
    kern = functools.partial(_gla_kernel, heads=heads, dk=dk, dv=dv, nlev=nlev, c=c)
    return pl.pallas_call(
        kern,
        grid=(2, bsz, l // c),
        in_specs=[
            pl.BlockSpec((1, c, key_w), lambda d, b, n: (b, chunk(d, n), cq // key_w)),
            pl.BlockSpec((1, c, key_w), lambda d, b, n: (b, chunk(d, n), ck // key_w)),
            pl.BlockSpec((1, c, width), lambda d, b, n: (b, chunk(d, n), cv // width)),
            pl.BlockSpec((1, c, LANES), lambda d, b, n: (b, chunk(d, n), clr // LANES)),
            pl.BlockSpec((1, LANES, key_w), lambda d, b, n: (d, 0, 0)),
            pl.BlockSpec((1, 1, key_w), lambda d, b, n: (d, 0, 0)),
            pl.BlockSpec((1,) + w_np.shape[1:], lambda d, b, n: (d, 0, 0)),
            pl.BlockSpec((1,) + m_np.shape[1:], lambda d, b, n: (d, 0, 0, 0)),
        ],
        out_specs=pl.BlockSpec((1, 1, c, width), lambda d, b, n: (d, b, chunk(d, n), 0)),
        out_shape=jax.ShapeDtypeStruct((2, bsz, l, width), F32),
        scratch_shapes=[pltpu.VMEM((heads, dv, dk), F32)],
        compiler_params=_cparams(("parallel", "parallel", "arbitrary")),
        name="gla_mixer",
    )(proj, proj, proj, proj, wgk_pad, b_gk.reshape(2, 1, key_w),
      jnp.asarray(w_np, BF16), jnp.asarray(m_np))


def _retention_kernel(q_ref, k_ref, v_ref, cos_ref, sin_ref, dl_ref, o_ref, state_ref, *,
                      heads, dk, dv, c):
    d = pl.program_id(0)

    @pl.when(pl.program_id(2) == 0)
    def _():
        state_ref[...] = jnp.zeros_like(state_ref)

    sgn = (1 - 2 * d).astype(F32)
    ii = lax.broadcasted_iota(jnp.int32, (c, c), 0).astype(F32)
    jj = lax.broadcasted_iota(jnp.int32, (c, c), 1).astype(F32)
    rel = (ii - jj) * sgn
    t_col = lax.broadcasted_iota(jnp.int32, (c, 1), 0).astype(F32)
    since = jnp.where(d == 0, t_col, (c - 1.0) - t_col)
    log_gamma = _log_sigmoid(dl_ref[0])
    cos, sin = cos_ref[...], sin_ref[...]
    half = dk // 2
    for h in range(heads):
        ks = slice(h * dk, (h + 1) * dk)
        vs = slice(h * dv, (h + 1) * dv)
        lg = log_gamma[:, h:h + 1]
        qh, kh = q_ref[0, :, ks], k_ref[0, :, ks]
        qh = (qh * cos + pltpu.roll(qh, half, 1) * sin) * (dk ** -0.5)
        kh = kh * cos + pltpu.roll(kh, half, 1) * sin
        vb = v_ref[0, :, vs].astype(BF16)
        st = state_ref[h]
        dmat = jnp.where(rel >= 0, jnp.exp(lg * jnp.maximum(rel, 0.0)), 0.0)
        scores = _dot_nt(qh.astype(BF16), kh.astype(BF16)) * dmat
        xi = jnp.exp(lg * (since + 1.0))
        zeta = jnp.exp(lg * ((c - 1.0) - since))
        o = _dot(scores.astype(BF16), vb) + _dot_nt(qh.astype(BF16), st.astype(BF16)) * xi
        state_ref[h] = st * jnp.exp(lg * c) + _dot_tn(vb, (kh * zeta).astype(BF16))
        o_ref[0, 0, :, vs] = o


def retention_mixer(proj, col, cos_t, sin_t, decay_logit_pad, n_ctx, width):
    bsz, l, _ = proj.shape
    heads = D_HEADS
    key_w = width // 2
    dk, dv = key_w // heads, width // heads
    c = _pick_tile(math.gcd(n_ctx, l - n_ctx), (256, 128, 64))
    chunk = _chunk_order(n_ctx // c, l // c)
    cq, ck, cv = col
    kern = functools.partial(_retention_kernel, heads=heads, dk=dk, dv=dv, c=c)
    return pl.pallas_call(
        kern,
        grid=(2, bsz, l // c),
        in_specs=[
            pl.BlockSpec((1, c, key_w), lambda d, b, n: (b, chunk(d, n), cq // key_w)),
            pl.BlockSpec((1, c, key_w), lambda d, b, n: (b, chunk(d, n), ck // key_w)),
            pl.BlockSpec((1, c, width), lambda d, b, n: (b, chunk(d, n), cv // width)),
            pl.BlockSpec((c, dk), lambda d, b, n: (chunk(d, n), 0)),
            pl.BlockSpec((c, dk), lambda d, b, n: (chunk(d, n), 0)),
            pl.BlockSpec((1, 1, LANES), lambda d, b, n: (d, 0, 0)),
        ],
        out_specs=pl.BlockSpec((1, 1, c, width), lambda d, b, n: (d, b, chunk(d, n), 0)),
        out_shape=jax.ShapeDtypeStruct((2, bsz, l, width), F32),
        scratch_shapes=[pltpu.VMEM((heads, dv, dk), F32)],
        compiler_params=_cparams(("parallel", "parallel", "arbitrary")),
        name="retention_mixer",
    )(proj, proj, proj, cos_t, sin_t, decay_logit_pad)


def _combine_kernel(of_ref, ob_ref, gate_ref, g_ref, o_ref, *, heads, hd, center):
    gate = gate_ref[0]
    gs = _silu(gate)
    for h in range(heads):
        sl = slice(h * hd, (h + 1) * hd)
        x = of_ref[0, 0, :, sl] + ob_ref[0, 0, :, sl]
        if center:
            x = x - jnp.mean(x, axis=-1, keepdims=True)
        y = x * lax.rsqrt(jnp.mean(x * x, axis=-1, keepdims=True) + EPS) * g_ref[:, sl]
        o_ref[0, :, sl] = (y * gs[:, sl]).astype(BF16)


def combine(o2, proj, gate_col, norm_g, hd, center):
    _, bsz, l, width = o2.shape
    tr = ROW_TILE
    heads = width // hd
    kern = functools.partial(_combine_kernel, heads=heads, hd=hd, center=center)
    return pl.pallas_call(
        kern,
        grid=(bsz, l // tr),
        in_specs=[
            pl.BlockSpec((1, 1, tr, width), lambda b, i: (0, b, i, 0)),
            pl.BlockSpec((1, 1, tr, width), lambda b, i: (1, b, i, 0)),
            pl.BlockSpec((1, tr, width), lambda b, i: (b, i, gate_col // width)),
            pl.BlockSpec((1, width), lambda b, i: (0, 0)),
        ],
        out_specs=pl.BlockSpec((1, tr, width), lambda b, i: (b, i, 0)),
        out_shape=jax.ShapeDtypeStruct((bsz, l, width), BF16),
        compiler_params=_cparams(("parallel", "parallel")),
        name="combine",
    )(o2, o2, proj, jnp.tile(norm_g, heads).reshape(1, width))


def _s5_kernel(u_ref, lam_ref, bt_ref, cm_ref, y_ref,
               toep_ref, win_r_ref, win_i_ref, wout_ref, cl_ref, s_r_ref, s_i_ref, xp_ref, *,
               bsz, n_chunks_ctx, n_chunks):
    t_len, hc, half = S5_CHUNK, C_GROUP, C_STATE
    lane = lax.broadcasted_iota(jnp.int32, (1, LANES), 1)
    lo = lane < half
    sgn = jnp.where(lo, -1.0, 1.0)
    tau = lax.broadcasted_iota(jnp.int32, (t_len, 1), 0)
    u = u_ref[0]
    rows = u.shape[0]
    y_acc = jnp.zeros((rows, t_len * hc), F32)

    def cmul(ar, ai, br, bi):
        return ar * br - ai * bi, ar * bi + ai * br

    def expand(x1, pa, x2, pb, out_ref):
        for t in range(t_len):
            blk = (x1 * jnp.broadcast_to(pa[t:t + 1], (hc, LANES))
                   + x2 * jnp.broadcast_to(pb[t:t + 1], (hc, LANES)))
            out_ref[t * hc:(t + 1) * hc, :] = blk.astype(out_ref.dtype)

    for d in range(2):
        lam_re = jnp.minimum(lam_ref[d, 0, 0:1], C_MAX_RE)
        lam_im = lam_ref[d, 0, 1:2]
        dt = jnp.exp(lam_ref[d, 0, 2:3])
        mag = jnp.exp(lam_re * dt)
        lb_r, lb_i = mag * jnp.cos(lam_im * dt), mag * jnp.sin(lam_im * dt)
        den = lam_re * lam_re + lam_im * lam_im
        nr, ni = lb_r - 1.0, lb_i
        cf_r, cf_i = (nr * lam_re + ni * lam_im) / den, (ni * lam_re - nr * lam_im) / den
        bt_r, bt_i = bt_ref[d, 0, 0], bt_ref[d, 0, 1]
        bb_r, bb_i = cmul(cf_r, cf_i, bt_r, bt_i)
        c_r, c_i = cm_ref[d, 0, 0], cm_ref[d, 0, 1]

        p_r, p_i = jnp.ones((t_len, LANES), F32), jnp.zeros((t_len, LANES), F32)
        q_r, q_i = p_r, p_i
        sq_r, sq_i = lb_r, lb_i
        for bit in range(int(math.log2(t_len))):
            sel = ((tau >> bit) & 1) == 1
            p_r, p_i = cmul(p_r, p_i, jnp.where(sel, sq_r, 1.0), jnp.where(sel, sq_i, 0.0))
            selq = (((t_len - 1 - tau) >> bit) & 1) == 1
            q_r, q_i = cmul(q_r, q_i, jnp.where(selq, sq_r, 1.0), jnp.where(selq, sq_i, 0.0))
            sq_r, sq_i = cmul(sq_r, sq_i, sq_r, sq_i)
        lc_r, lc_i = sq_r, sq_i
        if d == 0:
            toep_p, in_p = (p_r, p_i), (q_r, q_i)
            out_p = cmul(p_r, p_i, lb_r, lb_i)
        else:
            toep_p, in_p = (q_r, q_i), (p_r, p_i)
            out_p = cmul(q_r, q_i, lb_r, lb_i)

        tp_r, tp_i = toep_p
        expand(c_r, jnp.where(lo, tp_r, tp_i), sgn * c_i, jnp.where(lo, tp_i, tp_r), cl_ref)
        bbs = jnp.where(lo, bb_r, -bb_i)
        kt = lax.dot_general(bbs, cl_ref[...], (((1,), (1,)), ((), ())),
                             precision=lax.Precision.HIGHEST, preferred_element_type=F32)
        width = t_len * hc
        glane = lax.broadcasted_iota(jnp.int32, (hc, width), 1)
        per_tile = LANES // hc
        for m in range(per_tile):
            if d == 0:
                base = kt if m == 0 else jnp.where(glane >= hc * m, pltpu.roll(kt, hc * m, 1), 0.0)
            else:
                base = kt if m == 0 else jnp.where(glane < width - hc * m,
                                                   pltpu.roll(kt, width - hc * m, 1), 0.0)
            base = base.astype(BF16)
            for a in range(t_len // per_tile):
                off = a * LANES
                if d == 0:
                    j = a * per_tile + m
                    if off:
                        toep_ref[d, j * hc:(j + 1) * hc, :off] = jnp.zeros((hc, off), BF16)
                    toep_ref[d, j * hc:(j + 1) * hc, off:] = base[:, :width - off]
                else:
                    j = t_len - 1 - (a * per_tile + m)
                    if off:
                        toep_ref[d, j * hc:(j + 1) * hc, width - off:] = jnp.zeros((hc, off), BF16)
                    toep_ref[d, j * hc:(j + 1) * hc, :width - off] = base[:, off:]

        ip_r, ip_i = in_p
        expand(bb_r, ip_r, -bb_i, ip_i, win_r_ref)
        expand(bb_i, ip_r, bb_r, ip_i, win_i_ref)
        op_r, op_i = out_p
        expand(c_r, jnp.where(lo, op_r, -op_i), c_i, jnp.where(lo, -op_i, -op_r), wout_ref)

        s_r_ref[...] = _dot(u, win_r_ref[...])
        s_i_ref[...] = _dot(u, win_i_ref[...])
        if d == 0:
            order = list(range(n_chunks))
        else:
            order = list(range(n_chunks_ctx - 1, -1, -1)) + list(range(n_chunks - 1, n_chunks_ctx - 1, -1))
        x_r, x_i = jnp.zeros((bsz, LANES), F32), jnp.zeros((bsz, LANES), F32)
        for n in order:
            rs = slice(n * bsz, (n + 1) * bsz)
            xp_ref[rs, :] = jnp.where(lo, x_r, x_i)
            nx_r, nx_i = cmul(lc_r, lc_i, x_r, x_i)
            x_r, x_i = nx_r + s_r_ref[rs, :], nx_i + s_i_ref[rs, :]

        y_acc = y_acc + _dot(u, toep_ref[d]) + _dot_nt(xp_ref[...].astype(BF16), wout_ref[...])
    y_ref[0] = y_acc


def s5_core(ug, lam_pk, bt_pk, cm_pk, bsz, n_chunks_ctx, n_chunks):
    groups, rows, width = ug.shape
    kern = functools.partial(_s5_kernel, bsz=bsz, n_chunks_ctx=n_chunks_ctx, n_chunks=n_chunks)
    return pl.pallas_call(
        kern,
        grid=(groups,),
        in_specs=[
            pl.BlockSpec((1, rows, width), lambda g: (g, 0, 0)),
            pl.BlockSpec((2, 1, SUBLANES, LANES), lambda g: (0, g, 0, 0)),
            pl.BlockSpec((2, 1, 2, C_GROUP, LANES), lambda g: (0, g, 0, 0, 0)),
            pl.BlockSpec((2, 1, 2, C_GROUP, LANES), lambda g: (0, g, 0, 0, 0)),
        ],
        out_specs=pl.BlockSpec((1, rows, width), lambda g: (g, 0, 0)),
        out_shape=jax.ShapeDtypeStruct((groups, rows, width), F32),
        scratch_shapes=[
            pltpu.VMEM((2, width, width), BF16),
            pltpu.VMEM((width, LANES), BF16),
            pltpu.VMEM((width, LANES), BF16),
            pltpu.VMEM((width, LANES), BF16),
            pltpu.VMEM((width, LANES), F32),
            pltpu.VMEM((rows, LANES), F32),
            pltpu.VMEM((rows, LANES), F32),
            pltpu.VMEM((rows, LANES), F32),
        ],
        compiler_params=_cparams(("parallel",)),
        name="s5_core",
    )(ug, lam_pk, bt_pk, cm_pk)


def _s5_post_kernel(y_ref, u_ref, gate_ref, d_ref, w_ref, b_ref, o_ref):
    y = y_ref[0] + d_ref[...] * u_ref[0]
    z = jax.nn.gelu(y)
    t = _dot(z.astype(BF16), w_ref[...]) + b_ref[...]
    o_ref[0] = (z * _sigmoid(t) * _silu(gate_ref[0])).astype(BF16)


def s5_post(y, proj, u_col, gate_col, d_skip, w_glu, b_glu):
    bsz, l, width = y.shape
    tr = ROW_TILE
    return pl.pallas_call(
        _s5_post_kernel,
        grid=(bsz, l // tr),
        in_specs=[
            pl.BlockSpec((1, tr, width), lambda b, i: (b, i, 0)),
            pl.BlockSpec((1, tr, width), lambda b, i: (b, i, u_col // width)),
            pl.BlockSpec((1, tr, width), lambda b, i: (b, i, gate_col // width)),
            pl.BlockSpec((1, width), lambda b, i: (0, 0)),
            pl.BlockSpec((width, width), lambda b, i: (0, 0)),
            pl.BlockSpec((1, width), lambda b, i: (0, 0)),
        ],
        out_specs=pl.BlockSpec((1, tr, width), lambda b, i: (b, i, 0)),
        out_shape=jax.ShapeDtypeStruct((bsz, l, width), BF16),
        compiler_params=_cparams(("parallel", "parallel")),
        name="s5_post",
    )(y, proj, proj, d_skip.reshape(1, width), w_glu.astype(BF16), b_glu.reshape(1, width))


def _outproj_kernel(oa_ref, ob_ref, oc_ref, od_ref, w_ref, h_ref, gl_ref, gc_ref, o_ref, *,
                    n_ctx, tm, bw):
    acc = _dot(oa_ref[0], w_ref[0:bw, :])
    acc = acc + _dot(ob_ref[0], w_ref[bw:2 * bw, :])
    acc = acc + _dot(oc_ref[0], w_ref[2 * bw:3 * bw, :])
    acc = acc + _dot(od_ref[0], w_ref[3 * bw:4 * bw, :])
    row = pl.program_id(1) * tm + lax.broadcasted_iota(jnp.int32, (tm, 1), 0)
    gate = jnp.where(row < n_ctx, gc_ref[0], gl_ref[0])
    o_ref[0] = h_ref[0] + gate * acc


def out_projection(o_parts, w_out, h, mods_flat, layer, n_ctx):
    bsz, l, d = h.shape
    bw = o_parts[0].shape[-1]
    tm = l if l <= 2048 else _pick_tile(l, (l // 4, l // 8, l // 16))
    tn = _pick_tile(d, (512, 256, 128))
    kern = functools.partial(_outproj_kernel, n_ctx=n_ctx, tm=tm, bw=bw)
    o_spec = pl.BlockSpec((1, tm, bw), lambda b, i, j: (b, i, 0))
    return pl.pallas_call(
        kern,
        grid=(bsz, l // tm, d // tn),
        in_specs=[
            o_spec, o_spec, o_spec, o_spec,
            pl.BlockSpec((4 * bw, tn), lambda b, i, j: (0, j)),
            pl.BlockSpec((1, tm, tn), lambda b, i, j: (b, i, j)),
            pl.BlockSpec((1, 1, tn), lambda b, i, j: ((layer * SUBLANES + b) * 3 + 2, 0, j)),
            pl.BlockSpec((1, 1, tn), lambda b, i, j: ((layer * SUBLANES + bsz) * 3 + 2, 0, j)),
        ],
        out_specs=pl.BlockSpec((1, tm, tn), lambda b, i, j: (b, i, j)),
        out_shape=jax.ShapeDtypeStruct((bsz, l, d), F32),
        compiler_params=_cparams(("parallel", "parallel", "parallel")),
        name="out_projection",
    )(*o_parts, w_out, h, mods_flat, mods_flat)


def _final_norm_kernel(x_ref, g_ref, o_ref):
    x = x_ref[0]
    o_ref[0] = x * lax.rsqrt(jnp.mean(x * x, axis=-1, keepdims=True) + EPS) * g_ref[...]


def final_norm(h, g, n_ctx):
    bsz, l, d = h.shape
    tr = ROW_TILE
    skip = n_ctx // tr
    return pl.pallas_call(
        _final_norm_kernel,
        grid=(bsz, (l - n_ctx) // tr),
        in_specs=[
            pl.BlockSpec((1, tr, d), lambda b, i: (b, i + skip, 0)),
            pl.BlockSpec((1, d), lambda b, i: (0, 0)),
        ],
        out_specs=pl.BlockSpec((1, tr, d), lambda b, i: (b, i, 0)),
        out_shape=jax.ShapeDtypeStruct((bsz, l - n_ctx, d), F32),
        compiler_params=_cparams(("parallel", "parallel")),
        name="final_norm",
    )(h, g.reshape(1, d))


def _rope_tables(rows, n_ctx, dk):
    quarter = dk // 4
    freqs = ROPE_BASE ** (-jnp.arange(quarter, dtype=F32) / quarter)
    t = jnp.arange(rows * GRID_W)
    r = (t // GRID_W).astype(F32)
    col = (t % GRID_W).astype(F32)
    ang = jnp.concatenate([r[:, None] * freqs, col[:, None] * freqs], axis=-1)
    ang = jnp.concatenate([jnp.zeros((n_ctx, dk // 2), F32), ang], axis=0)
    cos, sin = jnp.cos(ang), jnp.sin(ang)
    return jnp.concatenate([cos, cos], axis=-1), jnp.concatenate([-sin, sin], axis=-1)


def _dup(x):
    return jnp.concatenate([x, x], axis=-1)


def mixer_layer(h, mods_flat, layer, n_ctx, norm_g, w_in, hgrn_lb_logits, hgrn_norm_g, gla_w_gk,
                gla_b_gk, gla_norm_g, s5_lam_re, s5_lam_im, s5_log_dt, s5_b_re, s5_b_im, s5_c_re,
                s5_c_im, s5_d, s5_w_glu, s5_b_glu, ret_decay_logit, ret_norm_g, w_out, rope):
    bsz, l, d = h.shape
    bw = d // 4
    kw = bw // 2
    rank = B_GATE_RANK

    lr0 = 5 * bw + 2 * kw + bw
    w_l = w_in[layer].astype(BF16)
    w_perm = jnp.concatenate(
        [w_l[:, :lr0], w_l[:, lr0 + 2 * rank:], w_l[:, lr0:lr0 + 2 * rank],
         jnp.zeros((d, LANES - 2 * rank), BF16)], axis=1)
    names = ("a_q", "a_ff", "a_fb", "a_i", "a_g", "b_q", "b_k", "b_v", "b_g", "c_u", "c_g",
             "d_q", "d_k", "d_v", "d_g", "b_lr")
    widths = (bw, bw, bw, bw, bw, kw, kw, bw, bw, bw, bw, kw, kw, bw, bw, LANES)
    col = dict(zip(names, np.concatenate([[0], np.cumsum(widths)[:-1]]).tolist()))

    xn = prenorm(h, norm_g[layer], mods_flat, layer, n_ctx)
    proj = in_projection(xn.reshape(bsz * l, d), w_perm).reshape(bsz, l, -1)

    o2 = hgrn_mixer(proj, (col["a_q"], col["a_ff"], col["a_i"]), hgrn_lb_logits, layer, n_ctx)
    o_a = combine(o2, proj, col["a_g"], hgrn_norm_g[layer], A_HEAD_DIM, False)

    wgk = gla_w_gk[layer].astype(BF16)
    wgk_pad = jnp.zeros((2, LANES, kw), BF16)
    wgk_pad = wgk_pad.at[0, :rank].set(wgk[0]).at[1, rank:2 * rank].set(wgk[1])
    o2 = gla_mixer(proj, (col["b_q"], col["b_k"], col["b_v"], col["b_lr"]), wgk_pad,
                   gla_b_gk[layer], n_ctx, bw)
    o_b = combine(o2, proj, col["b_g"], gla_norm_g[layer], bw // B_HEADS, False)

    groups = bw // C_GROUP
    nck = l // S5_CHUNK
    u = proj[:, :, col["c_u"]:col["c_u"] + bw].astype(BF16)
    ug = u.reshape(bsz, nck, S5_CHUNK, groups, C_GROUP).transpose(3, 1, 0, 2, 4)
    ug = ug.reshape(groups, nck * bsz, S5_CHUNK * C_GROUP)
    dt_row = jnp.broadcast_to(s5_log_dt[layer][..., None], (2, groups, C_STATE))
    lam_pk = jnp.stack([_dup(s5_lam_re[layer]), _dup(s5_lam_im[layer]), _dup(dt_row)], axis=2)
    lam_pk = jnp.pad(lam_pk, ((0, 0), (0, 0), (0, SUBLANES - 3), (0, 0)))
    bt_pk = jnp.stack([_dup(jnp.swapaxes(s5_b_re[layer], -1, -2)),
                       _dup(jnp.swapaxes(s5_b_im[layer], -1, -2))], axis=2)
    cm_pk = jnp.stack([_dup(s5_c_re[layer]), _dup(s5_c_im[layer])], axis=2)
    yg = s5_core(ug, lam_pk, bt_pk, cm_pk, bsz, n_ctx // S5_CHUNK, nck)
    y = yg.reshape(groups, nck, bsz, S5_CHUNK, C_GROUP).transpose(2, 1, 3, 0, 4).reshape(bsz, l, bw)
    o_c = s5_post(y, proj, col["c_u"], col["c_g"], s5_d[layer], s5_w_glu[layer], s5_b_glu[layer])

    dl = jnp.pad(ret_decay_logit[layer], ((0, 0), (0, LANES - D_HEADS))).reshape(2, 1, LANES)
    o2 = retention_mixer(proj, (col["d_q"], col["d_k"], col["d_v"]), rope[0], rope[1], dl, n_ctx, bw)
    o_d = combine(o2, proj, col["d_g"], ret_norm_g[layer], bw // D_HEADS, True)

    return out_projection((o_a, o_b, o_c, o_d), w_out[layer].astype(BF16), h, mods_flat, layer, n_ctx)


def kernel(x, c, ctx, c_ctx, norm_g, w_ada, b_ada, w_in, hgrn_lb_logits, hgrn_norm_g, gla_w_gk,
           gla_b_gk, gla_norm_g, s5_lam_re, s5_lam_im, s5_log_dt, s5_b_re, s5_b_im, s5_c_re, s5_c_im,
           s5_d, s5_w_glu, s5_b_glu, ret_decay_logit, ret_norm_g, w_out, final_norm_g):
    bsz, seq, d = x.shape
    n_ctx = ctx.shape[1]
    depth = w_in.shape[0]
    assert bsz < SUBLANES and n_ctx % ROW_TILE == 0 and seq % ROW_TILE == 0

    cvec = jnp.concatenate([c, c_ctx[None], jnp.zeros((SUBLANES - bsz - 1, d), F32)], axis=0)
    mods = ada_modulation(cvec, w_ada, b_ada)
    mods_flat = mods.reshape(depth * SUBLANES * 3, 1, d)
    rope = _rope_tables(seq // GRID_W, n_ctx, (d // 8) // D_HEADS)

    h = jnp.concatenate([ctx, x], axis=1)
    for layer in range(depth):
        h = mixer_layer(h, mods_flat, layer, n_ctx, norm_g, w_in, hgrn_lb_logits, hgrn_norm_g,
                        gla_w_gk, gla_b_gk, gla_norm_g, s5_lam_re, s5_lam_im, s5_log_dt, s5_b_re,
                        s5_b_im, s5_c_re, s5_c_im, s5_d, s5_w_glu, s5_b_glu, ret_decay_logit,
                        ret_norm_g, w_out, rope)
    return final_norm(h, final_norm_g, n_ctx)
```

```python
import functools
import math

import numpy as np
import jax
import jax.numpy as jnp
from jax import lax
from jax.experimental import pallas as pl
from jax.experimental.pallas import tpu as pltpu

F32 = jnp.float32
BF16 = jnp.bfloat16

EPS = 1e-6
A_HEAD_DIM = 128
A_MIN_FORGET = 1e-6
B_HEADS = 4
B_GATE_RANK = 16
B_GATE_NORM = 16.0
C_GROUP = 16
C_STATE = 64
C_MAX_RE = -1e-4
D_HEADS = 4
GRID_W = 64
ROPE_BASE = 10000.0

LANES = 128
SUBLANES = 8
BF16_ROWS = 16
VMEM_LIMIT = 56 * 1024 * 1024

CHUNK = 64
SHORT_SPAN = 60.0
S5_CHUNK = 64
ROW_TILE = 256


def _cparams(sem):
    return pltpu.CompilerParams(dimension_semantics=sem, vmem_limit_bytes=VMEM_LIMIT)


def _dot(a, b):
    return jnp.dot(a, b, preferred_element_type=F32)


def _dot_nt(a, b):
    return lax.dot_general(a, b, (((1,), (1,)), ((), ())), preferred_element_type=F32)


def _dot_tn(a, b):
    return lax.dot_general(a, b, (((0,), (0,)), ((), ())), preferred_element_type=F32)


def _sigmoid(x):
    return 1.0 / (1.0 + jnp.exp(-x))


def _silu(x):
    return x * _sigmoid(x)


def _log_sigmoid(x):
    return jnp.minimum(x, 0.0) - jnp.log(1.0 + jnp.exp(-jnp.abs(x)))


def _ada_kernel(c_ref, w_ref, b_ref, o_ref):
    cv = _silu(c_ref[...]).astype(BF16)
    o_ref[0] = _dot(cv, w_ref[0].astype(BF16)) + b_ref[0]


def ada_modulation(cvec, w_ada, b_ada):
    depth, d, n3 = w_ada.shape
    tn = 512
    return pl.pallas_call(
        _ada_kernel,
        grid=(depth, n3 // tn),
        in_specs=[
            pl.BlockSpec((SUBLANES, d), lambda l, j: (0, 0)),
            pl.BlockSpec((1, d, tn), lambda l, j: (l, 0, j)),
            pl.BlockSpec((1, 1, tn), lambda l, j: (l, 0, j)),
        ],
        out_specs=pl.BlockSpec((1, SUBLANES, tn), lambda l, j: (l, 0, j)),
        out_shape=jax.ShapeDtypeStruct((depth, SUBLANES, n3), F32),
        compiler_params=_cparams(("parallel", "parallel")),
        name="ada_modulation",
    )(cvec, w_ada, b_ada.reshape(depth, 1, n3))


def _prenorm_kernel(x_ref, g_ref, sh_ref, sc_ref, o_ref):
    x = x_ref[0]
    y = x * lax.rsqrt(jnp.mean(x * x, axis=-1, keepdims=True) + EPS) * g_ref[...]
    o_ref[0] = (y * (1.0 + sc_ref[0]) + sh_ref[0]).astype(BF16)


def prenorm(h, g, mods_flat, layer, n_ctx):
    bsz, l, d = h.shape
    tr = ROW_TILE
    nct = n_ctx // tr

    def mod_map(part):
        def f(b, i):
            row = jnp.where(i < nct, bsz, b)
            return ((layer * SUBLANES + row) * 3 + part, 0, 0)
        return f

    return pl.pallas_call(
        _prenorm_kernel,
        grid=(bsz, l // tr),
        in_specs=[
            pl.BlockSpec((1, tr, d), lambda b, i: (b, i, 0)),
            pl.BlockSpec((1, d), lambda b, i: (0, 0)),
            pl.BlockSpec((1, 1, d), mod_map(0)),
            pl.BlockSpec((1, 1, d), mod_map(1)),
        ],
        out_specs=pl.BlockSpec((1, tr, d), lambda b, i: (b, i, 0)),
        out_shape=jax.ShapeDtypeStruct((bsz, l, d), BF16),
        compiler_params=_cparams(("parallel", "parallel")),
        name="prenorm",
    )(h, g.reshape(1, d), mods_flat, mods_flat)


def _matmul_kernel(x_ref, w_ref, o_ref):
    o_ref[...] = _dot(x_ref[...], w_ref[0])


def _pick_tile(n, candidates):
    for c in candidates:
        if n % c == 0:
            return c
    return n


def in_projection(xn, w_all, layer):
    m, d = xn.shape
    n = w_all.shape[2]
    tm = _pick_tile(m, (1024, 512, 256, 128, 64, 32, 16, 8))
    tn = _pick_tile(n, (896, 640, 512, 384, 256, 128))
    return pl.pallas_call(
        _matmul_kernel,
        grid=(m // tm, n // tn),
        in_specs=[
            pl.BlockSpec((tm, d), lambda i, j: (i, 0)),
            pl.BlockSpec((1, d, tn), lambda i, j: (layer, 0, j)),
        ],
        out_specs=pl.BlockSpec((tm, tn), lambda i, j: (i, j)),
        out_shape=jax.ShapeDtypeStruct((m, n), F32),
        compiler_params=_cparams(("parallel", "parallel")),
        name="in_projection",
    )(xn, w_all)


def _decay_constants(c, bwd):
    nlev = int(math.log2(c))
    w = np.zeros((nlev + 2, c, c), np.float32)
    masks = np.zeros((nlev, c, c), np.float32)
    for lev in range(nlev):
        s = 1 << lev
        for r in range(c):
            pos = r % (2 * s)
            mid = r - pos + s
            if pos >= s:
                w[lev, r, mid:r + 1] = 1.0
            else:
                w[lev, r, r + 1:mid] = 1.0
        for i in range(c):
            for j in range(c):
                if i // (2 * s) == j // (2 * s) and i % (2 * s) >= s and j % (2 * s) < s:
                    masks[lev, i, j] = 1.0
    for r in range(c):
        w[nlev, r, :r + 1] = 1.0
        w[nlev + 1, r, r + 1:] = 1.0
    half = c // 2
    ii, jj = np.meshgrid(np.arange(c), np.arange(c), indexing="ij")
    diag = ((ii // half == jj // half) & (jj <= ii)).astype(np.float32)
    masks = np.concatenate([masks, diag[None]], axis=0)
    w = w.reshape((nlev + 2) * c, c)
    w = np.concatenate([w, np.ones((BF16_ROWS, c), np.float32)], axis=0)
    if bwd:
        w = np.concatenate([w[:-BF16_ROWS].reshape(nlev + 2, c, c)[:, ::-1, ::-1].reshape(-1, c),
                            w[-BF16_ROWS:]], axis=0)
        masks = masks[:, ::-1, ::-1]
    return np.ascontiguousarray(w), np.ascontiguousarray(masks), nlev


def _split3(x):
    hi = x.astype(BF16)
    r1 = x - hi.astype(F32)
    mid = r1.astype(BF16)
    lo = (r1 - mid.astype(F32)).astype(BF16)
    return hi, mid, lo


def _gated_core(q, k, v, g, w_ref, mask_ref, o_scr, state_ref, heads, dk, dv, nlev, c, bwd):
    half = c // 2
    g_parts = _split3(g)
    w_in = w_ref[nlev * c:(nlev + 1) * c, :]
    b_in = _dot(w_in, g_parts[0]) + _dot(w_in, g_parts[1]) + _dot(w_in, g_parts[2])
    first_a, last_a, first_b, last_b = (half - 1, 0, c - 1, half) if bwd else (0, half - 1, half, c - 1)
    top_ref, exit_row = (half, 0) if bwd else (half - 1, c - 1)
    span = jnp.minimum(b_in[last_a:last_a + 1] - b_in[first_a:first_a + 1],
                       b_in[last_b:last_b + 1] - b_in[first_b:first_b + 1])
    short = jnp.min(span) >= -SHORT_SPAN

    def head_update(h, scores, e_in, e_out, e_tot, extra):
        ks = slice(h * dk, (h + 1) * dk)
        vs = slice(h * dv, (h + 1) * dv)
        qh, kh, vh = q[:, ks], k[:, ks], v[:, vs]
        vb = vh.astype(BF16)
        st = state_ref[h]
        o = _dot_nt((qh * e_in).astype(BF16), st.astype(BF16)) + _dot(scores.astype(BF16), vb)
        if extra is not None:
            o = o + extra * vh
        state_ref[h] = st * e_tot + _dot_tn(vb, (kh * e_out).astype(BF16))
        o_scr[:, vs] = o

    @pl.when(short)
    def _():
        row = lax.broadcasted_iota(jnp.int32, (c, 1), 0)
        in_a = row < half
        m = jnp.where(in_a, b_in[first_a:first_a + 1], b_in[first_b:first_b + 1])
        fq = jnp.exp(b_in - m)
        fk = jnp.exp(m - b_in)
        r1 = b_in[top_ref:top_ref + 1]
        later = in_a if bwd else jnp.logical_not(in_a)
        ft = jnp.exp(jnp.where(later, b_in - r1, r1 - b_in))
        tot = b_in[exit_row:exit_row + 1]
        e_in_all = jnp.exp(b_in)
        e_out_all = jnp.exp(tot - b_in)
        e_tot_all = jnp.exp(tot)
        for h in range(heads):
            ks = slice(h * dk, (h + 1) * dk)
            qh, kh = q[:, ks], k[:, ks]
            scores = (mask_ref[nlev] * _dot_nt((qh * fq[:, ks]).astype(BF16), (kh * fk[:, ks]).astype(BF16))
                      + mask_ref[nlev - 1] * _dot_nt((qh * ft[:, ks]).astype(BF16),
                                                     (kh * ft[:, ks]).astype(BF16)))
            head_update(h, scores, e_in_all[:, ks], e_out_all[:, ks], e_tot_all[:, ks], None)

    @pl.when(jnp.logical_not(short))
    def _():
        w = w_ref[...]
        e_all = jnp.exp(_dot(w, g_parts[0]) + _dot(w, g_parts[1]) + _dot(w, g_parts[2]))
        for h in range(heads):
            ks = slice(h * dk, (h + 1) * dk)
            qh, kh = q[:, ks], k[:, ks]
            scores = jnp.zeros((c, c), F32)
            for lev in range(nlev):
                f = e_all[lev * c:(lev + 1) * c, ks]
                scores = scores + mask_ref[lev] * _dot_nt((qh * f).astype(BF16), (kh * f).astype(BF16))
            head_update(h, scores, e_all[nlev * c:(nlev + 1) * c, ks],
                        e_all[(nlev + 1) * c:(nlev + 2) * c, ks],
                        e_all[(nlev + 2) * c:(nlev + 2) * c + 1, ks],
                        jnp.sum(qh * kh, axis=-1, keepdims=True))


def _finish(o_scr, of_ref, gate_ref, ng_ref, o_ref, heads, hd, center, final):
    if not final:
        o_ref[0] = o_scr[...]
        return
    gs = _silu(gate_ref[0])
    for h in range(heads):
        sl = slice(h * hd, (h + 1) * hd)
        x = o_scr[:, sl] + of_ref[0, :, sl]
        if center:
            x = x - jnp.mean(x, axis=-1, keepdims=True)
        y = x * lax.rsqrt(jnp.mean(x * x, axis=-1, keepdims=True) + EPS) * ng_ref[:, sl]
        o_ref[0, :, sl] = (y * gs[:, sl]).astype(BF16)


def _split_refs(refs, n_in, bwd):
    ins = refs[:n_in]
    fin = refs[n_in:n_in + 3] if bwd else (None, None, None)
    rest = refs[n_in + 3:] if bwd else refs[n_in:]
    return ins, fin, rest


def _hgrn_kernel(*refs, layer, heads, dk, nlev, c, bwd):
    (q_ref, z_ref, v_ref, lbl_ref, w_ref, mask_ref), fin, (o_ref, state_ref, o_scr) = \
        _split_refs(refs, 6, bwd)

    @pl.when(pl.program_id(1) == 0)
    def _():
        state_ref[...] = jnp.zeros_like(state_ref)

    logits = lbl_ref[...]
    ex = jnp.exp(logits - jnp.max(logits, axis=0, keepdims=True))
    p = ex / jnp.sum(ex, axis=0, keepdims=True)
    lb = jnp.sum(p[:layer + 1], axis=0, keepdims=True) - p[0:1]
    z = z_ref[0]
    e = jnp.exp(-jnp.abs(z))
    s_big = 1.0 / (1.0 + e)
    s_small = e / (1.0 + e)
    sig_pos = jnp.where(z >= 0, s_big, s_small)
    sig_neg = jnp.where(z >= 0, s_small, s_big)
    g = jnp.log(jnp.maximum(lb + (1.0 - lb) * sig_pos, A_MIN_FORGET))
    k = (1.0 - lb) * sig_neg
    _gated_core(q_ref[0], k, v_ref[0], g, w_ref, mask_ref, o_scr, state_ref, heads, dk, dk, nlev, c, bwd)
    _finish(o_scr, *fin, o_ref, heads, dk, False, bwd)


def _gla_kernel(*refs, heads, dk, dv, nlev, c, bwd):
    (q_ref, k_ref, v_ref, lr_ref, wgk_ref, bgk_ref, w_ref, mask_ref), fin, (o_ref, state_ref, o_scr) = \
        _split_refs(refs, 8, bwd)

    @pl.when(pl.program_id(1) == 0)
    def _():
        state_ref[...] = jnp.zeros_like(state_ref)

    logit = _dot(lr_ref[0].astype(BF16), wgk_ref[...]) + bgk_ref[...]
    g = _log_sigmoid(logit) / B_GATE_NORM
    q = q_ref[0] * (dk ** -0.5)
    _gated_core(q, k_ref[0], v_ref[0], g, w_ref, mask_ref, o_scr, state_ref, heads, dk, dv, nlev, c, bwd)
    _finish(o_scr, *fin, o_ref, heads, dv, False, bwd)


def _chunk_order(n_chunks_ctx, n_chunks, bwd):
    def chunk(n):
        if not bwd:
            return n
        return jnp.where(n < n_chunks_ctx, n_chunks_ctx - 1 - n, n_chunks - 1 - n + n_chunks_ctx)
    return chunk


def _mixer_call(kern, name, proj, in_arrays, in_specs, chunk, c, width, state_shape,
                o_fwd, gate_col, norm_row):
    bsz, l, _ = proj.shape
    bwd = o_fwd is not None
    blk = pl.BlockSpec((1, c, width), lambda b, n: (b, chunk(n), 0))
    if bwd:
        in_arrays = in_arrays + [o_fwd, proj, norm_row]
        in_specs = in_specs + [
            blk,
            pl.BlockSpec((1, c, width), lambda b, n: (b, chunk(n), gate_col // width)),
            pl.BlockSpec((1, width), lambda b, n: (0, 0)),
        ]
    return pl.pallas_call(
        kern,
        grid=(bsz, l // c),
        in_specs=in_specs,
        out_specs=blk,
        out_shape=jax.ShapeDtypeStruct((bsz, l, width), BF16 if bwd else F32),
        scratch_shapes=[pltpu.VMEM(state_shape, F32), pltpu.VMEM((c, width), F32)],
        compiler_params=_cparams(("parallel", "arbitrary")),
        name=name + ("_bwd" if bwd else "_fwd"),
    )(*in_arrays)


def hgrn_mixer(proj, col, lb_logits, layer, n_ctx, gate_col, norm_g):
    bsz, l, _ = proj.shape
    depth, _, width = lb_logits.shape
    c = CHUNK
    heads, dk = width // A_HEAD_DIM, A_HEAD_DIM
    cq, cf, ci = (x // width for x in col)
    norm_row = jnp.tile(norm_g, heads).reshape(1, width)
    o_fwd = None
    for bwd in (False, True):
        w_np, m_np, nlev = _decay_constants(c, bwd)
        chunk = _chunk_order(n_ctx // c, l // c, bwd)
        d = int(bwd)
        kern = functools.partial(_hgrn_kernel, layer=layer, heads=heads, dk=dk, nlev=nlev, c=c, bwd=bwd)
        in_specs = [
            pl.BlockSpec((1, c, width), lambda b, n, chunk=chunk: (b, chunk(n), cq)),
            pl.BlockSpec((1, c, width), lambda b, n, chunk=chunk, d=d: (b, chunk(n), cf + d)),
            pl.BlockSpec((1, c, width), lambda b, n, chunk=chunk: (b, chunk(n), ci)),
            pl.BlockSpec((depth, width), lambda b, n: (0, 0)),
            pl.BlockSpec(w_np.shape, lambda b, n: (0, 0)),
            pl.BlockSpec(m_np.shape, lambda b, n: (0, 0, 0)),
        ]
        in_arrays = [proj, proj, proj, lb_logits[:, d], jnp.asarray(w_np, BF16), jnp.asarray(m_np)]
        o_fwd = _mixer_call(kern, "hgrn_mixer", proj, in_arrays, in_specs, chunk, c, width,
                            (heads, dk, dk), o_fwd, gate_col, norm_row)
    return o_fwd


def gla_mixer(proj, col, wgk_pad, b_gk, n_ctx, width, gate_col, norm_g):
    bsz, l, _ = proj.shape
    c = CHUNK
    key_w = wgk_pad.shape[-1]
    heads = B_HEADS
    dk, dv = key_w // heads, width // heads
    cq, ck, cv, clr = col
    norm_row = jnp.tile(norm_g, heads).reshape(1, width)
    o_fwd = None
    for bwd in (False, True):
        w_np, m_np, nlev = _decay_constants(c, bwd)
        chunk = _chunk_order(n_ctx // c, l // c, bwd)
        d = int(bwd)
        kern = functools.partial(_gla_kernel, heads=heads, dk=dk, dv=dv, nlev=nlev, c=c, bwd=bwd)
        in_specs = [
            pl.BlockSpec((1, c, key_w), lambda b, n, chunk=chunk: (b, chunk(n), cq // key_w)),
            pl.BlockSpec((1, c, key_w), lambda b, n, chunk=chunk: (b, chunk(n), ck // key_w)),
            pl.BlockSpec((1, c, width), lambda b, n, chunk=chunk: (b, chunk(n), cv // width)),
            pl.BlockSpec((1, c, LANES), lambda b, n, chunk=chunk: (b, chunk(n), clr // LANES)),
            pl.BlockSpec((LANES, key_w), lambda b, n: (0, 0)),
            pl.BlockSpec((1, key_w), lambda b, n: (0, 0)),
            pl.BlockSpec(w_np.shape, lambda b, n: (0, 0)),
            pl.BlockSpec(m_np.shape, lambda b, n: (0, 0, 0)),
        ]
        in_arrays = [proj, proj, proj, proj, wgk_pad[d], b_gk[d].reshape(1, key_w),
                     jnp.asarray(w_np, BF16), jnp.asarray(m_np)]
        o_fwd = _mixer_call(kern, "gla_mixer", proj, in_arrays, in_specs, chunk, c, width,
                            (heads, dv, dk), o_fwd, gate_col, norm_row)
    return o_fwd


def _retention_kernel(*refs, heads, dk, dv, c, bwd):
    (q_ref, k_ref, v_ref, cos_ref, sin_ref, dl_ref), fin, (o_ref, state_ref, o_scr) = \
        _split_refs(refs, 6, bwd)

    @pl.when(pl.program_id(1) == 0)
    def _():
        state_ref[...] = jnp.zeros_like(state_ref)

    ii = lax.broadcasted_iota(jnp.int32, (c, c), 0).astype(F32)
    jj = lax.broadcasted_iota(jnp.int32, (c, c), 1).astype(F32)
    rel = (jj - ii) if bwd else (ii - jj)
    t_col = lax.broadcasted_iota(jnp.int32, (c, 1), 0).astype(F32)
    since = ((c - 1.0) - t_col) if bwd else t_col
    log_gamma = _log_sigmoid(dl_ref[...])
    cos, sin = cos_ref[...], sin_ref[...]
    half = dk // 2
    for h in range(heads):
        ks = slice(h * dk, (h + 1) * dk)
        vs = slice(h * dv, (h + 1) * dv)
        lg = log_gamma[:, h:h + 1]
        qh, kh = q_ref[0, :, ks], k_ref[0, :, ks]
        qh = (qh * cos + pltpu.roll(qh, half, 1) * sin) * (dk ** -0.5)
        kh = kh * cos + pltpu.roll(kh, half, 1) * sin
        vb = v_ref[0, :, vs].astype(BF16)
        st = state_ref[h]
        dmat = jnp.where(rel >= 0, jnp.exp(lg * jnp.maximum(rel, 0.0)), 0.0)
        scores = _dot_nt(qh.astype(BF16), kh.astype(BF16)) * dmat
        xi = jnp.exp(lg * (since + 1.0))
        zeta = jnp.exp(lg * ((c - 1.0) - since))
        o = _dot(scores.astype(BF16), vb) + _dot_nt(qh.astype(BF16), st.astype(BF16)) * xi
        state_ref[h] = st * jnp.exp(lg * c) + _dot_tn(vb, (kh * zeta).astype(BF16))
        o_scr[:, vs] = o
    _finish(o_scr, *fin, o_ref, heads, dv, True, bwd)


def retention_mixer(proj, col, cos_t, sin_t, decay_logit_pad, n_ctx, width, gate_col, norm_g):
    bsz, l, _ = proj.shape
    heads = D_HEADS
    key_w = width // 2
    dk, dv = key_w // heads, width // heads
    c = _pick_tile(math.gcd(n_ctx, l - n_ctx), (256, 128, 64))
    cq, ck, cv = col
    norm_row = jnp.tile(norm_g, heads).reshape(1, width)
    o_fwd = None
    for bwd in (False, True):
        chunk = _chunk_order(n_ctx // c, l // c, bwd)
        kern = functools.partial(_retention_kernel, heads=heads, dk=dk, dv=dv, c=c, bwd=bwd)
        in_specs = [
            pl.BlockSpec((1, c, key_w), lambda b, n, chunk=chunk: (b, chunk(n), cq // key_w)),
            pl.BlockSpec((1, c, key_w), lambda b, n, chunk=chunk: (b, chunk(n), ck // key_w)),
            pl.BlockSpec((1, c, width), lambda b, n, chunk=chunk: (b, chunk(n), cv // width)),
            pl.BlockSpec((c, dk), lambda b, n, chunk=chunk: (chunk(n), 0)),
            pl.BlockSpec((c, dk), lambda b, n, chunk=chunk: (chunk(n), 0)),
            pl.BlockSpec((1, LANES), lambda b, n: (0, 0)),
        ]
        in_arrays = [proj, proj, proj, cos_t, sin_t, decay_logit_pad[int(bwd)]]
        o_fwd = _mixer_call(kern, "retention_mixer", proj, in_arrays, in_specs, chunk, c, width,
                            (heads, dv, dk), o_fwd, gate_col, norm_row)
    return o_fwd


def _s5_kernel(u_ref, lam_ref, bt_ref, cm_ref, y_ref,
               toep_ref, win_r_ref, win_i_ref, wout_ref, cl_ref, s_r_ref, s_i_ref, xp_ref, *,
               bsz, n_chunks_ctx, n_chunks):
    t_len, hc, half = S5_CHUNK, C_GROUP, C_STATE
    lane = lax.broadcasted_iota(jnp.int32, (1, LANES), 1)
    lo = lane < half
    sgn = jnp.where(lo, -1.0, 1.0)
    tau = lax.broadcasted_iota(jnp.int32, (t_len, 1), 0)
    u = u_ref[0]
    rows = u.shape[0]
    y_acc = jnp.zeros((rows, t_len * hc), F32)

    def cmul(ar, ai, br, bi):
        return ar * br - ai * bi, ar * bi + ai * br

    def expand(x1, pa, x2, pb, out_ref):
        for t in range(t_len):
            blk = (x1 * jnp.broadcast_to(pa[t:t + 1], (hc, LANES))
                   + x2 * jnp.broadcast_to(pb[t:t + 1], (hc, LANES)))
            out_ref[t * hc:(t + 1) * hc, :] = blk.astype(out_ref.dtype)

    for d in range(2):
        lam_re = jnp.minimum(lam_ref[d, 0, 0:1], C_MAX_RE)
        lam_im = lam_ref[d, 0, 1:2]
        dt = jnp.exp(lam_ref[d, 0, 2:3])
        mag = jnp.exp(lam_re * dt)
        lb_r, lb_i = mag * jnp.cos(lam_im * dt), mag * jnp.sin(lam_im * dt)
        den = lam_re * lam_re + lam_im * lam_im
        nr, ni = lb_r - 1.0, lb_i
        cf_r, cf_i = (nr * lam_re + ni * lam_im) / den, (ni * lam_re - nr * lam_im) / den
        bt_r, bt_i = bt_ref[d, 0, 0], bt_ref[d, 0, 1]
        bb_r, bb_i = cmul(cf_r, cf_i, bt_r, bt_i)
        c_r, c_i = cm_ref[d, 0, 0], cm_ref[d, 0, 1]

        p_r, p_i = jnp.ones((t_len, LANES), F32), jnp.zeros((t_len, LANES), F32)
        q_r, q_i = p_r, p_i
        sq_r, sq_i = lb_r, lb_i
        for bit in range(int(math.log2(t_len))):
            sel = ((tau >> bit) & 1) == 1
            p_r, p_i = cmul(p_r, p_i, jnp.where(sel, sq_r, 1.0), jnp.where(sel, sq_i, 0.0))
            selq = (((t_len - 1 - tau) >> bit) & 1) == 1
            q_r, q_i = cmul(q_r, q_i, jnp.where(selq, sq_r, 1.0), jnp.where(selq, sq_i, 0.0))
            sq_r, sq_i = cmul(sq_r, sq_i, sq_r, sq_i)
        lc_r, lc_i = sq_r, sq_i
        if d == 0:
            toep_p, in_p = (p_r, p_i), (q_r, q_i)
            out_p = cmul(p_r, p_i, lb_r, lb_i)
        else:
            toep_p, in_p = (q_r, q_i), (p_r, p_i)
            out_p = cmul(q_r, q_i, lb_r, lb_i)

        tp_r, tp_i = toep_p
        expand(c_r, jnp.where(lo, tp_r, tp_i), sgn * c_i, jnp.where(lo, tp_i, tp_r), cl_ref)
        bbs = jnp.where(lo, bb_r, -bb_i)
        kt = lax.dot_general(bbs, cl_ref[...], (((1,), (1,)), ((), ())),
                             precision=lax.Precision.HIGHEST, preferred_element_type=F32)
        width = t_len * hc
        glane = lax.broadcasted_iota(jnp.int32, (hc, width), 1)
        per_tile = LANES // hc
        for m in range(per_tile):
            if d == 0:
                base = kt if m == 0 else jnp.where(glane >= hc * m, pltpu.roll(kt, hc * m, 1), 0.0)
            else:
                base = kt if m == 0 else jnp.where(glane < width - hc * m,
                                                   pltpu.roll(kt, width - hc * m, 1), 0.0)
            base = base.astype(BF16)
            for a in range(t_len // per_tile):
                off = a * LANES
                if d == 0:
                    j = a * per_tile + m
                    if off:
                        toep_ref[d, j * hc:(j + 1) * hc, :off] = jnp.zeros((hc, off), BF16)
                    toep_ref[d, j * hc:(j + 1) * hc, off:] = base[:, :width - off]
                else:
                    j = t_len - 1 - (a * per_tile + m)
                    if off:
                        toep_ref[d, j * hc:(j + 1) * hc, width - off:] = jnp.zeros((hc, off), BF16)
                    toep_ref[d, j * hc:(j + 1) * hc, :width - off] = base[:, off:]

        ip_r, ip_i = in_p
        expand(bb_r, ip_r, -bb_i, ip_i, win_r_ref)
        expand(bb_i, ip_r, bb_r, ip_i, win_i_ref)
        op_r, op_i = out_p
        expand(c_r, jnp.where(lo, op_r, -op_i), c_i, jnp.where(lo, -op_i, -op_r), wout_ref)

        s_r_ref[...] = _dot(u, win_r_ref[...])
        s_i_ref[...] = _dot(u, win_i_ref[...])
        if d == 0:
            order = list(range(n_chunks))
        else:
            order = list(range(n_chunks_ctx - 1, -1, -1)) + list(range(n_chunks - 1, n_chunks_ctx - 1, -1))
        x_r, x_i = jnp.zeros((bsz, LANES), F32), jnp.zeros((bsz, LANES), F32)
        for n in order:
            rs = slice(n * bsz, (n + 1) * bsz)
            xp_ref[rs, :] = jnp.where(lo, x_r, x_i)
            nx_r, nx_i = cmul(lc_r, lc_i, x_r, x_i)
            x_r, x_i = nx_r + s_r_ref[rs, :], nx_i + s_i_ref[rs, :]

        y_acc = y_acc + _dot(u, toep_ref[d]) + _dot_nt(xp_ref[...].astype(BF16), wout_ref[...])
    y_ref[0] = y_acc


def s5_core(ug, lam_pk, bt_pk, cm_pk, bsz, n_chunks_ctx, n_chunks):
    groups, rows, width = ug.shape
    kern = functools.partial(_s5_kernel, bsz=bsz, n_chunks_ctx=n_chunks_ctx, n_chunks=n_chunks)
    return pl.pallas_call(
        kern,
        grid=(groups,),
        in_specs=[
            pl.BlockSpec((1, rows, width), lambda g: (g, 0, 0)),
            pl.BlockSpec((2, 1, SUBLANES, LANES), lambda g: (0, g, 0, 0)),
            pl.BlockSpec((2, 1, 2, C_GROUP, LANES), lambda g: (0, g, 0, 0, 0)),
            pl.BlockSpec((2, 1, 2, C_GROUP, LANES), lambda g: (0, g, 0, 0, 0)),
        ],
        out_specs=pl.BlockSpec((1, rows, width), lambda g: (g, 0, 0)),
        out_shape=jax.ShapeDtypeStruct((groups, rows, width), F32),
        scratch_shapes=[
            pltpu.VMEM((2, width, width), BF16),
            pltpu.VMEM((width, LANES), BF16),
            pltpu.VMEM((width, LANES), BF16),
            pltpu.VMEM((width, LANES), BF16),
            pltpu.VMEM((width, LANES), F32),
            pltpu.VMEM((rows, LANES), F32),
            pltpu.VMEM((rows, LANES), F32),
            pltpu.VMEM((rows, LANES), F32),
        ],
        compiler_params=_cparams(("parallel",)),
        name="s5_core",
    )(ug, lam_pk, bt_pk, cm_pk)


def _s5_post_kernel(y_ref, u_ref, gate_ref, d_ref, w_ref, b_ref, o_ref):
    y = y_ref[0] + d_ref[...] * u_ref[0]
    z = jax.nn.gelu(y)
    t = _dot(z.astype(BF16), w_ref[...]) + b_ref[...]
    o_ref[0] = (z * _sigmoid(t) * _silu(gate_ref[0])).astype(BF16)


def s5_post(y, proj, u_col, gate_col, d_skip, w_glu, b_glu):
    bsz, l, width = y.shape
    tr = ROW_TILE
    return pl.pallas_call(
        _s5_post_kernel,
        grid=(bsz, l // tr),
        in_specs=[
            pl.BlockSpec((1, tr, width), lambda b, i: (b, i, 0)),
            pl.BlockSpec((1, tr, width), lambda b, i: (b, i, u_col // width)),
            pl.BlockSpec((1, tr, width), lambda b, i: (b, i, gate_col // width)),
            pl.BlockSpec((1, width), lambda b, i: (0, 0)),
            pl.BlockSpec((width, width), lambda b, i: (0, 0)),
            pl.BlockSpec((1, width), lambda b, i: (0, 0)),
        ],
        out_specs=pl.BlockSpec((1, tr, width), lambda b, i: (b, i, 0)),
        out_shape=jax.ShapeDtypeStruct((bsz, l, width), BF16),
        compiler_params=_cparams(("parallel", "parallel")),
        name="s5_post",
    )(y, proj, proj, d_skip.reshape(1, width), w_glu.astype(BF16), b_glu.reshape(1, width))


def _outproj_kernel(oa_ref, ob_ref, oc_ref, od_ref, w_ref, h_ref, gl_ref, gc_ref, o_ref, *,
                    n_ctx, tm, bw):
    acc = _dot(oa_ref[0], w_ref[0, 0:bw, :])
    acc = acc + _dot(ob_ref[0], w_ref[0, bw:2 * bw, :])
    acc = acc + _dot(oc_ref[0], w_ref[0, 2 * bw:3 * bw, :])
    acc = acc + _dot(od_ref[0], w_ref[0, 3 * bw:4 * bw, :])
    row = pl.program_id(1) * tm + lax.broadcasted_iota(jnp.int32, (tm, 1), 0)
    gate = jnp.where(row < n_ctx, gc_ref[0], gl_ref[0])
    o_ref[0] = h_ref[0] + gate * acc


def out_projection(o_parts, w_out, h, mods_flat, layer, n_ctx):
    bsz, l, d = h.shape
    bw = o_parts[0].shape[-1]
    tm = l if l <= 2048 else _pick_tile(l, (l // 4, l // 8, l // 16))
    tn = _pick_tile(d, (512, 256, 128))
    kern = functools.partial(_outproj_kernel, n_ctx=n_ctx, tm=tm, bw=bw)
    o_spec = pl.BlockSpec((1, tm, bw), lambda b, i, j: (b, i, 0))
    return pl.pallas_call(
        kern,
        grid=(bsz, l // tm, d // tn),
        in_specs=[
            o_spec, o_spec, o_spec, o_spec,
            pl.BlockSpec((1, 4 * bw, tn), lambda b, i, j: (layer, 0, j)),
            pl.BlockSpec((1, tm, tn), lambda b, i, j: (b, i, j)),
            pl.BlockSpec((1, 1, tn), lambda b, i, j: ((layer * SUBLANES + b) * 3 + 2, 0, j)),
            pl.BlockSpec((1, 1, tn), lambda b, i, j: ((layer * SUBLANES + bsz) * 3 + 2, 0, j)),
        ],
        out_specs=pl.BlockSpec((1, tm, tn), lambda b, i, j: (b, i, j)),
        out_shape=jax.ShapeDtypeStruct((bsz, l, d), F32),
        compiler_params=_cparams(("parallel", "parallel", "parallel")),
        name="out_projection",
    )(*o_parts, w_out, h, mods_flat, mods_flat)


def _final_norm_kernel(x_ref, g_ref, o_ref):
    x = x_ref[0]
    o_ref[0] = x * lax.rsqrt(jnp.mean(x * x, axis=-1, keepdims=True) + EPS) * g_ref[...]


def final_norm(h, g, n_ctx):
    bsz, l, d = h.shape
    tr = ROW_TILE
    skip = n_ctx // tr
    return pl.pallas_call(
        _final_norm_kernel,
        grid=(bsz, (l - n_ctx) // tr),
        in_specs=[
            pl.BlockSpec((1, tr, d), lambda b, i: (b, i + skip, 0)),
            pl.BlockSpec((1, d), lambda b, i: (0, 0)),
        ],
        out_specs=pl.BlockSpec((1, tr, d), lambda b, i: (b, i, 0)),
        out_shape=jax.ShapeDtypeStruct((bsz, l - n_ctx, d), F32),
        compiler_params=_cparams(("parallel", "parallel")),
        name="final_norm",
    )(h, g.reshape(1, d))


def _rope_tables(rows, n_ctx, dk):
    quarter = dk // 4
    freqs = ROPE_BASE ** (-jnp.arange(quarter, dtype=F32) / quarter)
    t = jnp.arange(rows * GRID_W)
    r = (t // GRID_W).astype(F32)
    col = (t % GRID_W).astype(F32)
    ang = jnp.concatenate([r[:, None] * freqs, col[:, None] * freqs], axis=-1)
    ang = jnp.concatenate([jnp.zeros((n_ctx, dk // 2), F32), ang], axis=0)
    cos, sin = jnp.cos(ang), jnp.sin(ang)
    return jnp.concatenate([cos, cos], axis=-1), jnp.concatenate([-sin, sin], axis=-1)


def _dup(x):
    return jnp.concatenate([x, x], axis=-1)


def _permute_in_weights(w_in):
    depth, d, _ = w_in.shape
    bw, rank = d // 4, B_GATE_RANK
    lr0 = 5 * bw + 2 * (bw // 2) + bw
    w = w_in.astype(BF16)
    return jnp.concatenate(
        [w[..., :lr0], w[..., lr0 + 2 * rank:], w[..., lr0:lr0 + 2 * rank],
         jnp.zeros((depth, d, LANES - 2 * rank), BF16)], axis=-1)


def mixer_layer(h, mods_flat, layer, n_ctx, norm_g, w_in_perm, hgrn_lb_logits, hgrn_norm_g, gla_w_gk,
                gla_b_gk, gla_norm_g, s5_lam_re, s5_lam_im, s5_log_dt, s5_b_re, s5_b_im, s5_c_re,
                s5_c_im, s5_d, s5_w_glu, s5_b_glu, ret_decay_logit, ret_norm_g, w_out_bf, rope):
    bsz, l, d = h.shape
    bw = d // 4
    kw = bw // 2
    rank = B_GATE_RANK

    names = ("a_q", "a_ff", "a_fb", "a_i", "a_g", "b_q", "b_k", "b_v", "b_g", "c_u", "c_g",
             "d_q", "d_k", "d_v", "d_g", "b_lr")
    widths = (bw, bw, bw, bw, bw, kw, kw, bw, bw, bw, bw, kw, kw, bw, bw, LANES)
    col = dict(zip(names, np.concatenate([[0], np.cumsum(widths)[:-1]]).tolist()))

    xn = prenorm(h, norm_g[layer], mods_flat, layer, n_ctx)
    proj = in_projection(xn.reshape(bsz * l, d), w_in_perm, layer).reshape(bsz, l, -1)

    o_a = hgrn_mixer(proj, (col["a_q"], col["a_ff"], col["a_i"]), hgrn_lb_logits, layer, n_ctx,
                     col["a_g"], hgrn_norm_g[layer])

    wgk = gla_w_gk[layer].astype(BF16)
    wgk_pad = jnp.zeros((2, LANES, kw), BF16)
    wgk_pad = wgk_pad.at[0, :rank].set(wgk[0]).at[1, rank:2 * rank].set(wgk[1])
    o_b = gla_mixer(proj, (col["b_q"], col["b_k"], col["b_v"], col["b_lr"]), wgk_pad,
                    gla_b_gk[layer], n_ctx, bw, col["b_g"], gla_norm_g[layer])

    groups = bw // C_GROUP
    nck = l // S5_CHUNK
    u = proj[:, :, col["c_u"]:col["c_u"] + bw].astype(BF16)
    ug = u.reshape(bsz, nck, S5_CHUNK, groups, C_GROUP).transpose(3, 1, 0, 2, 4)
    ug = ug.reshape(groups, nck * bsz, S5_CHUNK * C_GROUP)
    dt_row = jnp.broadcast_to(s5_log_dt[layer][..., None], (2, groups, C_STATE))
    lam_pk = jnp.stack([_dup(s5_lam_re[layer]), _dup(s5_lam_im[layer]), _dup(dt_row)], axis=2)
    lam_pk = jnp.pad(lam_pk, ((0, 0), (0, 0), (0, SUBLANES - 3), (0, 0)))
    bt_pk = jnp.stack([_dup(jnp.swapaxes(s5_b_re[layer], -1, -2)),
                       _dup(jnp.swapaxes(s5_b_im[layer], -1, -2))], axis=2)
    cm_pk = jnp.stack([_dup(s5_c_re[layer]), _dup(s5_c_im[layer])], axis=2)
    yg = s5_core(ug, lam_pk, bt_pk, cm_pk, bsz, n_ctx // S5_CHUNK, nck)
    y = yg.reshape(groups, nck, bsz, S5_CHUNK, C_GROUP).transpose(2, 1, 3, 0, 4).reshape(bsz, l, bw)
    o_c = s5_post(y, proj, col["c_u"], col["c_g"], s5_d[layer], s5_w_glu[layer], s5_b_glu[layer])

    dl = jnp.pad(ret_decay_logit[layer], ((0, 0), (0, LANES - D_HEADS))).reshape(2, 1, LANES)
    o_d = retention_mixer(proj, (col["d_q"], col["d_k"], col["d_v"]), rope[0], rope[1], dl, n_ctx, bw,
                          col["d_g"], ret_norm_g[layer])

    return out_projection((o_a, o_b, o_c, o_d), w_out_bf, h, mods_flat, layer, n_ctx)


def kernel(x, c, ctx, c_ctx, norm_g, w_ada, b_ada, w_in, hgrn_lb_logits, hgrn_norm_g, gla_w_gk,
           gla_b_gk, gla_norm_g, s5_lam_re, s5_lam_im, s5_log_dt, s5_b_re, s5_b_im, s5_c_re, s5_c_im,
           s5_d, s5_w_glu, s5_b_glu, ret_decay_logit, ret_norm_g, w_out, final_norm_g):
    bsz, seq, d = x.shape
    n_ctx = ctx.shape[1]
    depth = w_in.shape[0]
    assert bsz < SUBLANES and n_ctx % ROW_TILE == 0 and seq % ROW_TILE == 0

    cvec = jnp.concatenate([c, c_ctx[None], jnp.zeros((SUBLANES - bsz - 1, d), F32)], axis=0)
    mods = ada_modulation(cvec, w_ada, b_ada)
    mods_flat = mods.reshape(depth * SUBLANES * 3, 1, d)
    rope = _rope_tables(seq // GRID_W, n_ctx, (d // 8) // D_HEADS)

    w_in_perm = _permute_in_weights(w_in)
    w_out_bf = w_out.astype(BF16)
    h = jnp.concatenate([ctx, x], axis=1)
    for layer in range(depth):
        h = mixer_layer(h, mods_flat, layer, n_ctx, norm_g, w_in_perm, hgrn_lb_logits, hgrn_norm_g,
                        gla_w_gk, gla_b_gk, gla_norm_g, s5_lam_re, s5_lam_im, s5_log_dt, s5_b_re,
                        s5_b_im, s5_c_re, s5_c_im, s5_d, s5_w_glu, s5_b_glu, ret_decay_logit,
                        ret_norm_g, w_out_bf, rope)
    return final_norm(h, final_norm_g, n_ctx)
```

```python
import functools
import math

import numpy as np
import jax
import jax.numpy as jnp
from jax import lax
from jax.experimental import pallas as pl
from jax.experimental.pallas import tpu as pltpu

F32 = jnp.float32
BF16 = jnp.bfloat16

EPS = 1e-6
A_HEAD_DIM = 128
A_MIN_FORGET = 1e-6
B_HEADS = 4
B_GATE_RANK = 16
B_GATE_NORM = 16.0
C_GROUP = 16
C_STATE = 64
C_MAX_RE = -1e-4
D_HEADS = 4
GRID_W = 64
ROPE_BASE = 10000.0

LANES = 128
SUBLANES = 8
BF16_ROWS = 16
VMEM_LIMIT = 56 * 1024 * 1024

CHUNK = 64
SHORT_SPAN = 60.0
S5_CHUNK = 64
ROW_TILE = 256


def _cparams(sem):
    return pltpu.CompilerParams(dimension_semantics=sem, vmem_limit_bytes=VMEM_LIMIT)


def _dot(a, b):
    return jnp.dot(a, b, preferred_element_type=F32)


def _dot_nt(a, b):
    return lax.dot_general(a, b, (((1,), (1,)), ((), ())), preferred_element_type=F32)


def _dot_tn(a, b):
    return lax.dot_general(a, b, (((0,), (0,)), ((), ())), preferred_element_type=F32)


def _sigmoid(x):
    return 1.0 / (1.0 + jnp.exp(-x))


def _silu(x):
    return x * _sigmoid(x)


def _log_sigmoid(x):
    return jnp.minimum(x, 0.0) - jnp.log(1.0 + jnp.exp(-jnp.abs(x)))


def _ada_kernel(c_ref, w_ref, b_ref, o_ref):
    cv = _silu(c_ref[...]).astype(BF16)
    o_ref[0] = _dot(cv, w_ref[0].astype(BF16)) + b_ref[0]


def ada_modulation(cvec, w_ada, b_ada):
    depth, d, n3 = w_ada.shape
    tn = 512
    return pl.pallas_call(
        _ada_kernel,
        grid=(depth, n3 // tn),
        in_specs=[
            pl.BlockSpec((SUBLANES, d), lambda l, j: (0, 0)),
            pl.BlockSpec((1, d, tn), lambda l, j: (l, 0, j)),
            pl.BlockSpec((1, 1, tn), lambda l, j: (l, 0, j)),
        ],
        out_specs=pl.BlockSpec((1, SUBLANES, tn), lambda l, j: (l, 0, j)),
        out_shape=jax.ShapeDtypeStruct((depth, SUBLANES, n3), F32),
        compiler_params=_cparams(("parallel", "parallel")),
        name="ada_modulation",
    )(cvec, w_ada, b_ada.reshape(depth, 1, n3))


def _prenorm_kernel(x_ref, g_ref, sh_ref, sc_ref, o_ref):
    x = x_ref[0]
    y = x * lax.rsqrt(jnp.mean(x * x, axis=-1, keepdims=True) + EPS) * g_ref[...]
    o_ref[0] = (y * (1.0 + sc_ref[0]) + sh_ref[0]).astype(BF16)


def prenorm(h, g, mods_flat, layer, n_ctx):
    bsz, l, d = h.shape
    tr = ROW_TILE
    nct = n_ctx // tr

    def mod_map(part):
        def f(b, i):
            row = jnp.where(i < nct, bsz, b)
            return ((layer * SUBLANES + row) * 3 + part, 0, 0)
        return f

    return pl.pallas_call(
        _prenorm_kernel,
        grid=(bsz, l // tr),
        in_specs=[
            pl.BlockSpec((1, tr, d), lambda b, i: (b, i, 0)),
            pl.BlockSpec((1, d), lambda b, i: (0, 0)),
            pl.BlockSpec((1, 1, d), mod_map(0)),
            pl.BlockSpec((1, 1, d), mod_map(1)),
        ],
        out_specs=pl.BlockSpec((1, tr, d), lambda b, i: (b, i, 0)),
        out_shape=jax.ShapeDtypeStruct((bsz, l, d), BF16),
        compiler_params=_cparams(("parallel", "parallel")),
        name="prenorm",
    )(h, g.reshape(1, d), mods_flat, mods_flat)


def _prenorm_first_kernel(c_ref, x_ref, g_ref, sh_ref, sc_ref, o_ref, h_ref, *, nct):
    x = jnp.where(pl.program_id(1) < nct, c_ref[0], x_ref[0])
    h_ref[0] = x
    y = x * lax.rsqrt(jnp.mean(x * x, axis=-1, keepdims=True) + EPS) * g_ref[...]
    o_ref[0] = (y * (1.0 + sc_ref[0]) + sh_ref[0]).astype(BF16)


def prenorm_first(ctx, x, g, mods_flat, n_ctx):
    bsz, seq, d = x.shape
    l = n_ctx + seq
    tr = ROW_TILE
    nct = n_ctx // tr

    def mod_map(part):
        def f(b, i):
            row = jnp.where(i < nct, bsz, b)
            return (row * 3 + part, 0, 0)
        return f

    row_spec = pl.BlockSpec((1, tr, d), lambda b, i: (b, i, 0))
    return pl.pallas_call(
        functools.partial(_prenorm_first_kernel, nct=nct),
        grid=(bsz, l // tr),
        in_specs=[
            pl.BlockSpec((1, tr, d), lambda b, i: (b, jnp.minimum(i, nct - 1), 0)),
            pl.BlockSpec((1, tr, d), lambda b, i: (b, jnp.maximum(i - nct, 0), 0)),
            pl.BlockSpec((1, d), lambda b, i: (0, 0)),
            pl.BlockSpec((1, 1, d), mod_map(0)),
            pl.BlockSpec((1, 1, d), mod_map(1)),
        ],
        out_specs=[row_spec, row_spec],
        out_shape=[jax.ShapeDtypeStruct((bsz, l, d), BF16), jax.ShapeDtypeStruct((bsz, l, d), F32)],
        compiler_params=_cparams(("parallel", "arbitrary")),
        name="prenorm_first",
    )(ctx, x, g.reshape(1, d), mods_flat, mods_flat)


def _wprep_kernel(w_ref, o_ref, *, lr0, lr_w):
    n = w_ref.shape[2]
    x = w_ref[0]
    o_ref[0, :, :lr0] = x[:, :lr0].astype(BF16)
    o_ref[0, :, lr0:n - lr_w] = x[:, lr0 + lr_w:].astype(BF16)
    tail = jnp.concatenate([x[:, lr0:lr0 + lr_w], jnp.zeros((x.shape[0], LANES - lr_w), F32)], axis=1)
    o_ref[0, :, n - lr_w:] = tail.astype(BF16)


def permute_in_weights(w_in):
    depth, d, n = w_in.shape
    bw, lr_w = d // 4, 2 * B_GATE_RANK
    lr0 = 5 * bw + 2 * (bw // 2) + bw
    assert (n - lr_w) % LANES == 0 and lr0 % LANES == 0
    tr = 128
    return pl.pallas_call(
        functools.partial(_wprep_kernel, lr0=lr0, lr_w=lr_w),
        grid=(depth, d // tr),
        in_specs=[pl.BlockSpec((1, tr, n), lambda l, i: (l, i, 0))],
        out_specs=pl.BlockSpec((1, tr, n - lr_w + LANES), lambda l, i: (l, i, 0)),
        out_shape=jax.ShapeDtypeStruct((depth, d, n - lr_w + LANES), BF16),
        compiler_params=_cparams(("parallel", "parallel")),
        name="permute_in_weights",
    )(w_in)


def _matmul_kernel(x_ref, w_ref, o_ref):
    o_ref[...] = _dot(x_ref[...], w_ref[0])


def _pick_tile(n, candidates):
    for c in candidates:
        if n % c == 0:
            return c
    return n


def in_projection(xn, w_all, layer):
    m, d = xn.shape
    n = w_all.shape[2]
    tm = _pick_tile(m, (1024, 512, 256, 128, 64, 32, 16))
    tn = _pick_tile(n, (896, 640, 512, 384, 256, 128))
    return pl.pallas_call(
        _matmul_kernel,
        grid=(m // tm, n // tn),
        in_specs=[
            pl.BlockSpec((tm, d), lambda i, j: (i, 0), pipeline_mode=pl.Buffered(1)),
            pl.BlockSpec((1, d, tn), lambda i, j: (layer, 0, j)),
        ],
        out_specs=pl.BlockSpec((tm, tn), lambda i, j: (i, j)),
        out_shape=jax.ShapeDtypeStruct((m, n), F32),
        compiler_params=_cparams(("parallel", "parallel")),
        name="in_projection",
    )(xn, w_all)


def _decay_constants(c, bwd):
    nlev = int(math.log2(c))
    w = np.zeros((nlev + 2, c, c), np.float32)
    masks = np.zeros((nlev, c, c), np.float32)
    for lev in range(nlev):
        s = 1 << lev
        for r in range(c):
            pos = r % (2 * s)
            mid = r - pos + s
            if pos >= s:
                w[lev, r, mid:r + 1] = 1.0
            else:
                w[lev, r, r + 1:mid] = 1.0
        for i in range(c):
            for j in range(c):
                if i // (2 * s) == j // (2 * s) and i % (2 * s) >= s and j % (2 * s) < s:
                    masks[lev, i, j] = 1.0
    for r in range(c):
        w[nlev, r, :r + 1] = 1.0
        w[nlev + 1, r, r + 1:] = 1.0
    half = c // 2
    ii, jj = np.meshgrid(np.arange(c), np.arange(c), indexing="ij")
    diag = ((ii // half == jj // half) & (jj <= ii)).astype(np.float32)
    masks = np.concatenate([masks, diag[None]], axis=0)
    w = w.reshape((nlev + 2) * c, c)
    w = np.concatenate([w, np.ones((BF16_ROWS, c), np.float32)], axis=0)
    if bwd:
        w = np.concatenate([w[:-BF16_ROWS].reshape(nlev + 2, c, c)[:, ::-1, ::-1].reshape(-1, c),
                            w[-BF16_ROWS:]], axis=0)
        masks = masks[:, ::-1, ::-1]
    return np.ascontiguousarray(w), np.ascontiguousarray(masks), nlev


def _split3(x):
    hi = x.astype(BF16)
    r1 = x - hi.astype(F32)
    mid = r1.astype(BF16)
    lo = (r1 - mid.astype(F32)).astype(BF16)
    return hi, mid, lo


def _gated_core(qs, ks_, vs_, gs, w_ref, mask_ref, o_scr, state_ref, heads, dk, dv, nlev, c, bwd):
    half = c // 2
    nb = len(qs)
    w_in = w_ref[nlev * c:(nlev + 1) * c, :]
    first_a, last_a, first_b, last_b = (half - 1, 0, c - 1, half) if bwd else (0, half - 1, half, c - 1)
    top_ref, exit_row = (half, 0) if bwd else (half - 1, c - 1)
    g_parts, b_ins, short = [], [], None
    for g in gs:
        parts = _split3(g)
        b_in = _dot(w_in, parts[0]) + _dot(w_in, parts[1]) + _dot(w_in, parts[2])
        span = jnp.minimum(b_in[last_a:last_a + 1] - b_in[first_a:first_a + 1],
                           b_in[last_b:last_b + 1] - b_in[first_b:first_b + 1])
        ok = jnp.min(span) >= -SHORT_SPAN
        short = ok if short is None else jnp.logical_and(short, ok)
        g_parts.append(parts)
        b_ins.append(b_in)

    def head_update(bb, h, scores, e_in, e_out, e_tot, extra):
        ks = slice(h * dk, (h + 1) * dk)
        vs = slice(h * dv, (h + 1) * dv)
        qh, kh, vh = qs[bb][:, ks], ks_[bb][:, ks], vs_[bb][:, vs]
        vb = vh.astype(BF16)
        st = state_ref[bb * heads + h]
        o = _dot_nt((qh * e_in).astype(BF16), st.astype(BF16)) + _dot(scores.astype(BF16), vb)
        if extra is not None:
            o = o + extra * vh
        state_ref[bb * heads + h] = st * e_tot + _dot_tn(vb, (kh * e_out).astype(BF16))
        o_scr[bb, :, vs] = o

    @pl.when(short)
    def _():
        row = lax.broadcasted_iota(jnp.int32, (c, 1), 0)
        in_a = row < half
        later = in_a if bwd else jnp.logical_not(in_a)
        for bb in range(nb):
            b_in = b_ins[bb]
            m = jnp.where(in_a, b_in[first_a:first_a + 1], b_in[first_b:first_b + 1])
            fq = jnp.exp(b_in - m)
            fk = jnp.exp(m - b_in)
            r1 = b_in[top_ref:top_ref + 1]
            ft = jnp.exp(jnp.where(later, b_in - r1, r1 - b_in))
            tot = b_in[exit_row:exit_row + 1]
            e_in_all = jnp.exp(b_in)
            e_out_all = jnp.exp(tot - b_in)
            e_tot_all = jnp.exp(tot)
            for h in range(heads):
                ks = slice(h * dk, (h + 1) * dk)
                qh, kh = qs[bb][:, ks], ks_[bb][:, ks]
                scores = (mask_ref[nlev] * _dot_nt((qh * fq[:, ks]).astype(BF16),
                                                   (kh * fk[:, ks]).astype(BF16))
                          + mask_ref[nlev - 1] * _dot_nt((qh * ft[:, ks]).astype(BF16),
                                                         (kh * ft[:, ks]).astype(BF16)))
                head_update(bb, h, scores, e_in_all[:, ks], e_out_all[:, ks], e_tot_all[:, ks], None)

    @pl.when(jnp.logical_not(short))
    def _():
        w = w_ref[...]
        for bb in range(nb):
            parts = g_parts[bb]
            e_all = jnp.exp(_dot(w, parts[0]) + _dot(w, parts[1]) + _dot(w, parts[2]))
            for h in range(heads):
                ks = slice(h * dk, (h + 1) * dk)
                qh, kh = qs[bb][:, ks], ks_[bb][:, ks]
                scores = jnp.zeros((c, c), F32)
                for lev in range(nlev):
                    f = e_all[lev * c:(lev + 1) * c, ks]
                    scores = scores + mask_ref[lev] * _dot_nt((qh * f).astype(BF16), (kh * f).astype(BF16))
                head_update(bb, h, scores, e_all[nlev * c:(nlev + 1) * c, ks],
                            e_all[(nlev + 1) * c:(nlev + 2) * c, ks],
                            e_all[(nlev + 2) * c:(nlev + 2) * c + 1, ks],
                            jnp.sum(qh * kh, axis=-1, keepdims=True))


def _finish(o_scr, of_ref, gate_ref, ng_ref, o_ref, heads, hd, center, final):
    if not final:
        return
    for bb in range(o_scr.shape[0]):
        gs = _silu(gate_ref[bb])
        for h in range(heads):
            sl = slice(h * hd, (h + 1) * hd)
            x = o_scr[bb, :, sl] + of_ref[bb, :, sl]
            if center:
                x = x - jnp.mean(x, axis=-1, keepdims=True)
            y = x * lax.rsqrt(jnp.mean(x * x, axis=-1, keepdims=True) + EPS) * ng_ref[:, sl]
            o_ref[bb, :, sl] = (y * gs[:, sl]).astype(BF16)


def _split_refs(refs, n_in, bwd):
    ins = refs[:n_in]
    if bwd:
        return ins, refs[n_in:n_in + 3], refs[n_in + 3:]
    o_ref, state_ref = refs[n_in:]
    return ins, (None, None, None), (o_ref, state_ref, o_ref)


def _hgrn_kernel(*refs, layer, heads, dk, nlev, c, bwd):
    (q_ref, z_ref, v_ref, lbl_ref, w_ref, mask_ref), fin, (o_ref, state_ref, o_scr) = \
        _split_refs(refs, 6, bwd)

    @pl.when(pl.program_id(1) == 0)
    def _():
        state_ref[...] = jnp.zeros_like(state_ref)

    logits = lbl_ref[...]
    ex = jnp.exp(logits - jnp.max(logits, axis=0, keepdims=True))
    p = ex / jnp.sum(ex, axis=0, keepdims=True)
    lb = jnp.sum(p[:layer + 1], axis=0, keepdims=True) - p[0:1]
    nb = q_ref.shape[0]
    ks_, gs = [], []
    for bb in range(nb):
        z = z_ref[bb]
        e = jnp.exp(-jnp.abs(z))
        s_big = 1.0 / (1.0 + e)
        s_small = e * s_big
        sig_pos = jnp.where(z >= 0, s_big, s_small)
        sig_neg = jnp.where(z >= 0, s_small, s_big)
        gs.append(jnp.log(jnp.maximum(lb + (1.0 - lb) * sig_pos, A_MIN_FORGET)))
        ks_.append((1.0 - lb) * sig_neg)
    _gated_core([q_ref[bb] for bb in range(nb)], ks_, [v_ref[bb] for bb in range(nb)], gs,
                w_ref, mask_ref, o_scr, state_ref, heads, dk, dk, nlev, c, bwd)
    _finish(o_scr, *fin, o_ref, heads, dk, False, bwd)


def _gla_kernel(*refs, heads, dk, dv, nlev, c, bwd):
    (q_ref, k_ref, v_ref, lr_ref, wgk_ref, bgk_ref, w_ref, mask_ref), fin, (o_ref, state_ref, o_scr) = \
        _split_refs(refs, 8, bwd)

    @pl.when(pl.program_id(1) == 0)
    def _():
        state_ref[...] = jnp.zeros_like(state_ref)

    nb = q_ref.shape[0]
    qs, gs = [], []
    for bb in range(nb):
        logit = _dot(lr_ref[bb].astype(BF16), wgk_ref[...]) + bgk_ref[...]
        gs.append(_log_sigmoid(logit) / B_GATE_NORM)
        qs.append(q_ref[bb] * (dk ** -0.5))
    _gated_core(qs, [k_ref[bb] for bb in range(nb)], [v_ref[bb] for bb in range(nb)], gs,
                w_ref, mask_ref, o_scr, state_ref, heads, dk, dv, nlev, c, bwd)
    _finish(o_scr, *fin, o_ref, heads, dv, False, bwd)


def _chunk_order(n_chunks_ctx, n_chunks, bwd):
    def chunk(n):
        if not bwd:
            return n
        return jnp.where(n < n_chunks_ctx, n_chunks_ctx - 1 - n, n_chunks - 1 - n + n_chunks_ctx)
    return chunk


def _mixer_call(kern, name, proj, in_arrays, in_specs, chunk, nb, c, width, state_shape,
                o_fwd, gate_col, norm_row):
    bsz, l, _ = proj.shape
    bwd = o_fwd is not None
    blk = pl.BlockSpec((nb, c, width), lambda b, n: (b, chunk(n), 0))
    if bwd:
        in_arrays = in_arrays + [o_fwd, proj, norm_row]
        in_specs = in_specs + [
            blk,
            pl.BlockSpec((nb, c, width), lambda b, n: (b, chunk(n), gate_col // width)),
            pl.BlockSpec((1, width), lambda b, n: (0, 0)),
        ]
    return pl.pallas_call(
        kern,
        grid=(bsz // nb, l // c),
        in_specs=in_specs,
        out_specs=blk,
        out_shape=jax.ShapeDtypeStruct((bsz, l, width), BF16 if bwd else F32),
        scratch_shapes=[pltpu.VMEM((nb * state_shape[0],) + state_shape[1:], F32)]
        + ([pltpu.VMEM((nb, c, width), F32)] if bwd else []),
        compiler_params=_cparams(("parallel", "arbitrary")),
        name=name + ("_bwd" if bwd else "_fwd"),
    )(*in_arrays)


def _batch_per_step(bsz):
    return 2 if bsz % 2 == 0 else 1


def hgrn_mixer(proj, col, lb_logits, layer, n_ctx, gate_col, norm_g):
    bsz, l, _ = proj.shape
    depth, _, width = lb_logits.shape
    c = CHUNK
    heads, dk = width // A_HEAD_DIM, A_HEAD_DIM
    cq, cf, ci = (x // width for x in col)
    norm_row = jnp.tile(norm_g, heads).reshape(1, width)
    nb = _batch_per_step(bsz)
    o_fwd = None
    for bwd in (False, True):
        w_np, m_np, nlev = _decay_constants(c, bwd)
        chunk = _chunk_order(n_ctx // c, l // c, bwd)
        d = int(bwd)
        kern = functools.partial(_hgrn_kernel, layer=layer, heads=heads, dk=dk, nlev=nlev, c=c, bwd=bwd)
        in_specs = [
            pl.BlockSpec((nb, c, width), lambda b, n, chunk=chunk: (b, chunk(n), cq)),
            pl.BlockSpec((nb, c, width), lambda b, n, chunk=chunk, d=d: (b, chunk(n), cf + d)),
            pl.BlockSpec((nb, c, width), lambda b, n, chunk=chunk: (b, chunk(n), ci)),
            pl.BlockSpec((depth, width), lambda b, n: (0, 0)),
            pl.BlockSpec(w_np.shape, lambda b, n: (0, 0)),
            pl.BlockSpec(m_np.shape, lambda b, n: (0, 0, 0)),
        ]
        in_arrays = [proj, proj, proj, lb_logits[:, d], jnp.asarray(w_np, BF16), jnp.asarray(m_np)]
        o_fwd = _mixer_call(kern, "hgrn_mixer", proj, in_arrays, in_specs, chunk, nb, c, width,
                            (heads, dk, dk), o_fwd, gate_col, norm_row)
    return o_fwd


def gla_mixer(proj, col, wgk_pad, b_gk, n_ctx, width, gate_col, norm_g):
    bsz, l, _ = proj.shape
    c = CHUNK
    key_w = wgk_pad.shape[-1]
    heads = B_HEADS
    dk, dv = key_w // heads, width // heads
    cq, ck, cv, clr = col
    norm_row = jnp.tile(norm_g, heads).reshape(1, width)
    nb = _batch_per_step(bsz)
    o_fwd = None
    for bwd in (False, True):
        w_np, m_np, nlev = _decay_constants(c, bwd)
        chunk = _chunk_order(n_ctx // c, l // c, bwd)
        d = int(bwd)
        kern = functools.partial(_gla_kernel, heads=heads, dk=dk, dv=dv, nlev=nlev, c=c, bwd=bwd)
        in_specs = [
            pl.BlockSpec((nb, c, key_w), lambda b, n, chunk=chunk: (b, chunk(n), cq // key_w)),
            pl.BlockSpec((nb, c, key_w), lambda b, n, chunk=chunk: (b, chunk(n), ck // key_w)),
            pl.BlockSpec((nb, c, width), lambda b, n, chunk=chunk: (b, chunk(n), cv // width)),
            pl.BlockSpec((nb, c, LANES), lambda b, n, chunk=chunk: (b, chunk(n), clr // LANES)),
            pl.BlockSpec((LANES, key_w), lambda b, n: (0, 0)),
            pl.BlockSpec((1, key_w), lambda b, n: (0, 0)),
            pl.BlockSpec(w_np.shape, lambda b, n: (0, 0)),
            pl.BlockSpec(m_np.shape, lambda b, n: (0, 0, 0)),
        ]
        in_arrays = [proj, proj, proj, proj, wgk_pad[d], b_gk[d].reshape(1, key_w),
                     jnp.asarray(w_np, BF16), jnp.asarray(m_np)]
        o_fwd = _mixer_call(kern, "gla_mixer", proj, in_arrays, in_specs, chunk, nb, c, width,
                            (heads, dv, dk), o_fwd, gate_col, norm_row)
    return o_fwd


def _retention_kernel(*refs, heads, dk, dv, c, bwd):
    (q_ref, k_ref, v_ref, cos_ref, sin_ref, dl_ref), fin, (o_ref, state_ref, o_scr) = \
        _split_refs(refs, 6, bwd)

    @pl.when(pl.program_id(1) == 0)
    def _():
        state_ref[...] = jnp.zeros_like(state_ref)

    ii = lax.broadcasted_iota(jnp.int32, (c, c), 0).astype(F32)
    jj = lax.broadcasted_iota(jnp.int32, (c, c), 1).astype(F32)
    rel = (jj - ii) if bwd else (ii - jj)
    t_col = lax.broadcasted_iota(jnp.int32, (c, 1), 0).astype(F32)
    since = ((c - 1.0) - t_col) if bwd else t_col
    log_gamma = _log_sigmoid(dl_ref[...])
    cos, sin = cos_ref[...], sin_ref[...]
    half = dk // 2
    for h in range(heads):
        ks = slice(h * dk, (h + 1) * dk)
        vs = slice(h * dv, (h + 1) * dv)
        lg = log_gamma[:, h:h + 1]
        qh, kh = q_ref[0, :, ks], k_ref[0, :, ks]
        qh = (qh * cos + pltpu.roll(qh, half, 1) * sin) * (dk ** -0.5)
        kh = kh * cos + pltpu.roll(kh, half, 1) * sin
        vb = v_ref[0, :, vs].astype(BF16)
        st = state_ref[h]
        dmat = jnp.where(rel >= 0, jnp.exp(lg * jnp.maximum(rel, 0.0)), 0.0)
        scores = _dot_nt(qh.astype(BF16), kh.astype(BF16)) * dmat
        xi = jnp.exp(lg * (since + 1.0))
        zeta = jnp.exp(lg * ((c - 1.0) - since))
        o = _dot(scores.astype(BF16), vb) + _dot_nt(qh.astype(BF16), st.astype(BF16)) * xi
        state_ref[h] = st * jnp.exp(lg * c) + _dot_tn(vb, (kh * zeta).astype(BF16))
        o_scr[0, :, vs] = o
    _finish(o_scr, *fin, o_ref, heads, dv, True, bwd)


def retention_mixer(proj, col, cos_t, sin_t, decay_logit_pad, n_ctx, width, gate_col, norm_g):
    bsz, l, _ = proj.shape
    heads = D_HEADS
    key_w = width // 2
    dk, dv = key_w // heads, width // heads
    c = _pick_tile(math.gcd(n_ctx, l - n_ctx), (256, 128, 64))
    cq, ck, cv = col
    norm_row = jnp.tile(norm_g, heads).reshape(1, width)
    o_fwd = None
    for bwd in (False, True):
        chunk = _chunk_order(n_ctx // c, l // c, bwd)
        kern = functools.partial(_retention_kernel, heads=heads, dk=dk, dv=dv, c=c, bwd=bwd)
        in_specs = [
            pl.BlockSpec((1, c, key_w), lambda b, n, chunk=chunk: (b, chunk(n), cq // key_w)),
            pl.BlockSpec((1, c, key_w), lambda b, n, chunk=chunk: (b, chunk(n), ck // key_w)),
            pl.BlockSpec((1, c, width), lambda b, n, chunk=chunk: (b, chunk(n), cv // width)),
            pl.BlockSpec((c, dk), lambda b, n, chunk=chunk: (chunk(n), 0)),
            pl.BlockSpec((c, dk), lambda b, n, chunk=chunk: (chunk(n), 0)),
            pl.BlockSpec((1, LANES), lambda b, n: (0, 0)),
        ]
        in_arrays = [proj, proj, proj, cos_t, sin_t, decay_logit_pad[int(bwd)]]
        o_fwd = _mixer_call(kern, "retention_mixer", proj, in_arrays, in_specs, chunk, 1, c, width,
                            (heads, dv, dk), o_fwd, gate_col, norm_row)
    return o_fwd


def _s5_kernel(u_ref, lam_ref, bt_ref, cm_ref, y_ref,
               toep_ref, win_r_ref, win_i_ref, wout_ref, cl_ref, s_r_ref, s_i_ref, xp_ref, *,
               bsz, n_chunks_ctx, n_chunks):
    t_len, hc, half = S5_CHUNK, C_GROUP, C_STATE
    lane = lax.broadcasted_iota(jnp.int32, (1, LANES), 1)
    lo = lane < half
    sgn = jnp.where(lo, -1.0, 1.0)
    tau = lax.broadcasted_iota(jnp.int32, (t_len, 1), 0)
    u = u_ref[0]
    rows = u.shape[0]
    y_acc = jnp.zeros((rows, t_len * hc), F32)

    def cmul(ar, ai, br, bi):
        return ar * br - ai * bi, ar * bi + ai * br

    def expand(x1, pa, x2, pb, out_ref):
        for t in range(t_len):
            blk = (x1 * jnp.broadcast_to(pa[t:t + 1], (hc, LANES))
                   + x2 * jnp.broadcast_to(pb[t:t + 1], (hc, LANES)))
            out_ref[t * hc:(t + 1) * hc, :] = blk.astype(out_ref.dtype)

    for d in range(2):
        lam_re = jnp.minimum(lam_ref[d, 0, 0:1], C_MAX_RE)
        lam_im = lam_ref[d, 0, 1:2]
        dt = jnp.exp(lam_ref[d, 0, 2:3])
        mag = jnp.exp(lam_re * dt)
        lb_r, lb_i = mag * jnp.cos(lam_im * dt), mag * jnp.sin(lam_im * dt)
        den = lam_re * lam_re + lam_im * lam_im
        nr, ni = lb_r - 1.0, lb_i
        cf_r, cf_i = (nr * lam_re + ni * lam_im) / den, (ni * lam_re - nr * lam_im) / den
        bt_r, bt_i = bt_ref[d, 0, 0], bt_ref[d, 0, 1]
        bb_r, bb_i = cmul(cf_r, cf_i, bt_r, bt_i)
        c_r, c_i = cm_ref[d, 0, 0], cm_ref[d, 0, 1]

        p_r, p_i = jnp.ones((t_len, LANES), F32), jnp.zeros((t_len, LANES), F32)
        q_r, q_i = p_r, p_i
        sq_r, sq_i = lb_r, lb_i
        for bit in range(int(math.log2(t_len))):
            sel = ((tau >> bit) & 1) == 1
            p_r, p_i = cmul(p_r, p_i, jnp.where(sel, sq_r, 1.0), jnp.where(sel, sq_i, 0.0))
            selq = (((t_len - 1 - tau) >> bit) & 1) == 1
            q_r, q_i = cmul(q_r, q_i, jnp.where(selq, sq_r, 1.0), jnp.where(selq, sq_i, 0.0))
            sq_r, sq_i = cmul(sq_r, sq_i, sq_r, sq_i)
        lc_r, lc_i = sq_r, sq_i
        if d == 0:
            toep_p, in_p = (p_r, p_i), (q_r, q_i)
            out_p = cmul(p_r, p_i, lb_r, lb_i)
        else:
            toep_p, in_p = (q_r, q_i), (p_r, p_i)
            out_p = cmul(q_r, q_i, lb_r, lb_i)

        tp_r, tp_i = toep_p
        expand(c_r, jnp.where(lo, tp_r, tp_i), sgn * c_i, jnp.where(lo, tp_i, tp_r), cl_ref)
        bbs = jnp.where(lo, bb_r, -bb_i)
        kt = lax.dot_general(bbs, cl_ref[...], (((1,), (1,)), ((), ())),
                             precision=lax.Precision.HIGHEST, preferred_element_type=F32)
        width = t_len * hc
        glane = lax.broadcasted_iota(jnp.int32, (hc, width), 1)
        per_tile = LANES // hc
        for m in range(per_tile):
            if d == 0:
                base = kt if m == 0 else jnp.where(glane >= hc * m, pltpu.roll(kt, hc * m, 1), 0.0)
            else:
                base = kt if m == 0 else jnp.where(glane < width - hc * m,
                                                   pltpu.roll(kt, width - hc * m, 1), 0.0)
            base = base.astype(BF16)
            for a in range(t_len // per_tile):
                off = a * LANES
                if d == 0:
                    j = a * per_tile + m
                    if off:
                        toep_ref[d, j * hc:(j + 1) * hc, :off] = jnp.zeros((hc, off), BF16)
                    toep_ref[d, j * hc:(j + 1) * hc, off:] = base[:, :width - off]
                else:
                    j = t_len - 1 - (a * per_tile + m)
                    if off:
                        toep_ref[d, j * hc:(j + 1) * hc, width - off:] = jnp.zeros((hc, off), BF16)
                    toep_ref[d, j * hc:(j + 1) * hc, :width - off] = base[:, off:]

        ip_r, ip_i = in_p
        expand(bb_r, ip_r, -bb_i, ip_i, win_r_ref)
        expand(bb_i, ip_r, bb_r, ip_i, win_i_ref)
        op_r, op_i = out_p
        expand(c_r, jnp.where(lo, op_r, -op_i), c_i, jnp.where(lo, -op_i, -op_r), wout_ref)

        s_r_ref[...] = _dot(u, win_r_ref[...])
        s_i_ref[...] = _dot(u, win_i_ref[...])
        if d == 0:
            order = list(range(n_chunks))
        else:
            order = list(range(n_chunks_ctx - 1, -1, -1)) + list(range(n_chunks - 1, n_chunks_ctx - 1, -1))
        x_r, x_i = jnp.zeros((bsz, LANES), F32), jnp.zeros((bsz, LANES), F32)
        for n in order:
            rs = slice(n * bsz, (n + 1) * bsz)
            xp_ref[rs, :] = jnp.where(lo, x_r, x_i)
            nx_r, nx_i = cmul(lc_r, lc_i, x_r, x_i)
            x_r, x_i = nx_r + s_r_ref[rs, :], nx_i + s_i_ref[rs, :]

        y_acc = y_acc + _dot(u, toep_ref[d]) + _dot_nt(xp_ref[...].astype(BF16), wout_ref[...])
    y_ref[0] = y_acc


def s5_core(ug, lam_pk, bt_pk, cm_pk, bsz, n_chunks_ctx, n_chunks):
    groups, rows, width = ug.shape
    kern = functools.partial(_s5_kernel, bsz=bsz, n_chunks_ctx=n_chunks_ctx, n_chunks=n_chunks)
    return pl.pallas_call(
        kern,
        grid=(groups,),
        in_specs=[
            pl.BlockSpec((1, rows, width), lambda g: (g, 0, 0)),
            pl.BlockSpec((2, 1, SUBLANES, LANES), lambda g: (0, g, 0, 0)),
            pl.BlockSpec((2, 1, 2, C_GROUP, LANES), lambda g: (0, g, 0, 0, 0)),
            pl.BlockSpec((2, 1, 2, C_GROUP, LANES), lambda g: (0, g, 0, 0, 0)),
        ],
        out_specs=pl.BlockSpec((1, rows, width), lambda g: (g, 0, 0)),
        out_shape=jax.ShapeDtypeStruct((groups, rows, width), F32),
        scratch_shapes=[
            pltpu.VMEM((2, width, width), BF16),
            pltpu.VMEM((width, LANES), BF16),
            pltpu.VMEM((width, LANES), BF16),
            pltpu.VMEM((width, LANES), BF16),
            pltpu.VMEM((width, LANES), F32),
            pltpu.VMEM((rows, LANES), F32),
            pltpu.VMEM((rows, LANES), F32),
            pltpu.VMEM((rows, LANES), F32),
        ],
        compiler_params=_cparams(("parallel",)),
        name="s5_core",
    )(ug, lam_pk, bt_pk, cm_pk)


def _s5_post_kernel(y_ref, u_ref, gate_ref, d_ref, w_ref, b_ref, o_ref):
    y = y_ref[0] + d_ref[...] * u_ref[0]
    z = jax.nn.gelu(y)
    t = _dot(z.astype(BF16), w_ref[...]) + b_ref[...]
    o_ref[0] = (z * _sigmoid(t) * _silu(gate_ref[0])).astype(BF16)


def s5_post(y, proj, u_col, gate_col, d_skip, w_glu, b_glu):
    bsz, l, width = y.shape
    tr = ROW_TILE
    return pl.pallas_call(
        _s5_post_kernel,
        grid=(bsz, l // tr),
        in_specs=[
            pl.BlockSpec((1, tr, width), lambda b, i: (b, i, 0)),
            pl.BlockSpec((1, tr, width), lambda b, i: (b, i, u_col // width)),
            pl.BlockSpec((1, tr, width), lambda b, i: (b, i, gate_col // width)),
            pl.BlockSpec((1, width), lambda b, i: (0, 0)),
            pl.BlockSpec((width, width), lambda b, i: (0, 0)),
            pl.BlockSpec((1, width), lambda b, i: (0, 0)),
        ],
        out_specs=pl.BlockSpec((1, tr, width), lambda b, i: (b, i, 0)),
        out_shape=jax.ShapeDtypeStruct((bsz, l, width), BF16),
        compiler_params=_cparams(("parallel", "parallel")),
        name="s5_post",
    )(y, proj, proj, d_skip.reshape(1, width), w_glu.astype(BF16), b_glu.reshape(1, width))


def _outproj_kernel(oa_ref, ob_ref, oc_ref, od_ref, w_ref, h_ref, gl_ref, gc_ref, o_ref, *,
                    n_ctx, tm, bw):
    acc = _dot(oa_ref[0], w_ref[0, 0:bw, :])
    acc = acc + _dot(ob_ref[0], w_ref[0, bw:2 * bw, :])
    acc = acc + _dot(oc_ref[0], w_ref[0, 2 * bw:3 * bw, :])
    acc = acc + _dot(od_ref[0], w_ref[0, 3 * bw:4 * bw, :])
    row = pl.program_id(1) * tm + lax.broadcasted_iota(jnp.int32, (tm, 1), 0)
    gate = jnp.where(row < n_ctx, gc_ref[0], gl_ref[0])
    o_ref[0] = h_ref[0] + gate * acc


def out_projection(o_parts, w_out, h, mods_flat, layer, n_ctx):
    bsz, l, d = h.shape
    bw = o_parts[0].shape[-1]
    tm = l if l <= 2048 else _pick_tile(l, (l // 4, l // 8, l // 16))
    tn = _pick_tile(d, (1024, 512, 256, 128))
    kern = functools.partial(_outproj_kernel, n_ctx=n_ctx, tm=tm, bw=bw)
    o_spec = pl.BlockSpec((1, tm, bw), lambda b, i, j: (b, i, 0), pipeline_mode=pl.Buffered(1))
    return pl.pallas_call(
        kern,
        grid=(bsz, l // tm, d // tn),
        in_specs=[
            o_spec, o_spec, o_spec, o_spec,
            pl.BlockSpec((1, 4 * bw, tn), lambda b, i, j: (layer, 0, j)),
            pl.BlockSpec((1, tm, tn), lambda b, i, j: (b, i, j)),
            pl.BlockSpec((1, 1, tn), lambda b, i, j: ((layer * SUBLANES + b) * 3 + 2, 0, j)),
            pl.BlockSpec((1, 1, tn), lambda b, i, j: ((layer * SUBLANES + bsz) * 3 + 2, 0, j)),
        ],
        out_specs=pl.BlockSpec((1, tm, tn), lambda b, i, j: (b, i, j)),
        out_shape=jax.ShapeDtypeStruct((bsz, l, d), F32),
        compiler_params=_cparams(("parallel", "parallel", "parallel")),
        name="out_projection",
    )(*o_parts, w_out, h, mods_flat, mods_flat)


def _final_norm_kernel(x_ref, g_ref, o_ref):
    x = x_ref[0]
    o_ref[0] = x * lax.rsqrt(jnp.mean(x * x, axis=-1, keepdims=True) + EPS) * g_ref[...]


def final_norm(h, g, n_ctx):
    bsz, l, d = h.shape
    tr = ROW_TILE
    skip = n_ctx // tr
    return pl.pallas_call(
        _final_norm_kernel,
        grid=(bsz, (l - n_ctx) // tr),
        in_specs=[
            pl.BlockSpec((1, tr, d), lambda b, i: (b, i + skip, 0)),
            pl.BlockSpec((1, d), lambda b, i: (0, 0)),
        ],
        out_specs=pl.BlockSpec((1, tr, d), lambda b, i: (b, i, 0)),
        out_shape=jax.ShapeDtypeStruct((bsz, l - n_ctx, d), F32),
        compiler_params=_cparams(("parallel", "parallel")),
        name="final_norm",
    )(h, g.reshape(1, d))


def _rope_tables(rows, n_ctx, dk):
    quarter = dk // 4
    freqs = ROPE_BASE ** (-jnp.arange(quarter, dtype=F32) / quarter)
    t = jnp.arange(rows * GRID_W)
    r = (t // GRID_W).astype(F32)
    col = (t % GRID_W).astype(F32)
    ang = jnp.concatenate([r[:, None] * freqs, col[:, None] * freqs], axis=-1)
    ang = jnp.concatenate([jnp.zeros((n_ctx, dk // 2), F32), ang], axis=0)
    cos, sin = jnp.cos(ang), jnp.sin(ang)
    return jnp.concatenate([cos, cos], axis=-1), jnp.concatenate([-sin, sin], axis=-1)


def _dup(x):
    return jnp.concatenate([x, x], axis=-1)


def mixer_layer(xn, h, mods_flat, layer, n_ctx, w_in_perm, hgrn_lb_logits, hgrn_norm_g, gla_w_gk,
                gla_b_gk, gla_norm_g, s5_lam_re, s5_lam_im, s5_log_dt, s5_b_re, s5_b_im, s5_c_re,
                s5_c_im, s5_d, s5_w_glu, s5_b_glu, ret_decay_logit, ret_norm_g, w_out_bf, rope):
    bsz, l, d = h.shape
    bw = d // 4
    kw = bw // 2
    rank = B_GATE_RANK

    names = ("a_q", "a_ff", "a_fb", "a_i", "a_g", "b_q", "b_k", "b_v", "b_g", "c_u", "c_g",
             "d_q", "d_k", "d_v", "d_g", "b_lr")
    widths = (bw, bw, bw, bw, bw, kw, kw, bw, bw, bw, bw, kw, kw, bw, bw, LANES)
    col = dict(zip(names, np.concatenate([[0], np.cumsum(widths)[:-1]]).tolist()))

    proj = in_projection(xn.reshape(bsz * l, d), w_in_perm, layer).reshape(bsz, l, -1)

    o_a = hgrn_mixer(proj, (col["a_q"], col["a_ff"], col["a_i"]), hgrn_lb_logits, layer, n_ctx,
                     col["a_g"], hgrn_norm_g[layer])

    wgk = gla_w_gk[layer].astype(BF16)
    wgk_pad = jnp.zeros((2, LANES, kw), BF16)
    wgk_pad = wgk_pad.at[0, :rank].set(wgk[0]).at[1, rank:2 * rank].set(wgk[1])
    o_b = gla_mixer(proj, (col["b_q"], col["b_k"], col["b_v"], col["b_lr"]), wgk_pad,
                    gla_b_gk[layer], n_ctx, bw, col["b_g"], gla_norm_g[layer])

    groups = bw // C_GROUP
    nck = l // S5_CHUNK
    u = proj[:, :, col["c_u"]:col["c_u"] + bw].astype(BF16)
    ug = u.reshape(bsz, nck, S5_CHUNK, groups, C_GROUP).transpose(3, 1, 0, 2, 4)
    ug = ug.reshape(groups, nck * bsz, S5_CHUNK * C_GROUP)
    dt_row = jnp.broadcast_to(s5_log_dt[layer][..., None], (2, groups, C_STATE))
    lam_pk = jnp.stack([_dup(s5_lam_re[layer]), _dup(s5_lam_im[layer]), _dup(dt_row)], axis=2)
    lam_pk = jnp.pad(lam_pk, ((0, 0), (0, 0), (0, SUBLANES - 3), (0, 0)))
    bt_pk = jnp.stack([_dup(jnp.swapaxes(s5_b_re[layer], -1, -2)),
                       _dup(jnp.swapaxes(s5_b_im[layer], -1, -2))], axis=2)
    cm_pk = jnp.stack([_dup(s5_c_re[layer]), _dup(s5_c_im[layer])], axis=2)
    yg = s5_core(ug, lam_pk, bt_pk, cm_pk, bsz, n_ctx // S5_CHUNK, nck)
    y = yg.reshape(groups, nck, bsz, S5_CHUNK, C_GROUP).transpose(2, 1, 3, 0, 4).reshape(bsz, l, bw)
    o_c = s5_post(y, proj, col["c_u"], col["c_g"], s5_d[layer], s5_w_glu[layer], s5_b_glu[layer])

    dl = jnp.pad(ret_decay_logit[layer], ((0, 0), (0, LANES - D_HEADS))).reshape(2, 1, LANES)
    o_d = retention_mixer(proj, (col["d_q"], col["d_k"], col["d_v"]), rope[0], rope[1], dl, n_ctx, bw,
                          col["d_g"], ret_norm_g[layer])

    return out_projection((o_a, o_b, o_c, o_d), w_out_bf, h, mods_flat, layer, n_ctx)


def kernel(x, c, ctx, c_ctx, norm_g, w_ada, b_ada, w_in, hgrn_lb_logits, hgrn_norm_g, gla_w_gk,
           gla_b_gk, gla_norm_g, s5_lam_re, s5_lam_im, s5_log_dt, s5_b_re, s5_b_im, s5_c_re, s5_c_im,
           s5_d, s5_w_glu, s5_b_glu, ret_decay_logit, ret_norm_g, w_out, final_norm_g):
    bsz, seq, d = x.shape
    n_ctx = ctx.shape[1]
    depth = w_in.shape[0]
    assert bsz < SUBLANES and n_ctx % ROW_TILE == 0 and seq % ROW_TILE == 0

    cvec = jnp.concatenate([c, c_ctx[None], jnp.zeros((SUBLANES - bsz - 1, d), F32)], axis=0)
    mods = ada_modulation(cvec, w_ada, b_ada)
    mods_flat = mods.reshape(depth * SUBLANES * 3, 1, d)
    rope = _rope_tables(seq // GRID_W, n_ctx, (d // 8) // D_HEADS)

    w_in_perm = permute_in_weights(w_in)
    w_out_bf = w_out.astype(BF16)
    h = None
    for layer in range(depth):
        if layer == 0:
            xn, h = prenorm_first(ctx, x, norm_g[0], mods_flat, n_ctx)
        else:
            xn = prenorm(h, norm_g[layer], mods_flat, layer, n_ctx)
        h = mixer_layer(xn, h, mods_flat, layer, n_ctx, w_in_perm, hgrn_lb_logits, hgrn_norm_g,
                        gla_w_gk, gla_b_gk, gla_norm_g, s5_lam_re, s5_lam_im, s5_log_dt, s5_b_re,
                        s5_b_im, s5_c_re, s5_c_im, s5_d, s5_w_glu, s5_b_glu, ret_decay_logit,
                        ret_norm_g, w_out_bf, rope)
    return final_norm(h, final_norm_g, n_ctx)
```

```python
import functools
import math

import numpy as np
import jax
import jax.numpy as jnp
from jax import lax
from jax.experimental import pallas as pl
from jax.experimental.pallas import tpu as pltpu

F32 = jnp.float32
BF16 = jnp.bfloat16

EPS = 1e-6
A_HEAD_DIM = 128
A_MIN_FORGET = 1e-6
B_HEADS = 4
B_GATE_RANK = 16
B_GATE_NORM = 16.0
C_GROUP = 16
C_STATE = 64
C_MAX_RE = -1e-4
D_HEADS = 4
GRID_W = 64
ROPE_BASE = 10000.0

LANES = 128
SUBLANES = 8
BF16_ROWS = 16
VMEM_LIMIT = 56 * 1024 * 1024

CHUNK = 64
SHORT_SPAN = 60.0
S5_CHUNK = 64
ROW_TILE = 256


def _cparams(sem):
    return pltpu.CompilerParams(dimension_semantics=sem, vmem_limit_bytes=VMEM_LIMIT)


def _dot(a, b):
    return jnp.dot(a, b, preferred_element_type=F32)


def _dot_nt(a, b):
    return lax.dot_general(a, b, (((1,), (1,)), ((), ())), preferred_element_type=F32)


def _dot_tn(a, b):
    return lax.dot_general(a, b, (((0,), (0,)), ((), ())), preferred_element_type=F32)


def _sigmoid(x):
    return 1.0 / (1.0 + jnp.exp(-x))


def _silu(x):
    return x * _sigmoid(x)


def _log_sigmoid(x):
    return jnp.minimum(x, 0.0) - jnp.log(1.0 + jnp.exp(-jnp.abs(x)))


def _ada_kernel(c_ref, w_ref, b_ref, o_ref):
    cv = _silu(c_ref[...]).astype(BF16)
    o_ref[0] = _dot(cv, w_ref[0].astype(BF16)) + b_ref[0]


def ada_modulation(cvec, w_ada, b_ada):
    depth, d, n3 = w_ada.shape
    tn = 512
    return pl.pallas_call(
        _ada_kernel,
        grid=(depth, n3 // tn),
        in_specs=[
            pl.BlockSpec((SUBLANES, d), lambda l, j: (0, 0)),
            pl.BlockSpec((1, d, tn), lambda l, j: (l, 0, j)),
            pl.BlockSpec((1, 1, tn), lambda l, j: (l, 0, j)),
        ],
        out_specs=pl.BlockSpec((1, SUBLANES, tn), lambda l, j: (l, 0, j)),
        out_shape=jax.ShapeDtypeStruct((depth, SUBLANES, n3), F32),
        compiler_params=_cparams(("parallel", "parallel")),
        name="ada_modulation",
    )(cvec, w_ada, b_ada.reshape(depth, 1, n3))


def _prenorm_kernel(x_ref, g_ref, sh_ref, sc_ref, o_ref):
    x = x_ref[0]
    y = x * lax.rsqrt(jnp.mean(x * x, axis=-1, keepdims=True) + EPS) * g_ref[...]
    o_ref[0] = (y * (1.0 + sc_ref[0]) + sh_ref[0]).astype(BF16)


def prenorm(h, g, mods_flat, layer, n_ctx):
    bsz, l, d = h.shape
    tr = ROW_TILE
    nct = n_ctx // tr

    def mod_map(part):
        def f(b, i):
            row = jnp.where(i < nct, bsz, b)
            return ((layer * SUBLANES + row) * 3 + part, 0, 0)
        return f

    return pl.pallas_call(
        _prenorm_kernel,
        grid=(bsz, l // tr),
        in_specs=[
            pl.BlockSpec((1, tr, d), lambda b, i: (b, i, 0)),
            pl.BlockSpec((1, d), lambda b, i: (0, 0)),
            pl.BlockSpec((1, 1, d), mod_map(0)),
            pl.BlockSpec((1, 1, d), mod_map(1)),
        ],
        out_specs=pl.BlockSpec((1, tr, d), lambda b, i: (b, i, 0)),
        out_shape=jax.ShapeDtypeStruct((bsz, l, d), BF16),
        compiler_params=_cparams(("parallel", "parallel")),
        name="prenorm",
    )(h, g.reshape(1, d), mods_flat, mods_flat)


def _prenorm_first_kernel(c_ref, x_ref, g_ref, sh_ref, sc_ref, o_ref, h_ref, *, nct):
    x = jnp.where(pl.program_id(1) < nct, c_ref[0], x_ref[0])
    h_ref[0] = x
    y = x * lax.rsqrt(jnp.mean(x * x, axis=-1, keepdims=True) + EPS) * g_ref[...]
    o_ref[0] = (y * (1.0 + sc_ref[0]) + sh_ref[0]).astype(BF16)


def prenorm_first(ctx, x, g, mods_flat, n_ctx):
    bsz, seq, d = x.shape
    l = n_ctx + seq
    tr = ROW_TILE
    nct = n_ctx // tr

    def mod_map(part):
        def f(b, i):
            row = jnp.where(i < nct, bsz, b)
            return (row * 3 + part, 0, 0)
        return f

    row_spec = pl.BlockSpec((1, tr, d), lambda b, i: (b, i, 0))
    return pl.pallas_call(
        functools.partial(_prenorm_first_kernel, nct=nct),
        grid=(bsz, l // tr),
        in_specs=[
            pl.BlockSpec((1, tr, d), lambda b, i: (b, jnp.minimum(i, nct - 1), 0)),
            pl.BlockSpec((1, tr, d), lambda b, i: (b, jnp.maximum(i - nct, 0), 0)),
            pl.BlockSpec((1, d), lambda b, i: (0, 0)),
            pl.BlockSpec((1, 1, d), mod_map(0)),
            pl.BlockSpec((1, 1, d), mod_map(1)),
        ],
        out_specs=[row_spec, row_spec],
        out_shape=[jax.ShapeDtypeStruct((bsz, l, d), BF16), jax.ShapeDtypeStruct((bsz, l, d), F32)],
        compiler_params=_cparams(("parallel", "arbitrary")),
        name="prenorm_first",
    )(ctx, x, g.reshape(1, d), mods_flat, mods_flat)


def _wprep_kernel(a_ref, b_ref, o_ref, *, first_shifted, tail_block, lr_w):
    i = pl.program_id(1)
    tr = a_ref.shape[1]

    @pl.when(i < first_shifted)
    def _():
        o_ref[0] = a_ref[0].astype(BF16)

    @pl.when(jnp.logical_and(i >= first_shifted, i < tail_block))
    def _():
        o_ref[0, :tr - lr_w] = a_ref[0, lr_w:].astype(BF16)
        o_ref[0, tr - lr_w:] = b_ref[0].astype(BF16)

    @pl.when(i == tail_block)
    def _():
        o_ref[0, :lr_w] = b_ref[0].astype(BF16)
        o_ref[0, lr_w:] = jnp.zeros((tr - lr_w, a_ref.shape[2]), BF16)


def permute_in_weights(w_in):
    depth, d, n = w_in.shape
    bw, lr_w = d // 4, 2 * B_GATE_RANK
    lr0 = 5 * bw + 2 * (bw // 2) + bw
    tr = LANES
    assert (n - lr_w) % tr == 0 and lr0 % tr == 0 and tr % lr_w == 0
    w_t = jnp.swapaxes(w_in, 1, 2)
    first_shifted, tail_block = lr0 // tr, (n - lr_w) // tr
    per = tr // lr_w

    def a_map(l, i):
        return (l, jnp.minimum(i, tail_block - 1), 0)

    def b_map(l, i):
        return (l, jnp.where(i == tail_block, lr0 // lr_w,
                             jnp.minimum(i + 1, tail_block) * per), 0)

    return pl.pallas_call(
        functools.partial(_wprep_kernel, first_shifted=first_shifted, tail_block=tail_block, lr_w=lr_w),
        grid=(depth, tail_block + 1),
        in_specs=[pl.BlockSpec((1, tr, d), a_map), pl.BlockSpec((1, lr_w, d), b_map)],
        out_specs=pl.BlockSpec((1, tr, d), lambda l, i: (l, i, 0)),
        out_shape=jax.ShapeDtypeStruct((depth, (tail_block + 1) * tr, d), BF16),
        compiler_params=_cparams(("parallel", "parallel")),
        name="permute_in_weights",
    )(w_t, w_t)


def _matmul_kernel(x_ref, wt_ref, o_ref):
    o_ref[...] = _dot_nt(x_ref[...], wt_ref[0])


def _pick_tile(n, candidates):
    for c in candidates:
        if n % c == 0:
            return c
    return n


def in_projection(xn, w_all, layer):
    m, d = xn.shape
    n = w_all.shape[1]
    tm = _pick_tile(m, (1024, 512, 256, 128, 64, 32, 16))
    tn = _pick_tile(n, (896, 640, 512, 384, 256, 128))
    return pl.pallas_call(
        _matmul_kernel,
        grid=(m // tm, n // tn),
        in_specs=[
            pl.BlockSpec((tm, d), lambda i, j: (i, 0)),
            pl.BlockSpec((1, tn, d), lambda i, j: (layer, j, 0)),
        ],
        out_specs=pl.BlockSpec((tm, tn), lambda i, j: (i, j)),
        out_shape=jax.ShapeDtypeStruct((m, n), F32),
        compiler_params=_cparams(("parallel", "parallel")),
        name="in_projection",
    )(xn, w_all)


def _decay_constants(c, bwd):
    nlev = int(math.log2(c))
    w = np.zeros((nlev + 2, c, c), np.float32)
    masks = np.zeros((nlev, c, c), np.float32)
    for lev in range(nlev):
        s = 1 << lev
        for r in range(c):
            pos = r % (2 * s)
            mid = r - pos + s
            if pos >= s:
                w[lev, r, mid:r + 1] = 1.0
            else:
                w[lev, r, r + 1:mid] = 1.0
        for i in range(c):
            for j in range(c):
                if i // (2 * s) == j // (2 * s) and i % (2 * s) >= s and j % (2 * s) < s:
                    masks[lev, i, j] = 1.0
    for r in range(c):
        w[nlev, r, :r + 1] = 1.0
        w[nlev + 1, r, r + 1:] = 1.0
    half = c // 2
    ii, jj = np.meshgrid(np.arange(c), np.arange(c), indexing="ij")
    diag = ((ii // half == jj // half) & (jj <= ii)).astype(np.float32)
    masks = np.concatenate([masks, diag[None]], axis=0)
    w = w.reshape((nlev + 2) * c, c)
    w = np.concatenate([w, np.ones((BF16_ROWS, c), np.float32)], axis=0)
    if bwd:
        w = np.concatenate([w[:-BF16_ROWS].reshape(nlev + 2, c, c)[:, ::-1, ::-1].reshape(-1, c),
                            w[-BF16_ROWS:]], axis=0)
        masks = masks[:, ::-1, ::-1]
    return np.ascontiguousarray(w), np.ascontiguousarray(masks), nlev


def _split3(x):
    hi = x.astype(BF16)
    r1 = x - hi.astype(F32)
    mid = r1.astype(BF16)
    lo = (r1 - mid.astype(F32)).astype(BF16)
    return hi, mid, lo


def _gated_core(qs, ks_, vs_, gs, w_ref, mask_ref, o_scr, state_ref, heads, dk, dv, nlev, c, bwd):
    half = c // 2
    nb = len(qs)
    w_in = w_ref[nlev * c:(nlev + 1) * c, :]
    first_a, last_a, first_b, last_b = (half - 1, 0, c - 1, half) if bwd else (0, half - 1, half, c - 1)
    top_ref, exit_row = (half, 0) if bwd else (half - 1, c - 1)
    g_parts, b_ins, short = [], [], None
    for g in gs:
        parts = _split3(g)
        b_in = _dot(w_in, parts[0]) + _dot(w_in, parts[1]) + _dot(w_in, parts[2])
        span = jnp.minimum(b_in[last_a:last_a + 1] - b_in[first_a:first_a + 1],
                           b_in[last_b:last_b + 1] - b_in[first_b:first_b + 1])
        ok = jnp.min(span) >= -SHORT_SPAN
        short = ok if short is None else jnp.logical_and(short, ok)
        g_parts.append(parts)
        b_ins.append(b_in)

    def head_update(bb, h, scores, e_in, e_out, e_tot, extra):
        ks = slice(h * dk, (h + 1) * dk)
        vs = slice(h * dv, (h + 1) * dv)
        qh, kh, vh = qs[bb][:, ks], ks_[bb][:, ks], vs_[bb][:, vs]
        vb = vh.astype(BF16)
        st = state_ref[bb * heads + h]
        o = _dot_nt((qh * e_in).astype(BF16), st.astype(BF16)) + _dot(scores.astype(BF16), vb)
        if extra is not None:
            o = o + extra * vh
        state_ref[bb * heads + h] = st * e_tot + _dot_tn(vb, (kh * e_out).astype(BF16))
        o_scr[bb, :, vs] = o

    @pl.when(short)
    def _():
        row = lax.broadcasted_iota(jnp.int32, (c, 1), 0)
        in_a = row < half
        later = in_a if bwd else jnp.logical_not(in_a)
        for bb in range(nb):
            b_in = b_ins[bb]
            m = jnp.where(in_a, b_in[first_a:first_a + 1], b_in[first_b:first_b + 1])
            fq = jnp.exp(b_in - m)
            fk = jnp.exp(m - b_in)
            r1 = b_in[top_ref:top_ref + 1]
            ft = jnp.exp(jnp.where(later, b_in - r1, r1 - b_in))
            tot = b_in[exit_row:exit_row + 1]
            e_in_all = jnp.exp(b_in)
            e_out_all = jnp.exp(tot - b_in)
            e_tot_all = jnp.exp(tot)
            for h in range(heads):
                ks = slice(h * dk, (h + 1) * dk)
                qh, kh = qs[bb][:, ks], ks_[bb][:, ks]
                scores = (mask_ref[nlev] * _dot_nt((qh * fq[:, ks]).astype(BF16),
                                                   (kh * fk[:, ks]).astype(BF16))
                          + mask_ref[nlev - 1] * _dot_nt((qh * ft[:, ks]).astype(BF16),
                                                         (kh * ft[:, ks]).astype(BF16)))
                head_update(bb, h, scores, e_in_all[:, ks], e_out_all[:, ks], e_tot_all[:, ks], None)

    @pl.when(jnp.logical_not(short))
    def _():
        w = w_ref[...]
        for bb in range(nb):
            parts = g_parts[bb]
            e_all = jnp.exp(_dot(w, parts[0]) + _dot(w, parts[1]) + _dot(w, parts[2]))
            for h in range(heads):
                ks = slice(h * dk, (h + 1) * dk)
                qh, kh = qs[bb][:, ks], ks_[bb][:, ks]
                scores = jnp.zeros((c, c), F32)
                for lev in range(nlev):
                    f = e_all[lev * c:(lev + 1) * c, ks]
                    scores = scores + mask_ref[lev] * _dot_nt((qh * f).astype(BF16), (kh * f).astype(BF16))
                head_update(bb, h, scores, e_all[nlev * c:(nlev + 1) * c, ks],
                            e_all[(nlev + 1) * c:(nlev + 2) * c, ks],
                            e_all[(nlev + 2) * c:(nlev + 2) * c + 1, ks],
                            jnp.sum(qh * kh, axis=-1, keepdims=True))


def _finish(o_scr, of_ref, gate_ref, ng_ref, o_ref, heads, hd, center, final):
    if not final:
        return
    for bb in range(o_scr.shape[0]):
        gs = _silu(gate_ref[bb])
        for h in range(heads):
            sl = slice(h * hd, (h + 1) * hd)
            x = o_scr[bb, :, sl] + of_ref[bb, :, sl]
            if center:
                x = x - jnp.mean(x, axis=-1, keepdims=True)
            y = x * lax.rsqrt(jnp.mean(x * x, axis=-1, keepdims=True) + EPS) * ng_ref[:, sl]
            o_ref[bb, :, sl] = (y * gs[:, sl]).astype(BF16)


def _split_refs(refs, n_in, bwd):
    ins = refs[:n_in]
    if bwd:
        return ins, refs[n_in:n_in + 3], refs[n_in + 3:]
    o_ref, state_ref = refs[n_in:]
    return ins, (None, None, None), (o_ref, state_ref, o_ref)


def _hgrn_kernel(*refs, layer, heads, dk, nlev, c, bwd):
    (q_ref, z_ref, v_ref, lbl_ref, w_ref, mask_ref), fin, (o_ref, state_ref, o_scr) = \
        _split_refs(refs, 6, bwd)

    @pl.when(pl.program_id(1) == 0)
    def _():
        state_ref[...] = jnp.zeros_like(state_ref)

    logits = lbl_ref[...]
    ex = jnp.exp(logits - jnp.max(logits, axis=0, keepdims=True))
    p = ex / jnp.sum(ex, axis=0, keepdims=True)
    lb = jnp.sum(p[:layer + 1], axis=0, keepdims=True) - p[0:1]
    nb = q_ref.shape[0]
    ks_, gs = [], []
    for bb in range(nb):
        z = z_ref[bb]
        e = jnp.exp(-jnp.abs(z))
        s_big = 1.0 / (1.0 + e)
        s_small = e * s_big
        sig_pos = jnp.where(z >= 0, s_big, s_small)
        sig_neg = jnp.where(z >= 0, s_small, s_big)
        gs.append(jnp.log(jnp.maximum(lb + (1.0 - lb) * sig_pos, A_MIN_FORGET)))
        ks_.append((1.0 - lb) * sig_neg)
    _gated_core([q_ref[bb] for bb in range(nb)], ks_, [v_ref[bb] for bb in range(nb)], gs,
                w_ref, mask_ref, o_scr, state_ref, heads, dk, dk, nlev, c, bwd)
    _finish(o_scr, *fin, o_ref, heads, dk, False, bwd)


def _gla_kernel(*refs, heads, dk, dv, nlev, c, bwd):
    (q_ref, k_ref, v_ref, lr_ref, wgk_ref, bgk_ref, w_ref, mask_ref), fin, (o_ref, state_ref, o_scr) = \
        _split_refs(refs, 8, bwd)

    @pl.when(pl.program_id(1) == 0)
    def _():
        state_ref[...] = jnp.zeros_like(state_ref)

    nb = q_ref.shape[0]
    qs, gs = [], []
    for bb in range(nb):
        logit = _dot(lr_ref[bb].astype(BF16), wgk_ref[...]) + bgk_ref[...]
        gs.append(_log_sigmoid(logit) / B_GATE_NORM)
        qs.append(q_ref[bb] * (dk ** -0.5))
    _gated_core(qs, [k_ref[bb] for bb in range(nb)], [v_ref[bb] for bb in range(nb)], gs,
                w_ref, mask_ref, o_scr, state_ref, heads, dk, dv, nlev, c, bwd)
    _finish(o_scr, *fin, o_ref, heads, dv, False, bwd)


def _chunk_order(n_chunks_ctx, n_chunks, bwd):
    def chunk(n):
        if not bwd:
            return n
        return jnp.where(n < n_chunks_ctx, n_chunks_ctx - 1 - n, n_chunks - 1 - n + n_chunks_ctx)
    return chunk


def _mixer_call(kern, name, proj, in_arrays, in_specs, chunk, nb, c, width, state_shape,
                o_fwd, gate_col, norm_row):
    bsz, l, _ = proj.shape
    bwd = o_fwd is not None
    blk = pl.BlockSpec((nb, c, width), lambda b, n: (b, chunk(n), 0))
    if bwd:
        in_arrays = in_arrays + [o_fwd, proj, norm_row]
        in_specs = in_specs + [
            blk,
            pl.BlockSpec((nb, c, width), lambda b, n: (b, chunk(n), gate_col // width)),
            pl.BlockSpec((1, width), lambda b, n: (0, 0)),
        ]
    return pl.pallas_call(
        kern,
        grid=(bsz // nb, l // c),
        in_specs=in_specs,
        out_specs=blk,
        out_shape=jax.ShapeDtypeStruct((bsz, l, width), BF16 if bwd else F32),
        scratch_shapes=[pltpu.VMEM((nb * state_shape[0],) + state_shape[1:], F32)]
        + ([pltpu.VMEM((nb, c, width), F32)] if bwd else []),
        compiler_params=_cparams(("parallel", "arbitrary")),
        name=name + ("_bwd" if bwd else "_fwd"),
    )(*in_arrays)


def _batch_per_step(bsz):
    return 2 if bsz % 2 == 0 else 1


def hgrn_mixer(proj, col, lb_logits, layer, n_ctx, gate_col, norm_g):
    bsz, l, _ = proj.shape
    depth, _, width = lb_logits.shape
    c = CHUNK
    heads, dk = width // A_HEAD_DIM, A_HEAD_DIM
    cq, cf, ci = (x // width for x in col)
    norm_row = jnp.tile(norm_g, heads).reshape(1, width)
    nb = _batch_per_step(bsz)
    o_fwd = None
    for bwd in (False, True):
        w_np, m_np, nlev = _decay_constants(c, bwd)
        chunk = _chunk_order(n_ctx // c, l // c, bwd)
        d = int(bwd)
        kern = functools.partial(_hgrn_kernel, layer=layer, heads=heads, dk=dk, nlev=nlev, c=c, bwd=bwd)
        in_specs = [
            pl.BlockSpec((nb, c, width), lambda b, n, chunk=chunk: (b, chunk(n), cq)),
            pl.BlockSpec((nb, c, width), lambda b, n, chunk=chunk, d=d: (b, chunk(n), cf + d)),
            pl.BlockSpec((nb, c, width), lambda b, n, chunk=chunk: (b, chunk(n), ci)),
            pl.BlockSpec((depth, width), lambda b, n: (0, 0)),
            pl.BlockSpec(w_np.shape, lambda b, n: (0, 0)),
            pl.BlockSpec(m_np.shape, lambda b, n: (0, 0, 0)),
        ]
        in_arrays = [proj, proj, proj, lb_logits[:, d], jnp.asarray(w_np, BF16), jnp.asarray(m_np)]
        o_fwd = _mixer_call(kern, "hgrn_mixer", proj, in_arrays, in_specs, chunk, nb, c, width,
                            (heads, dk, dk), o_fwd, gate_col, norm_row)
    return o_fwd


def gla_mixer(proj, col, wgk_pad, b_gk, n_ctx, width, gate_col, norm_g):
    bsz, l, _ = proj.shape
    c = CHUNK
    key_w = wgk_pad.shape[-1]
    heads = B_HEADS
    dk, dv = key_w // heads, width // heads
    cq, ck, cv, clr = col
    norm_row = jnp.tile(norm_g, heads).reshape(1, width)
    nb = _batch_per_step(bsz)
    o_fwd = None
    for bwd in (False, True):
        w_np, m_np, nlev = _decay_constants(c, bwd)
        chunk = _chunk_order(n_ctx // c, l // c, bwd)
        d = int(bwd)
        kern = functools.partial(_gla_kernel, heads=heads, dk=dk, dv=dv, nlev=nlev, c=c, bwd=bwd)
        in_specs = [
            pl.BlockSpec((nb, c, key_w), lambda b, n, chunk=chunk: (b, chunk(n), cq // key_w)),
            pl.BlockSpec((nb, c, key_w), lambda b, n, chunk=chunk: (b, chunk(n), ck // key_w)),
            pl.BlockSpec((nb, c, width), lambda b, n, chunk=chunk: (b, chunk(n), cv // width)),
            pl.BlockSpec((nb, c, LANES), lambda b, n, chunk=chunk: (b, chunk(n), clr // LANES)),
            pl.BlockSpec((LANES, key_w), lambda b, n: (0, 0)),
            pl.BlockSpec((1, key_w), lambda b, n: (0, 0)),
            pl.BlockSpec(w_np.shape, lambda b, n: (0, 0)),
            pl.BlockSpec(m_np.shape, lambda b, n: (0, 0, 0)),
        ]
        in_arrays = [proj, proj, proj, proj, wgk_pad[d], b_gk[d].reshape(1, key_w),
                     jnp.asarray(w_np, BF16), jnp.asarray(m_np)]
        o_fwd = _mixer_call(kern, "gla_mixer", proj, in_arrays, in_specs, chunk, nb, c, width,
                            (heads, dv, dk), o_fwd, gate_col, norm_row)
    return o_fwd


def _retention_kernel(*refs, heads, dk, dv, c, bwd):
    (q_ref, k_ref, v_ref, cos_ref, sin_ref, dl_ref), fin, (o_ref, state_ref, o_scr) = \
        _split_refs(refs, 6, bwd)

    @pl.when(pl.program_id(1) == 0)
    def _():
        state_ref[...] = jnp.zeros_like(state_ref)

    ii = lax.broadcasted_iota(jnp.int32, (c, c), 0).astype(F32)
    jj = lax.broadcasted_iota(jnp.int32, (c, c), 1).astype(F32)
    rel = (jj - ii) if bwd else (ii - jj)
    t_col = lax.broadcasted_iota(jnp.int32, (c, 1), 0).astype(F32)
    since = ((c - 1.0) - t_col) if bwd else t_col
    log_gamma = _log_sigmoid(dl_ref[...])
    cos, sin = cos_ref[...], sin_ref[...]
    half = dk // 2
    for h in range(heads):
        ks = slice(h * dk, (h + 1) * dk)
        vs = slice(h * dv, (h + 1) * dv)
        lg = log_gamma[:, h:h + 1]
        qh, kh = q_ref[0, :, ks], k_ref[0, :, ks]
        qh = (qh * cos + pltpu.roll(qh, half, 1) * sin) * (dk ** -0.5)
        kh = kh * cos + pltpu.roll(kh, half, 1) * sin
        vb = v_ref[0, :, vs].astype(BF16)
        st = state_ref[h]
        dmat = jnp.where(rel >= 0, jnp.exp(lg * jnp.maximum(rel, 0.0)), 0.0)
        scores = _dot_nt(qh.astype(BF16), kh.astype(BF16)) * dmat
        xi = jnp.exp(lg * (since + 1.0))
        zeta = jnp.exp(lg * ((c - 1.0) - since))
        o = _dot(scores.astype(BF16), vb) + _dot_nt(qh.astype(BF16), st.astype(BF16)) * xi
        state_ref[h] = st * jnp.exp(lg * c) + _dot_tn(vb, (kh * zeta).astype(BF16))
        o_scr[0, :, vs] = o
    _finish(o_scr, *fin, o_ref, heads, dv, True, bwd)


def retention_mixer(proj, col, cos_t, sin_t, decay_logit_pad, n_ctx, width, gate_col, norm_g):
    bsz, l, _ = proj.shape
    heads = D_HEADS
    key_w = width // 2
    dk, dv = key_w // heads, width // heads
    c = _pick_tile(math.gcd(n_ctx, l - n_ctx), (256, 128, 64))
    cq, ck, cv = col
    norm_row = jnp.tile(norm_g, heads).reshape(1, width)
    o_fwd = None
    for bwd in (False, True):
        chunk = _chunk_order(n_ctx // c, l // c, bwd)
        kern = functools.partial(_retention_kernel, heads=heads, dk=dk, dv=dv, c=c, bwd=bwd)
        in_specs = [
            pl.BlockSpec((1, c, key_w), lambda b, n, chunk=chunk: (b, chunk(n), cq // key_w)),
            pl.BlockSpec((1, c, key_w), lambda b, n, chunk=chunk: (b, chunk(n), ck // key_w)),
            pl.BlockSpec((1, c, width), lambda b, n, chunk=chunk: (b, chunk(n), cv // width)),
            pl.BlockSpec((c, dk), lambda b, n, chunk=chunk: (chunk(n), 0)),
            pl.BlockSpec((c, dk), lambda b, n, chunk=chunk: (chunk(n), 0)),
            pl.BlockSpec((1, LANES), lambda b, n: (0, 0)),
        ]
        in_arrays = [proj, proj, proj, cos_t, sin_t, decay_logit_pad[int(bwd)]]
        o_fwd = _mixer_call(kern, "retention_mixer", proj, in_arrays, in_specs, chunk, 1, c, width,
                            (heads, dv, dk), o_fwd, gate_col, norm_row)
    return o_fwd


def _s5_kernel(u_ref, lam_ref, bt_ref, cm_ref, y_ref,
               toep_ref, win_r_ref, win_i_ref, wout_ref, cl_ref, s_r_ref, s_i_ref, xp_ref, *,
               bsz, n_chunks_ctx, n_chunks):
    t_len, hc, half = S5_CHUNK, C_GROUP, C_STATE
    lane = lax.broadcasted_iota(jnp.int32, (1, LANES), 1)
    lo = lane < half
    sgn = jnp.where(lo, -1.0, 1.0)
    tau = lax.broadcasted_iota(jnp.int32, (t_len, 1), 0)
    u = u_ref[0]
    rows = u.shape[0]
    y_acc = jnp.zeros((rows, t_len * hc), F32)

    def cmul(ar, ai, br, bi):
        return ar * br - ai * bi, ar * bi + ai * br

    def expand(x1, pa, x2, pb, out_ref):
        for t in range(t_len):
            blk = (x1 * jnp.broadcast_to(pa[t:t + 1], (hc, LANES))
                   + x2 * jnp.broadcast_to(pb[t:t + 1], (hc, LANES)))
            out_ref[t * hc:(t + 1) * hc, :] = blk.astype(out_ref.dtype)

    for d in range(2):
        lam_re = jnp.minimum(lam_ref[d, 0, 0:1], C_MAX_RE)
        lam_im = lam_ref[d, 0, 1:2]
        dt = jnp.exp(lam_ref[d, 0, 2:3])
        mag = jnp.exp(lam_re * dt)
        lb_r, lb_i = mag * jnp.cos(lam_im * dt), mag * jnp.sin(lam_im * dt)
        den = lam_re * lam_re + lam_im * lam_im
        nr, ni = lb_r - 1.0, lb_i
        cf_r, cf_i = (nr * lam_re + ni * lam_im) / den, (ni * lam_re - nr * lam_im) / den
        bt_r, bt_i = bt_ref[d, 0, 0], bt_ref[d, 0, 1]
        bb_r, bb_i = cmul(cf_r, cf_i, bt_r, bt_i)
        c_r, c_i = cm_ref[d, 0, 0], cm_ref[d, 0, 1]

        p_r, p_i = jnp.ones((t_len, LANES), F32), jnp.zeros((t_len, LANES), F32)
        q_r, q_i = p_r, p_i
        sq_r, sq_i = lb_r, lb_i
        for bit in range(int(math.log2(t_len))):
            sel = ((tau >> bit) & 1) == 1
            p_r, p_i = cmul(p_r, p_i, jnp.where(sel, sq_r, 1.0), jnp.where(sel, sq_i, 0.0))
            selq = (((t_len - 1 - tau) >> bit) & 1) == 1
            q_r, q_i = cmul(q_r, q_i, jnp.where(selq, sq_r, 1.0), jnp.where(selq, sq_i, 0.0))
            sq_r, sq_i = cmul(sq_r, sq_i, sq_r, sq_i)
        lc_r, lc_i = sq_r, sq_i
        if d == 0:
            toep_p, in_p = (p_r, p_i), (q_r, q_i)
            out_p = cmul(p_r, p_i, lb_r, lb_i)
        else:
            toep_p, in_p = (q_r, q_i), (p_r, p_i)
            out_p = cmul(q_r, q_i, lb_r, lb_i)

        tp_r, tp_i = toep_p
        expand(c_r, jnp.where(lo, tp_r, tp_i), sgn * c_i, jnp.where(lo, tp_i, tp_r), cl_ref)
        bbs = jnp.where(lo, bb_r, -bb_i)
        kt = lax.dot_general(bbs, cl_ref[...], (((1,), (1,)), ((), ())),
                             precision=lax.Precision.HIGHEST, preferred_element_type=F32)
        width = t_len * hc
        glane = lax.broadcasted_iota(jnp.int32, (hc, width), 1)
        per_tile = LANES // hc
        for m in range(per_tile):
            if d == 0:
                base = kt if m == 0 else jnp.where(glane >= hc * m, pltpu.roll(kt, hc * m, 1), 0.0)
            else:
                base = kt if m == 0 else jnp.where(glane < width - hc * m,
                                                   pltpu.roll(kt, width - hc * m, 1), 0.0)
            base = base.astype(BF16)
            for a in range(t_len // per_tile):
                off = a * LANES
                if d == 0:
                    j = a * per_tile + m
                    if off:
                        toep_ref[d, j * hc:(j + 1) * hc, :off] = jnp.zeros((hc, off), BF16)
                    toep_ref[d, j * hc:(j + 1) * hc, off:] = base[:, :width - off]
                else:
                    j = t_len - 1 - (a * per_tile + m)
                    if off:
                        toep_ref[d, j * hc:(j + 1) * hc, width - off:] = jnp.zeros((hc, off), BF16)
                    toep_ref[d, j * hc:(j + 1) * hc, :width - off] = base[:, off:]

        ip_r, ip_i = in_p
        expand(bb_r, ip_r, -bb_i, ip_i, win_r_ref)
        expand(bb_i, ip_r, bb_r, ip_i, win_i_ref)
        op_r, op_i = out_p
        expand(c_r, jnp.where(lo, op_r, -op_i), c_i, jnp.where(lo, -op_i, -op_r), wout_ref)

        s_r_ref[...] = _dot(u, win_r_ref[...])
        s_i_ref[...] = _dot(u, win_i_ref[...])
        if d == 0:
            order = list(range(n_chunks))
        else:
            order = list(range(n_chunks_ctx - 1, -1, -1)) + list(range(n_chunks - 1, n_chunks_ctx - 1, -1))
        x_r, x_i = jnp.zeros((bsz, LANES), F32), jnp.zeros((bsz, LANES), F32)
        for n in order:
            rs = slice(n * bsz, (n + 1) * bsz)
            xp_ref[rs, :] = jnp.where(lo, x_r, x_i)
            nx_r, nx_i = cmul(lc_r, lc_i, x_r, x_i)
            x_r, x_i = nx_r + s_r_ref[rs, :], nx_i + s_i_ref[rs, :]

        y_acc = y_acc + _dot(u, toep_ref[d]) + _dot_nt(xp_ref[...].astype(BF16), wout_ref[...])
    y_ref[0] = y_acc


def s5_core(ug, lam_pk, bt_pk, cm_pk, bsz, n_chunks_ctx, n_chunks):
    groups, rows, width = ug.shape
    kern = functools.partial(_s5_kernel, bsz=bsz, n_chunks_ctx=n_chunks_ctx, n_chunks=n_chunks)
    return pl.pallas_call(
        kern,
        grid=(groups,),
        in_specs=[
            pl.BlockSpec((1, rows, width), lambda g: (g, 0, 0)),
            pl.BlockSpec((2, 1, SUBLANES, LANES), lambda g: (0, g, 0, 0)),
            pl.BlockSpec((2, 1, 2, C_GROUP, LANES), lambda g: (0, g, 0, 0, 0)),
            pl.BlockSpec((2, 1, 2, C_GROUP, LANES), lambda g: (0, g, 0, 0, 0)),
        ],
        out_specs=pl.BlockSpec((1, rows, width), lambda g: (g, 0, 0)),
        out_shape=jax.ShapeDtypeStruct((groups, rows, width), F32),
        scratch_shapes=[
            pltpu.VMEM((2, width, width), BF16),
            pltpu.VMEM((width, LANES), BF16),
            pltpu.VMEM((width, LANES), BF16),
            pltpu.VMEM((width, LANES), BF16),
            pltpu.VMEM((width, LANES), F32),
            pltpu.VMEM((rows, LANES), F32),
            pltpu.VMEM((rows, LANES), F32),
            pltpu.VMEM((rows, LANES), F32),
        ],
        compiler_params=_cparams(("parallel",)),
        name="s5_core",
    )(ug, lam_pk, bt_pk, cm_pk)


def _s5_post_kernel(y_ref, u_ref, gate_ref, d_ref, w_ref, b_ref, o_ref):
    y = y_ref[0] + d_ref[...] * u_ref[0]
    z = jax.nn.gelu(y)
    t = _dot(z.astype(BF16), w_ref[...]) + b_ref[...]
    o_ref[0] = (z * _sigmoid(t) * _silu(gate_ref[0])).astype(BF16)


def s5_post(y, proj, u_col, gate_col, d_skip, w_glu, b_glu):
    bsz, l, width = y.shape
    tr = ROW_TILE
    return pl.pallas_call(
        _s5_post_kernel,
        grid=(bsz, l // tr),
        in_specs=[
            pl.BlockSpec((1, tr, width), lambda b, i: (b, i, 0)),
            pl.BlockSpec((1, tr, width), lambda b, i: (b, i, u_col // width)),
            pl.BlockSpec((1, tr, width), lambda b, i: (b, i, gate_col // width)),
            pl.BlockSpec((1, width), lambda b, i: (0, 0)),
            pl.BlockSpec((width, width), lambda b, i: (0, 0)),
            pl.BlockSpec((1, width), lambda b, i: (0, 0)),
        ],
        out_specs=pl.BlockSpec((1, tr, width), lambda b, i: (b, i, 0)),
        out_shape=jax.ShapeDtypeStruct((bsz, l, width), BF16),
        compiler_params=_cparams(("parallel", "parallel")),
        name="s5_post",
    )(y, proj, proj, d_skip.reshape(1, width), w_glu.astype(BF16), b_glu.reshape(1, width))


def _outproj_kernel(oa_ref, ob_ref, oc_ref, od_ref, w_ref, h_ref, gl_ref, gc_ref, o_ref, *,
                    n_ctx, tm, bw):
    acc = _dot(oa_ref[0], w_ref[0, 0:bw, :].astype(BF16))
    acc = acc + _dot(ob_ref[0], w_ref[0, bw:2 * bw, :].astype(BF16))
    acc = acc + _dot(oc_ref[0], w_ref[0, 2 * bw:3 * bw, :].astype(BF16))
    acc = acc + _dot(od_ref[0], w_ref[0, 3 * bw:4 * bw, :].astype(BF16))
    row = pl.program_id(1) * tm + lax.broadcasted_iota(jnp.int32, (tm, 1), 0)
    gate = jnp.where(row < n_ctx, gc_ref[0], gl_ref[0])
    o_ref[0] = h_ref[0] + gate * acc


def out_projection(o_parts, w_out, h, mods_flat, layer, n_ctx):
    bsz, l, d = h.shape
    bw = o_parts[0].shape[-1]
    tm = l if l <= 2048 else _pick_tile(l, (l // 4, l // 8, l // 16))
    tn = _pick_tile(d, (512, 256, 128))
    kern = functools.partial(_outproj_kernel, n_ctx=n_ctx, tm=tm, bw=bw)
    o_spec = pl.BlockSpec((1, tm, bw), lambda b, i, j: (b, i, 0))
    return pl.pallas_call(
        kern,
        grid=(bsz, l // tm, d // tn),
        in_specs=[
            o_spec, o_spec, o_spec, o_spec,
            pl.BlockSpec((1, 4 * bw, tn), lambda b, i, j: (layer, 0, j)),
            pl.BlockSpec((1, tm, tn), lambda b, i, j: (b, i, j)),
            pl.BlockSpec((1, 1, tn), lambda b, i, j: ((layer * SUBLANES + b) * 3 + 2, 0, j)),
            pl.BlockSpec((1, 1, tn), lambda b, i, j: ((layer * SUBLANES + bsz) * 3 + 2, 0, j)),
        ],
        out_specs=pl.BlockSpec((1, tm, tn), lambda b, i, j: (b, i, j)),
        out_shape=jax.ShapeDtypeStruct((bsz, l, d), F32),
        compiler_params=_cparams(("parallel", "parallel", "parallel")),
        name="out_projection",
    )(*o_parts, w_out, h, mods_flat, mods_flat)


def _final_norm_kernel(x_ref, g_ref, o_ref):
    x = x_ref[0]
    o_ref[0] = x * lax.rsqrt(jnp.mean(x * x, axis=-1, keepdims=True) + EPS) * g_ref[...]


def final_norm(h, g, n_ctx):
    bsz, l, d = h.shape
    tr = ROW_TILE
    skip = n_ctx // tr
    return pl.pallas_call(
        _final_norm_kernel,
        grid=(bsz, (l - n_ctx) // tr),
        in_specs=[
            pl.BlockSpec((1, tr, d), lambda b, i: (b, i + skip, 0)),
            pl.BlockSpec((1, d), lambda b, i: (0, 0)),
        ],
        out_specs=pl.BlockSpec((1, tr, d), lambda b, i: (b, i, 0)),
        out_shape=jax.ShapeDtypeStruct((bsz, l - n_ctx, d), F32),
        compiler_params=_cparams(("parallel", "parallel")),
        name="final_norm",
    )(h, g.reshape(1, d))


def _rope_tables(rows, n_ctx, dk):
    quarter = dk // 4
    freqs = ROPE_BASE ** (-jnp.arange(quarter, dtype=F32) / quarter)
    t = jnp.arange(rows * GRID_W)
    r = (t // GRID_W).astype(F32)
    col = (t % GRID_W).astype(F32)
    ang = jnp.concatenate([r[:, None] * freqs, col[:, None] * freqs], axis=-1)
    ang = jnp.concatenate([jnp.zeros((n_ctx, dk // 2), F32), ang], axis=0)
    cos, sin = jnp.cos(ang), jnp.sin(ang)
    return jnp.concatenate([cos, cos], axis=-1), jnp.concatenate([-sin, sin], axis=-1)


def _dup(x):
    return jnp.concatenate([x, x], axis=-1)


def mixer_layer(xn, h, mods_flat, layer, n_ctx, w_in_perm, hgrn_lb_logits, hgrn_norm_g, gla_w_gk,
                gla_b_gk, gla_norm_g, s5_lam_re, s5_lam_im, s5_log_dt, s5_b_re, s5_b_im, s5_c_re,
                s5_c_im, s5_d, s5_w_glu, s5_b_glu, ret_decay_logit, ret_norm_g, w_out, rope):
    bsz, l, d = h.shape
    bw = d // 4
    kw = bw // 2
    rank = B_GATE_RANK

    names = ("a_q", "a_ff", "a_fb", "a_i", "a_g", "b_q", "b_k", "b_v", "b_g", "c_u", "c_g",
             "d_q", "d_k", "d_v", "d_g", "b_lr")
    widths = (bw, bw, bw, bw, bw, kw, kw, bw, bw, bw, bw, kw, kw, bw, bw, LANES)
    col = dict(zip(names, np.concatenate([[0], np.cumsum(widths)[:-1]]).tolist()))

    proj = in_projection(xn.reshape(bsz * l, d), w_in_perm, layer).reshape(bsz, l, -1)

    o_a = hgrn_mixer(proj, (col["a_q"], col["a_ff"], col["a_i"]), hgrn_lb_logits, layer, n_ctx,
                     col["a_g"], hgrn_norm_g[layer])

    wgk = gla_w_gk[layer].astype(BF16)
    wgk_pad = jnp.zeros((2, LANES, kw), BF16)
    wgk_pad = wgk_pad.at[0, :rank].set(wgk[0]).at[1, rank:2 * rank].set(wgk[1])
    o_b = gla_mixer(proj, (col["b_q"], col["b_k"], col["b_v"], col["b_lr"]), wgk_pad,
                    gla_b_gk[layer], n_ctx, bw, col["b_g"], gla_norm_g[layer])

    groups = bw // C_GROUP
    nck = l // S5_CHUNK
    u = proj[:, :, col["c_u"]:col["c_u"] + bw].astype(BF16)
    ug = u.reshape(bsz, nck, S5_CHUNK, groups, C_GROUP).transpose(3, 1, 0, 2, 4)
    ug = ug.reshape(groups, nck * bsz, S5_CHUNK * C_GROUP)
    dt_row = jnp.broadcast_to(s5_log_dt[layer][..., None], (2, groups, C_STATE))
    lam_pk = jnp.stack([_dup(s5_lam_re[layer]), _dup(s5_lam_im[layer]), _dup(dt_row)], axis=2)
    lam_pk = jnp.pad(lam_pk, ((0, 0), (0, 0), (0, SUBLANES - 3), (0, 0)))
    bt_pk = jnp.stack([_dup(jnp.swapaxes(s5_b_re[layer], -1, -2)),
                       _dup(jnp.swapaxes(s5_b_im[layer], -1, -2))], axis=2)
    cm_pk = jnp.stack([_dup(s5_c_re[layer]), _dup(s5_c_im[layer])], axis=2)
    yg = s5_core(ug, lam_pk, bt_pk, cm_pk, bsz, n_ctx // S5_CHUNK, nck)
    y = yg.reshape(groups, nck, bsz, S5_CHUNK, C_GROUP).transpose(2, 1, 3, 0, 4).reshape(bsz, l, bw)
    o_c = s5_post(y, proj, col["c_u"], col["c_g"], s5_d[layer], s5_w_glu[layer], s5_b_glu[layer])

    dl = jnp.pad(ret_decay_logit[layer], ((0, 0), (0, LANES - D_HEADS))).reshape(2, 1, LANES)
    o_d = retention_mixer(proj, (col["d_q"], col["d_k"], col["d_v"]), rope[0], rope[1], dl, n_ctx, bw,
                          col["d_g"], ret_norm_g[layer])

    return out_projection((o_a, o_b, o_c, o_d), w_out, h, mods_flat, layer, n_ctx)


def kernel(x, c, ctx, c_ctx, norm_g, w_ada, b_ada, w_in, hgrn_lb_logits, hgrn_norm_g, gla_w_gk,
           gla_b_gk, gla_norm_g, s5_lam_re, s5_lam_im, s5_log_dt, s5_b_re, s5_b_im, s5_c_re, s5_c_im,
           s5_d, s5_w_glu, s5_b_glu, ret_decay_logit, ret_norm_g, w_out, final_norm_g):
    bsz, seq, d = x.shape
    n_ctx = ctx.shape[1]
    depth = w_in.shape[0]
    assert bsz < SUBLANES and n_ctx % ROW_TILE == 0 and seq % ROW_TILE == 0

    cvec = jnp.concatenate([c, c_ctx[None], jnp.zeros((SUBLANES - bsz - 1, d), F32)], axis=0)
    mods = ada_modulation(cvec, w_ada, b_ada)
    mods_flat = mods.reshape(depth * SUBLANES * 3, 1, d)
    rope = _rope_tables(seq // GRID_W, n_ctx, (d // 8) // D_HEADS)

    w_in_perm = permute_in_weights(w_in)
    h = None
    for layer in range(depth):
        if layer == 0:
            xn, h = prenorm_first(ctx, x, norm_g[0], mods_flat, n_ctx)
        else:
            xn = prenorm(h, norm_g[layer], mods_flat, layer, n_ctx)
        h = mixer_layer(xn, h, mods_flat, layer, n_ctx, w_in_perm, hgrn_lb_logits, hgrn_norm_g,
                        gla_w_gk, gla_b_gk, gla_norm_g, s5_lam_re, s5_lam_im, s5_log_dt, s5_b_re,
                        s5_b_im, s5_c_re, s5_c_im, s5_d, s5_w_glu, s5_b_glu, ret_decay_logit,
                        ret_norm_g, w_out, rope)
    return final_norm(h, final_norm_g, n_ctx)
```

```python
import functools
import math

import numpy as np
import jax
import jax.numpy as jnp
from jax import lax
from jax.experimental import pallas as pl
from jax.experimental.pallas import tpu as pltpu

F32 = jnp.float32
BF16 = jnp.bfloat16

EPS = 1e-6
A_HEAD_DIM = 128
A_MIN_FORGET = 1e-6
B_HEADS = 4
B_GATE_RANK = 16
B_GATE_NORM = 16.0
C_GROUP = 16
C_STATE = 64
C_MAX_RE = -1e-4
D_HEADS = 4
GRID_W = 64
ROPE_BASE = 10000.0

LANES = 128
SUBLANES = 8
BF16_ROWS = 16
VMEM_LIMIT = 56 * 1024 * 1024

MXU_COLS = 256
IN_PROJ_TN = 6 * MXU_COLS

CHUNK = 64
SHORT_SPAN = 60.0
S5_CHUNK = 64
ROW_TILE = 256


def _cparams(sem):
    return pltpu.CompilerParams(dimension_semantics=sem, vmem_limit_bytes=VMEM_LIMIT)


def _dot(a, b):
    return jnp.dot(a, b, preferred_element_type=F32)


def _dot_nt(a, b):
    return lax.dot_general(a, b, (((1,), (1,)), ((), ())), preferred_element_type=F32)


def _dot_tn(a, b):
    return lax.dot_general(a, b, (((0,), (0,)), ((), ())), preferred_element_type=F32)


def _sigmoid(x):
    return 1.0 / (1.0 + jnp.exp(-x))


def _silu(x):
    return x * _sigmoid(x)


def _log_sigmoid(x):
    return jnp.minimum(x, 0.0) - jnp.log(1.0 + jnp.exp(-jnp.abs(x)))


def _ada_kernel(c_ref, w_ref, b_ref, o_ref):
    cv = _silu(c_ref[...]).astype(BF16)
    o_ref[0] = _dot(cv, w_ref[0].astype(BF16)) + b_ref[0]


def ada_modulation(cvec, w_ada, b_ada):
    depth, d, n3 = w_ada.shape
    tn = 512
    return pl.pallas_call(
        _ada_kernel,
        grid=(depth, n3 // tn),
        in_specs=[
            pl.BlockSpec((SUBLANES, d), lambda l, j: (0, 0)),
            pl.BlockSpec((1, d, tn), lambda l, j: (l, 0, j)),
            pl.BlockSpec((1, 1, tn), lambda l, j: (l, 0, j)),
        ],
        out_specs=pl.BlockSpec((1, SUBLANES, tn), lambda l, j: (l, 0, j)),
        out_shape=jax.ShapeDtypeStruct((depth, SUBLANES, n3), F32),
        compiler_params=_cparams(("parallel", "parallel")),
        name="ada_modulation",
    )(cvec, w_ada, b_ada.reshape(depth, 1, n3))


def _prenorm_kernel(x_ref, g_ref, sh_ref, sc_ref, o_ref):
    x = x_ref[0]
    y = x * lax.rsqrt(jnp.mean(x * x, axis=-1, keepdims=True) + EPS) * g_ref[...]
    o_ref[0] = (y * (1.0 + sc_ref[0]) + sh_ref[0]).astype(BF16)


def prenorm(h, g, mods_flat, layer, n_ctx):
    bsz, l, d = h.shape
    tr = ROW_TILE
    nct = n_ctx // tr

    def mod_map(part):
        def f(b, i):
            row = jnp.where(i < nct, bsz, b)
            return ((layer * SUBLANES + row) * 3 + part, 0, 0)
        return f

    return pl.pallas_call(
        _prenorm_kernel,
        grid=(bsz, l // tr),
        in_specs=[
            pl.BlockSpec((1, tr, d), lambda b, i: (b, i, 0)),
            pl.BlockSpec((1, d), lambda b, i: (0, 0)),
            pl.BlockSpec((1, 1, d), mod_map(0)),
            pl.BlockSpec((1, 1, d), mod_map(1)),
        ],
        out_specs=pl.BlockSpec((1, tr, d), lambda b, i: (b, i, 0)),
        out_shape=jax.ShapeDtypeStruct((bsz, l, d), BF16),
        compiler_params=_cparams(("parallel", "parallel")),
        name="prenorm",
    )(h, g.reshape(1, d), mods_flat, mods_flat)


def _prenorm_first_kernel(c_ref, x_ref, g_ref, sh_ref, sc_ref, o_ref, h_ref, *, nct):
    x = jnp.where(pl.program_id(1) < nct, c_ref[0], x_ref[0])
    h_ref[0] = x
    y = x * lax.rsqrt(jnp.mean(x * x, axis=-1, keepdims=True) + EPS) * g_ref[...]
    o_ref[0] = (y * (1.0 + sc_ref[0]) + sh_ref[0]).astype(BF16)


def prenorm_first(ctx, x, g, mods_flat, n_ctx):
    bsz, seq, d = x.shape
    l = n_ctx + seq
    tr = ROW_TILE
    nct = n_ctx // tr

    def mod_map(part):
        def f(b, i):
            row = jnp.where(i < nct, bsz, b)
            return (row * 3 + part, 0, 0)
        return f

    row_spec = pl.BlockSpec((1, tr, d), lambda b, i: (b, i, 0))
    return pl.pallas_call(
        functools.partial(_prenorm_first_kernel, nct=nct),
        grid=(bsz, l // tr),
        in_specs=[
            pl.BlockSpec((1, tr, d), lambda b, i: (b, jnp.minimum(i, nct - 1), 0)),
            pl.BlockSpec((1, tr, d), lambda b, i: (b, jnp.maximum(i - nct, 0), 0)),
            pl.BlockSpec((1, d), lambda b, i: (0, 0)),
            pl.BlockSpec((1, 1, d), mod_map(0)),
            pl.BlockSpec((1, 1, d), mod_map(1)),
        ],
        out_specs=[row_spec, row_spec],
        out_shape=[jax.ShapeDtypeStruct((bsz, l, d), BF16), jax.ShapeDtypeStruct((bsz, l, d), F32)],
        compiler_params=_cparams(("parallel", "arbitrary")),
        name="prenorm_first",
    )(ctx, x, g.reshape(1, d), mods_flat, mods_flat)


def _wprep_kernel(a_ref, b_ref, o_ref, *, first_shifted, tail_block, lr_w):
    i = pl.program_id(1)
    tr = a_ref.shape[1]

    @pl.when(i < first_shifted)
    def _():
        o_ref[0] = a_ref[0].astype(BF16)

    @pl.when(jnp.logical_and(i >= first_shifted, i < tail_block))
    def _():
        o_ref[0, :tr - lr_w] = a_ref[0, lr_w:].astype(BF16)
        o_ref[0, tr - lr_w:] = b_ref[0].astype(BF16)

    @pl.when(i == tail_block)
    def _():
        o_ref[0, :lr_w] = b_ref[0].astype(BF16)
        o_ref[0, lr_w:] = jnp.zeros((tr - lr_w, a_ref.shape[2]), BF16)

    @pl.when(i > tail_block)
    def _():
        o_ref[0] = jnp.zeros(o_ref.shape[1:], BF16)


def permute_in_weights(w_in):
    depth, d, n = w_in.shape
    bw, lr_w = d // 4, 2 * B_GATE_RANK
    lr0 = 5 * bw + 2 * (bw // 2) + bw
    tr = LANES
    assert (n - lr_w) % tr == 0 and lr0 % tr == 0 and tr % lr_w == 0
    w_t = jnp.swapaxes(w_in, 1, 2)
    first_shifted, tail_block = lr0 // tr, (n - lr_w) // tr
    per = tr // lr_w
    n_out = -(-(tail_block + 1) * tr // IN_PROJ_TN) * IN_PROJ_TN

    def a_map(l, i):
        return (l, jnp.minimum(i, tail_block - 1), 0)

    def b_map(l, i):
        return (l, jnp.where(i == tail_block, lr0 // lr_w,
                             jnp.minimum(i + 1, tail_block) * per), 0)

    return pl.pallas_call(
        functools.partial(_wprep_kernel, first_shifted=first_shifted, tail_block=tail_block, lr_w=lr_w),
        grid=(depth, n_out // tr),
        in_specs=[pl.BlockSpec((1, tr, d), a_map), pl.BlockSpec((1, lr_w, d), b_map)],
        out_specs=pl.BlockSpec((1, tr, d), lambda l, i: (l, i, 0)),
        out_shape=jax.ShapeDtypeStruct((depth, n_out, d), BF16),
        compiler_params=_cparams(("parallel", "parallel")),
        name="permute_in_weights",
    )(w_t, w_t)


def _matmul_kernel(x_ref, wt_ref, o_ref):
    o_ref[...] = _dot_nt(x_ref[...], wt_ref[0])


def _pick_tile(n, candidates):
    for c in candidates:
        if n % c == 0:
            return c
    return n


def in_projection(xn, w_all, layer):
    m, d = xn.shape
    n = w_all.shape[1]
    tm = _pick_tile(m, (1024, 512, 256, 128, 64, 32, 16))
    tn = IN_PROJ_TN
    return pl.pallas_call(
        _matmul_kernel,
        grid=(m // tm, n // tn),
        in_specs=[
            pl.BlockSpec((tm, d), lambda i, j: (i, 0), pipeline_mode=pl.Buffered(1)),
            pl.BlockSpec((1, tn, d), lambda i, j: (layer, j, 0)),
        ],
        out_specs=pl.BlockSpec((tm, tn), lambda i, j: (i, j)),
        out_shape=jax.ShapeDtypeStruct((m, n), F32),
        compiler_params=_cparams(("parallel", "parallel")),
        name="in_projection",
    )(xn, w_all)


def _decay_constants(c, bwd):
    nlev = int(math.log2(c))
    w = np.zeros((nlev + 2, c, c), np.float32)
    masks = np.zeros((nlev, c, c), np.float32)
    for lev in range(nlev):
        s = 1 << lev
        for r in range(c):
            pos = r % (2 * s)
            mid = r - pos + s
            if pos >= s:
                w[lev, r, mid:r + 1] = 1.0
            else:
                w[lev, r, r + 1:mid] = 1.0
        for i in range(c):
            for j in range(c):
                if i // (2 * s) == j // (2 * s) and i % (2 * s) >= s and j % (2 * s) < s:
                    masks[lev, i, j] = 1.0
    for r in range(c):
        w[nlev, r, :r + 1] = 1.0
        w[nlev + 1, r, r + 1:] = 1.0
    half = c // 2
    ii, jj = np.meshgrid(np.arange(c), np.arange(c), indexing="ij")
    diag = ((ii // half == jj // half) & (jj <= ii)).astype(np.float32)
    masks = np.concatenate([masks, diag[None]], axis=0)
    w = w.reshape((nlev + 2) * c, c)
    w = np.concatenate([w, np.ones((BF16_ROWS, c), np.float32)], axis=0)
    if bwd:
        w = np.concatenate([w[:-BF16_ROWS].reshape(nlev + 2, c, c)[:, ::-1, ::-1].reshape(-1, c),
                            w[-BF16_ROWS:]], axis=0)
        masks = masks[:, ::-1, ::-1]
    return np.ascontiguousarray(w), np.ascontiguousarray(masks), nlev


def _split3(x):
    hi = x.astype(BF16)
    r1 = x - hi.astype(F32)
    mid = r1.astype(BF16)
    lo = (r1 - mid.astype(F32)).astype(BF16)
    return hi, mid, lo


def _gated_core(qs, ks_, vs_, gs, w_ref, mask_ref, o_scr, state_ref, heads, dk, dv, nlev, c, bwd):
    half = c // 2
    nb = len(qs)
    w_in = w_ref[nlev * c:(nlev + 1) * c, :]
    first_a, last_a, first_b, last_b = (half - 1, 0, c - 1, half) if bwd else (0, half - 1, half, c - 1)
    top_ref, exit_row = (half, 0) if bwd else (half - 1, c - 1)
    g_parts, b_ins, short = [], [], None
    for g in gs:
        parts = _split3(g)
        b_in = _dot(w_in, parts[0]) + _dot(w_in, parts[1]) + _dot(w_in, parts[2])
        span = jnp.minimum(b_in[last_a:last_a + 1] - b_in[first_a:first_a + 1],
                           b_in[last_b:last_b + 1] - b_in[first_b:first_b + 1])
        ok = jnp.min(span) >= -SHORT_SPAN
        short = ok if short is None else jnp.logical_and(short, ok)
        g_parts.append(parts)
        b_ins.append(b_in)

    def head_update(bb, h, scores, e_in, e_out, e_tot, extra):
        ks = slice(h * dk, (h + 1) * dk)
        vs = slice(h * dv, (h + 1) * dv)
        qh, kh, vh = qs[bb][:, ks], ks_[bb][:, ks], vs_[bb][:, vs]
        vb = vh.astype(BF16)
        st = state_ref[bb * heads + h]
        o = _dot_nt((qh * e_in).astype(BF16), st.astype(BF16)) + _dot(scores.astype(BF16), vb)
        if extra is not None:
            o = o + extra * vh
        state_ref[bb * heads + h] = st * e_tot + _dot_tn(vb, (kh * e_out).astype(BF16))
        o_scr[bb, :, vs] = o

    @pl.when(short)
    def _():
        row = lax.broadcasted_iota(jnp.int32, (c, 1), 0)
        in_a = row < half
        later = in_a if bwd else jnp.logical_not(in_a)
        for bb in range(nb):
            b_in = b_ins[bb]
            m = jnp.where(in_a, b_in[first_a:first_a + 1], b_in[first_b:first_b + 1])
            fq = jnp.exp(b_in - m)
            fk = jnp.exp(m - b_in)
            r1 = b_in[top_ref:top_ref + 1]
            ft = jnp.exp(jnp.where(later, b_in - r1, r1 - b_in))
            tot = b_in[exit_row:exit_row + 1]
            e_in_all = jnp.exp(b_in)
            e_out_all = jnp.exp(tot - b_in)
            e_tot_all = jnp.exp(tot)
            for h in range(heads):
                ks = slice(h * dk, (h + 1) * dk)
                qh, kh = qs[bb][:, ks], ks_[bb][:, ks]
                scores = (mask_ref[nlev] * _dot_nt((qh * fq[:, ks]).astype(BF16),
                                                   (kh * fk[:, ks]).astype(BF16))
                          + mask_ref[nlev - 1] * _dot_nt((qh * ft[:, ks]).astype(BF16),
                                                         (kh * ft[:, ks]).astype(BF16)))
                head_update(bb, h, scores, e_in_all[:, ks], e_out_all[:, ks], e_tot_all[:, ks], None)

    @pl.when(jnp.logical_not(short))
    def _():
        w = w_ref[...]
        for bb in range(nb):
            parts = g_parts[bb]
            e_all = jnp.exp(_dot(w, parts[0]) + _dot(w, parts[1]) + _dot(w, parts[2]))
            for h in range(heads):
                ks = slice(h * dk, (h + 1) * dk)
                qh, kh = qs[bb][:, ks], ks_[bb][:, ks]
                scores = jnp.zeros((c, c), F32)
                for lev in range(nlev):
                    f = e_all[lev * c:(lev + 1) * c, ks]
                    scores = scores + mask_ref[lev] * _dot_nt((qh * f).astype(BF16), (kh * f).astype(BF16))
                head_update(bb, h, scores, e_all[nlev * c:(nlev + 1) * c, ks],
                            e_all[(nlev + 1) * c:(nlev + 2) * c, ks],
                            e_all[(nlev + 2) * c:(nlev + 2) * c + 1, ks],
                            jnp.sum(qh * kh, axis=-1, keepdims=True))


def _finish(o_scr, of_ref, gate_ref, ng_ref, o_ref, heads, hd, center, final):
    if not final:
        return
    for bb in range(o_scr.shape[0]):
        gs = _silu(gate_ref[bb])
        for h in range(heads):
            sl = slice(h * hd, (h + 1) * hd)
            x = o_scr[bb, :, sl] + of_ref[bb, :, sl]
            if center:
                x = x - jnp.mean(x, axis=-1, keepdims=True)
            y = x * lax.rsqrt(jnp.mean(x * x, axis=-1, keepdims=True) + EPS) * ng_ref[:, sl]
            o_ref[bb, :, sl] = (y * gs[:, sl]).astype(BF16)


def _split_refs(refs, n_in, bwd):
    ins = refs[:n_in]
    if bwd:
        return ins, refs[n_in:n_in + 3], refs[n_in + 3:]
    o_ref, state_ref = refs[n_in:]
    return ins, (None, None, None), (o_ref, state_ref, o_ref)


def _for_each_chunk(c, n_rows, bwd, body):
    cps = n_rows // c

    def step(s, carry):
        idx = (cps - 1 - s) if bwd else s
        body(pl.ds(pl.multiple_of(idx * c, c), c))
        return carry

    lax.fori_loop(0, cps, step, 0)


def _row_views(rows, o_ref, o_scr, fin, bwd):
    def at(r):
        return None if r is None else r.at[:, rows, :]
    of_ref, gate_ref, ng_ref = fin
    o_here = at(o_ref)
    return (o_scr if bwd else o_here), (at(of_ref), at(gate_ref), ng_ref), o_here


def _hgrn_kernel(*refs, layer, heads, dk, nlev, c, bwd):
    (q_ref, z_ref, v_ref, lbl_ref, w_ref, mask_ref), fin, (o_ref, state_ref, o_scr) = \
        _split_refs(refs, 6, bwd)

    @pl.when(pl.program_id(1) == 0)
    def _():
        state_ref[...] = jnp.zeros_like(state_ref)

    logits = lbl_ref[...]
    ex = jnp.exp(logits - jnp.max(logits, axis=0, keepdims=True))
    p = ex / jnp.sum(ex, axis=0, keepdims=True)
    lb = jnp.sum(p[:layer + 1], axis=0, keepdims=True) - p[0:1]
    nb = q_ref.shape[0]

    def chunk(rows):
        ks_, gs = [], []
        for bb in range(nb):
            z = z_ref[bb, rows, :]
            e = jnp.exp(-jnp.abs(z))
            s_big = 1.0 / (1.0 + e)
            s_small = e * s_big
            sig_pos = jnp.where(z >= 0, s_big, s_small)
            sig_neg = jnp.where(z >= 0, s_small, s_big)
            gs.append(jnp.log(jnp.maximum(lb + (1.0 - lb) * sig_pos, A_MIN_FORGET)))
            ks_.append((1.0 - lb) * sig_neg)
        o_dst, fin_here, o_here = _row_views(rows, o_ref, o_scr, fin, bwd)
        _gated_core([q_ref[bb, rows, :] for bb in range(nb)], ks_,
                    [v_ref[bb, rows, :] for bb in range(nb)], gs,
                    w_ref, mask_ref, o_dst, state_ref, heads, dk, dk, nlev, c, bwd)
        _finish(o_dst, *fin_here, o_here, heads, dk, False, bwd)

    _for_each_chunk(c, q_ref.shape[1], bwd, chunk)


def _gla_kernel(*refs, heads, dk, dv, nlev, c, bwd):
    (q_ref, k_ref, v_ref, lr_ref, wgk_ref, bgk_ref, w_ref, mask_ref), fin, (o_ref, state_ref, o_scr) = \
        _split_refs(refs, 8, bwd)

    @pl.when(pl.program_id(1) == 0)
    def _():
        state_ref[...] = jnp.zeros_like(state_ref)

    nb = q_ref.shape[0]

    def chunk(rows):
        qs, gs = [], []
        for bb in range(nb):
            logit = _dot(lr_ref[bb, rows, :].astype(BF16), wgk_ref[...]) + bgk_ref[...]
            gs.append(_log_sigmoid(logit) / B_GATE_NORM)
            qs.append(q_ref[bb, rows, :] * (dk ** -0.5))
        o_dst, fin_here, o_here = _row_views(rows, o_ref, o_scr, fin, bwd)
        _gated_core(qs, [k_ref[bb, rows, :] for bb in range(nb)],
                    [v_ref[bb, rows, :] for bb in range(nb)], gs,
                    w_ref, mask_ref, o_dst, state_ref, heads, dk, dv, nlev, c, bwd)
        _finish(o_dst, *fin_here, o_here, heads, dv, False, bwd)

    _for_each_chunk(c, q_ref.shape[1], bwd, chunk)


def _chunk_order(n_chunks_ctx, n_chunks, bwd):
    def chunk(n):
        if not bwd:
            return n
        return jnp.where(n < n_chunks_ctx, n_chunks_ctx - 1 - n, n_chunks - 1 - n + n_chunks_ctx)
    return chunk


def _mixer_call(kern, name, proj, in_arrays, in_specs, chunk, nb, rs, c, width, state_shape,
                o_fwd, gate_col, norm_row):
    bsz, l, _ = proj.shape
    bwd = o_fwd is not None
    blk = pl.BlockSpec((nb, rs, width), lambda b, n: (b, chunk(n), 0))
    if bwd:
        in_arrays = in_arrays + [o_fwd, proj, norm_row]
        in_specs = in_specs + [
            blk,
            pl.BlockSpec((nb, rs, width), lambda b, n: (b, chunk(n), gate_col // width)),
            pl.BlockSpec((1, width), lambda b, n: (0, 0)),
        ]
    return pl.pallas_call(
        kern,
        grid=(bsz // nb, l // rs),
        in_specs=in_specs,
        out_specs=blk,
        out_shape=jax.ShapeDtypeStruct((bsz, l, width), BF16 if bwd else F32),
        scratch_shapes=[pltpu.VMEM((nb * state_shape[0],) + state_shape[1:], F32)]
        + ([pltpu.VMEM((nb, c, width), F32)] if bwd else []),
        compiler_params=_cparams(("parallel", "arbitrary")),
        name=name + ("_bwd" if bwd else "_fwd"),
    )(*in_arrays)


def _batch_per_step(bsz):
    return 2 if bsz % 2 == 0 else 1


def _rows_per_step(n_ctx, l):
    return _pick_tile(math.gcd(n_ctx, l - n_ctx), (256, 128, 64))


def hgrn_mixer(proj, col, lb_logits, layer, n_ctx, gate_col, norm_g):
    bsz, l, _ = proj.shape
    depth, _, width = lb_logits.shape
    c = CHUNK
    heads, dk = width // A_HEAD_DIM, A_HEAD_DIM
    cq, cf, ci = (x // width for x in col)
    norm_row = jnp.tile(norm_g, heads).reshape(1, width)
    nb = _batch_per_step(bsz)
    rs = _rows_per_step(n_ctx, l)
    o_fwd = None
    for bwd in (False, True):
        w_np, m_np, nlev = _decay_constants(c, bwd)
        chunk = _chunk_order(n_ctx // rs, l // rs, bwd)
        d = int(bwd)
        kern = functools.partial(_hgrn_kernel, layer=layer, heads=heads, dk=dk, nlev=nlev, c=c, bwd=bwd)
        in_specs = [
            pl.BlockSpec((nb, rs, width), lambda b, n, chunk=chunk: (b, chunk(n), cq)),
            pl.BlockSpec((nb, rs, width), lambda b, n, chunk=chunk, d=d: (b, chunk(n), cf + d)),
            pl.BlockSpec((nb, rs, width), lambda b, n, chunk=chunk: (b, chunk(n), ci)),
            pl.BlockSpec((depth, width), lambda b, n: (0, 0)),
            pl.BlockSpec(w_np.shape, lambda b, n: (0, 0)),
            pl.BlockSpec(m_np.shape, lambda b, n: (0, 0, 0)),
        ]
        in_arrays = [proj, proj, proj, lb_logits[:, d], jnp.asarray(w_np, BF16), jnp.asarray(m_np)]
        o_fwd = _mixer_call(kern, "hgrn_mixer", proj, in_arrays, in_specs, chunk, nb, rs, c, width,
                            (heads, dk, dk), o_fwd, gate_col, norm_row)
    return o_fwd


def gla_mixer(proj, col, wgk_pad, b_gk, n_ctx, width, gate_col, norm_g):
    bsz, l, _ = proj.shape
    c = CHUNK
    key_w = wgk_pad.shape[-1]
    heads = B_HEADS
    dk, dv = key_w // heads, width // heads
    cq, ck, cv, clr = col
    norm_row = jnp.tile(norm_g, heads).reshape(1, width)
    nb = _batch_per_step(bsz)
    rs = _rows_per_step(n_ctx, l)
    o_fwd = None
    for bwd in (False, True):
        w_np, m_np, nlev = _decay_constants(c, bwd)
        chunk = _chunk_order(n_ctx // rs, l // rs, bwd)
        d = int(bwd)
        kern = functools.partial(_gla_kernel, heads=heads, dk=dk, dv=dv, nlev=nlev, c=c, bwd=bwd)
        in_specs = [
            pl.BlockSpec((nb, rs, key_w), lambda b, n, chunk=chunk: (b, chunk(n), cq // key_w)),
            pl.BlockSpec((nb, rs, key_w), lambda b, n, chunk=chunk: (b, chunk(n), ck // key_w)),
            pl.BlockSpec((nb, rs, width), lambda b, n, chunk=chunk: (b, chunk(n), cv // width)),
            pl.BlockSpec((nb, rs, LANES), lambda b, n, chunk=chunk: (b, chunk(n), clr // LANES)),
            pl.BlockSpec((LANES, key_w), lambda b, n: (0, 0)),
            pl.BlockSpec((1, key_w), lambda b, n: (0, 0)),
            pl.BlockSpec(w_np.shape, lambda b, n: (0, 0)),
            pl.BlockSpec(m_np.shape, lambda b, n: (0, 0, 0)),
        ]
        in_arrays = [proj, proj, proj, proj, wgk_pad[d], b_gk[d].reshape(1, key_w),
                     jnp.asarray(w_np, BF16), jnp.asarray(m_np)]
        o_fwd = _mixer_call(kern, "gla_mixer", proj, in_arrays, in_specs, chunk, nb, rs, c, width,
                            (heads, dv, dk), o_fwd, gate_col, norm_row)
    return o_fwd


def _retention_kernel(*refs, heads, dk, dv, c, bwd):
    (q_ref, k_ref, v_ref, cos_ref, sin_ref, dl_ref), fin, (o_ref, state_ref, o_scr) = \
        _split_refs(refs, 6, bwd)

    @pl.when(pl.program_id(1) == 0)
    def _():
        state_ref[...] = jnp.zeros_like(state_ref)

    ii = lax.broadcasted_iota(jnp.int32, (c, c), 0).astype(F32)
    jj = lax.broadcasted_iota(jnp.int32, (c, c), 1).astype(F32)
    rel = (jj - ii) if bwd else (ii - jj)
    t_col = lax.broadcasted_iota(jnp.int32, (c, 1), 0).astype(F32)
    since = ((c - 1.0) - t_col) if bwd else t_col
    log_gamma = _log_sigmoid(dl_ref[...])
    cos, sin = cos_ref[...], sin_ref[...]
    half = dk // 2
    for h in range(heads):
        ks = slice(h * dk, (h + 1) * dk)
        vs = slice(h * dv, (h + 1) * dv)
        lg = log_gamma[:, h:h + 1]
        dmat = jnp.where(rel >= 0, jnp.exp(lg * jnp.maximum(rel, 0.0)), 0.0)
        xi = jnp.exp(lg * (since + 1.0))
        zeta = jnp.exp(lg * ((c - 1.0) - since))
        for bb in range(q_ref.shape[0]):
            qh, kh = q_ref[bb, :, ks], k_ref[bb, :, ks]
            qh = (qh * cos + pltpu.roll(qh, half, 1) * sin) * (dk ** -0.5)
            kh = kh * cos + pltpu.roll(kh, half, 1) * sin
            vb = v_ref[bb, :, vs].astype(BF16)
            st = state_ref[bb * heads + h]
            scores = _dot_nt(qh.astype(BF16), kh.astype(BF16)) * dmat
            o = _dot(scores.astype(BF16), vb) + _dot_nt(qh.astype(BF16), st.astype(BF16)) * xi
            state_ref[bb * heads + h] = st * jnp.exp(lg * c) + _dot_tn(vb, (kh * zeta).astype(BF16))
            o_scr[bb, :, vs] = o
    _finish(o_scr, *fin, o_ref, heads, dv, True, bwd)


def retention_mixer(proj, col, cos_t, sin_t, decay_logit_pad, n_ctx, width, gate_col, norm_g):
    bsz, l, _ = proj.shape
    heads = D_HEADS
    key_w = width // 2
    dk, dv = key_w // heads, width // heads
    c = _rows_per_step(n_ctx, l)
    cq, ck, cv = col
    norm_row = jnp.tile(norm_g, heads).reshape(1, width)
    nb = _batch_per_step(bsz)
    o_fwd = None
    for bwd in (False, True):
        chunk = _chunk_order(n_ctx // c, l // c, bwd)
        kern = functools.partial(_retention_kernel, heads=heads, dk=dk, dv=dv, c=c, bwd=bwd)
        in_specs = [
            pl.BlockSpec((nb, c, key_w), lambda b, n, chunk=chunk: (b, chunk(n), cq // key_w)),
            pl.BlockSpec((nb, c, key_w), lambda b, n, chunk=chunk: (b, chunk(n), ck // key_w)),
            pl.BlockSpec((nb, c, width), lambda b, n, chunk=chunk: (b, chunk(n), cv // width)),
            pl.BlockSpec((c, dk), lambda b, n, chunk=chunk: (chunk(n), 0)),
            pl.BlockSpec((c, dk), lambda b, n, chunk=chunk: (chunk(n), 0)),
            pl.BlockSpec((1, LANES), lambda b, n: (0, 0)),
        ]
        in_arrays = [proj, proj, proj, cos_t, sin_t, decay_logit_pad[int(bwd)]]
        o_fwd = _mixer_call(kern, "retention_mixer", proj, in_arrays, in_specs, chunk, nb, c, c, width,
                            (heads, dv, dk), o_fwd, gate_col, norm_row)
    return o_fwd


def _s5_kernel(u_ref, lam_ref, bt_ref, cm_ref, y_ref,
               toep_ref, win_r_ref, win_i_ref, wout_ref, cl_ref, s_r_ref, s_i_ref, xp_ref, *,
               bsz, n_chunks_ctx, n_chunks):
    t_len, hc, half = S5_CHUNK, C_GROUP, C_STATE
    lane = lax.broadcasted_iota(jnp.int32, (1, LANES), 1)
    lo = lane < half
    sgn = jnp.where(lo, -1.0, 1.0)
    tau = lax.broadcasted_iota(jnp.int32, (t_len, 1), 0)
    u = u_ref[0]
    rows = u.shape[0]
    y_acc = jnp.zeros((rows, t_len * hc), F32)

    def cmul(ar, ai, br, bi):
        return ar * br - ai * bi, ar * bi + ai * br

    def expand(x1, pa, x2, pb, out_ref):
        for t in range(t_len):
            blk = (x1 * jnp.broadcast_to(pa[t:t + 1], (hc, LANES))
                   + x2 * jnp.broadcast_to(pb[t:t + 1], (hc, LANES)))
            out_ref[t * hc:(t + 1) * hc, :] = blk.astype(out_ref.dtype)

    for d in range(2):
        lam_re = jnp.minimum(lam_ref[d, 0, 0:1], C_MAX_RE)
        lam_im = lam_ref[d, 0, 1:2]
        dt = jnp.exp(lam_ref[d, 0, 2:3])
        mag = jnp.exp(lam_re * dt)
        lb_r, lb_i = mag * jnp.cos(lam_im * dt), mag * jnp.sin(lam_im * dt)
        den = lam_re * lam_re + lam_im * lam_im
        nr, ni = lb_r - 1.0, lb_i
        cf_r, cf_i = (nr * lam_re + ni * lam_im) / den, (ni * lam_re - nr * lam_im) / den
        bt_r, bt_i = bt_ref[d, 0, 0], bt_ref[d, 0, 1]
        bb_r, bb_i = cmul(cf_r, cf_i, bt_r, bt_i)
        c_r, c_i = cm_ref[d, 0, 0], cm_ref[d, 0, 1]

        p_r, p_i = jnp.ones((t_len, LANES), F32), jnp.zeros((t_len, LANES), F32)
        q_r, q_i = p_r, p_i
        sq_r, sq_i = lb_r, lb_i
        for bit in range(int(math.log2(t_len))):
            sel = ((tau >> bit) & 1) == 1
            p_r, p_i = cmul(p_r, p_i, jnp.where(sel, sq_r, 1.0), jnp.where(sel, sq_i, 0.0))
            selq = (((t_len - 1 - tau) >> bit) & 1) == 1
            q_r, q_i = cmul(q_r, q_i, jnp.where(selq, sq_r, 1.0), jnp.where(selq, sq_i, 0.0))
            sq_r, sq_i = cmul(sq_r, sq_i, sq_r, sq_i)
        lc_r, lc_i = sq_r, sq_i
        if d == 0:
            toep_p, in_p = (p_r, p_i), (q_r, q_i)
            out_p = cmul(p_r, p_i, lb_r, lb_i)
        else:
            toep_p, in_p = (q_r, q_i), (p_r, p_i)
            out_p = cmul(q_r, q_i, lb_r, lb_i)

        tp_r, tp_i = toep_p
        expand(c_r, jnp.where(lo, tp_r, tp_i), sgn * c_i, jnp.where(lo, tp_i, tp_r), cl_ref)
        bbs = jnp.where(lo, bb_r, -bb_i)
        kt = lax.dot_general(bbs, cl_ref[...], (((1,), (1,)), ((), ())),
                             precision=lax.Precision.HIGHEST, preferred_element_type=F32)
        width = t_len * hc
        glane = lax.broadcasted_iota(jnp.int32, (hc, width), 1)
        per_tile = LANES // hc
        for m in range(per_tile):
            if d == 0:
                base = kt if m == 0 else jnp.where(glane >= hc * m, pltpu.roll(kt, hc * m, 1), 0.0)
            else:
                base = kt if m == 0 else jnp.where(glane < width - hc * m,
                                                   pltpu.roll(kt, width - hc * m, 1), 0.0)
            base = base.astype(BF16)
            for a in range(t_len // per_tile):
                off = a * LANES
                if d == 0:
                    j = a * per_tile + m
                    if off:
                        toep_ref[d, j * hc:(j + 1) * hc, :off] = jnp.zeros((hc, off), BF16)
                    toep_ref[d, j * hc:(j + 1) * hc, off:] = base[:, :width - off]
                else:
                    j = t_len - 1 - (a * per_tile + m)
                    if off:
                        toep_ref[d, j * hc:(j + 1) * hc, width - off:] = jnp.zeros((hc, off), BF16)
                    toep_ref[d, j * hc:(j + 1) * hc, :width - off] = base[:, off:]

        ip_r, ip_i = in_p
        expand(bb_r, ip_r, -bb_i, ip_i, win_r_ref)
        expand(bb_i, ip_r, bb_r, ip_i, win_i_ref)
        op_r, op_i = out_p
        expand(c_r, jnp.where(lo, op_r, -op_i), c_i, jnp.where(lo, -op_i, -op_r), wout_ref)

        s_r_ref[...] = _dot(u, win_r_ref[...])
        s_i_ref[...] = _dot(u, win_i_ref[...])
        if d == 0:
            order = list(range(n_chunks))
        else:
            order = list(range(n_chunks_ctx - 1, -1, -1)) + list(range(n_chunks - 1, n_chunks_ctx - 1, -1))
        x_r, x_i = jnp.zeros((bsz, LANES), F32), jnp.zeros((bsz, LANES), F32)
        for n in order:
            rs = slice(n * bsz, (n + 1) * bsz)
            xp_ref[rs, :] = jnp.where(lo, x_r, x_i)
            nx_r, nx_i = cmul(lc_r, lc_i, x_r, x_i)
            x_r, x_i = nx_r + s_r_ref[rs, :], nx_i + s_i_ref[rs, :]

        y_acc = y_acc + _dot(u, toep_ref[d]) + _dot_nt(xp_ref[...].astype(BF16), wout_ref[...])
    y_ref[0] = y_acc


def s5_core(ug, lam_pk, bt_pk, cm_pk, bsz, n_chunks_ctx, n_chunks):
    groups, rows, width = ug.shape
    kern = functools.partial(_s5_kernel, bsz=bsz, n_chunks_ctx=n_chunks_ctx, n_chunks=n_chunks)
    return pl.pallas_call(
        kern,
        grid=(groups,),
        in_specs=[
            pl.BlockSpec((1, rows, width), lambda g: (g, 0, 0)),
            pl.BlockSpec((2, 1, SUBLANES, LANES), lambda g: (0, g, 0, 0)),
            pl.BlockSpec((2, 1, 2, C_GROUP, LANES), lambda g: (0, g, 0, 0, 0)),
            pl.BlockSpec((2, 1, 2, C_GROUP, LANES), lambda g: (0, g, 0, 0, 0)),
        ],
        out_specs=pl.BlockSpec((1, rows, width), lambda g: (g, 0, 0)),
        out_shape=jax.ShapeDtypeStruct((groups, rows, width), F32),
        scratch_shapes=[
            pltpu.VMEM((2, width, width), BF16),
            pltpu.VMEM((width, LANES), BF16),
            pltpu.VMEM((width, LANES), BF16),
            pltpu.VMEM((width, LANES), BF16),
            pltpu.VMEM((width, LANES), F32),
            pltpu.VMEM((rows, LANES), F32),
            pltpu.VMEM((rows, LANES), F32),
            pltpu.VMEM((rows, LANES), F32),
        ],
        compiler_params=_cparams(("parallel",)),
        name="s5_core",
    )(ug, lam_pk, bt_pk, cm_pk)


def _s5_post_kernel(y_ref, u_ref, gate_ref, d_ref, w_ref, b_ref, o_ref):
    y = y_ref[0] + d_ref[...] * u_ref[0]
    z = jax.nn.gelu(y)
    t = _dot(z.astype(BF16), w_ref[...]) + b_ref[...]
    o_ref[0] = (z * _sigmoid(t) * _silu(gate_ref[0])).astype(BF16)


def s5_post(y, proj, u_col, gate_col, d_skip, w_glu, b_glu):
    bsz, l, width = y.shape
    tr = ROW_TILE
    return pl.pallas_call(
        _s5_post_kernel,
        grid=(bsz, l // tr),
        in_specs=[
            pl.BlockSpec((1, tr, width), lambda b, i: (b, i, 0)),
            pl.BlockSpec((1, tr, width), lambda b, i: (b, i, u_col // width)),
            pl.BlockSpec((1, tr, width), lambda b, i: (b, i, gate_col // width)),
            pl.BlockSpec((1, width), lambda b, i: (0, 0)),
            pl.BlockSpec((width, width), lambda b, i: (0, 0)),
            pl.BlockSpec((1, width), lambda b, i: (0, 0)),
        ],
        out_specs=pl.BlockSpec((1, tr, width), lambda b, i: (b, i, 0)),
        out_shape=jax.ShapeDtypeStruct((bsz, l, width), BF16),
        compiler_params=_cparams(("parallel", "parallel")),
        name="s5_post",
    )(y, proj, proj, d_skip.reshape(1, width), w_glu.astype(BF16), b_glu.reshape(1, width))


def _outproj_kernel(oa_ref, ob_ref, oc_ref, od_ref, w_ref, h_ref, gl_ref, gc_ref, o_ref, *,
                    n_ctx, tm, bw):
    acc = _dot(oa_ref[0], w_ref[0, 0:bw, :].astype(BF16))
    acc = acc + _dot(ob_ref[0], w_ref[0, bw:2 * bw, :].astype(BF16))
    acc = acc + _dot(oc_ref[0], w_ref[0, 2 * bw:3 * bw, :].astype(BF16))
    acc = acc + _dot(od_ref[0], w_ref[0, 3 * bw:4 * bw, :].astype(BF16))
    row = pl.program_id(1) * tm + lax.broadcasted_iota(jnp.int32, (tm, 1), 0)
    gate = jnp.where(row < n_ctx, gc_ref[0], gl_ref[0])
    o_ref[0] = h_ref[0] + gate * acc


def out_projection(o_parts, w_out, h, mods_flat, layer, n_ctx):
    bsz, l, d = h.shape
    bw = o_parts[0].shape[-1]
    tm = l if l <= 2048 else _pick_tile(l, (l // 4, l // 8, l // 16))
    tn = _pick_tile(d, (512, 256, 128))
    kern = functools.partial(_outproj_kernel, n_ctx=n_ctx, tm=tm, bw=bw)
    o_spec = pl.BlockSpec((1, tm, bw), lambda b, i, j: (b, i, 0))
    return pl.pallas_call(
        kern,
        grid=(bsz, l // tm, d // tn),
        in_specs=[
            o_spec, o_spec, o_spec, o_spec,
            pl.BlockSpec((1, 4 * bw, tn), lambda b, i, j: (layer, 0, j)),
            pl.BlockSpec((1, tm, tn), lambda b, i, j: (b, i, j)),
            pl.BlockSpec((1, 1, tn), lambda b, i, j: ((layer * SUBLANES + b) * 3 + 2, 0, j)),
            pl.BlockSpec((1, 1, tn), lambda b, i, j: ((layer * SUBLANES + bsz) * 3 + 2, 0, j)),
        ],
        out_specs=pl.BlockSpec((1, tm, tn), lambda b, i, j: (b, i, j)),
        out_shape=jax.ShapeDtypeStruct((bsz, l, d), F32),
        compiler_params=_cparams(("parallel", "parallel", "parallel")),
        name="out_projection",
    )(*o_parts, w_out, h, mods_flat, mods_flat)


def _final_norm_kernel(x_ref, g_ref, o_ref):
    x = x_ref[0]
    o_ref[0] = x * lax.rsqrt(jnp.mean(x * x, axis=-1, keepdims=True) + EPS) * g_ref[...]


def final_norm(h, g, n_ctx):
    bsz, l, d = h.shape
    tr = ROW_TILE
    skip = n_ctx // tr
    return pl.pallas_call(
        _final_norm_kernel,
        grid=(bsz, (l - n_ctx) // tr),
        in_specs=[
            pl.BlockSpec((1, tr, d), lambda b, i: (b, i + skip, 0)),
            pl.BlockSpec((1, d), lambda b, i: (0, 0)),
        ],
        out_specs=pl.BlockSpec((1, tr, d), lambda b, i: (b, i, 0)),
        out_shape=jax.ShapeDtypeStruct((bsz, l - n_ctx, d), F32),
        compiler_params=_cparams(("parallel", "parallel")),
        name="final_norm",
    )(h, g.reshape(1, d))


def _rope_tables(rows, n_ctx, dk):
    quarter = dk // 4
    freqs = ROPE_BASE ** (-jnp.arange(quarter, dtype=F32) / quarter)
    t = jnp.arange(rows * GRID_W)
    r = (t // GRID_W).astype(F32)
    col = (t % GRID_W).astype(F32)
    ang = jnp.concatenate([r[:, None] * freqs, col[:, None] * freqs], axis=-1)
    ang = jnp.concatenate([jnp.zeros((n_ctx, dk // 2), F32), ang], axis=0)
    cos, sin = jnp.cos(ang), jnp.sin(ang)
    return jnp.concatenate([cos, cos], axis=-1), jnp.concatenate([-sin, sin], axis=-1)


def _dup(x):
    return jnp.concatenate([x, x], axis=-1)


def mixer_layer(xn, h, mods_flat, layer, n_ctx, w_in_perm, hgrn_lb_logits, hgrn_norm_g, gla_w_gk,
                gla_b_gk, gla_norm_g, s5_lam_re, s5_lam_im, s5_log_dt, s5_b_re, s5_b_im, s5_c_re,
                s5_c_im, s5_d, s5_w_glu, s5_b_glu, ret_decay_logit, ret_norm_g, w_out, rope):
    bsz, l, d = h.shape
    bw = d // 4
    kw = bw // 2
    rank = B_GATE_RANK

    names = ("a_q", "a_ff", "a_fb", "a_i", "a_g", "b_q", "b_k", "b_v", "b_g", "c_u", "c_g",
             "d_q", "d_k", "d_v", "d_g", "b_lr")
    widths = (bw, bw, bw, bw, bw, kw, kw, bw, bw, bw, bw, kw, kw, bw, bw, LANES)
    col = dict(zip(names, np.concatenate([[0], np.cumsum(widths)[:-1]]).tolist()))

    proj = in_projection(xn.reshape(bsz * l, d), w_in_perm, layer).reshape(bsz, l, -1)

    o_a = hgrn_mixer(proj, (col["a_q"], col["a_ff"], col["a_i"]), hgrn_lb_logits, layer, n_ctx,
                     col["a_g"], hgrn_norm_g[layer])

    wgk = gla_w_gk[layer].astype(BF16)
    wgk_pad = jnp.zeros((2, LANES, kw), BF16)
    wgk_pad = wgk_pad.at[0, :rank].set(wgk[0]).at[1, rank:2 * rank].set(wgk[1])
    o_b = gla_mixer(proj, (col["b_q"], col["b_k"], col["b_v"], col["b_lr"]), wgk_pad,
                    gla_b_gk[layer], n_ctx, bw, col["b_g"], gla_norm_g[layer])

    groups = bw // C_GROUP
    nck = l // S5_CHUNK
    u = proj[:, :, col["c_u"]:col["c_u"] + bw].astype(BF16)
    ug = u.reshape(bsz, nck, S5_CHUNK, groups, C_GROUP).transpose(3, 1, 0, 2, 4)
    ug = ug.reshape(groups, nck * bsz, S5_CHUNK * C_GROUP)
    dt_row = jnp.broadcast_to(s5_log_dt[layer][..., None], (2, groups, C_STATE))
    lam_pk = jnp.stack([_dup(s5_lam_re[layer]), _dup(s5_lam_im[layer]), _dup(dt_row)], axis=2)
    lam_pk = jnp.pad(lam_pk, ((0, 0), (0, 0), (0, SUBLANES - 3), (0, 0)))
    bt_pk = jnp.stack([_dup(jnp.swapaxes(s5_b_re[layer], -1, -2)),
                       _dup(jnp.swapaxes(s5_b_im[layer], -1, -2))], axis=2)
    cm_pk = jnp.stack([_dup(s5_c_re[layer]), _dup(s5_c_im[layer])], axis=2)
    yg = s5_core(ug, lam_pk, bt_pk, cm_pk, bsz, n_ctx // S5_CHUNK, nck)
    y = yg.reshape(groups, nck, bsz, S5_CHUNK, C_GROUP).transpose(2, 1, 3, 0, 4).reshape(bsz, l, bw)
    o_c = s5_post(y, proj, col["c_u"], col["c_g"], s5_d[layer], s5_w_glu[layer], s5_b_glu[layer])

    dl = jnp.pad(ret_decay_logit[layer], ((0, 0), (0, LANES - D_HEADS))).reshape(2, 1, LANES)
    o_d = retention_mixer(proj, (col["d_q"], col["d_k"], col["d_v"]), rope[0], rope[1], dl, n_ctx, bw,
                          col["d_g"], ret_norm_g[layer])

    return out_projection((o_a, o_b, o_c, o_d), w_out, h, mods_flat, layer, n_ctx)


def kernel(x, c, ctx, c_ctx, norm_g, w_ada, b_ada, w_in, hgrn_lb_logits, hgrn_norm_g, gla_w_gk,
           gla_b_gk, gla_norm_g, s5_lam_re, s5_lam_im, s5_log_dt, s5_b_re, s5_b_im, s5_c_re, s5_c_im,
           s5_d, s5_w_glu, s5_b_glu, ret_decay_logit, ret_norm_g, w_out, final_norm_g):
    bsz, seq, d = x.shape
    n_ctx = ctx.shape[1]
    depth = w_in.shape[0]
    assert bsz < SUBLANES and n_ctx % ROW_TILE == 0 and seq % ROW_TILE == 0

    cvec = jnp.concatenate([c, c_ctx[None], jnp.zeros((SUBLANES - bsz - 1, d), F32)], axis=0)
    mods = ada_modulation(cvec, w_ada, b_ada)
    mods_flat = mods.reshape(depth * SUBLANES * 3, 1, d)
    rope = _rope_tables(seq // GRID_W, n_ctx, (d // 8) // D_HEADS)

    w_in_perm = permute_in_weights(w_in)
    h = None
    for layer in range(depth):
        if layer == 0:
            xn, h = prenorm_first(ctx, x, norm_g[0], mods_flat, n_ctx)
        else:
            xn = prenorm(h, norm_g[layer], mods_flat, layer, n_ctx)
        h = mixer_layer(xn, h, mods_flat, layer, n_ctx, w_in_perm, hgrn_lb_logits, hgrn_norm_g,
                        gla_w_gk, gla_b_gk, gla_norm_g, s5_lam_re, s5_lam_im, s5_log_dt, s5_b_re,
                        s5_b_im, s5_c_re, s5_c_im, s5_d, s5_w_glu, s5_b_glu, ret_decay_logit,
                        ret_norm_g, w_out, rope)
    return final_norm(h, final_norm_g, n_ctx)
```

```python
import functools
import math

import numpy as np
import jax
import jax.numpy as jnp
from jax import lax
from jax.experimental import pallas as pl
from jax.experimental.pallas import tpu as pltpu

F32 = jnp.float32
BF16 = jnp.bfloat16

EPS = 1e-6
A_HEAD_DIM = 128
A_MIN_FORGET = 1e-6
B_HEADS = 4
B_GATE_RANK = 16
B_GATE_NORM = 16.0
C_GROUP = 16
C_STATE = 64
C_MAX_RE = -1e-4
D_HEADS = 4
GRID_W = 64
ROPE_BASE = 10000.0

LANES = 128
SUBLANES = 8
BF16_ROWS = 16
VMEM_LIMIT = 56 * 1024 * 1024

MXU_COLS = 256
IN_PROJ_TN = 6 * MXU_COLS

CHUNK = 64
LOG2E = math.log2(math.e)
SHORT_SPAN = 86.0
S5_CHUNK = 64
ROW_TILE = 256


def _cparams(sem):
    return pltpu.CompilerParams(dimension_semantics=sem, vmem_limit_bytes=VMEM_LIMIT)


def _dot(a, b):
    return jnp.dot(a, b, preferred_element_type=F32)


def _dot_nt(a, b):
    return lax.dot_general(a, b, (((1,), (1,)), ((), ())), preferred_element_type=F32)


def _dot_tn(a, b):
    return lax.dot_general(a, b, (((0,), (0,)), ((), ())), preferred_element_type=F32)


def _sigmoid(x):
    return 1.0 / (1.0 + jnp.exp(-x))


def _silu(x):
    return x * _sigmoid(x)


def _log_sigmoid(x):
    return jnp.minimum(x, 0.0) - jnp.log(1.0 + jnp.exp(-jnp.abs(x)))


def _ada_kernel(c_ref, w_ref, b_ref, o_ref):
    cv = _silu(c_ref[...]).astype(BF16)
    o_ref[0] = _dot(cv, w_ref[0].astype(BF16)) + b_ref[0]


def ada_modulation(cvec, w_ada, b_ada):
    depth, d, n3 = w_ada.shape
    tn = 512
    return pl.pallas_call(
        _ada_kernel,
        grid=(depth, n3 // tn),
        in_specs=[
            pl.BlockSpec((SUBLANES, d), lambda l, j: (0, 0)),
            pl.BlockSpec((1, d, tn), lambda l, j: (l, 0, j)),
            pl.BlockSpec((1, 1, tn), lambda l, j: (l, 0, j)),
        ],
        out_specs=pl.BlockSpec((1, SUBLANES, tn), lambda l, j: (l, 0, j)),
        out_shape=jax.ShapeDtypeStruct((depth, SUBLANES, n3), F32),
        compiler_params=_cparams(("parallel", "parallel")),
        name="ada_modulation",
    )(cvec, w_ada, b_ada.reshape(depth, 1, n3))


def _prenorm_kernel(x_ref, g_ref, sh_ref, sc_ref, o_ref):
    x = x_ref[0]
    y = x * lax.rsqrt(jnp.mean(x * x, axis=-1, keepdims=True) + EPS) * g_ref[...]
    o_ref[0] = (y * (1.0 + sc_ref[0]) + sh_ref[0]).astype(BF16)


def prenorm(h, g, mods_flat, layer, n_ctx):
    bsz, l, d = h.shape
    tr = ROW_TILE
    nct = n_ctx // tr

    def mod_map(part):
        def f(b, i):
            row = jnp.where(i < nct, bsz, b)
            return ((layer * SUBLANES + row) * 3 + part, 0, 0)
        return f

    return pl.pallas_call(
        _prenorm_kernel,
        grid=(bsz, l // tr),
        in_specs=[
            pl.BlockSpec((1, tr, d), lambda b, i: (b, i, 0)),
            pl.BlockSpec((1, d), lambda b, i: (0, 0)),
            pl.BlockSpec((1, 1, d), mod_map(0)),
            pl.BlockSpec((1, 1, d), mod_map(1)),
        ],
        out_specs=pl.BlockSpec((1, tr, d), lambda b, i: (b, i, 0)),
        out_shape=jax.ShapeDtypeStruct((bsz, l, d), BF16),
        compiler_params=_cparams(("parallel", "parallel")),
        name="prenorm",
    )(h, g.reshape(1, d), mods_flat, mods_flat)


def _prenorm_first_kernel(c_ref, x_ref, g_ref, sh_ref, sc_ref, o_ref, h_ref, *, nct):
    x = jnp.where(pl.program_id(1) < nct, c_ref[0], x_ref[0])
    h_ref[0] = x
    y = x * lax.rsqrt(jnp.mean(x * x, axis=-1, keepdims=True) + EPS) * g_ref[...]
    o_ref[0] = (y * (1.0 + sc_ref[0]) + sh_ref[0]).astype(BF16)


def prenorm_first(ctx, x, g, mods_flat, n_ctx):
    bsz, seq, d = x.shape
    l = n_ctx + seq
    tr = ROW_TILE
    nct = n_ctx // tr

    def mod_map(part):
        def f(b, i):
            row = jnp.where(i < nct, bsz, b)
            return (row * 3 + part, 0, 0)
        return f

    row_spec = pl.BlockSpec((1, tr, d), lambda b, i: (b, i, 0))
    return pl.pallas_call(
        functools.partial(_prenorm_first_kernel, nct=nct),
        grid=(bsz, l // tr),
        in_specs=[
            pl.BlockSpec((1, tr, d), lambda b, i: (b, jnp.minimum(i, nct - 1), 0)),
            pl.BlockSpec((1, tr, d), lambda b, i: (b, jnp.maximum(i - nct, 0), 0)),
            pl.BlockSpec((1, d), lambda b, i: (0, 0)),
            pl.BlockSpec((1, 1, d), mod_map(0)),
            pl.BlockSpec((1, 1, d), mod_map(1)),
        ],
        out_specs=[row_spec, row_spec],
        out_shape=[jax.ShapeDtypeStruct((bsz, l, d), BF16), jax.ShapeDtypeStruct((bsz, l, d), F32)],
        compiler_params=_cparams(("parallel", "arbitrary")),
        name="prenorm_first",
    )(ctx, x, g.reshape(1, d), mods_flat, mods_flat)


def _wprep_kernel(a_ref, b_ref, o_ref, *, first_shifted, tail_block, lr_w):
    i = pl.program_id(1)
    tr = a_ref.shape[1]

    @pl.when(i < first_shifted)
    def _():
        o_ref[0] = a_ref[0].astype(BF16)

    @pl.when(jnp.logical_and(i >= first_shifted, i < tail_block))
    def _():
        o_ref[0, :tr - lr_w] = a_ref[0, lr_w:].astype(BF16)
        o_ref[0, tr - lr_w:] = b_ref[0].astype(BF16)

    @pl.when(i == tail_block)
    def _():
        o_ref[0, :lr_w] = b_ref[0].astype(BF16)
        o_ref[0, lr_w:] = jnp.zeros((tr - lr_w, a_ref.shape[2]), BF16)

    @pl.when(i > tail_block)
    def _():
        o_ref[0] = jnp.zeros(o_ref.shape[1:], BF16)


def permute_in_weights(w_in):
    depth, d, n = w_in.shape
    bw, lr_w = d // 4, 2 * B_GATE_RANK
    lr0 = 5 * bw + 2 * (bw // 2) + bw
    tr = 2 * LANES
    assert (n - lr_w) % tr == 0 and lr0 % tr == 0 and tr % lr_w == 0
    w_t = jnp.swapaxes(w_in, 1, 2)
    first_shifted, tail_block = lr0 // tr, (n - lr_w) // tr
    per = tr // lr_w
    n_out = -(-(tail_block + 1) * tr // IN_PROJ_TN) * IN_PROJ_TN

    def a_map(l, i):
        return (l, jnp.minimum(i, tail_block - 1), 0)

    def b_map(l, i):
        return (l, jnp.where(i == tail_block, lr0 // lr_w,
                             jnp.minimum(i + 1, tail_block) * per), 0)

    return pl.pallas_call(
        functools.partial(_wprep_kernel, first_shifted=first_shifted, tail_block=tail_block, lr_w=lr_w),
        grid=(depth, n_out // tr),
        in_specs=[pl.BlockSpec((1, tr, d), a_map), pl.BlockSpec((1, lr_w, d), b_map)],
        out_specs=pl.BlockSpec((1, tr, d), lambda l, i: (l, i, 0)),
        out_shape=jax.ShapeDtypeStruct((depth, n_out, d), BF16),
        compiler_params=_cparams(("parallel", "parallel")),
        name="permute_in_weights",
    )(w_t, w_t)


def _matmul_kernel(x_ref, wt_ref, o_ref):
    o_ref[...] = _dot_nt(x_ref[...], wt_ref[0])


def _pick_tile(n, candidates):
    for c in candidates:
        if n % c == 0:
            return c
    return n


def in_projection(xn, w_all, layer):
    m, d = xn.shape
    n = w_all.shape[1]
    tm = _pick_tile(m, (1024, 512, 256, 128, 64, 32, 16))
    tn = IN_PROJ_TN
    return pl.pallas_call(
        _matmul_kernel,
        grid=(m // tm, n // tn),
        in_specs=[
            pl.BlockSpec((tm, d), lambda i, j: (i, 0), pipeline_mode=pl.Buffered(1)),
            pl.BlockSpec((1, tn, d), lambda i, j: (layer, j, 0)),
        ],
        out_specs=pl.BlockSpec((tm, tn), lambda i, j: (i, j)),
        out_shape=jax.ShapeDtypeStruct((m, n), F32),
        compiler_params=_cparams(("parallel", "parallel")),
        name="in_projection",
    )(xn, w_all)


def _decay_constants(c, bwd):
    nlev = int(math.log2(c))
    w = np.zeros((nlev + 2, c, c), np.float32)
    masks = np.zeros((nlev, c, c), np.float32)
    for lev in range(nlev):
        s = 1 << lev
        for r in range(c):
            pos = r % (2 * s)
            mid = r - pos + s
            if pos >= s:
                w[lev, r, mid:r + 1] = 1.0
            else:
                w[lev, r, r + 1:mid] = 1.0
        for i in range(c):
            for j in range(c):
                if i // (2 * s) == j // (2 * s) and i % (2 * s) >= s and j % (2 * s) < s:
                    masks[lev, i, j] = 1.0
    for r in range(c):
        w[nlev, r, :r + 1] = 1.0
        w[nlev + 1, r, r + 1:] = 1.0
    half = c // 2
    ii, jj = np.meshgrid(np.arange(c), np.arange(c), indexing="ij")
    diag = ((ii // half == jj // half) & (jj <= ii)).astype(np.float32)
    masks = np.concatenate([masks, diag[None]], axis=0)
    w = w.reshape((nlev + 2) * c, c)
    w = np.concatenate([w, np.ones((BF16_ROWS, c), np.float32)], axis=0)
    if bwd:
        w = np.concatenate([w[:-BF16_ROWS].reshape(nlev + 2, c, c)[:, ::-1, ::-1].reshape(-1, c),
                            w[-BF16_ROWS:]], axis=0)
        masks = masks[:, ::-1, ::-1]
    return np.ascontiguousarray(w), np.ascontiguousarray(masks), nlev


def _split2(x):
    hi = x.astype(BF16)
    return hi, (x - hi.astype(F32)).astype(BF16)


def _gated_core(qs, ks_, vs_, gs, w_ref, mask_ref, o_scr, state_ref, heads, dk, dv, nlev, c, bwd):
    half = c // 2
    nb = len(qs)
    w_in = w_ref[nlev * c:(nlev + 1) * c, :]
    first_a, last_a, first_b, last_b = (half - 1, 0, c - 1, half) if bwd else (0, half - 1, half, c - 1)
    top_ref, exit_row = (half, 0) if bwd else (half - 1, c - 1)
    g_parts, b_ins, short = [], [], None
    for g in gs:
        parts = _split2(g)
        b_in = _dot(w_in, parts[0]) + _dot(w_in, parts[1])
        span = jnp.minimum(b_in[last_a:last_a + 1] - b_in[first_a:first_a + 1],
                           b_in[last_b:last_b + 1] - b_in[first_b:first_b + 1])
        ok = jnp.min(span) >= -SHORT_SPAN
        short = ok if short is None else jnp.logical_and(short, ok)
        g_parts.append(parts)
        b_ins.append(b_in)

    def head_update(bb, h, scores, e_in, e_out, e_tot, extra):
        ks = slice(h * dk, (h + 1) * dk)
        vs = slice(h * dv, (h + 1) * dv)
        qh, kh, vh = qs[bb][:, ks], ks_[bb][:, ks], vs_[bb][:, vs]
        vb = vh.astype(BF16)
        st = state_ref[bb * heads + h]
        o = _dot_nt((qh * e_in).astype(BF16), st.astype(BF16)) + _dot(scores.astype(BF16), vb)
        if extra is not None:
            o = o + extra * vh
        state_ref[bb * heads + h] = st * e_tot + _dot_tn(vb, (kh * e_out).astype(BF16))
        o_scr[bb, :, vs] = o

    @pl.when(short)
    def _():
        row = lax.broadcasted_iota(jnp.int32, (c, 1), 0)
        in_a = row < half
        later = in_a if bwd else jnp.logical_not(in_a)
        for bb in range(nb):
            b_in = b_ins[bb]
            m = jnp.where(in_a, b_in[first_a:first_a + 1], b_in[first_b:first_b + 1])
            fq = jnp.exp2(b_in - m)
            fk = jnp.exp2(m - b_in)
            r1 = b_in[top_ref:top_ref + 1]
            ft = jnp.exp2(jnp.where(later, b_in - r1, r1 - b_in))
            tot = b_in[exit_row:exit_row + 1]
            e_in_all = jnp.exp2(b_in)
            e_out_all = jnp.exp2(tot - b_in)
            e_tot_all = jnp.exp2(tot)
            for h in range(heads):
                ks = slice(h * dk, (h + 1) * dk)
                qh, kh = qs[bb][:, ks], ks_[bb][:, ks]
                scores = (mask_ref[nlev] * _dot_nt((qh * fq[:, ks]).astype(BF16),
                                                   (kh * fk[:, ks]).astype(BF16))
                          + mask_ref[nlev - 1] * _dot_nt((qh * ft[:, ks]).astype(BF16),
                                                         (kh * ft[:, ks]).astype(BF16)))
                head_update(bb, h, scores, e_in_all[:, ks], e_out_all[:, ks], e_tot_all[:, ks], None)

    @pl.when(jnp.logical_not(short))
    def _():
        w = w_ref[...]
        for bb in range(nb):
            parts = g_parts[bb]
            e_all = jnp.exp2(_dot(w, parts[0]) + _dot(w, parts[1]))
            for h in range(heads):
                ks = slice(h * dk, (h + 1) * dk)
                qh, kh = qs[bb][:, ks], ks_[bb][:, ks]
                scores = jnp.zeros((c, c), F32)
                for lev in range(nlev):
                    f = e_all[lev * c:(lev + 1) * c, ks]
                    scores = scores + mask_ref[lev] * _dot_nt((qh * f).astype(BF16), (kh * f).astype(BF16))
                head_update(bb, h, scores, e_all[nlev * c:(nlev + 1) * c, ks],
                            e_all[(nlev + 1) * c:(nlev + 2) * c, ks],
                            e_all[(nlev + 2) * c:(nlev + 2) * c + 1, ks],
                            jnp.sum(qh * kh, axis=-1, keepdims=True))


def _finish(o_scr, of_ref, gate_ref, ng_ref, o_ref, heads, hd, center, final):
    if not final:
        return
    for bb in range(o_scr.shape[0]):
        gs = _silu(gate_ref[bb])
        for h in range(heads):
            sl = slice(h * hd, (h + 1) * hd)
            x = o_scr[bb, :, sl] + of_ref[bb, :, sl]
            if center:
                x = x - jnp.mean(x, axis=-1, keepdims=True)
            y = x * lax.rsqrt(jnp.mean(x * x, axis=-1, keepdims=True) + EPS) * ng_ref[:, sl]
            o_ref[bb, :, sl] = (y * gs[:, sl]).astype(BF16)


def _split_refs(refs, n_in, bwd):
    ins = refs[:n_in]
    if bwd:
        return ins, refs[n_in:n_in + 3], refs[n_in + 3:]
    o_ref, state_ref = refs[n_in:]
    return ins, (None, None, None), (o_ref, state_ref, o_ref)


def _for_each_chunk(c, n_rows, bwd, body):
    cps = n_rows // c

    def step(s, carry):
        idx = (cps - 1 - s) if bwd else s
        body(pl.ds(pl.multiple_of(idx * c, c), c))
        return carry

    lax.fori_loop(0, cps, step, 0)


def _row_views(rows, o_ref, o_scr, fin, bwd):
    def at(r):
        return None if r is None else r.at[:, rows, :]
    of_ref, gate_ref, ng_ref = fin
    o_here = at(o_ref)
    return (o_scr if bwd else o_here), (at(of_ref), at(gate_ref), ng_ref), o_here


def _hgrn_kernel(*refs, layer, heads, dk, nlev, c, bwd):
    (q_ref, z_ref, v_ref, lbl_ref, w_ref, mask_ref), fin, (o_ref, state_ref, o_scr) = \
        _split_refs(refs, 6, bwd)

    @pl.when(pl.program_id(1) == 0)
    def _():
        state_ref[...] = jnp.zeros_like(state_ref)

    logits = lbl_ref[...]
    ex = jnp.exp(logits - jnp.max(logits, axis=0, keepdims=True))
    p = ex / jnp.sum(ex, axis=0, keepdims=True)
    lb = jnp.sum(p[:layer + 1], axis=0, keepdims=True) - p[0:1]
    nb = q_ref.shape[0]

    def chunk(rows):
        ks_, gs = [], []
        for bb in range(nb):
            z = z_ref[bb, rows, :]
            e = jnp.exp(-jnp.abs(z))
            s_big = 1.0 / (1.0 + e)
            s_small = e * s_big
            sig_pos = jnp.where(z >= 0, s_big, s_small)
            sig_neg = jnp.where(z >= 0, s_small, s_big)
            gs.append(jnp.log2(jnp.maximum(lb + (1.0 - lb) * sig_pos, A_MIN_FORGET)))
            ks_.append((1.0 - lb) * sig_neg)
        o_dst, fin_here, o_here = _row_views(rows, o_ref, o_scr, fin, bwd)
        _gated_core([q_ref[bb, rows, :] for bb in range(nb)], ks_,
                    [v_ref[bb, rows, :] for bb in range(nb)], gs,
                    w_ref, mask_ref, o_dst, state_ref, heads, dk, dk, nlev, c, bwd)
        _finish(o_dst, *fin_here, o_here, heads, dk, False, bwd)

    _for_each_chunk(c, q_ref.shape[1], bwd, chunk)


def _gla_kernel(*refs, heads, dk, dv, nlev, c, bwd):
    (q_ref, k_ref, v_ref, lr_ref, wgk_ref, bgk_ref, w_ref, mask_ref), fin, (o_ref, state_ref, o_scr) = \
        _split_refs(refs, 8, bwd)

    @pl.when(pl.program_id(1) == 0)
    def _():
        state_ref[...] = jnp.zeros_like(state_ref)

    nb = q_ref.shape[0]

    def chunk(rows):
        qs, gs = [], []
        for bb in range(nb):
            logit = _dot(lr_ref[bb, rows, :].astype(BF16), wgk_ref[...]) + bgk_ref[...]
            t = logit * LOG2E
            gs.append((jnp.minimum(t, 0.0) - jnp.log2(1.0 + jnp.exp2(-jnp.abs(t)))) * (1.0 / B_GATE_NORM))
            qs.append(q_ref[bb, rows, :] * (dk ** -0.5))
        o_dst, fin_here, o_here = _row_views(rows, o_ref, o_scr, fin, bwd)
        _gated_core(qs, [k_ref[bb, rows, :] for bb in range(nb)],
                    [v_ref[bb, rows, :] for bb in range(nb)], gs,
                    w_ref, mask_ref, o_dst, state_ref, heads, dk, dv, nlev, c, bwd)
        _finish(o_dst, *fin_here, o_here, heads, dv, False, bwd)

    _for_each_chunk(c, q_ref.shape[1], bwd, chunk)


def _chunk_order(n_chunks_ctx, n_chunks, bwd):
    def chunk(n):
        if not bwd:
            return n
        return jnp.where(n < n_chunks_ctx, n_chunks_ctx - 1 - n, n_chunks - 1 - n + n_chunks_ctx)
    return chunk


def _mixer_call(kern, name, proj, in_arrays, in_specs, chunk, nb, rs, c, width, state_shape,
                o_fwd, gate_col, norm_row):
    bsz, l, _ = proj.shape
    bwd = o_fwd is not None
    blk = pl.BlockSpec((nb, rs, width), lambda b, n: (b, chunk(n), 0))
    if bwd:
        in_arrays = in_arrays + [o_fwd, proj, norm_row]
        in_specs = in_specs + [
            blk,
            pl.BlockSpec((nb, rs, width), lambda b, n: (b, chunk(n), gate_col // width)),
            pl.BlockSpec((1, width), lambda b, n: (0, 0)),
        ]
    return pl.pallas_call(
        kern,
        grid=(bsz // nb, l // rs),
        in_specs=in_specs,
        out_specs=blk,
        out_shape=jax.ShapeDtypeStruct((bsz, l, width), BF16 if bwd else F32),
        scratch_shapes=[pltpu.VMEM((nb * state_shape[0],) + state_shape[1:], F32)]
        + ([pltpu.VMEM((nb, c, width), F32)] if bwd else []),
        compiler_params=_cparams(("parallel", "arbitrary")),
        name=name + ("_bwd" if bwd else "_fwd"),
    )(*in_arrays)


def _batch_per_step(bsz, most):
    return max(nb for nb in (1, 2, 4) if nb <= most and bsz % nb == 0)


def _rows_per_step(n_ctx, l):
    return _pick_tile(math.gcd(n_ctx, l - n_ctx), (256, 128, 64))


def hgrn_mixer(proj, col, lb_logits, layer, n_ctx, gate_col, norm_g):
    bsz, l, _ = proj.shape
    depth, _, width = lb_logits.shape
    c = CHUNK
    heads, dk = width // A_HEAD_DIM, A_HEAD_DIM
    cq, cf, ci = (x // width for x in col)
    norm_row = jnp.tile(norm_g, heads).reshape(1, width)
    nb = _batch_per_step(bsz, 4)
    rs = _rows_per_step(n_ctx, l)
    o_fwd = None
    for bwd in (False, True):
        w_np, m_np, nlev = _decay_constants(c, bwd)
        chunk = _chunk_order(n_ctx // rs, l // rs, bwd)
        d = int(bwd)
        kern = functools.partial(_hgrn_kernel, layer=layer, heads=heads, dk=dk, nlev=nlev, c=c, bwd=bwd)
        in_specs = [
            pl.BlockSpec((nb, rs, width), lambda b, n, chunk=chunk: (b, chunk(n), cq)),
            pl.BlockSpec((nb, rs, width), lambda b, n, chunk=chunk, d=d: (b, chunk(n), cf + d)),
            pl.BlockSpec((nb, rs, width), lambda b, n, chunk=chunk: (b, chunk(n), ci)),
            pl.BlockSpec((depth, width), lambda b, n: (0, 0)),
            pl.BlockSpec(w_np.shape, lambda b, n: (0, 0)),
            pl.BlockSpec(m_np.shape, lambda b, n: (0, 0, 0)),
        ]
        in_arrays = [proj, proj, proj, lb_logits[:, d], jnp.asarray(w_np, BF16), jnp.asarray(m_np)]
        o_fwd = _mixer_call(kern, "hgrn_mixer", proj, in_arrays, in_specs, chunk, nb, rs, c, width,
                            (heads, dk, dk), o_fwd, gate_col, norm_row)
    return o_fwd


def gla_mixer(proj, col, wgk_pad, b_gk, n_ctx, width, gate_col, norm_g):
    bsz, l, _ = proj.shape
    c = CHUNK
    key_w = wgk_pad.shape[-1]
    heads = B_HEADS
    dk, dv = key_w // heads, width // heads
    cq, ck, cv, clr = col
    norm_row = jnp.tile(norm_g, heads).reshape(1, width)
    nb = _batch_per_step(bsz, 4)
    rs = _rows_per_step(n_ctx, l)
    o_fwd = None
    for bwd in (False, True):
        w_np, m_np, nlev = _decay_constants(c, bwd)
        chunk = _chunk_order(n_ctx // rs, l // rs, bwd)
        d = int(bwd)
        kern = functools.partial(_gla_kernel, heads=heads, dk=dk, dv=dv, nlev=nlev, c=c, bwd=bwd)
        in_specs = [
            pl.BlockSpec((nb, rs, key_w), lambda b, n, chunk=chunk: (b, chunk(n), cq // key_w)),
            pl.BlockSpec((nb, rs, key_w), lambda b, n, chunk=chunk: (b, chunk(n), ck // key_w)),
            pl.BlockSpec((nb, rs, width), lambda b, n, chunk=chunk: (b, chunk(n), cv // width)),
            pl.BlockSpec((nb, rs, LANES), lambda b, n, chunk=chunk: (b, chunk(n), clr // LANES)),
            pl.BlockSpec((LANES, key_w), lambda b, n: (0, 0)),
            pl.BlockSpec((1, key_w), lambda b, n: (0, 0)),
            pl.BlockSpec(w_np.shape, lambda b, n: (0, 0)),
            pl.BlockSpec(m_np.shape, lambda b, n: (0, 0, 0)),
        ]
        in_arrays = [proj, proj, proj, proj, wgk_pad[d], b_gk[d].reshape(1, key_w),
                     jnp.asarray(w_np, BF16), jnp.asarray(m_np)]
        o_fwd = _mixer_call(kern, "gla_mixer", proj, in_arrays, in_specs, chunk, nb, rs, c, width,
                            (heads, dv, dk), o_fwd, gate_col, norm_row)
    return o_fwd


def _retention_kernel(*refs, heads, dk, dv, c, bwd):
    (q_ref, k_ref, v_ref, cos_ref, sin_ref, dl_ref), fin, (o_ref, state_ref, o_scr) = \
        _split_refs(refs, 6, bwd)

    @pl.when(pl.program_id(1) == 0)
    def _():
        state_ref[...] = jnp.zeros_like(state_ref)

    ii = lax.broadcasted_iota(jnp.int32, (c, c), 0).astype(F32)
    jj = lax.broadcasted_iota(jnp.int32, (c, c), 1).astype(F32)
    rel = (jj - ii) if bwd else (ii - jj)
    t_col = lax.broadcasted_iota(jnp.int32, (c, 1), 0).astype(F32)
    since = ((c - 1.0) - t_col) if bwd else t_col
    log_gamma = _log_sigmoid(dl_ref[...]) * LOG2E
    cos, sin = cos_ref[...], sin_ref[...]
    half = dk // 2
    for h in range(heads):
        ks = slice(h * dk, (h + 1) * dk)
        vs = slice(h * dv, (h + 1) * dv)
        lg = log_gamma[:, h:h + 1]
        dmat = jnp.where(rel >= 0, jnp.exp2(lg * jnp.maximum(rel, 0.0)), 0.0)
        xi = jnp.exp2(lg * (since + 1.0))
        zeta = jnp.exp2(lg * ((c - 1.0) - since))
        for bb in range(q_ref.shape[0]):
            qh, kh = q_ref[bb, :, ks], k_ref[bb, :, ks]
            qh = (qh * cos + pltpu.roll(qh, half, 1) * sin) * (dk ** -0.5)
            kh = kh * cos + pltpu.roll(kh, half, 1) * sin
            vb = v_ref[bb, :, vs].astype(BF16)
            st = state_ref[bb * heads + h]
            scores = _dot_nt(qh.astype(BF16), kh.astype(BF16)) * dmat
            o = _dot(scores.astype(BF16), vb) + _dot_nt(qh.astype(BF16), st.astype(BF16)) * xi
            state_ref[bb * heads + h] = st * jnp.exp2(lg * c) + _dot_tn(vb, (kh * zeta).astype(BF16))
            o_scr[bb, :, vs] = o
    _finish(o_scr, *fin, o_ref, heads, dv, True, bwd)


def retention_mixer(proj, col, cos_t, sin_t, decay_logit_pad, n_ctx, width, gate_col, norm_g):
    bsz, l, _ = proj.shape
    heads = D_HEADS
    key_w = width // 2
    dk, dv = key_w // heads, width // heads
    c = _rows_per_step(n_ctx, l)
    cq, ck, cv = col
    norm_row = jnp.tile(norm_g, heads).reshape(1, width)
    nb = _batch_per_step(bsz, 2)
    o_fwd = None
    for bwd in (False, True):
        chunk = _chunk_order(n_ctx // c, l // c, bwd)
        kern = functools.partial(_retention_kernel, heads=heads, dk=dk, dv=dv, c=c, bwd=bwd)
        in_specs = [
            pl.BlockSpec((nb, c, key_w), lambda b, n, chunk=chunk: (b, chunk(n), cq // key_w)),
            pl.BlockSpec((nb, c, key_w), lambda b, n, chunk=chunk: (b, chunk(n), ck // key_w)),
            pl.BlockSpec((nb, c, width), lambda b, n, chunk=chunk: (b, chunk(n), cv // width)),
            pl.BlockSpec((c, dk), lambda b, n, chunk=chunk: (chunk(n), 0)),
            pl.BlockSpec((c, dk), lambda b, n, chunk=chunk: (chunk(n), 0)),
            pl.BlockSpec((1, LANES), lambda b, n: (0, 0)),
        ]
        in_arrays = [proj, proj, proj, cos_t, sin_t, decay_logit_pad[int(bwd)]]
        o_fwd = _mixer_call(kern, "retention_mixer", proj, in_arrays, in_specs, chunk, nb, c, c, width,
                            (heads, dv, dk), o_fwd, gate_col, norm_row)
    return o_fwd


def _s5_kernel(u_ref, lam_ref, bt_ref, cm_ref, y_ref,
               toep_ref, win_r_ref, win_i_ref, wout_ref, cl_ref, s_r_ref, s_i_ref, xp_ref, *,
               bsz, n_chunks_ctx, n_chunks):
    t_len, hc, half = S5_CHUNK, C_GROUP, C_STATE
    lane = lax.broadcasted_iota(jnp.int32, (1, LANES), 1)
    lo = lane < half
    sgn = jnp.where(lo, -1.0, 1.0)
    tau = lax.broadcasted_iota(jnp.int32, (t_len, 1), 0)
    u = u_ref[0]
    rows = u.shape[0]
    y_acc = jnp.zeros((rows, t_len * hc), F32)

    def cmul(ar, ai, br, bi):
        return ar * br - ai * bi, ar * bi + ai * br

    def expand(x1, pa, x2, pb, out_ref):
        for t in range(t_len):
            blk = (x1 * jnp.broadcast_to(pa[t:t + 1], (hc, LANES))
                   + x2 * jnp.broadcast_to(pb[t:t + 1], (hc, LANES)))
            out_ref[t * hc:(t + 1) * hc, :] = blk.astype(out_ref.dtype)

    for d in range(2):
        lam_re = jnp.minimum(lam_ref[d, 0, 0:1], C_MAX_RE)
        lam_im = lam_ref[d, 0, 1:2]
        dt = jnp.exp(lam_ref[d, 0, 2:3])
        mag = jnp.exp(lam_re * dt)
        lb_r, lb_i = mag * jnp.cos(lam_im * dt), mag * jnp.sin(lam_im * dt)
        den = lam_re * lam_re + lam_im * lam_im
        nr, ni = lb_r - 1.0, lb_i
        cf_r, cf_i = (nr * lam_re + ni * lam_im) / den, (ni * lam_re - nr * lam_im) / den
        bt_r, bt_i = bt_ref[d, 0, 0], bt_ref[d, 0, 1]
        bb_r, bb_i = cmul(cf_r, cf_i, bt_r, bt_i)
        c_r, c_i = cm_ref[d, 0, 0], cm_ref[d, 0, 1]

        p_r, p_i = jnp.ones((t_len, LANES), F32), jnp.zeros((t_len, LANES), F32)
        q_r, q_i = p_r, p_i
        sq_r, sq_i = lb_r, lb_i
        for bit in range(int(math.log2(t_len))):
            sel = ((tau >> bit) & 1) == 1
            p_r, p_i = cmul(p_r, p_i, jnp.where(sel, sq_r, 1.0), jnp.where(sel, sq_i, 0.0))
            selq = (((t_len - 1 - tau) >> bit) & 1) == 1
            q_r, q_i = cmul(q_r, q_i, jnp.where(selq, sq_r, 1.0), jnp.where(selq, sq_i, 0.0))
            sq_r, sq_i = cmul(sq_r, sq_i, sq_r, sq_i)
        lc_r, lc_i = sq_r, sq_i
        if d == 0:
            toep_p, in_p = (p_r, p_i), (q_r, q_i)
            out_p = cmul(p_r, p_i, lb_r, lb_i)
        else:
            toep_p, in_p = (q_r, q_i), (p_r, p_i)
            out_p = cmul(q_r, q_i, lb_r, lb_i)

        tp_r, tp_i = toep_p
        expand(c_r, jnp.where(lo, tp_r, tp_i), sgn * c_i, jnp.where(lo, tp_i, tp_r), cl_ref)
        bbs = jnp.where(lo, bb_r, -bb_i)
        kt = lax.dot_general(bbs, cl_ref[...], (((1,), (1,)), ((), ())),
                             precision=lax.Precision.HIGHEST, preferred_element_type=F32)
        width = t_len * hc
        glane = lax.broadcasted_iota(jnp.int32, (hc, width), 1)
        per_tile = LANES // hc
        for m in range(per_tile):
            if d == 0:
                base = kt if m == 0 else jnp.where(glane >= hc * m, pltpu.roll(kt, hc * m, 1), 0.0)
            else:
                base = kt if m == 0 else jnp.where(glane < width - hc * m,
                                                   pltpu.roll(kt, width - hc * m, 1), 0.0)
            base = base.astype(BF16)
            for a in range(t_len // per_tile):
                off = a * LANES
                if d == 0:
                    j = a * per_tile + m
                    if off:
                        toep_ref[d, j * hc:(j + 1) * hc, :off] = jnp.zeros((hc, off), BF16)
                    toep_ref[d, j * hc:(j + 1) * hc, off:] = base[:, :width - off]
                else:
                    j = t_len - 1 - (a * per_tile + m)
                    if off:
                        toep_ref[d, j * hc:(j + 1) * hc, width - off:] = jnp.zeros((hc, off), BF16)
                    toep_ref[d, j * hc:(j + 1) * hc, :width - off] = base[:, off:]

        ip_r, ip_i = in_p
        expand(bb_r, ip_r, -bb_i, ip_i, win_r_ref)
        expand(bb_i, ip_r, bb_r, ip_i, win_i_ref)
        op_r, op_i = out_p
        expand(c_r, jnp.where(lo, op_r, -op_i), c_i, jnp.where(lo, -op_i, -op_r), wout_ref)

        s_r_ref[...] = _dot(u, win_r_ref[...])
        s_i_ref[...] = _dot(u, win_i_ref[...])
        if d == 0:
            order = list(range(n_chunks))
        else:
            order = list(range(n_chunks_ctx - 1, -1, -1)) + list(range(n_chunks - 1, n_chunks_ctx - 1, -1))
        x_r, x_i = jnp.zeros((bsz, LANES), F32), jnp.zeros((bsz, LANES), F32)
        for n in order:
            rs = slice(n * bsz, (n + 1) * bsz)
            xp_ref[rs, :] = jnp.where(lo, x_r, x_i)
            nx_r, nx_i = cmul(lc_r, lc_i, x_r, x_i)
            x_r, x_i = nx_r + s_r_ref[rs, :], nx_i + s_i_ref[rs, :]

        y_acc = y_acc + _dot(u, toep_ref[d]) + _dot_nt(xp_ref[...].astype(BF16), wout_ref[...])
    y_ref[0] = y_acc


def s5_core(ug, lam_pk, bt_pk, cm_pk, bsz, n_chunks_ctx, n_chunks):
    groups, rows, width = ug.shape
    kern = functools.partial(_s5_kernel, bsz=bsz, n_chunks_ctx=n_chunks_ctx, n_chunks=n_chunks)
    return pl.pallas_call(
        kern,
        grid=(groups,),
        in_specs=[
            pl.BlockSpec((1, rows, width), lambda g: (g, 0, 0)),
            pl.BlockSpec((2, 1, SUBLANES, LANES), lambda g: (0, g, 0, 0)),
            pl.BlockSpec((2, 1, 2, C_GROUP, LANES), lambda g: (0, g, 0, 0, 0)),
            pl.BlockSpec((2, 1, 2, C_GROUP, LANES), lambda g: (0, g, 0, 0, 0)),
        ],
        out_specs=pl.BlockSpec((1, rows, width), lambda g: (g, 0, 0)),
        out_shape=jax.ShapeDtypeStruct((groups, rows, width), F32),
        scratch_shapes=[
            pltpu.VMEM((2, width, width), BF16),
            pltpu.VMEM((width, LANES), BF16),
            pltpu.VMEM((width, LANES), BF16),
            pltpu.VMEM((width, LANES), BF16),
            pltpu.VMEM((width, LANES), F32),
            pltpu.VMEM((rows, LANES), F32),
            pltpu.VMEM((rows, LANES), F32),
            pltpu.VMEM((rows, LANES), F32),
        ],
        compiler_params=_cparams(("parallel",)),
        name="s5_core",
    )(ug, lam_pk, bt_pk, cm_pk)


def _s5_post_kernel(y_ref, u_ref, gate_ref, d_ref, w_ref, b_ref, o_ref):
    y = y_ref[0] + d_ref[...] * u_ref[0]
    z = jax.nn.gelu(y)
    t = _dot(z.astype(BF16), w_ref[...]) + b_ref[...]
    o_ref[0] = (z * _sigmoid(t) * _silu(gate_ref[0])).astype(BF16)


def s5_post(y, proj, u_col, gate_col, d_skip, w_glu, b_glu):
    bsz, l, width = y.shape
    tr = ROW_TILE
    return pl.pallas_call(
        _s5_post_kernel,
        grid=(bsz, l // tr),
        in_specs=[
            pl.BlockSpec((1, tr, width), lambda b, i: (b, i, 0)),
            pl.BlockSpec((1, tr, width), lambda b, i: (b, i, u_col // width)),
            pl.BlockSpec((1, tr, width), lambda b, i: (b, i, gate_col // width)),
            pl.BlockSpec((1, width), lambda b, i: (0, 0)),
            pl.BlockSpec((width, width), lambda b, i: (0, 0)),
            pl.BlockSpec((1, width), lambda b, i: (0, 0)),
        ],
        out_specs=pl.BlockSpec((1, tr, width), lambda b, i: (b, i, 0)),
        out_shape=jax.ShapeDtypeStruct((bsz, l, width), BF16),
        compiler_params=_cparams(("parallel", "parallel")),
        name="s5_post",
    )(y, proj, proj, d_skip.reshape(1, width), w_glu.astype(BF16), b_glu.reshape(1, width))


def _outproj_kernel(oa_ref, ob_ref, oc_ref, od_ref, w_ref, h_ref, gl_ref, gc_ref, o_ref, *,
                    n_ctx, tm, bw):
    acc = _dot(oa_ref[0], w_ref[0, 0:bw, :].astype(BF16))
    acc = acc + _dot(ob_ref[0], w_ref[0, bw:2 * bw, :].astype(BF16))
    acc = acc + _dot(oc_ref[0], w_ref[0, 2 * bw:3 * bw, :].astype(BF16))
    acc = acc + _dot(od_ref[0], w_ref[0, 3 * bw:4 * bw, :].astype(BF16))
    row = pl.program_id(1) * tm + lax.broadcasted_iota(jnp.int32, (tm, 1), 0)
    gate = jnp.where(row < n_ctx, gc_ref[0], gl_ref[0])
    o_ref[0] = h_ref[0] + gate * acc


def out_projection(o_parts, w_out, h, mods_flat, layer, n_ctx):
    bsz, l, d = h.shape
    bw = o_parts[0].shape[-1]
    tm = l if l <= 2048 else _pick_tile(l, (l // 4, l // 8, l // 16))
    tn = _pick_tile(d, (512, 256, 128))
    kern = functools.partial(_outproj_kernel, n_ctx=n_ctx, tm=tm, bw=bw)
    o_spec = pl.BlockSpec((1, tm, bw), lambda b, i, j: (b, i, 0))
    return pl.pallas_call(
        kern,
        grid=(bsz, l // tm, d // tn),
        in_specs=[
            o_spec, o_spec, o_spec, o_spec,
            pl.BlockSpec((1, 4 * bw, tn), lambda b, i, j: (layer, 0, j)),
            pl.BlockSpec((1, tm, tn), lambda b, i, j: (b, i, j)),
            pl.BlockSpec((1, 1, tn), lambda b, i, j: ((layer * SUBLANES + b) * 3 + 2, 0, j)),
            pl.BlockSpec((1, 1, tn), lambda b, i, j: ((layer * SUBLANES + bsz) * 3 + 2, 0, j)),
        ],
        out_specs=pl.BlockSpec((1, tm, tn), lambda b, i, j: (b, i, j)),
        out_shape=jax.ShapeDtypeStruct((bsz, l, d), F32),
        compiler_params=_cparams(("parallel", "parallel", "parallel")),
        name="out_projection",
    )(*o_parts, w_out, h, mods_flat, mods_flat)


def _final_norm_kernel(x_ref, g_ref, o_ref):
    x = x_ref[0]
    o_ref[0] = x * lax.rsqrt(jnp.mean(x * x, axis=-1, keepdims=True) + EPS) * g_ref[...]


def final_norm(h, g, n_ctx):
    bsz, l, d = h.shape
    tr = ROW_TILE
    skip = n_ctx // tr
    return pl.pallas_call(
        _final_norm_kernel,
        grid=(bsz, (l - n_ctx) // tr),
        in_specs=[
            pl.BlockSpec((1, tr, d), lambda b, i: (b, i + skip, 0)),
            pl.BlockSpec((1, d), lambda b, i: (0, 0)),
        ],
        out_specs=pl.BlockSpec((1, tr, d), lambda b, i: (b, i, 0)),
        out_shape=jax.ShapeDtypeStruct((bsz, l - n_ctx, d), F32),
        compiler_params=_cparams(("parallel", "parallel")),
        name="final_norm",
    )(h, g.reshape(1, d))


def _rope_tables(rows, n_ctx, dk):
    quarter = dk // 4
    freqs = ROPE_BASE ** (-jnp.arange(quarter, dtype=F32) / quarter)
    t = jnp.arange(rows * GRID_W)
    r = (t // GRID_W).astype(F32)
    col = (t % GRID_W).astype(F32)
    ang = jnp.concatenate([r[:, None] * freqs, col[:, None] * freqs], axis=-1)
    ang = jnp.concatenate([jnp.zeros((n_ctx, dk // 2), F32), ang], axis=0)
    cos, sin = jnp.cos(ang), jnp.sin(ang)
    return jnp.concatenate([cos, cos], axis=-1), jnp.concatenate([-sin, sin], axis=-1)


def _dup(x):
    return jnp.concatenate([x, x], axis=-1)


def mixer_layer(xn, h, mods_flat, layer, n_ctx, w_in_perm, hgrn_lb_logits, hgrn_norm_g, gla_w_gk,
                gla_b_gk, gla_norm_g, s5_lam_re, s5_lam_im, s5_log_dt, s5_b_re, s5_b_im, s5_c_re,
                s5_c_im, s5_d, s5_w_glu, s5_b_glu, ret_decay_logit, ret_norm_g, w_out, rope):
    bsz, l, d = h.shape
    bw = d // 4
    kw = bw // 2
    rank = B_GATE_RANK

    names = ("a_q", "a_ff", "a_fb", "a_i", "a_g", "b_q", "b_k", "b_v", "b_g", "c_u", "c_g",
             "d_q", "d_k", "d_v", "d_g", "b_lr")
    widths = (bw, bw, bw, bw, bw, kw, kw, bw, bw, bw, bw, kw, kw, bw, bw, LANES)
    col = dict(zip(names, np.concatenate([[0], np.cumsum(widths)[:-1]]).tolist()))

    proj = in_projection(xn.reshape(bsz * l, d), w_in_perm, layer).reshape(bsz, l, -1)

    o_a = hgrn_mixer(proj, (col["a_q"], col["a_ff"], col["a_i"]), hgrn_lb_logits, layer, n_ctx,
                     col["a_g"], hgrn_norm_g[layer])

    wgk = gla_w_gk[layer].astype(BF16)
    wgk_pad = jnp.zeros((2, LANES, kw), BF16)
    wgk_pad = wgk_pad.at[0, :rank].set(wgk[0]).at[1, rank:2 * rank].set(wgk[1])
    o_b = gla_mixer(proj, (col["b_q"], col["b_k"], col["b_v"], col["b_lr"]), wgk_pad,
                    gla_b_gk[layer], n_ctx, bw, col["b_g"], gla_norm_g[layer])

    groups = bw // C_GROUP
    nck = l // S5_CHUNK
    u = proj[:, :, col["c_u"]:col["c_u"] + bw].astype(BF16)
    ug = u.reshape(bsz, nck, S5_CHUNK, groups, C_GROUP).transpose(3, 1, 0, 2, 4)
    ug = ug.reshape(groups, nck * bsz, S5_CHUNK * C_GROUP)
    dt_row = jnp.broadcast_to(s5_log_dt[layer][..., None], (2, groups, C_STATE))
    lam_pk = jnp.stack([_dup(s5_lam_re[layer]), _dup(s5_lam_im[layer]), _dup(dt_row)], axis=2)
    lam_pk = jnp.pad(lam_pk, ((0, 0), (0, 0), (0, SUBLANES - 3), (0, 0)))
    bt_pk = jnp.stack([_dup(jnp.swapaxes(s5_b_re[layer], -1, -2)),
                       _dup(jnp.swapaxes(s5_b_im[layer], -1, -2))], axis=2)
    cm_pk = jnp.stack([_dup(s5_c_re[layer]), _dup(s5_c_im[layer])], axis=2)
    yg = s5_core(ug, lam_pk, bt_pk, cm_pk, bsz, n_ctx // S5_CHUNK, nck)
    y = yg.reshape(groups, nck, bsz, S5_CHUNK, C_GROUP).transpose(2, 1, 3, 0, 4).reshape(bsz, l, bw)
    o_c = s5_post(y, proj, col["c_u"], col["c_g"], s5_d[layer], s5_w_glu[layer], s5_b_glu[layer])

    dl = jnp.pad(ret_decay_logit[layer], ((0, 0), (0, LANES - D_HEADS))).reshape(2, 1, LANES)
    o_d = retention_mixer(proj, (col["d_q"], col["d_k"], col["d_v"]), rope[0], rope[1], dl, n_ctx, bw,
                          col["d_g"], ret_norm_g[layer])

    return out_projection((o_a, o_b, o_c, o_d), w_out, h, mods_flat, layer, n_ctx)


def kernel(x, c, ctx, c_ctx, norm_g, w_ada, b_ada, w_in, hgrn_lb_logits, hgrn_norm_g, gla_w_gk,
           gla_b_gk, gla_norm_g, s5_lam_re, s5_lam_im, s5_log_dt, s5_b_re, s5_b_im, s5_c_re, s5_c_im,
           s5_d, s5_w_glu, s5_b_glu, ret_decay_logit, ret_norm_g, w_out, final_norm_g):
    bsz, seq, d = x.shape
    n_ctx = ctx.shape[1]
    depth = w_in.shape[0]
    assert bsz < SUBLANES and n_ctx % ROW_TILE == 0 and seq % ROW_TILE == 0

    cvec = jnp.concatenate([c, c_ctx[None], jnp.zeros((SUBLANES - bsz - 1, d), F32)], axis=0)
    mods = ada_modulation(cvec, w_ada, b_ada)
    mods_flat = mods.reshape(depth * SUBLANES * 3, 1, d)
    rope = _rope_tables(seq // GRID_W, n_ctx, (d // 8) // D_HEADS)

    w_in_perm = permute_in_weights(w_in)
    h = None
    for layer in range(depth):
        if layer == 0:
            xn, h = prenorm_first(ctx, x, norm_g[0], mods_flat, n_ctx)
        else:
            xn = prenorm(h, norm_g[layer], mods_flat, layer, n_ctx)
        h = mixer_layer(xn, h, mods_flat, layer, n_ctx, w_in_perm, hgrn_lb_logits, hgrn_norm_g,
                        gla_w_gk, gla_b_gk, gla_norm_g, s5_lam_re, s5_lam_im, s5_log_dt, s5_b_re,
                        s5_b_im, s5_c_re, s5_c_im, s5_d, s5_w_glu, s5_b_glu, ret_decay_logit,
                        ret_norm_g, w_out, rope)
    return final_norm(h, final_norm_g, n_ctx)
```

```python
import functools
import math

import numpy as np
import jax
import jax.numpy as jnp
from jax import lax
from jax.experimental import pallas as pl
from jax.experimental.pallas import tpu as pltpu

F32 = jnp.float32
BF16 = jnp.bfloat16

EPS = 1e-6
A_HEAD_DIM = 128
A_MIN_FORGET = 1e-6
B_HEADS = 4
B_GATE_RANK = 16
B_GATE_NORM = 16.0
C_GROUP = 16
C_STATE = 64
C_MAX_RE = -1e-4
D_HEADS = 4
GRID_W = 64
ROPE_BASE = 10000.0

LANES = 128
SUBLANES = 8
BF16_ROWS = 16
VMEM_LIMIT = 56 * 1024 * 1024

MXU_COLS = 256
IN_PROJ_TN = 6 * MXU_COLS

CHUNK = 64
LOG2E = math.log2(math.e)
SHORT_SPAN = 86.0
S5_CHUNK = 64
ROW_TILE = 256


def _cparams(sem):
    return pltpu.CompilerParams(dimension_semantics=sem, vmem_limit_bytes=VMEM_LIMIT)


def _dot(a, b):
    return jnp.dot(a, b, preferred_element_type=F32)


def _dot_nt(a, b):
    return lax.dot_general(a, b, (((1,), (1,)), ((), ())), preferred_element_type=F32)


def _dot_tn(a, b):
    return lax.dot_general(a, b, (((0,), (0,)), ((), ())), preferred_element_type=F32)


def _sigmoid(x):
    return 1.0 / (1.0 + jnp.exp(-x))


def _silu(x):
    return x * _sigmoid(x)


def _log_sigmoid(x):
    return jnp.minimum(x, 0.0) - jnp.log(1.0 + jnp.exp(-jnp.abs(x)))


def _ada_kernel(c_ref, w_ref, b_ref, o_ref):
    cv = _silu(c_ref[...]).astype(BF16)
    o_ref[0] = _dot(cv, w_ref[0].astype(BF16)) + b_ref[0]


def ada_modulation(cvec, w_ada, b_ada):
    depth, d, n3 = w_ada.shape
    tn = 512
    return pl.pallas_call(
        _ada_kernel,
        grid=(depth, n3 // tn),
        in_specs=[
            pl.BlockSpec((SUBLANES, d), lambda l, j: (0, 0)),
            pl.BlockSpec((1, d, tn), lambda l, j: (l, 0, j)),
            pl.BlockSpec((1, 1, tn), lambda l, j: (l, 0, j)),
        ],
        out_specs=pl.BlockSpec((1, SUBLANES, tn), lambda l, j: (l, 0, j)),
        out_shape=jax.ShapeDtypeStruct((depth, SUBLANES, n3), F32),
        compiler_params=_cparams(("parallel", "parallel")),
        name="ada_modulation",
    )(cvec, w_ada, b_ada.reshape(depth, 1, n3))


def _prenorm_kernel(x_ref, g_ref, sh_ref, sc_ref, o_ref):
    x = x_ref[0]
    y = x * lax.rsqrt(jnp.mean(x * x, axis=-1, keepdims=True) + EPS) * g_ref[...]
    o_ref[0] = (y * (1.0 + sc_ref[0]) + sh_ref[0]).astype(BF16)


def prenorm(h, g, mods_flat, layer, n_ctx):
    bsz, l, d = h.shape
    tr = ROW_TILE
    nct = n_ctx // tr

    def mod_map(part):
        def f(b, i):
            row = jnp.where(i < nct, bsz, b)
            return ((layer * SUBLANES + row) * 3 + part, 0, 0)
        return f

    return pl.pallas_call(
        _prenorm_kernel,
        grid=(bsz, l // tr),
        in_specs=[
            pl.BlockSpec((1, tr, d), lambda b, i: (b, i, 0)),
            pl.BlockSpec((1, d), lambda b, i: (0, 0)),
            pl.BlockSpec((1, 1, d), mod_map(0)),
            pl.BlockSpec((1, 1, d), mod_map(1)),
        ],
        out_specs=pl.BlockSpec((1, tr, d), lambda b, i: (b, i, 0)),
        out_shape=jax.ShapeDtypeStruct((bsz, l, d), BF16),
        compiler_params=_cparams(("parallel", "parallel")),
        name="prenorm",
    )(h, g.reshape(1, d), mods_flat, mods_flat)


def _prenorm_first_kernel(c_ref, x_ref, g_ref, sh_ref, sc_ref, o_ref, h_ref, *, nct):
    x = jnp.where(pl.program_id(1) < nct, c_ref[0], x_ref[0])
    h_ref[0] = x
    y = x * lax.rsqrt(jnp.mean(x * x, axis=-1, keepdims=True) + EPS) * g_ref[...]
    o_ref[0] = (y * (1.0 + sc_ref[0]) + sh_ref[0]).astype(BF16)


def prenorm_first(ctx, x, g, mods_flat, n_ctx):
    bsz, seq, d = x.shape
    l = n_ctx + seq
    tr = ROW_TILE
    nct = n_ctx // tr

    def mod_map(part):
        def f(b, i):
            row = jnp.where(i < nct, bsz, b)
            return (row * 3 + part, 0, 0)
        return f

    row_spec = pl.BlockSpec((1, tr, d), lambda b, i: (b, i, 0))
    return pl.pallas_call(
        functools.partial(_prenorm_first_kernel, nct=nct),
        grid=(bsz, l // tr),
        in_specs=[
            pl.BlockSpec((1, tr, d), lambda b, i: (b, jnp.minimum(i, nct - 1), 0)),
            pl.BlockSpec((1, tr, d), lambda b, i: (b, jnp.maximum(i - nct, 0), 0)),
            pl.BlockSpec((1, d), lambda b, i: (0, 0)),
            pl.BlockSpec((1, 1, d), mod_map(0)),
            pl.BlockSpec((1, 1, d), mod_map(1)),
        ],
        out_specs=[row_spec, row_spec],
        out_shape=[jax.ShapeDtypeStruct((bsz, l, d), BF16), jax.ShapeDtypeStruct((bsz, l, d), F32)],
        compiler_params=_cparams(("parallel", "arbitrary")),
        name="prenorm_first",
    )(ctx, x, g.reshape(1, d), mods_flat, mods_flat)


def _wprep_kernel(a_ref, b_ref, o_ref, *, first_shifted, tail_block, lr_w):
    i = pl.program_id(1)
    tr = a_ref.shape[1]

    @pl.when(i < first_shifted)
    def _():
        o_ref[0] = a_ref[0].astype(BF16)

    @pl.when(jnp.logical_and(i >= first_shifted, i < tail_block))
    def _():
        o_ref[0, :tr - lr_w] = a_ref[0, lr_w:].astype(BF16)
        o_ref[0, tr - lr_w:] = b_ref[0].astype(BF16)

    @pl.when(i == tail_block)
    def _():
        o_ref[0, :lr_w] = b_ref[0].astype(BF16)
        o_ref[0, lr_w:] = jnp.zeros((tr - lr_w, a_ref.shape[2]), BF16)

    @pl.when(i > tail_block)
    def _():
        o_ref[0] = jnp.zeros(o_ref.shape[1:], BF16)


def permute_in_weights(w_in):
    depth, d, n = w_in.shape
    bw, lr_w = d // 4, 2 * B_GATE_RANK
    lr0 = 5 * bw + 2 * (bw // 2) + bw
    tr = 2 * LANES
    assert (n - lr_w) % tr == 0 and lr0 % tr == 0 and tr % lr_w == 0
    w_t = jnp.swapaxes(w_in, 1, 2)
    first_shifted, tail_block = lr0 // tr, (n - lr_w) // tr
    per = tr // lr_w
    n_out = -(-(tail_block + 1) * tr // IN_PROJ_TN) * IN_PROJ_TN

    def a_map(l, i):
        return (l, jnp.minimum(i, tail_block - 1), 0)

    def b_map(l, i):
        return (l, jnp.where(i == tail_block, lr0 // lr_w,
                             jnp.minimum(i + 1, tail_block) * per), 0)

    return pl.pallas_call(
        functools.partial(_wprep_kernel, first_shifted=first_shifted, tail_block=tail_block, lr_w=lr_w),
        grid=(depth, n_out // tr),
        in_specs=[pl.BlockSpec((1, tr, d), a_map), pl.BlockSpec((1, lr_w, d), b_map)],
        out_specs=pl.BlockSpec((1, tr, d), lambda l, i: (l, i, 0)),
        out_shape=jax.ShapeDtypeStruct((depth, n_out, d), BF16),
        compiler_params=_cparams(("parallel", "parallel")),
        name="permute_in_weights",
    )(w_t, w_t)


def _matmul_kernel(x_ref, wt_ref, o_ref):
    o_ref[...] = _dot_nt(x_ref[...], wt_ref[0])


def _pick_tile(n, candidates):
    for c in candidates:
        if n % c == 0:
            return c
    return n


def in_projection(xn, w_all, layer):
    m, d = xn.shape
    n = w_all.shape[1]
    tm = _pick_tile(m, (1024, 512, 256, 128, 64, 32, 16))
    tn = IN_PROJ_TN
    return pl.pallas_call(
        _matmul_kernel,
        grid=(m // tm, n // tn),
        in_specs=[
            pl.BlockSpec((tm, d), lambda i, j: (i, 0), pipeline_mode=pl.Buffered(1)),
            pl.BlockSpec((1, tn, d), lambda i, j: (layer, j, 0)),
        ],
        out_specs=pl.BlockSpec((tm, tn), lambda i, j: (i, j)),
        out_shape=jax.ShapeDtypeStruct((m, n), F32),
        compiler_params=_cparams(("parallel", "parallel")),
        name="in_projection",
    )(xn, w_all)


def _decay_constants(c, bwd):
    nlev = int(math.log2(c))
    w = np.zeros((nlev + 2, c, c), np.float32)
    masks = np.zeros((nlev, c, c), np.float32)
    for lev in range(nlev):
        s = 1 << lev
        for r in range(c):
            pos = r % (2 * s)
            mid = r - pos + s
            if pos >= s:
                w[lev, r, mid:r + 1] = 1.0
            else:
                w[lev, r, r + 1:mid] = 1.0
        for i in range(c):
            for j in range(c):
                if i // (2 * s) == j // (2 * s) and i % (2 * s) >= s and j % (2 * s) < s:
                    masks[lev, i, j] = 1.0
    for r in range(c):
        w[nlev, r, :r + 1] = 1.0
        w[nlev + 1, r, r + 1:] = 1.0
    half = c // 2
    ii, jj = np.meshgrid(np.arange(c), np.arange(c), indexing="ij")
    diag = ((ii // half == jj // half) & (jj <= ii)).astype(np.float32)
    masks = np.concatenate([masks, diag[None]], axis=0)
    w = w.reshape((nlev + 2) * c, c)
    w = np.concatenate([w, np.ones((BF16_ROWS, c), np.float32)], axis=0)
    if bwd:
        w = np.concatenate([w[:-BF16_ROWS].reshape(nlev + 2, c, c)[:, ::-1, ::-1].reshape(-1, c),
                            w[-BF16_ROWS:]], axis=0)
        masks = masks[:, ::-1, ::-1]
    return np.ascontiguousarray(w), np.ascontiguousarray(masks), nlev


def _split2(x):
    hi = x.astype(BF16)
    return hi, (x - hi.astype(F32)).astype(BF16)


def _gated_core(qs, ks_, vs_, gs, w_ref, mask_ref, o_scr, state_ref, heads, dk, dv, nlev, c, bwd):
    half = c // 2
    nb = len(qs)
    w_in = w_ref[nlev * c:(nlev + 1) * c, :]
    first_a, first_b = (half - 1, c - 1) if bwd else (0, half)
    top_ref, exit_row = (half, 0) if bwd else (half - 1, c - 1)
    g_parts, b_ins, short = [], [], None
    for g in gs:
        span = jnp.minimum(jnp.sum(g[:half], axis=0, keepdims=True),
                           jnp.sum(g[half:], axis=0, keepdims=True))
        ok = jnp.min(span) >= -SHORT_SPAN
        short = ok if short is None else jnp.logical_and(short, ok)
        parts = _split2(g)
        g_parts.append(parts)
        b_ins.append(_dot(w_in, parts[0]) + _dot(w_in, parts[1]))

    def head_update(bb, h, scores, e_in, e_out, e_tot, extra):
        ks = slice(h * dk, (h + 1) * dk)
        vs = slice(h * dv, (h + 1) * dv)
        qh, kh, vh = qs[bb][:, ks], ks_[bb][:, ks], vs_[bb][:, vs]
        vb = vh.astype(BF16)
        st = state_ref[bb * heads + h]
        o = _dot_nt((qh * e_in).astype(BF16), st.astype(BF16)) + _dot(scores.astype(BF16), vb)
        if extra is not None:
            o = o + extra * vh
        state_ref[bb * heads + h] = st * e_tot + _dot_tn(vb, (kh * e_out).astype(BF16))
        o_scr[bb, :, vs] = o

    @pl.when(short)
    def _():
        row = lax.broadcasted_iota(jnp.int32, (c, 1), 0)
        in_a = row < half
        later = in_a if bwd else jnp.logical_not(in_a)
        for bb in range(nb):
            b_in = b_ins[bb]
            m = jnp.where(in_a, b_in[first_a:first_a + 1], b_in[first_b:first_b + 1])
            fq = jnp.exp2(b_in - m)
            fk = jnp.exp2(m - b_in)
            r1 = b_in[top_ref:top_ref + 1]
            ft = jnp.exp2(jnp.where(later, b_in - r1, r1 - b_in))
            tot = b_in[exit_row:exit_row + 1]
            e_in_all = jnp.exp2(b_in)
            e_out_all = jnp.exp2(tot - b_in)
            e_tot_all = jnp.exp2(tot)
            for h in range(heads):
                ks = slice(h * dk, (h + 1) * dk)
                qh, kh = qs[bb][:, ks], ks_[bb][:, ks]
                scores = (mask_ref[nlev] * _dot_nt((qh * fq[:, ks]).astype(BF16),
                                                   (kh * fk[:, ks]).astype(BF16))
                          + mask_ref[nlev - 1] * _dot_nt((qh * ft[:, ks]).astype(BF16),
                                                         (kh * ft[:, ks]).astype(BF16)))
                head_update(bb, h, scores, e_in_all[:, ks], e_out_all[:, ks], e_tot_all[:, ks], None)

    @pl.when(jnp.logical_not(short))
    def _():
        w = w_ref[...]
        for bb in range(nb):
            parts = g_parts[bb]
            e_all = jnp.exp2(_dot(w, parts[0]) + _dot(w, parts[1]))
            for h in range(heads):
                ks = slice(h * dk, (h + 1) * dk)
                qh, kh = qs[bb][:, ks], ks_[bb][:, ks]
                scores = jnp.zeros((c, c), F32)
                for lev in range(nlev):
                    f = e_all[lev * c:(lev + 1) * c, ks]
                    scores = scores + mask_ref[lev] * _dot_nt((qh * f).astype(BF16), (kh * f).astype(BF16))
                head_update(bb, h, scores, e_all[nlev * c:(nlev + 1) * c, ks],
                            e_all[(nlev + 1) * c:(nlev + 2) * c, ks],
                            e_all[(nlev + 2) * c:(nlev + 2) * c + 1, ks],
                            jnp.sum(qh * kh, axis=-1, keepdims=True))


def _finish(o_scr, of_ref, gate_ref, ng_ref, o_ref, heads, hd, center, final):
    if not final:
        return
    for bb in range(o_scr.shape[0]):
        gs = _silu(gate_ref[bb])
        for h in range(heads):
            sl = slice(h * hd, (h + 1) * hd)
            x = o_scr[bb, :, sl] + of_ref[bb, :, sl]
            if center:
                x = x - jnp.mean(x, axis=-1, keepdims=True)
            y = x * lax.rsqrt(jnp.mean(x * x, axis=-1, keepdims=True) + EPS) * ng_ref[:, sl]
            o_ref[bb, :, sl] = (y * gs[:, sl]).astype(BF16)


def _split_refs(refs, n_in, bwd):
    ins = refs[:n_in]
    if bwd:
        return ins, refs[n_in:n_in + 3], refs[n_in + 3:]
    o_ref, state_ref = refs[n_in:]
    return ins, (None, None, None), (o_ref, state_ref, o_ref)


def _for_each_chunk(c, n_rows, bwd, body):
    cps = n_rows // c

    def step(s, carry):
        idx = (cps - 1 - s) if bwd else s
        body(pl.ds(pl.multiple_of(idx * c, c), c))
        return carry

    lax.fori_loop(0, cps, step, 0)


def _row_views(rows, o_ref, o_scr, fin, bwd):
    def at(r):
        return None if r is None else r.at[:, rows, :]
    of_ref, gate_ref, ng_ref = fin
    o_here = at(o_ref)
    return (o_scr if bwd else o_here), (at(of_ref), at(gate_ref), ng_ref), o_here


def _hgrn_kernel(*refs, layer, heads, dk, nlev, c, bwd):
    (q_ref, z_ref, v_ref, lbl_ref, w_ref, mask_ref), fin, (o_ref, state_ref, o_scr) = \
        _split_refs(refs, 6, bwd)

    @pl.when(pl.program_id(1) == 0)
    def _():
        state_ref[...] = jnp.zeros_like(state_ref)

    logits = lbl_ref[...]
    ex = jnp.exp(logits - jnp.max(logits, axis=0, keepdims=True))
    p = ex / jnp.sum(ex, axis=0, keepdims=True)
    lb = jnp.sum(p[:layer + 1], axis=0, keepdims=True) - p[0:1]
    nb = q_ref.shape[0]

    def chunk(rows):
        ks_, gs = [], []
        for bb in range(nb):
            z = z_ref[bb, rows, :]
            e = jnp.exp(-jnp.abs(z))
            s_big = 1.0 / (1.0 + e)
            s_small = e * s_big
            sig_pos = jnp.where(z >= 0, s_big, s_small)
            sig_neg = jnp.where(z >= 0, s_small, s_big)
            gs.append(jnp.log2(jnp.maximum(lb + (1.0 - lb) * sig_pos, A_MIN_FORGET)))
            ks_.append((1.0 - lb) * sig_neg)
        o_dst, fin_here, o_here = _row_views(rows, o_ref, o_scr, fin, bwd)
        _gated_core([q_ref[bb, rows, :] for bb in range(nb)], ks_,
                    [v_ref[bb, rows, :] for bb in range(nb)], gs,
                    w_ref, mask_ref, o_dst, state_ref, heads, dk, dk, nlev, c, bwd)
        _finish(o_dst, *fin_here, o_here, heads, dk, False, bwd)

    _for_each_chunk(c, q_ref.shape[1], bwd, chunk)


def _gla_kernel(*refs, heads, dk, dv, nlev, c, bwd):
    (q_ref, k_ref, v_ref, lr_ref, wgk_ref, bgk_ref, w_ref, mask_ref), fin, (o_ref, state_ref, o_scr) = \
        _split_refs(refs, 8, bwd)

    @pl.when(pl.program_id(1) == 0)
    def _():
        state_ref[...] = jnp.zeros_like(state_ref)

    nb = q_ref.shape[0]

    def chunk(rows):
        qs, gs = [], []
        for bb in range(nb):
            logit = _dot(lr_ref[bb, rows, :].astype(BF16), wgk_ref[...]) + bgk_ref[...]
            t = logit * LOG2E
            gs.append((jnp.minimum(t, 0.0) - jnp.log2(1.0 + jnp.exp2(-jnp.abs(t)))) * (1.0 / B_GATE_NORM))
            qs.append(q_ref[bb, rows, :] * (dk ** -0.5))
        o_dst, fin_here, o_here = _row_views(rows, o_ref, o_scr, fin, bwd)
        _gated_core(qs, [k_ref[bb, rows, :] for bb in range(nb)],
                    [v_ref[bb, rows, :] for bb in range(nb)], gs,
                    w_ref, mask_ref, o_dst, state_ref, heads, dk, dv, nlev, c, bwd)
        _finish(o_dst, *fin_here, o_here, heads, dv, False, bwd)

    _for_each_chunk(c, q_ref.shape[1], bwd, chunk)


def _chunk_order(n_chunks_ctx, n_chunks, bwd):
    def chunk(n):
        if not bwd:
            return n
        return jnp.where(n < n_chunks_ctx, n_chunks_ctx - 1 - n, n_chunks - 1 - n + n_chunks_ctx)
    return chunk


def _mixer_call(kern, name, proj, in_arrays, in_specs, chunk, nb, rs, c, width, state_shape,
                o_fwd, gate_col, norm_row):
    bsz, l, _ = proj.shape
    bwd = o_fwd is not None
    blk = pl.BlockSpec((nb, rs, width), lambda b, n: (b, chunk(n), 0))
    if bwd:
        in_arrays = in_arrays + [o_fwd, proj, norm_row]
        in_specs = in_specs + [
            blk,
            pl.BlockSpec((nb, rs, width), lambda b, n: (b, chunk(n), gate_col // width)),
            pl.BlockSpec((1, width), lambda b, n: (0, 0)),
        ]
    return pl.pallas_call(
        kern,
        grid=(bsz // nb, l // rs),
        in_specs=in_specs,
        out_specs=blk,
        out_shape=jax.ShapeDtypeStruct((bsz, l, width), BF16 if bwd else F32),
        scratch_shapes=[pltpu.VMEM((nb * state_shape[0],) + state_shape[1:], F32)]
        + ([pltpu.VMEM((nb, c, width), F32)] if bwd else []),
        compiler_params=_cparams(("parallel", "arbitrary")),
        name=name + ("_bwd" if bwd else "_fwd"),
    )(*in_arrays)


def _batch_per_step(bsz, most):
    return max(nb for nb in (1, 2, 4) if nb <= most and bsz % nb == 0)


def _rows_per_step(n_ctx, l):
    return _pick_tile(math.gcd(n_ctx, l - n_ctx), (256, 128, 64))


def hgrn_mixer(proj, col, lb_logits, layer, n_ctx, gate_col, norm_g):
    bsz, l, _ = proj.shape
    depth, _, width = lb_logits.shape
    c = CHUNK
    heads, dk = width // A_HEAD_DIM, A_HEAD_DIM
    cq, cf, ci = (x // width for x in col)
    norm_row = jnp.tile(norm_g, heads).reshape(1, width)
    nb = _batch_per_step(bsz, 4)
    rs = _rows_per_step(n_ctx, l)
    o_fwd = None
    for bwd in (False, True):
        w_np, m_np, nlev = _decay_constants(c, bwd)
        chunk = _chunk_order(n_ctx // rs, l // rs, bwd)
        d = int(bwd)
        kern = functools.partial(_hgrn_kernel, layer=layer, heads=heads, dk=dk, nlev=nlev, c=c, bwd=bwd)
        in_specs = [
            pl.BlockSpec((nb, rs, width), lambda b, n, chunk=chunk: (b, chunk(n), cq)),
            pl.BlockSpec((nb, rs, width), lambda b, n, chunk=chunk, d=d: (b, chunk(n), cf + d)),
            pl.BlockSpec((nb, rs, width), lambda b, n, chunk=chunk: (b, chunk(n), ci)),
            pl.BlockSpec((depth, width), lambda b, n: (0, 0)),
            pl.BlockSpec(w_np.shape, lambda b, n: (0, 0)),
            pl.BlockSpec(m_np.shape, lambda b, n: (0, 0, 0)),
        ]
        in_arrays = [proj, proj, proj, lb_logits[:, d], jnp.asarray(w_np, BF16), jnp.asarray(m_np)]
        o_fwd = _mixer_call(kern, "hgrn_mixer", proj, in_arrays, in_specs, chunk, nb, rs, c, width,
                            (heads, dk, dk), o_fwd, gate_col, norm_row)
    return o_fwd


def gla_mixer(proj, col, wgk_pad, b_gk, n_ctx, width, gate_col, norm_g):
    bsz, l, _ = proj.shape
    c = CHUNK
    key_w = wgk_pad.shape[-1]
    heads = B_HEADS
    dk, dv = key_w // heads, width // heads
    cq, ck, cv, clr = col
    norm_row = jnp.tile(norm_g, heads).reshape(1, width)
    nb = _batch_per_step(bsz, 4)
    rs = _rows_per_step(n_ctx, l)
    o_fwd = None
    for bwd in (False, True):
        w_np, m_np, nlev = _decay_constants(c, bwd)
        chunk = _chunk_order(n_ctx // rs, l // rs, bwd)
        d = int(bwd)
        kern = functools.partial(_gla_kernel, heads=heads, dk=dk, dv=dv, nlev=nlev, c=c, bwd=bwd)
        in_specs = [
            pl.BlockSpec((nb, rs, key_w), lambda b, n, chunk=chunk: (b, chunk(n), cq // key_w)),
            pl.BlockSpec((nb, rs, key_w), lambda b, n, chunk=chunk: (b, chunk(n), ck // key_w)),
            pl.BlockSpec((nb, rs, width), lambda b, n, chunk=chunk: (b, chunk(n), cv // width)),
            pl.BlockSpec((nb, rs, LANES), lambda b, n, chunk=chunk: (b, chunk(n), clr // LANES)),
            pl.BlockSpec((LANES, key_w), lambda b, n: (0, 0)),
            pl.BlockSpec((1, key_w), lambda b, n: (0, 0)),
            pl.BlockSpec(w_np.shape, lambda b, n: (0, 0)),
            pl.BlockSpec(m_np.shape, lambda b, n: (0, 0, 0)),
        ]
        in_arrays = [proj, proj, proj, proj, wgk_pad[d], b_gk[d].reshape(1, key_w),
                     jnp.asarray(w_np, BF16), jnp.asarray(m_np)]
        o_fwd = _mixer_call(kern, "gla_mixer", proj, in_arrays, in_specs, chunk, nb, rs, c, width,
                            (heads, dv, dk), o_fwd, gate_col, norm_row)
    return o_fwd


def _retention_kernel(*refs, heads, dk, dv, c, bwd):
    (q_ref, k_ref, v_ref, cos_ref, sin_ref, dl_ref), fin, (o_ref, state_ref, o_scr) = \
        _split_refs(refs, 6, bwd)

    @pl.when(pl.program_id(1) == 0)
    def _():
        state_ref[...] = jnp.zeros_like(state_ref)

    ii = lax.broadcasted_iota(jnp.int32, (c, c), 0).astype(F32)
    jj = lax.broadcasted_iota(jnp.int32, (c, c), 1).astype(F32)
    rel = (jj - ii) if bwd else (ii - jj)
    t_col = lax.broadcasted_iota(jnp.int32, (c, 1), 0).astype(F32)
    since = ((c - 1.0) - t_col) if bwd else t_col
    log_gamma = _log_sigmoid(dl_ref[...]) * LOG2E
    cos, sin = cos_ref[...], sin_ref[...]
    half = dk // 2
    for h in range(heads):
        ks = slice(h * dk, (h + 1) * dk)
        vs = slice(h * dv, (h + 1) * dv)
        lg = log_gamma[:, h:h + 1]
        dmat = jnp.where(rel >= 0, jnp.exp2(lg * jnp.maximum(rel, 0.0)), 0.0)
        xi = jnp.exp2(lg * (since + 1.0))
        zeta = jnp.exp2(lg * ((c - 1.0) - since))
        for bb in range(q_ref.shape[0]):
            qh, kh = q_ref[bb, :, ks], k_ref[bb, :, ks]
            qh = (qh * cos + pltpu.roll(qh, half, 1) * sin) * (dk ** -0.5)
            kh = kh * cos + pltpu.roll(kh, half, 1) * sin
            vb = v_ref[bb, :, vs].astype(BF16)
            st = state_ref[bb * heads + h]
            scores = _dot_nt(qh.astype(BF16), kh.astype(BF16)) * dmat
            o = _dot(scores.astype(BF16), vb) + _dot_nt(qh.astype(BF16), st.astype(BF16)) * xi
            state_ref[bb * heads + h] = st * jnp.exp2(lg * c) + _dot_tn(vb, (kh * zeta).astype(BF16))
            o_scr[bb, :, vs] = o
    _finish(o_scr, *fin, o_ref, heads, dv, True, bwd)


def retention_mixer(proj, col, cos_t, sin_t, decay_logit_pad, n_ctx, width, gate_col, norm_g):
    bsz, l, _ = proj.shape
    heads = D_HEADS
    key_w = width // 2
    dk, dv = key_w // heads, width // heads
    c = _rows_per_step(n_ctx, l)
    cq, ck, cv = col
    norm_row = jnp.tile(norm_g, heads).reshape(1, width)
    nb = _batch_per_step(bsz, 2)
    o_fwd = None
    for bwd in (False, True):
        chunk = _chunk_order(n_ctx // c, l // c, bwd)
        kern = functools.partial(_retention_kernel, heads=heads, dk=dk, dv=dv, c=c, bwd=bwd)
        in_specs = [
            pl.BlockSpec((nb, c, key_w), lambda b, n, chunk=chunk: (b, chunk(n), cq // key_w)),
            pl.BlockSpec((nb, c, key_w), lambda b, n, chunk=chunk: (b, chunk(n), ck // key_w)),
            pl.BlockSpec((nb, c, width), lambda b, n, chunk=chunk: (b, chunk(n), cv // width)),
            pl.BlockSpec((c, dk), lambda b, n, chunk=chunk: (chunk(n), 0)),
            pl.BlockSpec((c, dk), lambda b, n, chunk=chunk: (chunk(n), 0)),
            pl.BlockSpec((1, LANES), lambda b, n: (0, 0)),
        ]
        in_arrays = [proj, proj, proj, cos_t, sin_t, decay_logit_pad[int(bwd)]]
        o_fwd = _mixer_call(kern, "retention_mixer", proj, in_arrays, in_specs, chunk, nb, c, c, width,
                            (heads, dv, dk), o_fwd, gate_col, norm_row)
    return o_fwd


def _s5_kernel(u_ref, lam_ref, bt_ref, cm_ref, y_ref,
               toep_ref, win_r_ref, win_i_ref, wout_ref, cl_ref, s_r_ref, s_i_ref, xp_ref, *,
               bsz, n_chunks_ctx, n_chunks):
    t_len, hc, half = S5_CHUNK, C_GROUP, C_STATE
    lane = lax.broadcasted_iota(jnp.int32, (1, LANES), 1)
    lo = lane < half
    sgn = jnp.where(lo, -1.0, 1.0)
    tau = lax.broadcasted_iota(jnp.int32, (t_len, 1), 0)
    u = u_ref[0]
    rows = u.shape[0]
    y_acc = jnp.zeros((rows, t_len * hc), F32)

    def cmul(ar, ai, br, bi):
        return ar * br - ai * bi, ar * bi + ai * br

    def expand(x1, pa, x2, pb, out_ref):
        for t in range(t_len):
            blk = (x1 * jnp.broadcast_to(pa[t:t + 1], (hc, LANES))
                   + x2 * jnp.broadcast_to(pb[t:t + 1], (hc, LANES)))
            out_ref[t * hc:(t + 1) * hc, :] = blk.astype(out_ref.dtype)

    for d in range(2):
        lam_re = jnp.minimum(lam_ref[d, 0, 0:1], C_MAX_RE)
        lam_im = lam_ref[d, 0, 1:2]
        dt = jnp.exp(lam_ref[d, 0, 2:3])
        mag = jnp.exp(lam_re * dt)
        lb_r, lb_i = mag * jnp.cos(lam_im * dt), mag * jnp.sin(lam_im * dt)
        den = lam_re * lam_re + lam_im * lam_im
        nr, ni = lb_r - 1.0, lb_i
        cf_r, cf_i = (nr * lam_re + ni * lam_im) / den, (ni * lam_re - nr * lam_im) / den
        bt_r, bt_i = bt_ref[d, 0, 0], bt_ref[d, 0, 1]
        bb_r, bb_i = cmul(cf_r, cf_i, bt_r, bt_i)
        c_r, c_i = cm_ref[d, 0, 0], cm_ref[d, 0, 1]

        p_r, p_i = jnp.ones((t_len, LANES), F32), jnp.zeros((t_len, LANES), F32)
        q_r, q_i = p_r, p_i
        sq_r, sq_i = lb_r, lb_i
        for bit in range(int(math.log2(t_len))):
            sel = ((tau >> bit) & 1) == 1
            p_r, p_i = cmul(p_r, p_i, jnp.where(sel, sq_r, 1.0), jnp.where(sel, sq_i, 0.0))
            selq = (((t_len - 1 - tau) >> bit) & 1) == 1
            q_r, q_i = cmul(q_r, q_i, jnp.where(selq, sq_r, 1.0), jnp.where(selq, sq_i, 0.0))
            sq_r, sq_i = cmul(sq_r, sq_i, sq_r, sq_i)
        lc_r, lc_i = sq_r, sq_i
        if d == 0:
            toep_p, in_p = (p_r, p_i), (q_r, q_i)
            out_p = cmul(p_r, p_i, lb_r, lb_i)
        else:
            toep_p, in_p = (q_r, q_i), (p_r, p_i)
            out_p = cmul(q_r, q_i, lb_r, lb_i)

        tp_r, tp_i = toep_p
        expand(c_r, jnp.where(lo, tp_r, tp_i), sgn * c_i, jnp.where(lo, tp_i, tp_r), cl_ref)
        bbs = jnp.where(lo, bb_r, -bb_i)
        kt = lax.dot_general(bbs, cl_ref[...], (((1,), (1,)), ((), ())),
                             precision=lax.Precision.HIGHEST, preferred_element_type=F32)
        width = t_len * hc
        glane = lax.broadcasted_iota(jnp.int32, (hc, width), 1)
        per_tile = LANES // hc
        for m in range(per_tile):
            if d == 0:
                base = kt if m == 0 else jnp.where(glane >= hc * m, pltpu.roll(kt, hc * m, 1), 0.0)
            else:
                base = kt if m == 0 else jnp.where(glane < width - hc * m,
                                                   pltpu.roll(kt, width - hc * m, 1), 0.0)
            base = base.astype(BF16)
            for a in range(t_len // per_tile):
                off = a * LANES
                if d == 0:
                    j = a * per_tile + m
                    if off:
                        toep_ref[d, j * hc:(j + 1) * hc, :off] = jnp.zeros((hc, off), BF16)
                    toep_ref[d, j * hc:(j + 1) * hc, off:] = base[:, :width - off]
                else:
                    j = t_len - 1 - (a * per_tile + m)
                    if off:
                        toep_ref[d, j * hc:(j + 1) * hc, width - off:] = jnp.zeros((hc, off), BF16)
                    toep_ref[d, j * hc:(j + 1) * hc, :width - off] = base[:, off:]

        ip_r, ip_i = in_p
        expand(bb_r, ip_r, -bb_i, ip_i, win_r_ref)
        expand(bb_i, ip_r, bb_r, ip_i, win_i_ref)
        op_r, op_i = out_p
        expand(c_r, jnp.where(lo, op_r, -op_i), c_i, jnp.where(lo, -op_i, -op_r), wout_ref)

        s_r_ref[...] = _dot(u, win_r_ref[...])
        s_i_ref[...] = _dot(u, win_i_ref[...])
        if d == 0:
            order = list(range(n_chunks))
        else:
            order = list(range(n_chunks_ctx - 1, -1, -1)) + list(range(n_chunks - 1, n_chunks_ctx - 1, -1))
        x_r, x_i = jnp.zeros((bsz, LANES), F32), jnp.zeros((bsz, LANES), F32)
        for n in order:
            rs = slice(n * bsz, (n + 1) * bsz)
            xp_ref[rs, :] = jnp.where(lo, x_r, x_i)
            nx_r, nx_i = cmul(lc_r, lc_i, x_r, x_i)
            x_r, x_i = nx_r + s_r_ref[rs, :], nx_i + s_i_ref[rs, :]

        nblk = width // MXU_COLS
        cols = []
        for ib in range(nblk):
            acc = None
            for jb in (range(ib + 1) if d == 0 else range(ib, nblk)):
                term = _dot(u[:, jb * MXU_COLS:(jb + 1) * MXU_COLS],
                            toep_ref[d, jb * MXU_COLS:(jb + 1) * MXU_COLS,
                                     ib * MXU_COLS:(ib + 1) * MXU_COLS])
                acc = term if acc is None else acc + term
            cols.append(acc)
        y_acc = (y_acc + jnp.concatenate(cols, axis=1)
                 + _dot_nt(xp_ref[...].astype(BF16), wout_ref[...]))
    y_ref[0] = y_acc


def s5_core(ug, lam_pk, bt_pk, cm_pk, bsz, n_chunks_ctx, n_chunks):
    groups, rows, width = ug.shape
    kern = functools.partial(_s5_kernel, bsz=bsz, n_chunks_ctx=n_chunks_ctx, n_chunks=n_chunks)
    return pl.pallas_call(
        kern,
        grid=(groups,),
        in_specs=[
            pl.BlockSpec((1, rows, width), lambda g: (g, 0, 0)),
            pl.BlockSpec((2, 1, SUBLANES, LANES), lambda g: (0, g, 0, 0)),
            pl.BlockSpec((2, 1, 2, C_GROUP, LANES), lambda g: (0, g, 0, 0, 0)),
            pl.BlockSpec((2, 1, 2, C_GROUP, LANES), lambda g: (0, g, 0, 0, 0)),
        ],
        out_specs=pl.BlockSpec((1, rows, width), lambda g: (g, 0, 0)),
        out_shape=jax.ShapeDtypeStruct((groups, rows, width), F32),
        scratch_shapes=[
            pltpu.VMEM((2, width, width), BF16),
            pltpu.VMEM((width, LANES), BF16),
            pltpu.VMEM((width, LANES), BF16),
            pltpu.VMEM((width, LANES), BF16),
            pltpu.VMEM((width, LANES), F32),
            pltpu.VMEM((rows, LANES), F32),
            pltpu.VMEM((rows, LANES), F32),
            pltpu.VMEM((rows, LANES), F32),
        ],
        compiler_params=_cparams(("parallel",)),
        name="s5_core",
    )(ug, lam_pk, bt_pk, cm_pk)


def _s5_post_kernel(y_ref, u_ref, gate_ref, d_ref, w_ref, b_ref, o_ref):
    y = y_ref[0] + d_ref[...] * u_ref[0]
    z = jax.nn.gelu(y)
    t = _dot(z.astype(BF16), w_ref[...]) + b_ref[...]
    o_ref[0] = (z * _sigmoid(t) * _silu(gate_ref[0])).astype(BF16)


def s5_post(y, proj, u_col, gate_col, d_skip, w_glu, b_glu):
    bsz, l, width = y.shape
    tr = ROW_TILE
    return pl.pallas_call(
        _s5_post_kernel,
        grid=(bsz, l // tr),
        in_specs=[
            pl.BlockSpec((1, tr, width), lambda b, i: (b, i, 0)),
            pl.BlockSpec((1, tr, width), lambda b, i: (b, i, u_col // width)),
            pl.BlockSpec((1, tr, width), lambda b, i: (b, i, gate_col // width)),
            pl.BlockSpec((1, width), lambda b, i: (0, 0)),
            pl.BlockSpec((width, width), lambda b, i: (0, 0)),
            pl.BlockSpec((1, width), lambda b, i: (0, 0)),
        ],
        out_specs=pl.BlockSpec((1, tr, width), lambda b, i: (b, i, 0)),
        out_shape=jax.ShapeDtypeStruct((bsz, l, width), BF16),
        compiler_params=_cparams(("parallel", "parallel")),
        name="s5_post",
    )(y, proj, proj, d_skip.reshape(1, width), w_glu.astype(BF16), b_glu.reshape(1, width))


def _outproj_kernel(oa_ref, ob_ref, oc_ref, od_ref, w_ref, h_ref, gl_ref, gc_ref, o_ref, *,
                    n_ctx, tm, bw):
    acc = _dot(oa_ref[0], w_ref[0, 0:bw, :].astype(BF16))
    acc = acc + _dot(ob_ref[0], w_ref[0, bw:2 * bw, :].astype(BF16))
    acc = acc + _dot(oc_ref[0], w_ref[0, 2 * bw:3 * bw, :].astype(BF16))
    acc = acc + _dot(od_ref[0], w_ref[0, 3 * bw:4 * bw, :].astype(BF16))
    row = pl.program_id(1) * tm + lax.broadcasted_iota(jnp.int32, (tm, 1), 0)
    gate = jnp.where(row < n_ctx, gc_ref[0], gl_ref[0])
    o_ref[0] = h_ref[0] + gate * acc


def out_projection(o_parts, w_out, h, mods_flat, layer, n_ctx):
    bsz, l, d = h.shape
    bw = o_parts[0].shape[-1]
    tm = l if l <= 2048 else _pick_tile(l, (l // 4, l // 8, l // 16))
    tn = _pick_tile(d, (512, 256, 128))
    kern = functools.partial(_outproj_kernel, n_ctx=n_ctx, tm=tm, bw=bw)
    o_spec = pl.BlockSpec((1, tm, bw), lambda b, i, j: (b, i, 0))
    return pl.pallas_call(
        kern,
        grid=(bsz, l // tm, d // tn),
        in_specs=[
            o_spec, o_spec, o_spec, o_spec,
            pl.BlockSpec((1, 4 * bw, tn), lambda b, i, j: (layer, 0, j)),
            pl.BlockSpec((1, tm, tn), lambda b, i, j: (b, i, j)),
            pl.BlockSpec((1, 1, tn), lambda b, i, j: ((layer * SUBLANES + b) * 3 + 2, 0, j)),
            pl.BlockSpec((1, 1, tn), lambda b, i, j: ((layer * SUBLANES + bsz) * 3 + 2, 0, j)),
        ],
        out_specs=pl.BlockSpec((1, tm, tn), lambda b, i, j: (b, i, j)),
        out_shape=jax.ShapeDtypeStruct((bsz, l, d), F32),
        compiler_params=_cparams(("parallel", "parallel", "parallel")),
        name="out_projection",
    )(*o_parts, w_out, h, mods_flat, mods_flat)


def _final_norm_kernel(x_ref, g_ref, o_ref):
    x = x_ref[0]
    o_ref[0] = x * lax.rsqrt(jnp.mean(x * x, axis=-1, keepdims=True) + EPS) * g_ref[...]


def final_norm(h, g, n_ctx):
    bsz, l, d = h.shape
    tr = ROW_TILE
    skip = n_ctx // tr
    return pl.pallas_call(
        _final_norm_kernel,
        grid=(bsz, (l - n_ctx) // tr),
        in_specs=[
            pl.BlockSpec((1, tr, d), lambda b, i: (b, i + skip, 0)),
            pl.BlockSpec((1, d), lambda b, i: (0, 0)),
        ],
        out_specs=pl.BlockSpec((1, tr, d), lambda b, i: (b, i, 0)),
        out_shape=jax.ShapeDtypeStruct((bsz, l - n_ctx, d), F32),
        compiler_params=_cparams(("parallel", "parallel")),
        name="final_norm",
    )(h, g.reshape(1, d))


def _rope_tables(rows, n_ctx, dk):
    quarter = dk // 4
    freqs = ROPE_BASE ** (-jnp.arange(quarter, dtype=F32) / quarter)
    t = jnp.arange(rows * GRID_W)
    r = (t // GRID_W).astype(F32)
    col = (t % GRID_W).astype(F32)
    ang = jnp.concatenate([r[:, None] * freqs, col[:, None] * freqs], axis=-1)
    ang = jnp.concatenate([jnp.zeros((n_ctx, dk // 2), F32), ang], axis=0)
    cos, sin = jnp.cos(ang), jnp.sin(ang)
    return jnp.concatenate([cos, cos], axis=-1), jnp.concatenate([-sin, sin], axis=-1)


def _dup(x):
    return jnp.concatenate([x, x], axis=-1)


def mixer_layer(xn, h, mods_flat, layer, n_ctx, w_in_perm, hgrn_lb_logits, hgrn_norm_g, gla_w_gk,
                gla_b_gk, gla_norm_g, s5_lam_re, s5_lam_im, s5_log_dt, s5_b_re, s5_b_im, s5_c_re,
                s5_c_im, s5_d, s5_w_glu, s5_b_glu, ret_decay_logit, ret_norm_g, w_out, rope):
    bsz, l, d = h.shape
    bw = d // 4
    kw = bw // 2
    rank = B_GATE_RANK

    names = ("a_q", "a_ff", "a_fb", "a_i", "a_g", "b_q", "b_k", "b_v", "b_g", "c_u", "c_g",
             "d_q", "d_k", "d_v", "d_g", "b_lr")
    widths = (bw, bw, bw, bw, bw, kw, kw, bw, bw, bw, bw, kw, kw, bw, bw, LANES)
    col = dict(zip(names, np.concatenate([[0], np.cumsum(widths)[:-1]]).tolist()))

    proj = in_projection(xn.reshape(bsz * l, d), w_in_perm, layer).reshape(bsz, l, -1)

    o_a = hgrn_mixer(proj, (col["a_q"], col["a_ff"], col["a_i"]), hgrn_lb_logits, layer, n_ctx,
                     col["a_g"], hgrn_norm_g[layer])

    wgk = gla_w_gk[layer].astype(BF16)
    wgk_pad = jnp.zeros((2, LANES, kw), BF16)
    wgk_pad = wgk_pad.at[0, :rank].set(wgk[0]).at[1, rank:2 * rank].set(wgk[1])
    o_b = gla_mixer(proj, (col["b_q"], col["b_k"], col["b_v"], col["b_lr"]), wgk_pad,
                    gla_b_gk[layer], n_ctx, bw, col["b_g"], gla_norm_g[layer])

    groups = bw // C_GROUP
    nck = l // S5_CHUNK
    u = proj[:, :, col["c_u"]:col["c_u"] + bw].astype(BF16)
    ug = u.reshape(bsz, nck, S5_CHUNK, groups, C_GROUP).transpose(3, 1, 0, 2, 4)
    ug = ug.reshape(groups, nck * bsz, S5_CHUNK * C_GROUP)
    dt_row = jnp.broadcast_to(s5_log_dt[layer][..., None], (2, groups, C_STATE))
    lam_pk = jnp.stack([_dup(s5_lam_re[layer]), _dup(s5_lam_im[layer]), _dup(dt_row)], axis=2)
    lam_pk = jnp.pad(lam_pk, ((0, 0), (0, 0), (0, SUBLANES - 3), (0, 0)))
    bt_pk = jnp.stack([_dup(jnp.swapaxes(s5_b_re[layer], -1, -2)),
                       _dup(jnp.swapaxes(s5_b_im[layer], -1, -2))], axis=2)
    cm_pk = jnp.stack([_dup(s5_c_re[layer]), _dup(s5_c_im[layer])], axis=2)
    yg = s5_core(ug, lam_pk, bt_pk, cm_pk, bsz, n_ctx // S5_CHUNK, nck)
    y = yg.reshape(groups, nck, bsz, S5_CHUNK, C_GROUP).transpose(2, 1, 3, 0, 4).reshape(bsz, l, bw)
    o_c = s5_post(y, proj, col["c_u"], col["c_g"], s5_d[layer], s5_w_glu[layer], s5_b_glu[layer])

    dl = jnp.pad(ret_decay_logit[layer], ((0, 0), (0, LANES - D_HEADS))).reshape(2, 1, LANES)
    o_d = retention_mixer(proj, (col["d_q"], col["d_k"], col["d_v"]), rope[0], rope[1], dl, n_ctx, bw,
                          col["d_g"], ret_norm_g[layer])

    return out_projection((o_a, o_b, o_c, o_d), w_out, h, mods_flat, layer, n_ctx)


def kernel(x, c, ctx, c_ctx, norm_g, w_ada, b_ada, w_in, hgrn_lb_logits, hgrn_norm_g, gla_w_gk,
           gla_b_gk, gla_norm_g, s5_lam_re, s5_lam_im, s5_log_dt, s5_b_re, s5_b_im, s5_c_re, s5_c_im,
           s5_d, s5_w_glu, s5_b_glu, ret_decay_logit, ret_norm_g, w_out, final_norm_g):
    bsz, seq, d = x.shape
    n_ctx = ctx.shape[1]
    depth = w_in.shape[0]
    assert bsz < SUBLANES and n_ctx % ROW_TILE == 0 and seq % ROW_TILE == 0

    cvec = jnp.concatenate([c, c_ctx[None], jnp.zeros((SUBLANES - bsz - 1, d), F32)], axis=0)
    mods = ada_modulation(cvec, w_ada, b_ada)
    mods_flat = mods.reshape(depth * SUBLANES * 3, 1, d)
    rope = _rope_tables(seq // GRID_W, n_ctx, (d // 8) // D_HEADS)

    w_in_perm = permute_in_weights(w_in)
    h = None
    for layer in range(depth):
        if layer == 0:
            xn, h = prenorm_first(ctx, x, norm_g[0], mods_flat, n_ctx)
        else:
            xn = prenorm(h, norm_g[layer], mods_flat, layer, n_ctx)
        h = mixer_layer(xn, h, mods_flat, layer, n_ctx, w_in_perm, hgrn_lb_logits, hgrn_norm_g,
                        gla_w_gk, gla_b_gk, gla_norm_g, s5_lam_re, s5_lam_im, s5_log_dt, s5_b_re,
                        s5_b_im, s5_c_re, s5_c_im, s5_d, s5_w_glu, s5_b_glu, ret_decay_logit,
                        ret_norm_g, w_out, rope)
    return final_norm(h, final_norm_g, n_ctx)
```

```python
import functools
import math

import numpy as np
import jax
import jax.numpy as jnp
from jax import lax
from jax.experimental import pallas as pl
from jax.experimental.pallas import tpu as pltpu

F32 = jnp.float32
BF16 = jnp.bfloat16

EPS = 1e-6
A_HEAD_DIM = 128
A_MIN_FORGET = 1e-6
B_HEADS = 4
B_GATE_RANK = 16
B_GATE_NORM = 16.0
C_GROUP = 16
C_STATE = 64
C_MAX_RE = -1e-4
D_HEADS = 4
GRID_W = 64
ROPE_BASE = 10000.0

LANES = 128
SUBLANES = 8
BF16_ROWS = 16
VMEM_LIMIT = 56 * 1024 * 1024

MXU_COLS = 256
IN_PROJ_TN = 6 * MXU_COLS

CHUNK = 64
LOG2E = math.log2(math.e)
SHORT_SPAN = 86.0
S5_CHUNK = 64
ROW_TILE = 256


def _cparams(sem):
    return pltpu.CompilerParams(dimension_semantics=sem, vmem_limit_bytes=VMEM_LIMIT)


def _dot(a, b):
    return jnp.dot(a, b, preferred_element_type=F32)


def _dot_nt(a, b):
    return lax.dot_general(a, b, (((1,), (1,)), ((), ())), preferred_element_type=F32)


def _dot_tn(a, b):
    return lax.dot_general(a, b, (((0,), (0,)), ((), ())), preferred_element_type=F32)


def _sigmoid(x):
    return 1.0 / (1.0 + jnp.exp(-x))


def _silu(x):
    return x * _sigmoid(x)


def _log_sigmoid(x):
    return jnp.minimum(x, 0.0) - jnp.log(1.0 + jnp.exp(-jnp.abs(x)))


def _ada_kernel(c_ref, w_ref, b_ref, o_ref):
    cv = _silu(c_ref[...]).astype(BF16)
    o_ref[0] = _dot(cv, w_ref[0].astype(BF16)) + b_ref[0]


def ada_modulation(cvec, w_ada, b_ada):
    depth, d, n3 = w_ada.shape
    tn = 512
    return pl.pallas_call(
        _ada_kernel,
        grid=(depth, n3 // tn),
        in_specs=[
            pl.BlockSpec((SUBLANES, d), lambda l, j: (0, 0)),
            pl.BlockSpec((1, d, tn), lambda l, j: (l, 0, j)),
            pl.BlockSpec((1, 1, tn), lambda l, j: (l, 0, j)),
        ],
        out_specs=pl.BlockSpec((1, SUBLANES, tn), lambda l, j: (l, 0, j)),
        out_shape=jax.ShapeDtypeStruct((depth, SUBLANES, n3), F32),
        compiler_params=_cparams(("parallel", "parallel")),
        name="ada_modulation",
    )(cvec, w_ada, b_ada.reshape(depth, 1, n3))


def _prenorm_kernel(x_ref, g_ref, sh_ref, sc_ref, o_ref):
    x = x_ref[0]
    y = x * lax.rsqrt(jnp.mean(x * x, axis=-1, keepdims=True) + EPS) * g_ref[...]
    o_ref[0] = (y * (1.0 + sc_ref[0]) + sh_ref[0]).astype(BF16)


def prenorm(h, g, mods_flat, layer, n_ctx):
    bsz, l, d = h.shape
    tr = ROW_TILE
    nct = n_ctx // tr

    def mod_map(part):
        def f(b, i):
            row = jnp.where(i < nct, bsz, b)
            return ((layer * SUBLANES + row) * 3 + part, 0, 0)
        return f

    return pl.pallas_call(
        _prenorm_kernel,
        grid=(bsz, l // tr),
        in_specs=[
            pl.BlockSpec((1, tr, d), lambda b, i: (b, i, 0)),
            pl.BlockSpec((1, d), lambda b, i: (0, 0)),
            pl.BlockSpec((1, 1, d), mod_map(0)),
            pl.BlockSpec((1, 1, d), mod_map(1)),
        ],
        out_specs=pl.BlockSpec((1, tr, d), lambda b, i: (b, i, 0)),
        out_shape=jax.ShapeDtypeStruct((bsz, l, d), BF16),
        compiler_params=_cparams(("parallel", "parallel")),
        name="prenorm",
    )(h, g.reshape(1, d), mods_flat, mods_flat)


def _prenorm_first_kernel(c_ref, x_ref, g_ref, sh_ref, sc_ref, o_ref, h_ref, *, nct):
    x = jnp.where(pl.program_id(1) < nct, c_ref[0], x_ref[0])
    h_ref[0] = x
    y = x * lax.rsqrt(jnp.mean(x * x, axis=-1, keepdims=True) + EPS) * g_ref[...]
    o_ref[0] = (y * (1.0 + sc_ref[0]) + sh_ref[0]).astype(BF16)


def prenorm_first(ctx, x, g, mods_flat, n_ctx):
    bsz, seq, d = x.shape
    l = n_ctx + seq
    tr = ROW_TILE
    nct = n_ctx // tr

    def mod_map(part):
        def f(b, i):
            row = jnp.where(i < nct, bsz, b)
            return (row * 3 + part, 0, 0)
        return f

    row_spec = pl.BlockSpec((1, tr, d), lambda b, i: (b, i, 0))
    return pl.pallas_call(
        functools.partial(_prenorm_first_kernel, nct=nct),
        grid=(bsz, l // tr),
        in_specs=[
            pl.BlockSpec((1, tr, d), lambda b, i: (b, jnp.minimum(i, nct - 1), 0)),
            pl.BlockSpec((1, tr, d), lambda b, i: (b, jnp.maximum(i - nct, 0), 0)),
            pl.BlockSpec((1, d), lambda b, i: (0, 0)),
            pl.BlockSpec((1, 1, d), mod_map(0)),
            pl.BlockSpec((1, 1, d), mod_map(1)),
        ],
        out_specs=[row_spec, row_spec],
        out_shape=[jax.ShapeDtypeStruct((bsz, l, d), BF16), jax.ShapeDtypeStruct((bsz, l, d), F32)],
        compiler_params=_cparams(("parallel", "arbitrary")),
        name="prenorm_first",
    )(ctx, x, g.reshape(1, d), mods_flat, mods_flat)


def _wprep_kernel(a_ref, b_ref, o_ref, *, first_shifted, tail_block, lr_w):
    i = pl.program_id(1)
    tr = a_ref.shape[1]

    @pl.when(i < first_shifted)
    def _():
        o_ref[0] = a_ref[0].astype(BF16)

    @pl.when(jnp.logical_and(i >= first_shifted, i < tail_block))
    def _():
        o_ref[0, :tr - lr_w] = a_ref[0, lr_w:].astype(BF16)
        o_ref[0, tr - lr_w:] = b_ref[0].astype(BF16)

    @pl.when(i == tail_block)
    def _():
        o_ref[0, :lr_w] = b_ref[0].astype(BF16)
        o_ref[0, lr_w:] = jnp.zeros((tr - lr_w, a_ref.shape[2]), BF16)

    @pl.when(i > tail_block)
    def _():
        o_ref[0] = jnp.zeros(o_ref.shape[1:], BF16)


def permute_in_weights(w_in):
    depth, d, n = w_in.shape
    bw, lr_w = d // 4, 2 * B_GATE_RANK
    lr0 = 5 * bw + 2 * (bw // 2) + bw
    tr = 2 * LANES
    assert (n - lr_w) % tr == 0 and lr0 % tr == 0 and tr % lr_w == 0
    w_t = jnp.swapaxes(w_in, 1, 2)
    first_shifted, tail_block = lr0 // tr, (n - lr_w) // tr
    per = tr // lr_w
    n_out = -(-(tail_block + 1) * tr // IN_PROJ_TN) * IN_PROJ_TN

    def a_map(l, i):
        return (l, jnp.minimum(i, tail_block - 1), 0)

    def b_map(l, i):
        return (l, jnp.where(i == tail_block, lr0 // lr_w,
                             jnp.minimum(i + 1, tail_block) * per), 0)

    return pl.pallas_call(
        functools.partial(_wprep_kernel, first_shifted=first_shifted, tail_block=tail_block, lr_w=lr_w),
        grid=(depth, n_out // tr),
        in_specs=[pl.BlockSpec((1, tr, d), a_map), pl.BlockSpec((1, lr_w, d), b_map)],
        out_specs=pl.BlockSpec((1, tr, d), lambda l, i: (l, i, 0)),
        out_shape=jax.ShapeDtypeStruct((depth, n_out, d), BF16),
        compiler_params=_cparams(("parallel", "parallel")),
        name="permute_in_weights",
    )(w_t, w_t)


def _matmul_kernel(x_ref, wt_ref, o_ref):
    o_ref[...] = _dot_nt(x_ref[...], wt_ref[0])


def _pick_tile(n, candidates):
    for c in candidates:
        if n % c == 0:
            return c
    return n


def in_projection(xn, w_all, layer):
    m, d = xn.shape
    n = w_all.shape[1]
    tm = _pick_tile(m, (1024, 512, 256, 128, 64, 32, 16))
    tn = IN_PROJ_TN
    return pl.pallas_call(
        _matmul_kernel,
        grid=(m // tm, n // tn),
        in_specs=[
            pl.BlockSpec((tm, d), lambda i, j: (i, 0), pipeline_mode=pl.Buffered(1)),
            pl.BlockSpec((1, tn, d), lambda i, j: (layer, j, 0)),
        ],
        out_specs=pl.BlockSpec((tm, tn), lambda i, j: (i, j)),
        out_shape=jax.ShapeDtypeStruct((m, n), F32),
        compiler_params=_cparams(("parallel", "parallel")),
        name="in_projection",
    )(xn, w_all)


def _decay_constants(c, bwd):
    nlev = int(math.log2(c))
    w = np.zeros((nlev + 2, c, c), np.float32)
    masks = np.zeros((nlev, c, c), np.float32)
    for lev in range(nlev):
        s = 1 << lev
        for r in range(c):
            pos = r % (2 * s)
            mid = r - pos + s
            if pos >= s:
                w[lev, r, mid:r + 1] = 1.0
            else:
                w[lev, r, r + 1:mid] = 1.0
        for i in range(c):
            for j in range(c):
                if i // (2 * s) == j // (2 * s) and i % (2 * s) >= s and j % (2 * s) < s:
                    masks[lev, i, j] = 1.0
    for r in range(c):
        w[nlev, r, :r + 1] = 1.0
        w[nlev + 1, r, r + 1:] = 1.0
    half = c // 2
    ii, jj = np.meshgrid(np.arange(c), np.arange(c), indexing="ij")
    diag = ((ii // half == jj // half) & (jj <= ii)).astype(np.float32)
    causal = (jj <= ii).astype(np.float32)
    masks = np.concatenate([masks, diag[None], causal[None]], axis=0)
    w = w.reshape((nlev + 2) * c, c)
    w = np.concatenate([w, np.ones((BF16_ROWS, c), np.float32)], axis=0)
    if bwd:
        w = np.concatenate([w[:-BF16_ROWS].reshape(nlev + 2, c, c)[:, ::-1, ::-1].reshape(-1, c),
                            w[-BF16_ROWS:]], axis=0)
        masks = masks[:, ::-1, ::-1]
    return np.ascontiguousarray(w), np.ascontiguousarray(masks), nlev


def _split2(x):
    hi = x.astype(BF16)
    return hi, (x - hi.astype(F32)).astype(BF16)


def _gated_core(qs, ks_, vs_, gs, w_ref, mask_ref, o_scr, state_ref, heads, dk, dv, nlev, c, bwd,
                single_route):
    half = c // 2
    nb = len(qs)
    w_in = w_ref[nlev * c:(nlev + 1) * c, :]
    first_a, first_b = (half - 1, c - 1) if bwd else (0, half)
    top_ref, exit_row = (half, 0) if bwd else (half - 1, c - 1)
    g_parts, b_ins, short, tiny = [], [], None, None
    for g in gs:
        sum_a = jnp.sum(g[:half], axis=0, keepdims=True)
        sum_b = jnp.sum(g[half:], axis=0, keepdims=True)
        ok = jnp.min(jnp.minimum(sum_a, sum_b)) >= -SHORT_SPAN
        ok1 = jnp.min(sum_a + sum_b) >= -SHORT_SPAN
        short = ok if short is None else jnp.logical_and(short, ok)
        tiny = ok1 if tiny is None else jnp.logical_and(tiny, ok1)
        parts = _split2(g)
        g_parts.append(parts)
        b_ins.append(_dot(w_in, parts[0]) + _dot(w_in, parts[1]))
    entry_row = c - 1 if bwd else 0

    def head_update(bb, h, scores, e_in, e_out, e_tot, extra):
        ks = slice(h * dk, (h + 1) * dk)
        vs = slice(h * dv, (h + 1) * dv)
        qh, kh, vh = qs[bb][:, ks], ks_[bb][:, ks], vs_[bb][:, vs]
        vb = vh.astype(BF16)
        st = state_ref[bb * heads + h]
        o = _dot_nt((qh * e_in).astype(BF16), st.astype(BF16)) + _dot(scores.astype(BF16), vb)
        if extra is not None:
            o = o + extra * vh
        state_ref[bb * heads + h] = st * e_tot + _dot_tn(vb, (kh * e_out).astype(BF16))
        o_scr[bb, :, vs] = o

    def single_reference_route():
        for bb in range(nb):
            b_in = b_ins[bb]
            m0 = b_in[entry_row:entry_row + 1]
            fq = jnp.exp2(b_in - m0)
            fk = jnp.exp2(m0 - b_in)
            tot = b_in[exit_row:exit_row + 1]
            e_in_all = jnp.exp2(b_in)
            e_out_all = jnp.exp2(tot - b_in)
            e_tot_all = jnp.exp2(tot)
            for h in range(heads):
                ks = slice(h * dk, (h + 1) * dk)
                scores = mask_ref[nlev + 1] * _dot_nt((qs[bb][:, ks] * fq[:, ks]).astype(BF16),
                                                      (ks_[bb][:, ks] * fk[:, ks]).astype(BF16))
                head_update(bb, h, scores, e_in_all[:, ks], e_out_all[:, ks], e_tot_all[:, ks], None)

    two_reference = short
    if single_route:
        pl.when(tiny)(single_reference_route)
        two_reference = jnp.logical_and(short, jnp.logical_not(tiny))

    @pl.when(two_reference)
    def _():
        row = lax.broadcasted_iota(jnp.int32, (c, 1), 0)
        in_a = row < half
        later = in_a if bwd else jnp.logical_not(in_a)
        for bb in range(nb):
            b_in = b_ins[bb]
            m = jnp.where(in_a, b_in[first_a:first_a + 1], b_in[first_b:first_b + 1])
            fq = jnp.exp2(b_in - m)
            fk = jnp.exp2(m - b_in)
            r1 = b_in[top_ref:top_ref + 1]
            ft = jnp.exp2(jnp.where(later, b_in - r1, r1 - b_in))
            tot = b_in[exit_row:exit_row + 1]
            e_in_all = jnp.exp2(b_in)
            e_out_all = jnp.exp2(tot - b_in)
            e_tot_all = jnp.exp2(tot)
            for h in range(heads):
                ks = slice(h * dk, (h + 1) * dk)
                qh, kh = qs[bb][:, ks], ks_[bb][:, ks]
                scores = (mask_ref[nlev] * _dot_nt((qh * fq[:, ks]).astype(BF16),
                                                   (kh * fk[:, ks]).astype(BF16))
                          + mask_ref[nlev - 1] * _dot_nt((qh * ft[:, ks]).astype(BF16),
                                                         (kh * ft[:, ks]).astype(BF16)))
                head_update(bb, h, scores, e_in_all[:, ks], e_out_all[:, ks], e_tot_all[:, ks], None)

    @pl.when(jnp.logical_not(short))
    def _():
        w = w_ref[...]
        for bb in range(nb):
            parts = g_parts[bb]
            e_all = jnp.exp2(_dot(w, parts[0]) + _dot(w, parts[1]))
            for h in range(heads):
                ks = slice(h * dk, (h + 1) * dk)
                qh, kh = qs[bb][:, ks], ks_[bb][:, ks]
                scores = jnp.zeros((c, c), F32)
                for lev in range(nlev):
                    f = e_all[lev * c:(lev + 1) * c, ks]
                    scores = scores + mask_ref[lev] * _dot_nt((qh * f).astype(BF16), (kh * f).astype(BF16))
                head_update(bb, h, scores, e_all[nlev * c:(nlev + 1) * c, ks],
                            e_all[(nlev + 1) * c:(nlev + 2) * c, ks],
                            e_all[(nlev + 2) * c:(nlev + 2) * c + 1, ks],
                            jnp.sum(qh * kh, axis=-1, keepdims=True))


def _finish(o_scr, of_ref, gate_ref, ng_ref, o_ref, heads, hd, center, final):
    if not final:
        return
    for bb in range(o_scr.shape[0]):
        gs = _silu(gate_ref[bb])
        for h in range(heads):
            sl = slice(h * hd, (h + 1) * hd)
            x = o_scr[bb, :, sl] + of_ref[bb, :, sl]
            if center:
                x = x - jnp.mean(x, axis=-1, keepdims=True)
            y = x * lax.rsqrt(jnp.mean(x * x, axis=-1, keepdims=True) + EPS) * ng_ref[:, sl]
            o_ref[bb, :, sl] = (y * gs[:, sl]).astype(BF16)


def _split_refs(refs, n_in, bwd):
    ins = refs[:n_in]
    if bwd:
        return ins, refs[n_in:n_in + 3], refs[n_in + 3:]
    o_ref, state_ref = refs[n_in:]
    return ins, (None, None, None), (o_ref, state_ref, o_ref)


def _for_each_chunk(c, n_rows, bwd, body):
    cps = n_rows // c

    def step(s, carry):
        idx = (cps - 1 - s) if bwd else s
        body(pl.ds(pl.multiple_of(idx * c, c), c))
        return carry

    lax.fori_loop(0, cps, step, 0)


def _row_views(rows, o_ref, o_scr, fin, bwd):
    def at(r):
        return None if r is None else r.at[:, rows, :]
    of_ref, gate_ref, ng_ref = fin
    o_here = at(o_ref)
    return (o_scr if bwd else o_here), (at(of_ref), at(gate_ref), ng_ref), o_here


def _hgrn_kernel(*refs, layer, heads, dk, nlev, c, bwd):
    (q_ref, z_ref, v_ref, lbl_ref, w_ref, mask_ref), fin, (o_ref, state_ref, o_scr) = \
        _split_refs(refs, 6, bwd)

    @pl.when(pl.program_id(1) == 0)
    def _():
        state_ref[...] = jnp.zeros_like(state_ref)

    logits = lbl_ref[...]
    ex = jnp.exp(logits - jnp.max(logits, axis=0, keepdims=True))
    p = ex / jnp.sum(ex, axis=0, keepdims=True)
    lb = jnp.sum(p[:layer + 1], axis=0, keepdims=True) - p[0:1]
    nb = q_ref.shape[0]

    def chunk(rows):
        ks_, gs = [], []
        for bb in range(nb):
            z = z_ref[bb, rows, :]
            e = jnp.exp(-jnp.abs(z))
            s_big = 1.0 / (1.0 + e)
            s_small = e * s_big
            sig_pos = jnp.where(z >= 0, s_big, s_small)
            sig_neg = jnp.where(z >= 0, s_small, s_big)
            gs.append(jnp.log2(jnp.maximum(lb + (1.0 - lb) * sig_pos, A_MIN_FORGET)))
            ks_.append((1.0 - lb) * sig_neg)
        o_dst, fin_here, o_here = _row_views(rows, o_ref, o_scr, fin, bwd)
        _gated_core([q_ref[bb, rows, :] for bb in range(nb)], ks_,
                    [v_ref[bb, rows, :] for bb in range(nb)], gs,
                    w_ref, mask_ref, o_dst, state_ref, heads, dk, dk, nlev, c, bwd,
                    single_route=False)
        _finish(o_dst, *fin_here, o_here, heads, dk, False, bwd)

    _for_each_chunk(c, q_ref.shape[1], bwd, chunk)


def _gla_kernel(*refs, heads, dk, dv, nlev, c, bwd):
    (q_ref, k_ref, v_ref, lr_ref, wgk_ref, bgk_ref, w_ref, mask_ref), fin, (o_ref, state_ref, o_scr) = \
        _split_refs(refs, 8, bwd)

    @pl.when(pl.program_id(1) == 0)
    def _():
        state_ref[...] = jnp.zeros_like(state_ref)

    nb = q_ref.shape[0]

    def chunk(rows):
        qs, gs = [], []
        for bb in range(nb):
            logit = _dot(lr_ref[bb, rows, :].astype(BF16), wgk_ref[...]) + bgk_ref[...]
            t = logit * LOG2E
            gs.append((jnp.minimum(t, 0.0) - jnp.log2(1.0 + jnp.exp2(-jnp.abs(t)))) * (1.0 / B_GATE_NORM))
            qs.append(q_ref[bb, rows, :] * (dk ** -0.5))
        o_dst, fin_here, o_here = _row_views(rows, o_ref, o_scr, fin, bwd)
        _gated_core(qs, [k_ref[bb, rows, :] for bb in range(nb)],
                    [v_ref[bb, rows, :] for bb in range(nb)], gs,
                    w_ref, mask_ref, o_dst, state_ref, heads, dk, dv, nlev, c, bwd,
                    single_route=True)
        _finish(o_dst, *fin_here, o_here, heads, dv, False, bwd)

    _for_each_chunk(c, q_ref.shape[1], bwd, chunk)


def _chunk_order(n_chunks_ctx, n_chunks, bwd):
    def chunk(n):
        if not bwd:
            return n
        return jnp.where(n < n_chunks_ctx, n_chunks_ctx - 1 - n, n_chunks - 1 - n + n_chunks_ctx)
    return chunk


def _mixer_call(kern, name, proj, in_arrays, in_specs, chunk, nb, rs, c, width, state_shape,
                o_fwd, gate_col, norm_row):
    bsz, l, _ = proj.shape
    bwd = o_fwd is not None
    blk = pl.BlockSpec((nb, rs, width), lambda b, n: (b, chunk(n), 0))
    if bwd:
        in_arrays = in_arrays + [o_fwd, proj, norm_row]
        in_specs = in_specs + [
            blk,
            pl.BlockSpec((nb, rs, width), lambda b, n: (b, chunk(n), gate_col // width)),
            pl.BlockSpec((1, width), lambda b, n: (0, 0)),
        ]
    return pl.pallas_call(
        kern,
        grid=(bsz // nb, l // rs),
        in_specs=in_specs,
        out_specs=blk,
        out_shape=jax.ShapeDtypeStruct((bsz, l, width), BF16 if bwd else F32),
        scratch_shapes=[pltpu.VMEM((nb * state_shape[0],) + state_shape[1:], F32)]
        + ([pltpu.VMEM((nb, c, width), F32)] if bwd else []),
        compiler_params=_cparams(("parallel", "arbitrary")),
        name=name + ("_bwd" if bwd else "_fwd"),
    )(*in_arrays)


def _batch_per_step(bsz, most):
    return max(nb for nb in (1, 2, 4) if nb <= most and bsz % nb == 0)


def _rows_per_step(n_ctx, l):
    return _pick_tile(math.gcd(n_ctx, l - n_ctx), (256, 128, 64))


def hgrn_mixer(proj, col, lb_logits, layer, n_ctx, gate_col, norm_g):
    bsz, l, _ = proj.shape
    depth, _, width = lb_logits.shape
    c = CHUNK
    heads, dk = width // A_HEAD_DIM, A_HEAD_DIM
    cq, cf, ci = (x // width for x in col)
    norm_row = jnp.tile(norm_g, heads).reshape(1, width)
    nb = _batch_per_step(bsz, 4)
    rs = _rows_per_step(n_ctx, l)
    o_fwd = None
    for bwd in (False, True):
        w_np, m_np, nlev = _decay_constants(c, bwd)
        chunk = _chunk_order(n_ctx // rs, l // rs, bwd)
        d = int(bwd)
        kern = functools.partial(_hgrn_kernel, layer=layer, heads=heads, dk=dk, nlev=nlev, c=c, bwd=bwd)
        in_specs = [
            pl.BlockSpec((nb, rs, width), lambda b, n, chunk=chunk: (b, chunk(n), cq)),
            pl.BlockSpec((nb, rs, width), lambda b, n, chunk=chunk, d=d: (b, chunk(n), cf + d)),
            pl.BlockSpec((nb, rs, width), lambda b, n, chunk=chunk: (b, chunk(n), ci)),
            pl.BlockSpec((depth, width), lambda b, n: (0, 0)),
            pl.BlockSpec(w_np.shape, lambda b, n: (0, 0)),
            pl.BlockSpec(m_np.shape, lambda b, n: (0, 0, 0)),
        ]
        in_arrays = [proj, proj, proj, lb_logits[:, d], jnp.asarray(w_np, BF16), jnp.asarray(m_np)]
        o_fwd = _mixer_call(kern, "hgrn_mixer", proj, in_arrays, in_specs, chunk, nb, rs, c, width,
                            (heads, dk, dk), o_fwd, gate_col, norm_row)
    return o_fwd


def gla_mixer(proj, col, wgk_pad, b_gk, n_ctx, width, gate_col, norm_g):
    bsz, l, _ = proj.shape
    c = CHUNK
    key_w = wgk_pad.shape[-1]
    heads = B_HEADS
    dk, dv = key_w // heads, width // heads
    cq, ck, cv, clr = col
    norm_row = jnp.tile(norm_g, heads).reshape(1, width)
    nb = _batch_per_step(bsz, 4)
    rs = _rows_per_step(n_ctx, l)
    o_fwd = None
    for bwd in (False, True):
        w_np, m_np, nlev = _decay_constants(c, bwd)
        chunk = _chunk_order(n_ctx // rs, l // rs, bwd)
        d = int(bwd)
        kern = functools.partial(_gla_kernel, heads=heads, dk=dk, dv=dv, nlev=nlev, c=c, bwd=bwd)
        in_specs = [
            pl.BlockSpec((nb, rs, key_w), lambda b, n, chunk=chunk: (b, chunk(n), cq // key_w)),
            pl.BlockSpec((nb, rs, key_w), lambda b, n, chunk=chunk: (b, chunk(n), ck // key_w)),
            pl.BlockSpec((nb, rs, width), lambda b, n, chunk=chunk: (b, chunk(n), cv // width)),
            pl.BlockSpec((nb, rs, LANES), lambda b, n, chunk=chunk: (b, chunk(n), clr // LANES)),
            pl.BlockSpec((LANES, key_w), lambda b, n: (0, 0)),
            pl.BlockSpec((1, key_w), lambda b, n: (0, 0)),
            pl.BlockSpec(w_np.shape, lambda b, n: (0, 0)),
            pl.BlockSpec(m_np.shape, lambda b, n: (0, 0, 0)),
        ]
        in_arrays = [proj, proj, proj, proj, wgk_pad[d], b_gk[d].reshape(1, key_w),
                     jnp.asarray(w_np, BF16), jnp.asarray(m_np)]
        o_fwd = _mixer_call(kern, "gla_mixer", proj, in_arrays, in_specs, chunk, nb, rs, c, width,
                            (heads, dv, dk), o_fwd, gate_col, norm_row)
    return o_fwd


def _retention_kernel(*refs, heads, dk, dv, c, bwd):
    (q_ref, k_ref, v_ref, cos_ref, sin_ref, dl_ref), fin, (o_ref, state_ref, o_scr) = \
        _split_refs(refs, 6, bwd)

    @pl.when(pl.program_id(1) == 0)
    def _():
        state_ref[...] = jnp.zeros_like(state_ref)

    ii = lax.broadcasted_iota(jnp.int32, (c, c), 0).astype(F32)
    jj = lax.broadcasted_iota(jnp.int32, (c, c), 1).astype(F32)
    rel = (jj - ii) if bwd else (ii - jj)
    t_col = lax.broadcasted_iota(jnp.int32, (c, 1), 0).astype(F32)
    since = ((c - 1.0) - t_col) if bwd else t_col
    log_gamma = _log_sigmoid(dl_ref[...]) * LOG2E
    cos, sin = cos_ref[...], sin_ref[...]
    half = dk // 2
    for h in range(heads):
        ks = slice(h * dk, (h + 1) * dk)
        vs = slice(h * dv, (h + 1) * dv)
        lg = log_gamma[:, h:h + 1]
        dmat = jnp.where(rel >= 0, jnp.exp2(lg * jnp.maximum(rel, 0.0)), 0.0)
        xi = jnp.exp2(lg * (since + 1.0))
        zeta = jnp.exp2(lg * ((c - 1.0) - since))
        for bb in range(q_ref.shape[0]):
            qh, kh = q_ref[bb, :, ks], k_ref[bb, :, ks]
            qh = (qh * cos + pltpu.roll(qh, half, 1) * sin) * (dk ** -0.5)
            kh = kh * cos + pltpu.roll(kh, half, 1) * sin
            vb = v_ref[bb, :, vs].astype(BF16)
            st = state_ref[bb * heads + h]
            scores = _dot_nt(qh.astype(BF16), kh.astype(BF16)) * dmat
            o = _dot(scores.astype(BF16), vb) + _dot_nt(qh.astype(BF16), st.astype(BF16)) * xi
            state_ref[bb * heads + h] = st * jnp.exp2(lg * c) + _dot_tn(vb, (kh * zeta).astype(BF16))
            o_scr[bb, :, vs] = o
    _finish(o_scr, *fin, o_ref, heads, dv, True, bwd)


def retention_mixer(proj, col, cos_t, sin_t, decay_logit_pad, n_ctx, width, gate_col, norm_g):
    bsz, l, _ = proj.shape
    heads = D_HEADS
    key_w = width // 2
    dk, dv = key_w // heads, width // heads
    c = _rows_per_step(n_ctx, l)
    cq, ck, cv = col
    norm_row = jnp.tile(norm_g, heads).reshape(1, width)
    nb = _batch_per_step(bsz, 2)
    o_fwd = None
    for bwd in (False, True):
        chunk = _chunk_order(n_ctx // c, l // c, bwd)
        kern = functools.partial(_retention_kernel, heads=heads, dk=dk, dv=dv, c=c, bwd=bwd)
        in_specs = [
            pl.BlockSpec((nb, c, key_w), lambda b, n, chunk=chunk: (b, chunk(n), cq // key_w)),
            pl.BlockSpec((nb, c, key_w), lambda b, n, chunk=chunk: (b, chunk(n), ck // key_w)),
            pl.BlockSpec((nb, c, width), lambda b, n, chunk=chunk: (b, chunk(n), cv // width)),
            pl.BlockSpec((c, dk), lambda b, n, chunk=chunk: (chunk(n), 0)),
            pl.BlockSpec((c, dk), lambda b, n, chunk=chunk: (chunk(n), 0)),
            pl.BlockSpec((1, LANES), lambda b, n: (0, 0)),
        ]
        in_arrays = [proj, proj, proj, cos_t, sin_t, decay_logit_pad[int(bwd)]]
        o_fwd = _mixer_call(kern, "retention_mixer", proj, in_arrays, in_specs, chunk, nb, c, c, width,
                            (heads, dv, dk), o_fwd, gate_col, norm_row)
    return o_fwd


def _s5_kernel(u_ref, lam_ref, bt_ref, cm_ref, y_ref,
               toep_ref, win_r_ref, win_i_ref, wout_ref, cl_ref, s_r_ref, s_i_ref, xp_ref, *,
               bsz, n_chunks_ctx, n_chunks):
    t_len, hc, half = S5_CHUNK, C_GROUP, C_STATE
    lane = lax.broadcasted_iota(jnp.int32, (1, LANES), 1)
    lo = lane < half
    sgn = jnp.where(lo, -1.0, 1.0)
    tau = lax.broadcasted_iota(jnp.int32, (t_len, 1), 0)
    u = u_ref[0]
    rows = u.shape[0]
    y_acc = jnp.zeros((rows, t_len * hc), F32)

    def cmul(ar, ai, br, bi):
        return ar * br - ai * bi, ar * bi + ai * br

    def expand(x1, pa, x2, pb, out_ref):
        for t in range(t_len):
            blk = (x1 * jnp.broadcast_to(pa[t:t + 1], (hc, LANES))
                   + x2 * jnp.broadcast_to(pb[t:t + 1], (hc, LANES)))
            out_ref[t * hc:(t + 1) * hc, :] = blk.astype(out_ref.dtype)

    for d in range(2):
        lam_re = jnp.minimum(lam_ref[d, 0, 0:1], C_MAX_RE)
        lam_im = lam_ref[d, 0, 1:2]
        dt = jnp.exp(lam_ref[d, 0, 2:3])
        mag = jnp.exp(lam_re * dt)
        lb_r, lb_i = mag * jnp.cos(lam_im * dt), mag * jnp.sin(lam_im * dt)
        den = lam_re * lam_re + lam_im * lam_im
        nr, ni = lb_r - 1.0, lb_i
        cf_r, cf_i = (nr * lam_re + ni * lam_im) / den, (ni * lam_re - nr * lam_im) / den
        bt_r, bt_i = bt_ref[d, 0, 0], bt_ref[d, 0, 1]
        bb_r, bb_i = cmul(cf_r, cf_i, bt_r, bt_i)
        c_r, c_i = cm_ref[d, 0, 0], cm_ref[d, 0, 1]

        p_r, p_i = jnp.ones((t_len, LANES), F32), jnp.zeros((t_len, LANES), F32)
        q_r, q_i = p_r, p_i
        sq_r, sq_i = lb_r, lb_i
        for bit in range(int(math.log2(t_len))):
            sel = ((tau >> bit) & 1) == 1
            p_r, p_i = cmul(p_r, p_i, jnp.where(sel, sq_r, 1.0), jnp.where(sel, sq_i, 0.0))
            selq = (((t_len - 1 - tau) >> bit) & 1) == 1
            q_r, q_i = cmul(q_r, q_i, jnp.where(selq, sq_r, 1.0), jnp.where(selq, sq_i, 0.0))
            sq_r, sq_i = cmul(sq_r, sq_i, sq_r, sq_i)
        lc_r, lc_i = sq_r, sq_i
        if d == 0:
            toep_p, in_p = (p_r, p_i), (q_r, q_i)
            out_p = cmul(p_r, p_i, lb_r, lb_i)
        else:
            toep_p, in_p = (q_r, q_i), (p_r, p_i)
            out_p = cmul(q_r, q_i, lb_r, lb_i)

        tp_r, tp_i = toep_p
        expand(c_r, jnp.where(lo, tp_r, tp_i), sgn * c_i, jnp.where(lo, tp_i, tp_r), cl_ref)
        bbs = jnp.where(lo, bb_r, -bb_i)
        kt = lax.dot_general(bbs, cl_ref[...], (((1,), (1,)), ((), ())),
                             precision=lax.Precision.HIGHEST, preferred_element_type=F32)
        width = t_len * hc
        glane = lax.broadcasted_iota(jnp.int32, (hc, width), 1)
        per_tile = LANES // hc
        for m in range(per_tile):
            if d == 0:
                base = kt if m == 0 else jnp.where(glane >= hc * m, pltpu.roll(kt, hc * m, 1), 0.0)
            else:
                base = kt if m == 0 else jnp.where(glane < width - hc * m,
                                                   pltpu.roll(kt, width - hc * m, 1), 0.0)
            base = base.astype(BF16)
            for a in range(t_len // per_tile):
                off = a * LANES
                if d == 0:
                    j = a * per_tile + m
                    if off:
                        toep_ref[d, j * hc:(j + 1) * hc, :off] = jnp.zeros((hc, off), BF16)
                    toep_ref[d, j * hc:(j + 1) * hc, off:] = base[:, :width - off]
                else:
                    j = t_len - 1 - (a * per_tile + m)
                    if off:
                        toep_ref[d, j * hc:(j + 1) * hc, width - off:] = jnp.zeros((hc, off), BF16)
                    toep_ref[d, j * hc:(j + 1) * hc, :width - off] = base[:, off:]

        ip_r, ip_i = in_p
        expand(bb_r, ip_r, -bb_i, ip_i, win_r_ref)
        expand(bb_i, ip_r, bb_r, ip_i, win_i_ref)
        op_r, op_i = out_p
        expand(c_r, jnp.where(lo, op_r, -op_i), c_i, jnp.where(lo, -op_i, -op_r), wout_ref)

        s_r_ref[...] = _dot(u, win_r_ref[...])
        s_i_ref[...] = _dot(u, win_i_ref[...])
        if d == 0:
            order = list(range(n_chunks))
        else:
            order = list(range(n_chunks_ctx - 1, -1, -1)) + list(range(n_chunks - 1, n_chunks_ctx - 1, -1))
        x_r, x_i = jnp.zeros((bsz, LANES), F32), jnp.zeros((bsz, LANES), F32)
        for n in order:
            rs = slice(n * bsz, (n + 1) * bsz)
            xp_ref[rs, :] = jnp.where(lo, x_r, x_i)
            nx_r, nx_i = cmul(lc_r, lc_i, x_r, x_i)
            x_r, x_i = nx_r + s_r_ref[rs, :], nx_i + s_i_ref[rs, :]

        nblk = width // MXU_COLS
        cols = []
        for ib in range(nblk):
            acc = None
            for jb in (range(ib + 1) if d == 0 else range(ib, nblk)):
                term = _dot(u[:, jb * MXU_COLS:(jb + 1) * MXU_COLS],
                            toep_ref[d, jb * MXU_COLS:(jb + 1) * MXU_COLS,
                                     ib * MXU_COLS:(ib + 1) * MXU_COLS])
                acc = term if acc is None else acc + term
            cols.append(acc)
        y_acc = (y_acc + jnp.concatenate(cols, axis=1)
                 + _dot_nt(xp_ref[...].astype(BF16), wout_ref[...]))
    y_ref[0] = y_acc


def s5_core(ug, lam_pk, bt_pk, cm_pk, bsz, n_chunks_ctx, n_chunks):
    groups, rows, width = ug.shape
    kern = functools.partial(_s5_kernel, bsz=bsz, n_chunks_ctx=n_chunks_ctx, n_chunks=n_chunks)
    return pl.pallas_call(
        kern,
        grid=(groups,),
        in_specs=[
            pl.BlockSpec((1, rows, width), lambda g: (g, 0, 0)),
            pl.BlockSpec((2, 1, SUBLANES, LANES), lambda g: (0, g, 0, 0)),
            pl.BlockSpec((2, 1, 2, C_GROUP, LANES), lambda g: (0, g, 0, 0, 0)),
            pl.BlockSpec((2, 1, 2, C_GROUP, LANES), lambda g: (0, g, 0, 0, 0)),
        ],
        out_specs=pl.BlockSpec((1, rows, width), lambda g: (g, 0, 0)),
        out_shape=jax.ShapeDtypeStruct((groups, rows, width), F32),
        scratch_shapes=[
            pltpu.VMEM((2, width, width), BF16),
            pltpu.VMEM((width, LANES), BF16),
            pltpu.VMEM((width, LANES), BF16),
            pltpu.VMEM((width, LANES), BF16),
            pltpu.VMEM((width, LANES), F32),
            pltpu.VMEM((rows, LANES), F32),
            pltpu.VMEM((rows, LANES), F32),
            pltpu.VMEM((rows, LANES), F32),
        ],
        compiler_params=_cparams(("parallel",)),
        name="s5_core",
    )(ug, lam_pk, bt_pk, cm_pk)


def _s5_post_kernel(y_ref, u_ref, gate_ref, d_ref, w_ref, b_ref, o_ref):
    y = y_ref[0] + d_ref[...] * u_ref[0]
    z = jax.nn.gelu(y)
    t = _dot(z.astype(BF16), w_ref[...]) + b_ref[...]
    o_ref[0] = (z * _sigmoid(t) * _silu(gate_ref[0])).astype(BF16)


def s5_post(y, proj, u_col, gate_col, d_skip, w_glu, b_glu):
    bsz, l, width = y.shape
    tr = ROW_TILE
    return pl.pallas_call(
        _s5_post_kernel,
        grid=(bsz, l // tr),
        in_specs=[
            pl.BlockSpec((1, tr, width), lambda b, i: (b, i, 0)),
            pl.BlockSpec((1, tr, width), lambda b, i: (b, i, u_col // width)),
            pl.BlockSpec((1, tr, width), lambda b, i: (b, i, gate_col // width)),
            pl.BlockSpec((1, width), lambda b, i: (0, 0)),
            pl.BlockSpec((width, width), lambda b, i: (0, 0)),
            pl.BlockSpec((1, width), lambda b, i: (0, 0)),
        ],
        out_specs=pl.BlockSpec((1, tr, width), lambda b, i: (b, i, 0)),
        out_shape=jax.ShapeDtypeStruct((bsz, l, width), BF16),
        compiler_params=_cparams(("parallel", "parallel")),
        name="s5_post",
    )(y, proj, proj, d_skip.reshape(1, width), w_glu.astype(BF16), b_glu.reshape(1, width))


def _outproj_kernel(oa_ref, ob_ref, oc_ref, od_ref, w_ref, h_ref, gl_ref, gc_ref, o_ref, *,
                    n_ctx, tm, bw):
    acc = _dot(oa_ref[0], w_ref[0, 0:bw, :].astype(BF16))
    acc = acc + _dot(ob_ref[0], w_ref[0, bw:2 * bw, :].astype(BF16))
    acc = acc + _dot(oc_ref[0], w_ref[0, 2 * bw:3 * bw, :].astype(BF16))
    acc = acc + _dot(od_ref[0], w_ref[0, 3 * bw:4 * bw, :].astype(BF16))
    row = pl.program_id(1) * tm + lax.broadcasted_iota(jnp.int32, (tm, 1), 0)
    gate = jnp.where(row < n_ctx, gc_ref[0], gl_ref[0])
    o_ref[0] = h_ref[0] + gate * acc


def out_projection(o_parts, w_out, h, mods_flat, layer, n_ctx):
    bsz, l, d = h.shape
    bw = o_parts[0].shape[-1]
    tm = l if l <= 2048 else _pick_tile(l, (l // 4, l // 8, l // 16))
    tn = _pick_tile(d, (512, 256, 128))
    kern = functools.partial(_outproj_kernel, n_ctx=n_ctx, tm=tm, bw=bw)
    o_spec = pl.BlockSpec((1, tm, bw), lambda b, i, j: (b, i, 0))
    return pl.pallas_call(
        kern,
        grid=(bsz, l // tm, d // tn),
        in_specs=[
            o_spec, o_spec, o_spec, o_spec,
            pl.BlockSpec((1, 4 * bw, tn), lambda b, i, j: (layer, 0, j)),
            pl.BlockSpec((1, tm, tn), lambda b, i, j: (b, i, j)),
            pl.BlockSpec((1, 1, tn), lambda b, i, j: ((layer * SUBLANES + b) * 3 + 2, 0, j)),
            pl.BlockSpec((1, 1, tn), lambda b, i, j: ((layer * SUBLANES + bsz) * 3 + 2, 0, j)),
        ],
        out_specs=pl.BlockSpec((1, tm, tn), lambda b, i, j: (b, i, j)),
        out_shape=jax.ShapeDtypeStruct((bsz, l, d), F32),
        compiler_params=_cparams(("parallel", "parallel", "parallel")),
        name="out_projection",
    )(*o_parts, w_out, h, mods_flat, mods_flat)


def _final_norm_kernel(x_ref, g_ref, o_ref):
    x = x_ref[0]
    o_ref[0] = x * lax.rsqrt(jnp.mean(x * x, axis=-1, keepdims=True) + EPS) * g_ref[...]


def final_norm(h, g, n_ctx):
    bsz, l, d = h.shape
    tr = ROW_TILE
    skip = n_ctx // tr
    return pl.pallas_call(
        _final_norm_kernel,
        grid=(bsz, (l - n_ctx) // tr),
        in_specs=[
            pl.BlockSpec((1, tr, d), lambda b, i: (b, i + skip, 0)),
            pl.BlockSpec((1, d), lambda b, i: (0, 0)),
        ],
        out_specs=pl.BlockSpec((1, tr, d), lambda b, i: (b, i, 0)),
        out_shape=jax.ShapeDtypeStruct((bsz, l - n_ctx, d), F32),
        compiler_params=_cparams(("parallel", "parallel")),
        name="final_norm",
    )(h, g.reshape(1, d))


def _rope_tables(rows, n_ctx, dk):
    quarter = dk // 4
    freqs = ROPE_BASE ** (-jnp.arange(quarter, dtype=F32) / quarter)
    t = jnp.arange(rows * GRID_W)
    r = (t // GRID_W).astype(F32)
    col = (t % GRID_W).astype(F32)
    ang = jnp.concatenate([r[:, None] * freqs, col[:, None] * freqs], axis=-1)
    ang = jnp.concatenate([jnp.zeros((n_ctx, dk // 2), F32), ang], axis=0)
    cos, sin = jnp.cos(ang), jnp.sin(ang)
    return jnp.concatenate([cos, cos], axis=-1), jnp.concatenate([-sin, sin], axis=-1)


def _dup(x):
    return jnp.concatenate([x, x], axis=-1)


def mixer_layer(xn, h, mods_flat, layer, n_ctx, w_in_perm, hgrn_lb_logits, hgrn_norm_g, gla_w_gk,
                gla_b_gk, gla_norm_g, s5_lam_re, s5_lam_im, s5_log_dt, s5_b_re, s5_b_im, s5_c_re,
                s5_c_im, s5_d, s5_w_glu, s5_b_glu, ret_decay_logit, ret_norm_g, w_out, rope):
    bsz, l, d = h.shape
    bw = d // 4
    kw = bw // 2
    rank = B_GATE_RANK

    names = ("a_q", "a_ff", "a_fb", "a_i", "a_g", "b_q", "b_k", "b_v", "b_g", "c_u", "c_g",
             "d_q", "d_k", "d_v", "d_g", "b_lr")
    widths = (bw, bw, bw, bw, bw, kw, kw, bw, bw, bw, bw, kw, kw, bw, bw, LANES)
    col = dict(zip(names, np.concatenate([[0], np.cumsum(widths)[:-1]]).tolist()))

    proj = in_projection(xn.reshape(bsz * l, d), w_in_perm, layer).reshape(bsz, l, -1)

    o_a = hgrn_mixer(proj, (col["a_q"], col["a_ff"], col["a_i"]), hgrn_lb_logits, layer, n_ctx,
                     col["a_g"], hgrn_norm_g[layer])

    wgk = gla_w_gk[layer].astype(BF16)
    wgk_pad = jnp.zeros((2, LANES, kw), BF16)
    wgk_pad = wgk_pad.at[0, :rank].set(wgk[0]).at[1, rank:2 * rank].set(wgk[1])
    o_b = gla_mixer(proj, (col["b_q"], col["b_k"], col["b_v"], col["b_lr"]), wgk_pad,
                    gla_b_gk[layer], n_ctx, bw, col["b_g"], gla_norm_g[layer])

    groups = bw // C_GROUP
    nck = l // S5_CHUNK
    u = proj[:, :, col["c_u"]:col["c_u"] + bw].astype(BF16)
    ug = u.reshape(bsz, nck, S5_CHUNK, groups, C_GROUP).transpose(3, 1, 0, 2, 4)
    ug = ug.reshape(groups, nck * bsz, S5_CHUNK * C_GROUP)
    dt_row = jnp.broadcast_to(s5_log_dt[layer][..., None], (2, groups, C_STATE))
    lam_pk = jnp.stack([_dup(s5_lam_re[layer]), _dup(s5_lam_im[layer]), _dup(dt_row)], axis=2)
    lam_pk = jnp.pad(lam_pk, ((0, 0), (0, 0), (0, SUBLANES - 3), (0, 0)))
    bt_pk = jnp.stack([_dup(jnp.swapaxes(s5_b_re[layer], -1, -2)),
                       _dup(jnp.swapaxes(s5_b_im[layer], -1, -2))], axis=2)
    cm_pk = jnp.stack([_dup(s5_c_re[layer]), _dup(s5_c_im[layer])], axis=2)
    yg = s5_core(ug, lam_pk, bt_pk, cm_pk, bsz, n_ctx // S5_CHUNK, nck)
    y = yg.reshape(groups, nck, bsz, S5_CHUNK, C_GROUP).transpose(2, 1, 3, 0, 4).reshape(bsz, l, bw)
    o_c = s5_post(y, proj, col["c_u"], col["c_g"], s5_d[layer], s5_w_glu[layer], s5_b_glu[layer])

    dl = jnp.pad(ret_decay_logit[layer], ((0, 0), (0, LANES - D_HEADS))).reshape(2, 1, LANES)
    o_d = retention_mixer(proj, (col["d_q"], col["d_k"], col["d_v"]), rope[0], rope[1], dl, n_ctx, bw,
                          col["d_g"], ret_norm_g[layer])

    return out_projection((o_a, o_b, o_c, o_d), w_out, h, mods_flat, layer, n_ctx)


def kernel(x, c, ctx, c_ctx, norm_g, w_ada, b_ada, w_in, hgrn_lb_logits, hgrn_norm_g, gla_w_gk,
           gla_b_gk, gla_norm_g, s5_lam_re, s5_lam_im, s5_log_dt, s5_b_re, s5_b_im, s5_c_re, s5_c_im,
           s5_d, s5_w_glu, s5_b_glu, ret_decay_logit, ret_norm_g, w_out, final_norm_g):
    bsz, seq, d = x.shape
    n_ctx = ctx.shape[1]
    depth = w_in.shape[0]
    assert bsz < SUBLANES and n_ctx % ROW_TILE == 0 and seq % ROW_TILE == 0

    cvec = jnp.concatenate([c, c_ctx[None], jnp.zeros((SUBLANES - bsz - 1, d), F32)], axis=0)
    mods = ada_modulation(cvec, w_ada, b_ada)
    mods_flat = mods.reshape(depth * SUBLANES * 3, 1, d)
    rope = _rope_tables(seq // GRID_W, n_ctx, (d // 8) // D_HEADS)

    w_in_perm = permute_in_weights(w_in)
    h = None
    for layer in range(depth):
        if layer == 0:
            xn, h = prenorm_first(ctx, x, norm_g[0], mods_flat, n_ctx)
        else:
            xn = prenorm(h, norm_g[layer], mods_flat, layer, n_ctx)
        h = mixer_layer(xn, h, mods_flat, layer, n_ctx, w_in_perm, hgrn_lb_logits, hgrn_norm_g,
                        gla_w_gk, gla_b_gk, gla_norm_g, s5_lam_re, s5_lam_im, s5_log_dt, s5_b_re,
                        s5_b_im, s5_c_re, s5_c_im, s5_d, s5_w_glu, s5_b_glu, ret_decay_logit,
                        ret_norm_g, w_out, rope)
    return final_norm(h, final_norm_g, n_ctx)
```

```python
import functools
import math

import numpy as np
import jax
import jax.numpy as jnp
from jax import lax
from jax.experimental import pallas as pl
from jax.experimental.pallas import tpu as pltpu

F32 = jnp.float32
BF16 = jnp.bfloat16

EPS = 1e-6
A_HEAD_DIM = 128
A_MIN_FORGET = 1e-6
B_HEADS = 4
B_GATE_RANK = 16
B_GATE_NORM = 16.0
C_GROUP = 16
C_STATE = 64
C_MAX_RE = -1e-4
D_HEADS = 4
GRID_W = 64
ROPE_BASE = 10000.0

LANES = 128
SUBLANES = 8
BF16_ROWS = 16
VMEM_LIMIT = 56 * 1024 * 1024

MXU_COLS = 256
IN_PROJ_TN = 6 * MXU_COLS

CHUNK = 64
GLA_CHUNK = 128
LOG2E = math.log2(math.e)
SHORT_SPAN = 86.0
S5_CHUNK = 64
ROW_TILE = 256


def _cparams(sem):
    return pltpu.CompilerParams(dimension_semantics=sem, vmem_limit_bytes=VMEM_LIMIT)


def _dot(a, b):
    return jnp.dot(a, b, preferred_element_type=F32)


def _dot_nt(a, b):
    return lax.dot_general(a, b, (((1,), (1,)), ((), ())), preferred_element_type=F32)


def _dot_tn(a, b):
    return lax.dot_general(a, b, (((0,), (0,)), ((), ())), preferred_element_type=F32)


def _sigmoid(x):
    return 1.0 / (1.0 + jnp.exp(-x))


def _silu(x):
    return x * _sigmoid(x)


def _log_sigmoid(x):
    return jnp.minimum(x, 0.0) - jnp.log(1.0 + jnp.exp(-jnp.abs(x)))


def _ada_kernel(c_ref, w_ref, b_ref, o_ref):
    cv = _silu(c_ref[...]).astype(BF16)
    o_ref[0] = _dot(cv, w_ref[0].astype(BF16)) + b_ref[0]


def ada_modulation(cvec, w_ada, b_ada):
    depth, d, n3 = w_ada.shape
    tn = 512
    return pl.pallas_call(
        _ada_kernel,
        grid=(depth, n3 // tn),
        in_specs=[
            pl.BlockSpec((SUBLANES, d), lambda l, j: (0, 0)),
            pl.BlockSpec((1, d, tn), lambda l, j: (l, 0, j)),
            pl.BlockSpec((1, 1, tn), lambda l, j: (l, 0, j)),
        ],
        out_specs=pl.BlockSpec((1, SUBLANES, tn), lambda l, j: (l, 0, j)),
        out_shape=jax.ShapeDtypeStruct((depth, SUBLANES, n3), F32),
        compiler_params=_cparams(("parallel", "parallel")),
        name="ada_modulation",
    )(cvec, w_ada, b_ada.reshape(depth, 1, n3))


def _prenorm_kernel(x_ref, g_ref, sh_ref, sc_ref, o_ref):
    x = x_ref[0]
    y = x * lax.rsqrt(jnp.mean(x * x, axis=-1, keepdims=True) + EPS) * g_ref[...]
    o_ref[0] = (y * (1.0 + sc_ref[0]) + sh_ref[0]).astype(BF16)


def prenorm(h, g, mods_flat, layer, n_ctx):
    bsz, l, d = h.shape
    tr = ROW_TILE
    nct = n_ctx // tr

    def mod_map(part):
        def f(b, i):
            row = jnp.where(i < nct, bsz, b)
            return ((layer * SUBLANES + row) * 3 + part, 0, 0)
        return f

    return pl.pallas_call(
        _prenorm_kernel,
        grid=(bsz, l // tr),
        in_specs=[
            pl.BlockSpec((1, tr, d), lambda b, i: (b, i, 0)),
            pl.BlockSpec((1, d), lambda b, i: (0, 0)),
            pl.BlockSpec((1, 1, d), mod_map(0)),
            pl.BlockSpec((1, 1, d), mod_map(1)),
        ],
        out_specs=pl.BlockSpec((1, tr, d), lambda b, i: (b, i, 0)),
        out_shape=jax.ShapeDtypeStruct((bsz, l, d), BF16),
        compiler_params=_cparams(("parallel", "parallel")),
        name="prenorm",
    )(h, g.reshape(1, d), mods_flat, mods_flat)


def _prenorm_first_kernel(c_ref, x_ref, g_ref, sh_ref, sc_ref, o_ref, h_ref, *, nct):
    x = jnp.where(pl.program_id(1) < nct, c_ref[0], x_ref[0])
    h_ref[0] = x
    y = x * lax.rsqrt(jnp.mean(x * x, axis=-1, keepdims=True) + EPS) * g_ref[...]
    o_ref[0] = (y * (1.0 + sc_ref[0]) + sh_ref[0]).astype(BF16)


def prenorm_first(ctx, x, g, mods_flat, n_ctx):
    bsz, seq, d = x.shape
    l = n_ctx + seq
    tr = ROW_TILE
    nct = n_ctx // tr

    def mod_map(part):
        def f(b, i):
            row = jnp.where(i < nct, bsz, b)
            return (row * 3 + part, 0, 0)
        return f

    row_spec = pl.BlockSpec((1, tr, d), lambda b, i: (b, i, 0))
    return pl.pallas_call(
        functools.partial(_prenorm_first_kernel, nct=nct),
        grid=(bsz, l // tr),
        in_specs=[
            pl.BlockSpec((1, tr, d), lambda b, i: (b, jnp.minimum(i, nct - 1), 0)),
            pl.BlockSpec((1, tr, d), lambda b, i: (b, jnp.maximum(i - nct, 0), 0)),
            pl.BlockSpec((1, d), lambda b, i: (0, 0)),
            pl.BlockSpec((1, 1, d), mod_map(0)),
            pl.BlockSpec((1, 1, d), mod_map(1)),
        ],
        out_specs=[row_spec, row_spec],
        out_shape=[jax.ShapeDtypeStruct((bsz, l, d), BF16), jax.ShapeDtypeStruct((bsz, l, d), F32)],
        compiler_params=_cparams(("parallel", "arbitrary")),
        name="prenorm_first",
    )(ctx, x, g.reshape(1, d), mods_flat, mods_flat)


def _wprep_kernel(a_ref, b_ref, o_ref, *, first_shifted, tail_block, lr_w):
    i = pl.program_id(1)
    tr = a_ref.shape[1]

    @pl.when(i < first_shifted)
    def _():
        o_ref[0] = a_ref[0].astype(BF16)

    @pl.when(jnp.logical_and(i >= first_shifted, i < tail_block))
    def _():
        o_ref[0, :tr - lr_w] = a_ref[0, lr_w:].astype(BF16)
        o_ref[0, tr - lr_w:] = b_ref[0].astype(BF16)

    @pl.when(i == tail_block)
    def _():
        o_ref[0, :lr_w] = b_ref[0].astype(BF16)
        o_ref[0, lr_w:] = jnp.zeros((tr - lr_w, a_ref.shape[2]), BF16)

    @pl.when(i > tail_block)
    def _():
        o_ref[0] = jnp.zeros(o_ref.shape[1:], BF16)


def permute_in_weights(w_in):
    depth, d, n = w_in.shape
    bw, lr_w = d // 4, 2 * B_GATE_RANK
    lr0 = 5 * bw + 2 * (bw // 2) + bw
    tr = 2 * LANES
    assert (n - lr_w) % tr == 0 and lr0 % tr == 0 and tr % lr_w == 0
    w_t = jnp.swapaxes(w_in, 1, 2)
    first_shifted, tail_block = lr0 // tr, (n - lr_w) // tr
    per = tr // lr_w
    n_out = -(-(tail_block + 1) * tr // IN_PROJ_TN) * IN_PROJ_TN

    def a_map(l, i):
        return (l, jnp.minimum(i, tail_block - 1), 0)

    def b_map(l, i):
        return (l, jnp.where(i == tail_block, lr0 // lr_w,
                             jnp.minimum(i + 1, tail_block) * per), 0)

    return pl.pallas_call(
        functools.partial(_wprep_kernel, first_shifted=first_shifted, tail_block=tail_block, lr_w=lr_w),
        grid=(depth, n_out // tr),
        in_specs=[pl.BlockSpec((1, tr, d), a_map), pl.BlockSpec((1, lr_w, d), b_map)],
        out_specs=pl.BlockSpec((1, tr, d), lambda l, i: (l, i, 0)),
        out_shape=jax.ShapeDtypeStruct((depth, n_out, d), BF16),
        compiler_params=_cparams(("parallel", "parallel")),
        name="permute_in_weights",
    )(w_t, w_t)


def _matmul_kernel(x_ref, wt_ref, o_ref):
    o_ref[...] = _dot_nt(x_ref[...], wt_ref[0])


def _pick_tile(n, candidates):
    for c in candidates:
        if n % c == 0:
            return c
    return n


def in_projection(xn, w_all, layer):
    m, d = xn.shape
    n = w_all.shape[1]
    tm = _pick_tile(m, (1024, 512, 256, 128, 64, 32, 16))
    tn = IN_PROJ_TN
    return pl.pallas_call(
        _matmul_kernel,
        grid=(m // tm, n // tn),
        in_specs=[
            pl.BlockSpec((tm, d), lambda i, j: (i, 0), pipeline_mode=pl.Buffered(1)),
            pl.BlockSpec((1, tn, d), lambda i, j: (layer, j, 0)),
        ],
        out_specs=pl.BlockSpec((tm, tn), lambda i, j: (i, j)),
        out_shape=jax.ShapeDtypeStruct((m, n), F32),
        compiler_params=_cparams(("parallel", "parallel")),
        name="in_projection",
    )(xn, w_all)


def _decay_constants(c, bwd):
    nlev = int(math.log2(c))
    w = np.zeros((nlev + 2, c, c), np.float32)
    masks = np.zeros((nlev, c, c), np.float32)
    for lev in range(nlev):
        s = 1 << lev
        for r in range(c):
            pos = r % (2 * s)
            mid = r - pos + s
            if pos >= s:
                w[lev, r, mid:r + 1] = 1.0
            else:
                w[lev, r, r + 1:mid] = 1.0
        for i in range(c):
            for j in range(c):
                if i // (2 * s) == j // (2 * s) and i % (2 * s) >= s and j % (2 * s) < s:
                    masks[lev, i, j] = 1.0
    for r in range(c):
        w[nlev, r, :r + 1] = 1.0
        w[nlev + 1, r, r + 1:] = 1.0
    half = c // 2
    ii, jj = np.meshgrid(np.arange(c), np.arange(c), indexing="ij")
    diag = ((ii // half == jj // half) & (jj <= ii)).astype(np.float32)
    causal = (jj <= ii).astype(np.float32)
    masks = np.concatenate([masks, diag[None], causal[None]], axis=0)
    w = w.reshape((nlev + 2) * c, c)
    w = np.concatenate([w, np.ones((BF16_ROWS, c), np.float32)], axis=0)
    if bwd:
        w = np.concatenate([w[:-BF16_ROWS].reshape(nlev + 2, c, c)[:, ::-1, ::-1].reshape(-1, c),
                            w[-BF16_ROWS:]], axis=0)
        masks = masks[:, ::-1, ::-1]
    return np.ascontiguousarray(w), np.ascontiguousarray(masks), nlev


def _split2(x):
    hi = x.astype(BF16)
    return hi, (x - hi.astype(F32)).astype(BF16)


def _gated_core(qs, ks_, vs_, gs, w_ref, mask_ref, o_scr, state_ref, heads, dk, dv, nlev, c, bwd,
                single_route):
    half = c // 2
    nb = len(qs)
    w_in = w_ref[nlev * c:(nlev + 1) * c, :]
    first_a, first_b = (half - 1, c - 1) if bwd else (0, half)
    top_ref, exit_row = (half, 0) if bwd else (half - 1, c - 1)
    g_parts, b_ins, short, tiny = [], [], None, None
    for g in gs:
        sum_a = jnp.sum(g[:half], axis=0, keepdims=True)
        sum_b = jnp.sum(g[half:], axis=0, keepdims=True)
        ok = jnp.min(jnp.minimum(sum_a, sum_b)) >= -SHORT_SPAN
        ok1 = jnp.min(sum_a + sum_b) >= -SHORT_SPAN
        short = ok if short is None else jnp.logical_and(short, ok)
        tiny = ok1 if tiny is None else jnp.logical_and(tiny, ok1)
        parts = _split2(g)
        g_parts.append(parts)
        b_ins.append(_dot(w_in, parts[0]) + _dot(w_in, parts[1]))
    entry_row = c - 1 if bwd else 0

    def head_update(bb, h, scores, e_in, e_out, e_tot, extra):
        ks = slice(h * dk, (h + 1) * dk)
        vs = slice(h * dv, (h + 1) * dv)
        qh, kh, vh = qs[bb][:, ks], ks_[bb][:, ks], vs_[bb][:, vs]
        vb = vh.astype(BF16)
        st = state_ref[bb * heads + h]
        o = _dot_nt((qh * e_in).astype(BF16), st.astype(BF16)) + _dot(scores.astype(BF16), vb)
        if extra is not None:
            o = o + extra * vh
        state_ref[bb * heads + h] = st * e_tot + _dot_tn(vb, (kh * e_out).astype(BF16))
        o_scr[bb, :, vs] = o

    def single_reference_route():
        for bb in range(nb):
            b_in = b_ins[bb]
            m0 = b_in[entry_row:entry_row + 1]
            fq = jnp.exp2(b_in - m0)
            fk = jnp.exp2(m0 - b_in)
            tot = b_in[exit_row:exit_row + 1]
            e_in_all = jnp.exp2(b_in)
            e_out_all = jnp.exp2(tot - b_in)
            e_tot_all = jnp.exp2(tot)
            for h in range(heads):
                ks = slice(h * dk, (h + 1) * dk)
                scores = mask_ref[nlev + 1] * _dot_nt((qs[bb][:, ks] * fq[:, ks]).astype(BF16),
                                                      (ks_[bb][:, ks] * fk[:, ks]).astype(BF16))
                head_update(bb, h, scores, e_in_all[:, ks], e_out_all[:, ks], e_tot_all[:, ks], None)

    two_reference = short
    if single_route:
        pl.when(tiny)(single_reference_route)
        two_reference = jnp.logical_and(short, jnp.logical_not(tiny))

    @pl.when(two_reference)
    def _():
        row = lax.broadcasted_iota(jnp.int32, (c, 1), 0)
        in_a = row < half
        later = in_a if bwd else jnp.logical_not(in_a)
        for bb in range(nb):
            b_in = b_ins[bb]
            m = jnp.where(in_a, b_in[first_a:first_a + 1], b_in[first_b:first_b + 1])
            fq = jnp.exp2(b_in - m)
            fk = jnp.exp2(m - b_in)
            r1 = b_in[top_ref:top_ref + 1]
            ft = jnp.exp2(jnp.where(later, b_in - r1, r1 - b_in))
            tot = b_in[exit_row:exit_row + 1]
            e_in_all = jnp.exp2(b_in)
            e_out_all = jnp.exp2(tot - b_in)
            e_tot_all = jnp.exp2(tot)
            for h in range(heads):
                ks = slice(h * dk, (h + 1) * dk)
                qh, kh = qs[bb][:, ks], ks_[bb][:, ks]
                scores = (mask_ref[nlev] * _dot_nt((qh * fq[:, ks]).astype(BF16),
                                                   (kh * fk[:, ks]).astype(BF16))
                          + mask_ref[nlev - 1] * _dot_nt((qh * ft[:, ks]).astype(BF16),
                                                         (kh * ft[:, ks]).astype(BF16)))
                head_update(bb, h, scores, e_in_all[:, ks], e_out_all[:, ks], e_tot_all[:, ks], None)

    @pl.when(jnp.logical_not(short))
    def _():
        w = w_ref[...]
        for bb in range(nb):
            parts = g_parts[bb]
            e_all = jnp.exp2(_dot(w, parts[0]) + _dot(w, parts[1]))
            for h in range(heads):
                ks = slice(h * dk, (h + 1) * dk)
                qh, kh = qs[bb][:, ks], ks_[bb][:, ks]
                scores = jnp.zeros((c, c), F32)
                for lev in range(nlev):
                    f = e_all[lev * c:(lev + 1) * c, ks]
                    scores = scores + mask_ref[lev] * _dot_nt((qh * f).astype(BF16), (kh * f).astype(BF16))
                head_update(bb, h, scores, e_all[nlev * c:(nlev + 1) * c, ks],
                            e_all[(nlev + 1) * c:(nlev + 2) * c, ks],
                            e_all[(nlev + 2) * c:(nlev + 2) * c + 1, ks],
                            jnp.sum(qh * kh, axis=-1, keepdims=True))


def _finish(o_scr, of_ref, gate_ref, ng_ref, o_ref, heads, hd, center, final):
    if not final:
        return
    for bb in range(o_scr.shape[0]):
        gs = _silu(gate_ref[bb])
        for h in range(heads):
            sl = slice(h * hd, (h + 1) * hd)
            x = o_scr[bb, :, sl] + of_ref[bb, :, sl]
            if center:
                x = x - jnp.mean(x, axis=-1, keepdims=True)
            y = x * lax.rsqrt(jnp.mean(x * x, axis=-1, keepdims=True) + EPS) * ng_ref[:, sl]
            o_ref[bb, :, sl] = (y * gs[:, sl]).astype(BF16)


def _split_refs(refs, n_in, bwd):
    ins = refs[:n_in]
    if bwd:
        return ins, refs[n_in:n_in + 3], refs[n_in + 3:]
    o_ref, state_ref = refs[n_in:]
    return ins, (None, None, None), (o_ref, state_ref, o_ref)


def _for_each_chunk(c, n_rows, bwd, body):
    cps = n_rows // c

    def step(s, carry):
        idx = (cps - 1 - s) if bwd else s
        body(pl.ds(pl.multiple_of(idx * c, c), c))
        return carry

    lax.fori_loop(0, cps, step, 0)


def _row_views(rows, o_ref, o_scr, fin, bwd):
    def at(r):
        return None if r is None else r.at[:, rows, :]
    of_ref, gate_ref, ng_ref = fin
    o_here = at(o_ref)
    return (o_scr if bwd else o_here), (at(of_ref), at(gate_ref), ng_ref), o_here


def _hgrn_kernel(*refs, layer, heads, dk, nlev, c, bwd):
    (q_ref, z_ref, v_ref, lbl_ref, w_ref, mask_ref), fin, (o_ref, state_ref, o_scr) = \
        _split_refs(refs, 6, bwd)

    @pl.when(pl.program_id(1) == 0)
    def _():
        state_ref[...] = jnp.zeros_like(state_ref)

    logits = lbl_ref[...]
    ex = jnp.exp(logits - jnp.max(logits, axis=0, keepdims=True))
    p = ex / jnp.sum(ex, axis=0, keepdims=True)
    lb = jnp.sum(p[:layer + 1], axis=0, keepdims=True) - p[0:1]
    nb = q_ref.shape[0]

    def chunk(rows):
        ks_, gs = [], []
        for bb in range(nb):
            z = z_ref[bb, rows, :]
            e = jnp.exp(-jnp.abs(z))
            s_big = 1.0 / (1.0 + e)
            s_small = e * s_big
            sig_pos = jnp.where(z >= 0, s_big, s_small)
            sig_neg = jnp.where(z >= 0, s_small, s_big)
            gs.append(jnp.log2(jnp.maximum(lb + (1.0 - lb) * sig_pos, A_MIN_FORGET)))
            ks_.append((1.0 - lb) * sig_neg)
        o_dst, fin_here, o_here = _row_views(rows, o_ref, o_scr, fin, bwd)
        _gated_core([q_ref[bb, rows, :] for bb in range(nb)], ks_,
                    [v_ref[bb, rows, :] for bb in range(nb)], gs,
                    w_ref, mask_ref, o_dst, state_ref, heads, dk, dk, nlev, c, bwd,
                    single_route=False)
        _finish(o_dst, *fin_here, o_here, heads, dk, False, bwd)

    _for_each_chunk(c, q_ref.shape[1], bwd, chunk)


def _gla_kernel(*refs, heads, dk, dv, nlev, c, bwd):
    (q_ref, k_ref, v_ref, lr_ref, wgk_ref, bgk_ref, w_ref, mask_ref), fin, (o_ref, state_ref, o_scr) = \
        _split_refs(refs, 8, bwd)

    @pl.when(pl.program_id(1) == 0)
    def _():
        state_ref[...] = jnp.zeros_like(state_ref)

    nb = q_ref.shape[0]

    def chunk(rows):
        qs, gs = [], []
        for bb in range(nb):
            logit = _dot(lr_ref[bb, rows, :].astype(BF16), wgk_ref[...]) + bgk_ref[...]
            t = logit * LOG2E
            gs.append((jnp.minimum(t, 0.0) - jnp.log2(1.0 + jnp.exp2(-jnp.abs(t)))) * (1.0 / B_GATE_NORM))
            qs.append(q_ref[bb, rows, :] * (dk ** -0.5))
        o_dst, fin_here, o_here = _row_views(rows, o_ref, o_scr, fin, bwd)
        _gated_core(qs, [k_ref[bb, rows, :] for bb in range(nb)],
                    [v_ref[bb, rows, :] for bb in range(nb)], gs,
                    w_ref, mask_ref, o_dst, state_ref, heads, dk, dv, nlev, c, bwd,
                    single_route=True)
        _finish(o_dst, *fin_here, o_here, heads, dv, False, bwd)

    _for_each_chunk(c, q_ref.shape[1], bwd, chunk)


def _chunk_order(n_chunks_ctx, n_chunks, bwd):
    def chunk(n):
        if not bwd:
            return n
        return jnp.where(n < n_chunks_ctx, n_chunks_ctx - 1 - n, n_chunks - 1 - n + n_chunks_ctx)
    return chunk


def _mixer_call(kern, name, proj, in_arrays, in_specs, chunk, nb, rs, c, width, state_shape,
                o_fwd, gate_col, norm_row):
    bsz, l, _ = proj.shape
    bwd = o_fwd is not None
    blk = pl.BlockSpec((nb, rs, width), lambda b, n: (b, chunk(n), 0))
    if bwd:
        in_arrays = in_arrays + [o_fwd, proj, norm_row]
        in_specs = in_specs + [
            blk,
            pl.BlockSpec((nb, rs, width), lambda b, n: (b, chunk(n), gate_col // width)),
            pl.BlockSpec((1, width), lambda b, n: (0, 0)),
        ]
    return pl.pallas_call(
        kern,
        grid=(bsz // nb, l // rs),
        in_specs=in_specs,
        out_specs=blk,
        out_shape=jax.ShapeDtypeStruct((bsz, l, width), BF16 if bwd else F32),
        scratch_shapes=[pltpu.VMEM((nb * state_shape[0],) + state_shape[1:], F32)]
        + ([pltpu.VMEM((nb, c, width), F32)] if bwd else []),
        compiler_params=_cparams(("parallel", "arbitrary")),
        name=name + ("_bwd" if bwd else "_fwd"),
    )(*in_arrays)


def _batch_per_step(bsz, most):
    return max(nb for nb in (1, 2, 4) if nb <= most and bsz % nb == 0)


def _rows_per_step(n_ctx, l):
    return _pick_tile(math.gcd(n_ctx, l - n_ctx), (256, 128, 64))


def hgrn_mixer(proj, col, lb_logits, layer, n_ctx, gate_col, norm_g):
    bsz, l, _ = proj.shape
    depth, _, width = lb_logits.shape
    c = CHUNK
    heads, dk = width // A_HEAD_DIM, A_HEAD_DIM
    cq, cf, ci = (x // width for x in col)
    norm_row = jnp.tile(norm_g, heads).reshape(1, width)
    nb = _batch_per_step(bsz, 4)
    rs = _rows_per_step(n_ctx, l)
    o_fwd = None
    for bwd in (False, True):
        w_np, m_np, nlev = _decay_constants(c, bwd)
        chunk = _chunk_order(n_ctx // rs, l // rs, bwd)
        d = int(bwd)
        kern = functools.partial(_hgrn_kernel, layer=layer, heads=heads, dk=dk, nlev=nlev, c=c, bwd=bwd)
        in_specs = [
            pl.BlockSpec((nb, rs, width), lambda b, n, chunk=chunk: (b, chunk(n), cq)),
            pl.BlockSpec((nb, rs, width), lambda b, n, chunk=chunk, d=d: (b, chunk(n), cf + d)),
            pl.BlockSpec((nb, rs, width), lambda b, n, chunk=chunk: (b, chunk(n), ci)),
            pl.BlockSpec((depth, width), lambda b, n: (0, 0)),
            pl.BlockSpec(w_np.shape, lambda b, n: (0, 0)),
            pl.BlockSpec(m_np.shape, lambda b, n: (0, 0, 0)),
        ]
        in_arrays = [proj, proj, proj, lb_logits[:, d], jnp.asarray(w_np, BF16), jnp.asarray(m_np)]
        o_fwd = _mixer_call(kern, "hgrn_mixer", proj, in_arrays, in_specs, chunk, nb, rs, c, width,
                            (heads, dk, dk), o_fwd, gate_col, norm_row)
    return o_fwd


def gla_mixer(proj, col, wgk_pad, b_gk, n_ctx, width, gate_col, norm_g):
    bsz, l, _ = proj.shape
    c = GLA_CHUNK
    key_w = wgk_pad.shape[-1]
    heads = B_HEADS
    dk, dv = key_w // heads, width // heads
    cq, ck, cv, clr = col
    norm_row = jnp.tile(norm_g, heads).reshape(1, width)
    nb = _batch_per_step(bsz, 4)
    rs = _rows_per_step(n_ctx, l)
    o_fwd = None
    for bwd in (False, True):
        w_np, m_np, nlev = _decay_constants(c, bwd)
        chunk = _chunk_order(n_ctx // rs, l // rs, bwd)
        d = int(bwd)
        kern = functools.partial(_gla_kernel, heads=heads, dk=dk, dv=dv, nlev=nlev, c=c, bwd=bwd)
        in_specs = [
            pl.BlockSpec((nb, rs, key_w), lambda b, n, chunk=chunk: (b, chunk(n), cq // key_w)),
            pl.BlockSpec((nb, rs, key_w), lambda b, n, chunk=chunk: (b, chunk(n), ck // key_w)),
            pl.BlockSpec((nb, rs, width), lambda b, n, chunk=chunk: (b, chunk(n), cv // width)),
            pl.BlockSpec((nb, rs, LANES), lambda b, n, chunk=chunk: (b, chunk(n), clr // LANES)),
            pl.BlockSpec((LANES, key_w), lambda b, n: (0, 0)),
            pl.BlockSpec((1, key_w), lambda b, n: (0, 0)),
            pl.BlockSpec(w_np.shape, lambda b, n: (0, 0)),
            pl.BlockSpec(m_np.shape, lambda b, n: (0, 0, 0)),
        ]
        in_arrays = [proj, proj, proj, proj, wgk_pad[d], b_gk[d].reshape(1, key_w),
                     jnp.asarray(w_np, BF16), jnp.asarray(m_np)]
        o_fwd = _mixer_call(kern, "gla_mixer", proj, in_arrays, in_specs, chunk, nb, rs, c, width,
                            (heads, dv, dk), o_fwd, gate_col, norm_row)
    return o_fwd


def _retention_kernel(*refs, heads, dk, dv, c, bwd):
    (q_ref, k_ref, v_ref, cos_ref, sin_ref, dl_ref), fin, (o_ref, state_ref, o_scr) = \
        _split_refs(refs, 6, bwd)

    @pl.when(pl.program_id(1) == 0)
    def _():
        state_ref[...] = jnp.zeros_like(state_ref)

    ii = lax.broadcasted_iota(jnp.int32, (c, c), 0).astype(F32)
    jj = lax.broadcasted_iota(jnp.int32, (c, c), 1).astype(F32)
    rel = (jj - ii) if bwd else (ii - jj)
    t_col = lax.broadcasted_iota(jnp.int32, (c, 1), 0).astype(F32)
    since = ((c - 1.0) - t_col) if bwd else t_col
    log_gamma = _log_sigmoid(dl_ref[...]) * LOG2E
    cos, sin = cos_ref[...], sin_ref[...]
    half = dk // 2
    for h in range(heads):
        ks = slice(h * dk, (h + 1) * dk)
        vs = slice(h * dv, (h + 1) * dv)
        lg = log_gamma[:, h:h + 1]
        dmat = jnp.where(rel >= 0, jnp.exp2(lg * jnp.maximum(rel, 0.0)), 0.0)
        xi = jnp.exp2(lg * (since + 1.0))
        zeta = jnp.exp2(lg * ((c - 1.0) - since))
        for bb in range(q_ref.shape[0]):
            qh, kh = q_ref[bb, :, ks], k_ref[bb, :, ks]
            qh = (qh * cos + pltpu.roll(qh, half, 1) * sin) * (dk ** -0.5)
            kh = kh * cos + pltpu.roll(kh, half, 1) * sin
            vb = v_ref[bb, :, vs].astype(BF16)
            st = state_ref[bb * heads + h]
            scores = _dot_nt(qh.astype(BF16), kh.astype(BF16)) * dmat
            o = _dot(scores.astype(BF16), vb) + _dot_nt(qh.astype(BF16), st.astype(BF16)) * xi
            state_ref[bb * heads + h] = st * jnp.exp2(lg * c) + _dot_tn(vb, (kh * zeta).astype(BF16))
            o_scr[bb, :, vs] = o
    _finish(o_scr, *fin, o_ref, heads, dv, True, bwd)


def retention_mixer(proj, col, cos_t, sin_t, decay_logit_pad, n_ctx, width, gate_col, norm_g):
    bsz, l, _ = proj.shape
    heads = D_HEADS
    key_w = width // 2
    dk, dv = key_w // heads, width // heads
    c = _rows_per_step(n_ctx, l)
    cq, ck, cv = col
    norm_row = jnp.tile(norm_g, heads).reshape(1, width)
    nb = _batch_per_step(bsz, 2)
    o_fwd = None
    for bwd in (False, True):
        chunk = _chunk_order(n_ctx // c, l // c, bwd)
        kern = functools.partial(_retention_kernel, heads=heads, dk=dk, dv=dv, c=c, bwd=bwd)
        in_specs = [
            pl.BlockSpec((nb, c, key_w), lambda b, n, chunk=chunk: (b, chunk(n), cq // key_w)),
            pl.BlockSpec((nb, c, key_w), lambda b, n, chunk=chunk: (b, chunk(n), ck // key_w)),
            pl.BlockSpec((nb, c, width), lambda b, n, chunk=chunk: (b, chunk(n), cv // width)),
            pl.BlockSpec((c, dk), lambda b, n, chunk=chunk: (chunk(n), 0)),
            pl.BlockSpec((c, dk), lambda b, n, chunk=chunk: (chunk(n), 0)),
            pl.BlockSpec((1, LANES), lambda b, n: (0, 0)),
        ]
        in_arrays = [proj, proj, proj, cos_t, sin_t, decay_logit_pad[int(bwd)]]
        o_fwd = _mixer_call(kern, "retention_mixer", proj, in_arrays, in_specs, chunk, nb, c, c, width,
                            (heads, dv, dk), o_fwd, gate_col, norm_row)
    return o_fwd


def _s5_kernel(u_ref, lam_ref, bt_ref, cm_ref, y_ref,
               toep_ref, win_r_ref, win_i_ref, wout_ref, cl_ref, s_r_ref, s_i_ref, xp_ref, *,
               bsz, n_chunks_ctx, n_chunks):
    t_len, hc, half = S5_CHUNK, C_GROUP, C_STATE
    lane = lax.broadcasted_iota(jnp.int32, (1, LANES), 1)
    lo = lane < half
    sgn = jnp.where(lo, -1.0, 1.0)
    tau = lax.broadcasted_iota(jnp.int32, (t_len, 1), 0)
    u = u_ref[0]
    rows = u.shape[0]
    y_acc = jnp.zeros((rows, t_len * hc), F32)

    def cmul(ar, ai, br, bi):
        return ar * br - ai * bi, ar * bi + ai * br

    def expand(x1, pa, x2, pb, out_ref):
        for t in range(t_len):
            blk = (x1 * jnp.broadcast_to(pa[t:t + 1], (hc, LANES))
                   + x2 * jnp.broadcast_to(pb[t:t + 1], (hc, LANES)))
            out_ref[t * hc:(t + 1) * hc, :] = blk.astype(out_ref.dtype)

    for d in range(2):
        lam_re = jnp.minimum(lam_ref[d, 0, 0:1], C_MAX_RE)
        lam_im = lam_ref[d, 0, 1:2]
        dt = jnp.exp(lam_ref[d, 0, 2:3])
        mag = jnp.exp(lam_re * dt)
        lb_r, lb_i = mag * jnp.cos(lam_im * dt), mag * jnp.sin(lam_im * dt)
        den = lam_re * lam_re + lam_im * lam_im
        nr, ni = lb_r - 1.0, lb_i
        cf_r, cf_i = (nr * lam_re + ni * lam_im) / den, (ni * lam_re - nr * lam_im) / den
        bt_r, bt_i = bt_ref[d, 0, 0], bt_ref[d, 0, 1]
        bb_r, bb_i = cmul(cf_r, cf_i, bt_r, bt_i)
        c_r, c_i = cm_ref[d, 0, 0], cm_ref[d, 0, 1]

        p_r, p_i = jnp.ones((t_len, LANES), F32), jnp.zeros((t_len, LANES), F32)
        q_r, q_i = p_r, p_i
        sq_r, sq_i = lb_r, lb_i
        for bit in range(int(math.log2(t_len))):
            sel = ((tau >> bit) & 1) == 1
            p_r, p_i = cmul(p_r, p_i, jnp.where(sel, sq_r, 1.0), jnp.where(sel, sq_i, 0.0))
            selq = (((t_len - 1 - tau) >> bit) & 1) == 1
            q_r, q_i = cmul(q_r, q_i, jnp.where(selq, sq_r, 1.0), jnp.where(selq, sq_i, 0.0))
            sq_r, sq_i = cmul(sq_r, sq_i, sq_r, sq_i)
        lc_r, lc_i = sq_r, sq_i
        if d == 0:
            toep_p, in_p = (p_r, p_i), (q_r, q_i)
            out_p = cmul(p_r, p_i, lb_r, lb_i)
        else:
            toep_p, in_p = (q_r, q_i), (p_r, p_i)
            out_p = cmul(q_r, q_i, lb_r, lb_i)

        tp_r, tp_i = toep_p
        expand(c_r, jnp.where(lo, tp_r, tp_i), sgn * c_i, jnp.where(lo, tp_i, tp_r), cl_ref)
        bbs = jnp.where(lo, bb_r, -bb_i)
        kt = lax.dot_general(bbs, cl_ref[...], (((1,), (1,)), ((), ())),
                             precision=lax.Precision.HIGHEST, preferred_element_type=F32)
        width = t_len * hc
        glane = lax.broadcasted_iota(jnp.int32, (hc, width), 1)
        per_tile = LANES // hc
        for m in range(per_tile):
            if d == 0:
                base = kt if m == 0 else jnp.where(glane >= hc * m, pltpu.roll(kt, hc * m, 1), 0.0)
            else:
                base = kt if m == 0 else jnp.where(glane < width - hc * m,
                                                   pltpu.roll(kt, width - hc * m, 1), 0.0)
            base = base.astype(BF16)
            for a in range(t_len // per_tile):
                off = a * LANES
                if d == 0:
                    j = a * per_tile + m
                    if off:
                        toep_ref[d, j * hc:(j + 1) * hc, :off] = jnp.zeros((hc, off), BF16)
                    toep_ref[d, j * hc:(j + 1) * hc, off:] = base[:, :width - off]
                else:
                    j = t_len - 1 - (a * per_tile + m)
                    if off:
                        toep_ref[d, j * hc:(j + 1) * hc, width - off:] = jnp.zeros((hc, off), BF16)
                    toep_ref[d, j * hc:(j + 1) * hc, :width - off] = base[:, off:]

        ip_r, ip_i = in_p
        expand(bb_r, ip_r, -bb_i, ip_i, win_r_ref)
        expand(bb_i, ip_r, bb_r, ip_i, win_i_ref)
        op_r, op_i = out_p
        expand(c_r, jnp.where(lo, op_r, -op_i), c_i, jnp.where(lo, -op_i, -op_r), wout_ref)

        s_r_ref[...] = _dot(u, win_r_ref[...])
        s_i_ref[...] = _dot(u, win_i_ref[...])
        if d == 0:
            order = list(range(n_chunks))
        else:
            order = list(range(n_chunks_ctx - 1, -1, -1)) + list(range(n_chunks - 1, n_chunks_ctx - 1, -1))
        x_r, x_i = jnp.zeros((bsz, LANES), F32), jnp.zeros((bsz, LANES), F32)
        for n in order:
            rs = slice(n * bsz, (n + 1) * bsz)
            xp_ref[rs, :] = jnp.where(lo, x_r, x_i)
            nx_r, nx_i = cmul(lc_r, lc_i, x_r, x_i)
            x_r, x_i = nx_r + s_r_ref[rs, :], nx_i + s_i_ref[rs, :]

        nblk = width // MXU_COLS
        cols = []
        for ib in range(nblk):
            acc = None
            for jb in (range(ib + 1) if d == 0 else range(ib, nblk)):
                term = _dot(u[:, jb * MXU_COLS:(jb + 1) * MXU_COLS],
                            toep_ref[d, jb * MXU_COLS:(jb + 1) * MXU_COLS,
                                     ib * MXU_COLS:(ib + 1) * MXU_COLS])
                acc = term if acc is None else acc + term
            cols.append(acc)
        y_acc = (y_acc + jnp.concatenate(cols, axis=1)
                 + _dot_nt(xp_ref[...].astype(BF16), wout_ref[...]))
    y_ref[0] = y_acc


def s5_core(ug, lam_pk, bt_pk, cm_pk, bsz, n_chunks_ctx, n_chunks):
    groups, rows, width = ug.shape
    kern = functools.partial(_s5_kernel, bsz=bsz, n_chunks_ctx=n_chunks_ctx, n_chunks=n_chunks)
    return pl.pallas_call(
        kern,
        grid=(groups,),
        in_specs=[
            pl.BlockSpec((1, rows, width), lambda g: (g, 0, 0)),
            pl.BlockSpec((2, 1, SUBLANES, LANES), lambda g: (0, g, 0, 0)),
            pl.BlockSpec((2, 1, 2, C_GROUP, LANES), lambda g: (0, g, 0, 0, 0)),
            pl.BlockSpec((2, 1, 2, C_GROUP, LANES), lambda g: (0, g, 0, 0, 0)),
        ],
        out_specs=pl.BlockSpec((1, rows, width), lambda g: (g, 0, 0)),
        out_shape=jax.ShapeDtypeStruct((groups, rows, width), F32),
        scratch_shapes=[
            pltpu.VMEM((2, width, width), BF16),
            pltpu.VMEM((width, LANES), BF16),
            pltpu.VMEM((width, LANES), BF16),
            pltpu.VMEM((width, LANES), BF16),
            pltpu.VMEM((width, LANES), F32),
            pltpu.VMEM((rows, LANES), F32),
            pltpu.VMEM((rows, LANES), F32),
            pltpu.VMEM((rows, LANES), F32),
        ],
        compiler_params=_cparams(("parallel",)),
        name="s5_core",
    )(ug, lam_pk, bt_pk, cm_pk)


def _s5_post_kernel(y_ref, u_ref, gate_ref, d_ref, w_ref, b_ref, o_ref):
    y = y_ref[0] + d_ref[...] * u_ref[0]
    z = jax.nn.gelu(y)
    t = _dot(z.astype(BF16), w_ref[...]) + b_ref[...]
    o_ref[0] = (z * _sigmoid(t) * _silu(gate_ref[0])).astype(BF16)


def s5_post(y, proj, u_col, gate_col, d_skip, w_glu, b_glu):
    bsz, l, width = y.shape
    tr = ROW_TILE
    return pl.pallas_call(
        _s5_post_kernel,
        grid=(bsz, l // tr),
        in_specs=[
            pl.BlockSpec((1, tr, width), lambda b, i: (b, i, 0)),
            pl.BlockSpec((1, tr, width), lambda b, i: (b, i, u_col // width)),
            pl.BlockSpec((1, tr, width), lambda b, i: (b, i, gate_col // width)),
            pl.BlockSpec((1, width), lambda b, i: (0, 0)),
            pl.BlockSpec((width, width), lambda b, i: (0, 0)),
            pl.BlockSpec((1, width), lambda b, i: (0, 0)),
        ],
        out_specs=pl.BlockSpec((1, tr, width), lambda b, i: (b, i, 0)),
        out_shape=jax.ShapeDtypeStruct((bsz, l, width), BF16),
        compiler_params=_cparams(("parallel", "parallel")),
        name="s5_post",
    )(y, proj, proj, d_skip.reshape(1, width), w_glu.astype(BF16), b_glu.reshape(1, width))


def _outproj_kernel(oa_ref, ob_ref, oc_ref, od_ref, w_ref, h_ref, gl_ref, gc_ref, o_ref, *,
                    n_ctx, tm, bw):
    acc = _dot(oa_ref[0], w_ref[0, 0:bw, :].astype(BF16))
    acc = acc + _dot(ob_ref[0], w_ref[0, bw:2 * bw, :].astype(BF16))
    acc = acc + _dot(oc_ref[0], w_ref[0, 2 * bw:3 * bw, :].astype(BF16))
    acc = acc + _dot(od_ref[0], w_ref[0, 3 * bw:4 * bw, :].astype(BF16))
    row = pl.program_id(1) * tm + lax.broadcasted_iota(jnp.int32, (tm, 1), 0)
    gate = jnp.where(row < n_ctx, gc_ref[0], gl_ref[0])
    o_ref[0] = h_ref[0] + gate * acc


def out_projection(o_parts, w_out, h, mods_flat, layer, n_ctx):
    bsz, l, d = h.shape
    bw = o_parts[0].shape[-1]
    tm = l if l <= 2048 else _pick_tile(l, (l // 4, l // 8, l // 16))
    tn = _pick_tile(d, (512, 256, 128))
    kern = functools.partial(_outproj_kernel, n_ctx=n_ctx, tm=tm, bw=bw)
    o_spec = pl.BlockSpec((1, tm, bw), lambda b, i, j: (b, i, 0))
    return pl.pallas_call(
        kern,
        grid=(bsz, l // tm, d // tn),
        in_specs=[
            o_spec, o_spec, o_spec, o_spec,
            pl.BlockSpec((1, 4 * bw, tn), lambda b, i, j: (layer, 0, j)),
            pl.BlockSpec((1, tm, tn), lambda b, i, j: (b, i, j)),
            pl.BlockSpec((1, 1, tn), lambda b, i, j: ((layer * SUBLANES + b) * 3 + 2, 0, j)),
            pl.BlockSpec((1, 1, tn), lambda b, i, j: ((layer * SUBLANES + bsz) * 3 + 2, 0, j)),
        ],
        out_specs=pl.BlockSpec((1, tm, tn), lambda b, i, j: (b, i, j)),
        out_shape=jax.ShapeDtypeStruct((bsz, l, d), F32),
        compiler_params=_cparams(("parallel", "parallel", "parallel")),
        name="out_projection",
    )(*o_parts, w_out, h, mods_flat, mods_flat)


def _final_norm_kernel(x_ref, g_ref, o_ref):
    x = x_ref[0]
    o_ref[0] = x * lax.rsqrt(jnp.mean(x * x, axis=-1, keepdims=True) + EPS) * g_ref[...]


def final_norm(h, g, n_ctx):
    bsz, l, d = h.shape
    tr = ROW_TILE
    skip = n_ctx // tr
    return pl.pallas_call(
        _final_norm_kernel,
        grid=(bsz, (l - n_ctx) // tr),
        in_specs=[
            pl.BlockSpec((1, tr, d), lambda b, i: (b, i + skip, 0)),
            pl.BlockSpec((1, d), lambda b, i: (0, 0)),
        ],
        out_specs=pl.BlockSpec((1, tr, d), lambda b, i: (b, i, 0)),
        out_shape=jax.ShapeDtypeStruct((bsz, l - n_ctx, d), F32),
        compiler_params=_cparams(("parallel", "parallel")),
        name="final_norm",
    )(h, g.reshape(1, d))


def _rope_tables(rows, n_ctx, dk):
    quarter = dk // 4
    freqs = ROPE_BASE ** (-jnp.arange(quarter, dtype=F32) / quarter)
    t = jnp.arange(rows * GRID_W)
    r = (t // GRID_W).astype(F32)
    col = (t % GRID_W).astype(F32)
    ang = jnp.concatenate([r[:, None] * freqs, col[:, None] * freqs], axis=-1)
    ang = jnp.concatenate([jnp.zeros((n_ctx, dk // 2), F32), ang], axis=0)
    cos, sin = jnp.cos(ang), jnp.sin(ang)
    return jnp.concatenate([cos, cos], axis=-1), jnp.concatenate([-sin, sin], axis=-1)


def _dup(x):
    return jnp.concatenate([x, x], axis=-1)


def mixer_layer(xn, h, mods_flat, layer, n_ctx, w_in_perm, hgrn_lb_logits, hgrn_norm_g, gla_w_gk,
                gla_b_gk, gla_norm_g, s5_lam_re, s5_lam_im, s5_log_dt, s5_b_re, s5_b_im, s5_c_re,
                s5_c_im, s5_d, s5_w_glu, s5_b_glu, ret_decay_logit, ret_norm_g, w_out, rope):
    bsz, l, d = h.shape
    bw = d // 4
    kw = bw // 2
    rank = B_GATE_RANK

    names = ("a_q", "a_ff", "a_fb", "a_i", "a_g", "b_q", "b_k", "b_v", "b_g", "c_u", "c_g",
             "d_q", "d_k", "d_v", "d_g", "b_lr")
    widths = (bw, bw, bw, bw, bw, kw, kw, bw, bw, bw, bw, kw, kw, bw, bw, LANES)
    col = dict(zip(names, np.concatenate([[0], np.cumsum(widths)[:-1]]).tolist()))

    proj = in_projection(xn.reshape(bsz * l, d), w_in_perm, layer).reshape(bsz, l, -1)

    o_a = hgrn_mixer(proj, (col["a_q"], col["a_ff"], col["a_i"]), hgrn_lb_logits, layer, n_ctx,
                     col["a_g"], hgrn_norm_g[layer])

    wgk = gla_w_gk[layer].astype(BF16)
    wgk_pad = jnp.zeros((2, LANES, kw), BF16)
    wgk_pad = wgk_pad.at[0, :rank].set(wgk[0]).at[1, rank:2 * rank].set(wgk[1])
    o_b = gla_mixer(proj, (col["b_q"], col["b_k"], col["b_v"], col["b_lr"]), wgk_pad,
                    gla_b_gk[layer], n_ctx, bw, col["b_g"], gla_norm_g[layer])

    groups = bw // C_GROUP
    nck = l // S5_CHUNK
    u = proj[:, :, col["c_u"]:col["c_u"] + bw].astype(BF16)
    ug = u.reshape(bsz, nck, S5_CHUNK, groups, C_GROUP).transpose(3, 1, 0, 2, 4)
    ug = ug.reshape(groups, nck * bsz, S5_CHUNK * C_GROUP)
    dt_row = jnp.broadcast_to(s5_log_dt[layer][..., None], (2, groups, C_STATE))
    lam_pk = jnp.stack([_dup(s5_lam_re[layer]), _dup(s5_lam_im[layer]), _dup(dt_row)], axis=2)
    lam_pk = jnp.pad(lam_pk, ((0, 0), (0, 0), (0, SUBLANES - 3), (0, 0)))
    bt_pk = jnp.stack([_dup(jnp.swapaxes(s5_b_re[layer], -1, -2)),
                       _dup(jnp.swapaxes(s5_b_im[layer], -1, -2))], axis=2)
    cm_pk = jnp.stack([_dup(s5_c_re[layer]), _dup(s5_c_im[layer])], axis=2)
    yg = s5_core(ug, lam_pk, bt_pk, cm_pk, bsz, n_ctx // S5_CHUNK, nck)
    y = yg.reshape(groups, nck, bsz, S5_CHUNK, C_GROUP).transpose(2, 1, 3, 0, 4).reshape(bsz, l, bw)
    o_c = s5_post(y, proj, col["c_u"], col["c_g"], s5_d[layer], s5_w_glu[layer], s5_b_glu[layer])

    dl = jnp.pad(ret_decay_logit[layer], ((0, 0), (0, LANES - D_HEADS))).reshape(2, 1, LANES)
    o_d = retention_mixer(proj, (col["d_q"], col["d_k"], col["d_v"]), rope[0], rope[1], dl, n_ctx, bw,
                          col["d_g"], ret_norm_g[layer])

    return out_projection((o_a, o_b, o_c, o_d), w_out, h, mods_flat, layer, n_ctx)


def kernel(x, c, ctx, c_ctx, norm_g, w_ada, b_ada, w_in, hgrn_lb_logits, hgrn_norm_g, gla_w_gk,
           gla_b_gk, gla_norm_g, s5_lam_re, s5_lam_im, s5_log_dt, s5_b_re, s5_b_im, s5_c_re, s5_c_im,
           s5_d, s5_w_glu, s5_b_glu, ret_decay_logit, ret_norm_g, w_out, final_norm_g):
    bsz, seq, d = x.shape
    n_ctx = ctx.shape[1]
    depth = w_in.shape[0]
    assert bsz < SUBLANES and n_ctx % ROW_TILE == 0 and seq % ROW_TILE == 0

    cvec = jnp.concatenate([c, c_ctx[None], jnp.zeros((SUBLANES - bsz - 1, d), F32)], axis=0)
    mods = ada_modulation(cvec, w_ada, b_ada)
    mods_flat = mods.reshape(depth * SUBLANES * 3, 1, d)
    rope = _rope_tables(seq // GRID_W, n_ctx, (d // 8) // D_HEADS)

    w_in_perm = permute_in_weights(w_in)
    h = None
    for layer in range(depth):
        if layer == 0:
            xn, h = prenorm_first(ctx, x, norm_g[0], mods_flat, n_ctx)
        else:
            xn = prenorm(h, norm_g[layer], mods_flat, layer, n_ctx)
        h = mixer_layer(xn, h, mods_flat, layer, n_ctx, w_in_perm, hgrn_lb_logits, hgrn_norm_g,
                        gla_w_gk, gla_b_gk, gla_norm_g, s5_lam_re, s5_lam_im, s5_log_dt, s5_b_re,
                        s5_b_im, s5_c_re, s5_c_im, s5_d, s5_w_glu, s5_b_glu, ret_decay_logit,
                        ret_norm_g, w_out, rope)
    return final_norm(h, final_norm_g, n_ctx)
```

```python
import functools
import math

import numpy as np
import jax
import jax.numpy as jnp
from jax import lax
from jax.experimental import pallas as pl
from jax.experimental.pallas import tpu as pltpu

F32 = jnp.float32
BF16 = jnp.bfloat16

EPS = 1e-6
A_HEAD_DIM = 128
A_MIN_FORGET = 1e-6
B_HEADS = 4
B_GATE_RANK = 16
B_GATE_NORM = 16.0
C_GROUP = 16
C_STATE = 64
C_MAX_RE = -1e-4
D_HEADS = 4
GRID_W = 64
ROPE_BASE = 10000.0

LANES = 128
SUBLANES = 8
BF16_ROWS = 16
VMEM_LIMIT = 56 * 1024 * 1024

MXU_COLS = 256
IN_PROJ_TN = 6 * MXU_COLS

CHUNK = 64
GLA_CHUNK = 256
LOG2E = math.log2(math.e)
SHORT_SPAN = 86.0
S5_CHUNK = 64
ROW_TILE = 256


def _cparams(sem):
    return pltpu.CompilerParams(dimension_semantics=sem, vmem_limit_bytes=VMEM_LIMIT)


def _dot(a, b):
    return jnp.dot(a, b, preferred_element_type=F32)


def _dot_nt(a, b):
    return lax.dot_general(a, b, (((1,), (1,)), ((), ())), preferred_element_type=F32)


def _dot_tn(a, b):
    return lax.dot_general(a, b, (((0,), (0,)), ((), ())), preferred_element_type=F32)


def _sigmoid(x):
    return 1.0 / (1.0 + jnp.exp(-x))


def _silu(x):
    return x * _sigmoid(x)


def _log_sigmoid(x):
    return jnp.minimum(x, 0.0) - jnp.log(1.0 + jnp.exp(-jnp.abs(x)))


def _ada_kernel(c_ref, w_ref, b_ref, o_ref):
    cv = _silu(c_ref[...]).astype(BF16)
    o_ref[0] = _dot(cv, w_ref[0].astype(BF16)) + b_ref[0]


def ada_modulation(cvec, w_ada, b_ada):
    depth, d, n3 = w_ada.shape
    tn = 512
    return pl.pallas_call(
        _ada_kernel,
        grid=(depth, n3 // tn),
        in_specs=[
            pl.BlockSpec((SUBLANES, d), lambda l, j: (0, 0)),
            pl.BlockSpec((1, d, tn), lambda l, j: (l, 0, j)),
            pl.BlockSpec((1, 1, tn), lambda l, j: (l, 0, j)),
        ],
        out_specs=pl.BlockSpec((1, SUBLANES, tn), lambda l, j: (l, 0, j)),
        out_shape=jax.ShapeDtypeStruct((depth, SUBLANES, n3), F32),
        compiler_params=_cparams(("parallel", "parallel")),
        name="ada_modulation",
    )(cvec, w_ada, b_ada.reshape(depth, 1, n3))


def _prenorm_kernel(x_ref, g_ref, sh_ref, sc_ref, o_ref):
    x = x_ref[0]
    y = x * lax.rsqrt(jnp.mean(x * x, axis=-1, keepdims=True) + EPS) * g_ref[...]
    o_ref[0] = (y * (1.0 + sc_ref[0]) + sh_ref[0]).astype(BF16)


def prenorm(h, g, mods_flat, layer, n_ctx):
    bsz, l, d = h.shape
    tr = ROW_TILE
    nct = n_ctx // tr

    def mod_map(part):
        def f(b, i):
            row = jnp.where(i < nct, bsz, b)
            return ((layer * SUBLANES + row) * 3 + part, 0, 0)
        return f

    return pl.pallas_call(
        _prenorm_kernel,
        grid=(bsz, l // tr),
        in_specs=[
            pl.BlockSpec((1, tr, d), lambda b, i: (b, i, 0)),
            pl.BlockSpec((1, d), lambda b, i: (0, 0)),
            pl.BlockSpec((1, 1, d), mod_map(0)),
            pl.BlockSpec((1, 1, d), mod_map(1)),
        ],
        out_specs=pl.BlockSpec((1, tr, d), lambda b, i: (b, i, 0)),
        out_shape=jax.ShapeDtypeStruct((bsz, l, d), BF16),
        compiler_params=_cparams(("parallel", "parallel")),
        name="prenorm",
    )(h, g.reshape(1, d), mods_flat, mods_flat)


def _prenorm_first_kernel(c_ref, x_ref, g_ref, sh_ref, sc_ref, o_ref, h_ref, *, nct):
    x = jnp.where(pl.program_id(1) < nct, c_ref[0], x_ref[0])
    h_ref[0] = x
    y = x * lax.rsqrt(jnp.mean(x * x, axis=-1, keepdims=True) + EPS) * g_ref[...]
    o_ref[0] = (y * (1.0 + sc_ref[0]) + sh_ref[0]).astype(BF16)


def prenorm_first(ctx, x, g, mods_flat, n_ctx):
    bsz, seq, d = x.shape
    l = n_ctx + seq
    tr = ROW_TILE
    nct = n_ctx // tr

    def mod_map(part):
        def f(b, i):
            row = jnp.where(i < nct, bsz, b)
            return (row * 3 + part, 0, 0)
        return f

    row_spec = pl.BlockSpec((1, tr, d), lambda b, i: (b, i, 0))
    return pl.pallas_call(
        functools.partial(_prenorm_first_kernel, nct=nct),
        grid=(bsz, l // tr),
        in_specs=[
            pl.BlockSpec((1, tr, d), lambda b, i: (b, jnp.minimum(i, nct - 1), 0)),
            pl.BlockSpec((1, tr, d), lambda b, i: (b, jnp.maximum(i - nct, 0), 0)),
            pl.BlockSpec((1, d), lambda b, i: (0, 0)),
            pl.BlockSpec((1, 1, d), mod_map(0)),
            pl.BlockSpec((1, 1, d), mod_map(1)),
        ],
        out_specs=[row_spec, row_spec],
        out_shape=[jax.ShapeDtypeStruct((bsz, l, d), BF16), jax.ShapeDtypeStruct((bsz, l, d), F32)],
        compiler_params=_cparams(("parallel", "arbitrary")),
        name="prenorm_first",
    )(ctx, x, g.reshape(1, d), mods_flat, mods_flat)


def _wprep_kernel(a_ref, b_ref, o_ref, *, first_shifted, tail_block, lr_w):
    i = pl.program_id(1)
    tr = a_ref.shape[1]

    @pl.when(i < first_shifted)
    def _():
        o_ref[0] = a_ref[0].astype(BF16)

    @pl.when(jnp.logical_and(i >= first_shifted, i < tail_block))
    def _():
        o_ref[0, :tr - lr_w] = a_ref[0, lr_w:].astype(BF16)
        o_ref[0, tr - lr_w:] = b_ref[0].astype(BF16)

    @pl.when(i == tail_block)
    def _():
        o_ref[0, :lr_w] = b_ref[0].astype(BF16)
        o_ref[0, lr_w:] = jnp.zeros((tr - lr_w, a_ref.shape[2]), BF16)

    @pl.when(i > tail_block)
    def _():
        o_ref[0] = jnp.zeros(o_ref.shape[1:], BF16)


def permute_in_weights(w_in):
    depth, d, n = w_in.shape
    bw, lr_w = d // 4, 2 * B_GATE_RANK
    lr0 = 5 * bw + 2 * (bw // 2) + bw
    tr = 2 * LANES
    assert (n - lr_w) % tr == 0 and lr0 % tr == 0 and tr % lr_w == 0
    w_t = jnp.swapaxes(w_in, 1, 2)
    first_shifted, tail_block = lr0 // tr, (n - lr_w) // tr
    per = tr // lr_w
    n_out = -(-(tail_block + 1) * tr // IN_PROJ_TN) * IN_PROJ_TN

    def a_map(l, i):
        return (l, jnp.minimum(i, tail_block - 1), 0)

    def b_map(l, i):
        return (l, jnp.where(i == tail_block, lr0 // lr_w,
                             jnp.minimum(i + 1, tail_block) * per), 0)

    return pl.pallas_call(
        functools.partial(_wprep_kernel, first_shifted=first_shifted, tail_block=tail_block, lr_w=lr_w),
        grid=(depth, n_out // tr),
        in_specs=[pl.BlockSpec((1, tr, d), a_map), pl.BlockSpec((1, lr_w, d), b_map)],
        out_specs=pl.BlockSpec((1, tr, d), lambda l, i: (l, i, 0)),
        out_shape=jax.ShapeDtypeStruct((depth, n_out, d), BF16),
        compiler_params=_cparams(("parallel", "parallel")),
        name="permute_in_weights",
    )(w_t, w_t)


def _matmul_kernel(x_ref, wt_ref, o_ref):
    o_ref[...] = _dot_nt(x_ref[...], wt_ref[0])


def _pick_tile(n, candidates):
    for c in candidates:
        if n % c == 0:
            return c
    return n


def in_projection(xn, w_all, layer):
    m, d = xn.shape
    n = w_all.shape[1]
    tm = _pick_tile(m, (1024, 512, 256, 128, 64, 32, 16))
    tn = IN_PROJ_TN
    return pl.pallas_call(
        _matmul_kernel,
        grid=(m // tm, n // tn),
        in_specs=[
            pl.BlockSpec((tm, d), lambda i, j: (i, 0), pipeline_mode=pl.Buffered(1)),
            pl.BlockSpec((1, tn, d), lambda i, j: (layer, j, 0)),
        ],
        out_specs=pl.BlockSpec((tm, tn), lambda i, j: (i, j)),
        out_shape=jax.ShapeDtypeStruct((m, n), F32),
        compiler_params=_cparams(("parallel", "parallel")),
        name="in_projection",
    )(xn, w_all)


def _decay_constants(c, bwd):
    nlev = int(math.log2(c))
    w = np.zeros((nlev + 2, c, c), np.float32)
    masks = np.zeros((nlev, c, c), np.float32)
    for lev in range(nlev):
        s = 1 << lev
        for r in range(c):
            pos = r % (2 * s)
            mid = r - pos + s
            if pos >= s:
                w[lev, r, mid:r + 1] = 1.0
            else:
                w[lev, r, r + 1:mid] = 1.0
        for i in range(c):
            for j in range(c):
                if i // (2 * s) == j // (2 * s) and i % (2 * s) >= s and j % (2 * s) < s:
                    masks[lev, i, j] = 1.0
    for r in range(c):
        w[nlev, r, :r + 1] = 1.0
        w[nlev + 1, r, r + 1:] = 1.0
    half = c // 2
    ii, jj = np.meshgrid(np.arange(c), np.arange(c), indexing="ij")
    diag = ((ii // half == jj // half) & (jj <= ii)).astype(np.float32)
    causal = (jj <= ii).astype(np.float32)
    masks = np.concatenate([masks, diag[None], causal[None]], axis=0)
    w = w.reshape((nlev + 2) * c, c)
    w = np.concatenate([w, np.ones((BF16_ROWS, c), np.float32)], axis=0)
    if bwd:
        w = np.concatenate([w[:-BF16_ROWS].reshape(nlev + 2, c, c)[:, ::-1, ::-1].reshape(-1, c),
                            w[-BF16_ROWS:]], axis=0)
        masks = masks[:, ::-1, ::-1]
    return np.ascontiguousarray(w), np.ascontiguousarray(masks), nlev


def _split2(x):
    hi = x.astype(BF16)
    return hi, (x - hi.astype(F32)).astype(BF16)


def _gated_core(qs, ks_, vs_, gs, w_ref, mask_ref, o_scr, state_ref, heads, dk, dv, nlev, c, bwd,
                single_route):
    half = c // 2
    nb = len(qs)
    w_in = w_ref[nlev * c:(nlev + 1) * c, :]
    first_a, first_b = (half - 1, c - 1) if bwd else (0, half)
    top_ref, exit_row = (half, 0) if bwd else (half - 1, c - 1)
    g_parts, b_ins, short, tiny = [], [], None, None
    for g in gs:
        sum_a = jnp.sum(g[:half], axis=0, keepdims=True)
        sum_b = jnp.sum(g[half:], axis=0, keepdims=True)
        ok = jnp.min(jnp.minimum(sum_a, sum_b)) >= -SHORT_SPAN
        ok1 = jnp.min(sum_a + sum_b) >= -SHORT_SPAN
        short = ok if short is None else jnp.logical_and(short, ok)
        tiny = ok1 if tiny is None else jnp.logical_and(tiny, ok1)
        parts = _split2(g)
        g_parts.append(parts)
        b_ins.append(_dot(w_in, parts[0]) + _dot(w_in, parts[1]))
    entry_row = c - 1 if bwd else 0

    def head_update(bb, h, scores, e_in, e_out, e_tot, extra):
        ks = slice(h * dk, (h + 1) * dk)
        vs = slice(h * dv, (h + 1) * dv)
        qh, kh, vh = qs[bb][:, ks], ks_[bb][:, ks], vs_[bb][:, vs]
        vb = vh.astype(BF16)
        st = state_ref[bb * heads + h]
        o = _dot_nt((qh * e_in).astype(BF16), st.astype(BF16)) + _dot(scores.astype(BF16), vb)
        if extra is not None:
            o = o + extra * vh
        state_ref[bb * heads + h] = st * e_tot + _dot_tn(vb, (kh * e_out).astype(BF16))
        o_scr[bb, :, vs] = o

    def single_reference_route():
        for bb in range(nb):
            b_in = b_ins[bb]
            m0 = b_in[entry_row:entry_row + 1]
            fq = jnp.exp2(b_in - m0)
            fk = jnp.exp2(m0 - b_in)
            tot = b_in[exit_row:exit_row + 1]
            e_in_all = jnp.exp2(b_in)
            e_out_all = jnp.exp2(tot - b_in)
            e_tot_all = jnp.exp2(tot)
            for h in range(heads):
                ks = slice(h * dk, (h + 1) * dk)
                scores = mask_ref[nlev + 1] * _dot_nt((qs[bb][:, ks] * fq[:, ks]).astype(BF16),
                                                      (ks_[bb][:, ks] * fk[:, ks]).astype(BF16))
                head_update(bb, h, scores, e_in_all[:, ks], e_out_all[:, ks], e_tot_all[:, ks], None)

    two_reference = short
    if single_route:
        pl.when(tiny)(single_reference_route)
        two_reference = jnp.logical_and(short, jnp.logical_not(tiny))

    @pl.when(two_reference)
    def _():
        row = lax.broadcasted_iota(jnp.int32, (c, 1), 0)
        in_a = row < half
        later = in_a if bwd else jnp.logical_not(in_a)
        for bb in range(nb):
            b_in = b_ins[bb]
            m = jnp.where(in_a, b_in[first_a:first_a + 1], b_in[first_b:first_b + 1])
            fq = jnp.exp2(b_in - m)
            fk = jnp.exp2(m - b_in)
            r1 = b_in[top_ref:top_ref + 1]
            ft = jnp.exp2(jnp.where(later, b_in - r1, r1 - b_in))
            tot = b_in[exit_row:exit_row + 1]
            e_in_all = jnp.exp2(b_in)
            e_out_all = jnp.exp2(tot - b_in)
            e_tot_all = jnp.exp2(tot)
            for h in range(heads):
                ks = slice(h * dk, (h + 1) * dk)
                qh, kh = qs[bb][:, ks], ks_[bb][:, ks]
                scores = (mask_ref[nlev] * _dot_nt((qh * fq[:, ks]).astype(BF16),
                                                   (kh * fk[:, ks]).astype(BF16))
                          + mask_ref[nlev - 1] * _dot_nt((qh * ft[:, ks]).astype(BF16),
                                                         (kh * ft[:, ks]).astype(BF16)))
                head_update(bb, h, scores, e_in_all[:, ks], e_out_all[:, ks], e_tot_all[:, ks], None)

    @pl.when(jnp.logical_not(short))
    def _():
        w = w_ref[...]
        for bb in range(nb):
            parts = g_parts[bb]
            e_all = jnp.exp2(_dot(w, parts[0]) + _dot(w, parts[1]))
            for h in range(heads):
                ks = slice(h * dk, (h + 1) * dk)
                qh, kh = qs[bb][:, ks], ks_[bb][:, ks]
                scores = jnp.zeros((c, c), F32)
                for lev in range(nlev):
                    f = e_all[lev * c:(lev + 1) * c, ks]
                    scores = scores + mask_ref[lev] * _dot_nt((qh * f).astype(BF16), (kh * f).astype(BF16))
                head_update(bb, h, scores, e_all[nlev * c:(nlev + 1) * c, ks],
                            e_all[(nlev + 1) * c:(nlev + 2) * c, ks],
                            e_all[(nlev + 2) * c:(nlev + 2) * c + 1, ks],
                            jnp.sum(qh * kh, axis=-1, keepdims=True))


def _finish(o_scr, of_ref, gate_ref, ng_ref, o_ref, heads, hd, center, final):
    if not final:
        return
    for bb in range(o_scr.shape[0]):
        gs = _silu(gate_ref[bb])
        for h in range(heads):
            sl = slice(h * hd, (h + 1) * hd)
            x = o_scr[bb, :, sl] + of_ref[bb, :, sl]
            if center:
                x = x - jnp.mean(x, axis=-1, keepdims=True)
            y = x * lax.rsqrt(jnp.mean(x * x, axis=-1, keepdims=True) + EPS) * ng_ref[:, sl]
            o_ref[bb, :, sl] = (y * gs[:, sl]).astype(BF16)


def _split_refs(refs, n_in, bwd):
    ins = refs[:n_in]
    if bwd:
        return ins, refs[n_in:n_in + 3], refs[n_in + 3:]
    o_ref, state_ref = refs[n_in:]
    return ins, (None, None, None), (o_ref, state_ref, o_ref)


def _for_each_chunk(c, n_rows, bwd, body):
    cps = n_rows // c

    def step(s, carry):
        idx = (cps - 1 - s) if bwd else s
        body(pl.ds(pl.multiple_of(idx * c, c), c))
        return carry

    lax.fori_loop(0, cps, step, 0)


def _row_views(rows, o_ref, o_scr, fin, bwd):
    def at(r):
        return None if r is None else r.at[:, rows, :]
    of_ref, gate_ref, ng_ref = fin
    o_here = at(o_ref)
    return (o_scr if bwd else o_here), (at(of_ref), at(gate_ref), ng_ref), o_here


def _hgrn_kernel(*refs, layer, heads, dk, nlev, c, bwd):
    (q_ref, z_ref, v_ref, lbl_ref, w_ref, mask_ref), fin, (o_ref, state_ref, o_scr) = \
        _split_refs(refs, 6, bwd)

    @pl.when(pl.program_id(1) == 0)
    def _():
        state_ref[...] = jnp.zeros_like(state_ref)

    logits = lbl_ref[...]
    ex = jnp.exp(logits - jnp.max(logits, axis=0, keepdims=True))
    p = ex / jnp.sum(ex, axis=0, keepdims=True)
    lb = jnp.sum(p[:layer + 1], axis=0, keepdims=True) - p[0:1]
    nb = q_ref.shape[0]

    def chunk(rows):
        ks_, gs = [], []
        for bb in range(nb):
            z = z_ref[bb, rows, :]
            e = jnp.exp(-jnp.abs(z))
            s_big = 1.0 / (1.0 + e)
            s_small = e * s_big
            sig_pos = jnp.where(z >= 0, s_big, s_small)
            sig_neg = jnp.where(z >= 0, s_small, s_big)
            gs.append(jnp.log2(jnp.maximum(lb + (1.0 - lb) * sig_pos, A_MIN_FORGET)))
            ks_.append((1.0 - lb) * sig_neg)
        o_dst, fin_here, o_here = _row_views(rows, o_ref, o_scr, fin, bwd)
        _gated_core([q_ref[bb, rows, :] for bb in range(nb)], ks_,
                    [v_ref[bb, rows, :] for bb in range(nb)], gs,
                    w_ref, mask_ref, o_dst, state_ref, heads, dk, dk, nlev, c, bwd,
                    single_route=False)
        _finish(o_dst, *fin_here, o_here, heads, dk, False, bwd)

    _for_each_chunk(c, q_ref.shape[1], bwd, chunk)


def _gla_kernel(*refs, heads, dk, dv, nlev, c, bwd):
    (q_ref, k_ref, v_ref, lr_ref, wgk_ref, bgk_ref, w_ref, mask_ref), fin, (o_ref, state_ref, o_scr) = \
        _split_refs(refs, 8, bwd)

    @pl.when(pl.program_id(1) == 0)
    def _():
        state_ref[...] = jnp.zeros_like(state_ref)

    nb = q_ref.shape[0]

    def chunk(rows):
        qs, gs = [], []
        for bb in range(nb):
            logit = _dot(lr_ref[bb, rows, :].astype(BF16), wgk_ref[...]) + bgk_ref[...]
            t = logit * LOG2E
            gs.append((jnp.minimum(t, 0.0) - jnp.log2(1.0 + jnp.exp2(-jnp.abs(t)))) * (1.0 / B_GATE_NORM))
            qs.append(q_ref[bb, rows, :] * (dk ** -0.5))
        o_dst, fin_here, o_here = _row_views(rows, o_ref, o_scr, fin, bwd)
        _gated_core(qs, [k_ref[bb, rows, :] for bb in range(nb)],
                    [v_ref[bb, rows, :] for bb in range(nb)], gs,
                    w_ref, mask_ref, o_dst, state_ref, heads, dk, dv, nlev, c, bwd,
                    single_route=True)
        _finish(o_dst, *fin_here, o_here, heads, dv, False, bwd)

    _for_each_chunk(c, q_ref.shape[1], bwd, chunk)


def _chunk_order(n_chunks_ctx, n_chunks, bwd):
    def chunk(n):
        if not bwd:
            return n
        return jnp.where(n < n_chunks_ctx, n_chunks_ctx - 1 - n, n_chunks - 1 - n + n_chunks_ctx)
    return chunk


def _mixer_call(kern, name, proj, in_arrays, in_specs, chunk, nb, rs, c, width, state_shape,
                o_fwd, gate_col, norm_row):
    bsz, l, _ = proj.shape
    bwd = o_fwd is not None
    blk = pl.BlockSpec((nb, rs, width), lambda b, n: (b, chunk(n), 0))
    if bwd:
        in_arrays = in_arrays + [o_fwd, proj, norm_row]
        in_specs = in_specs + [
            blk,
            pl.BlockSpec((nb, rs, width), lambda b, n: (b, chunk(n), gate_col // width)),
            pl.BlockSpec((1, width), lambda b, n: (0, 0)),
        ]
    return pl.pallas_call(
        kern,
        grid=(bsz // nb, l // rs),
        in_specs=in_specs,
        out_specs=blk,
        out_shape=jax.ShapeDtypeStruct((bsz, l, width), BF16 if bwd else F32),
        scratch_shapes=[pltpu.VMEM((nb * state_shape[0],) + state_shape[1:], F32)]
        + ([pltpu.VMEM((nb, c, width), F32)] if bwd else []),
        compiler_params=_cparams(("parallel", "arbitrary")),
        name=name + ("_bwd" if bwd else "_fwd"),
    )(*in_arrays)


def _batch_per_step(bsz, most):
    return max(nb for nb in (1, 2, 4) if nb <= most and bsz % nb == 0)


def _rows_per_step(n_ctx, l):
    return _pick_tile(math.gcd(n_ctx, l - n_ctx), (256, 128, 64))


def hgrn_mixer(proj, col, lb_logits, layer, n_ctx, gate_col, norm_g):
    bsz, l, _ = proj.shape
    depth, _, width = lb_logits.shape
    c = CHUNK
    heads, dk = width // A_HEAD_DIM, A_HEAD_DIM
    cq, cf, ci = (x // width for x in col)
    norm_row = jnp.tile(norm_g, heads).reshape(1, width)
    nb = _batch_per_step(bsz, 4)
    rs = _rows_per_step(n_ctx, l)
    o_fwd = None
    for bwd in (False, True):
        w_np, m_np, nlev = _decay_constants(c, bwd)
        chunk = _chunk_order(n_ctx // rs, l // rs, bwd)
        d = int(bwd)
        kern = functools.partial(_hgrn_kernel, layer=layer, heads=heads, dk=dk, nlev=nlev, c=c, bwd=bwd)
        in_specs = [
            pl.BlockSpec((nb, rs, width), lambda b, n, chunk=chunk: (b, chunk(n), cq)),
            pl.BlockSpec((nb, rs, width), lambda b, n, chunk=chunk, d=d: (b, chunk(n), cf + d)),
            pl.BlockSpec((nb, rs, width), lambda b, n, chunk=chunk: (b, chunk(n), ci)),
            pl.BlockSpec((depth, width), lambda b, n: (0, 0)),
            pl.BlockSpec(w_np.shape, lambda b, n: (0, 0)),
            pl.BlockSpec(m_np.shape, lambda b, n: (0, 0, 0)),
        ]
        in_arrays = [proj, proj, proj, lb_logits[:, d], jnp.asarray(w_np, BF16), jnp.asarray(m_np)]
        o_fwd = _mixer_call(kern, "hgrn_mixer", proj, in_arrays, in_specs, chunk, nb, rs, c, width,
                            (heads, dk, dk), o_fwd, gate_col, norm_row)
    return o_fwd


def gla_mixer(proj, col, wgk_pad, b_gk, n_ctx, width, gate_col, norm_g):
    bsz, l, _ = proj.shape
    key_w = wgk_pad.shape[-1]
    heads = B_HEADS
    dk, dv = key_w // heads, width // heads
    cq, ck, cv, clr = col
    norm_row = jnp.tile(norm_g, heads).reshape(1, width)
    nb = _batch_per_step(bsz, 2)
    rs = _rows_per_step(n_ctx, l)
    c = min(GLA_CHUNK, rs)
    o_fwd = None
    for bwd in (False, True):
        w_np, m_np, nlev = _decay_constants(c, bwd)
        chunk = _chunk_order(n_ctx // rs, l // rs, bwd)
        d = int(bwd)
        kern = functools.partial(_gla_kernel, heads=heads, dk=dk, dv=dv, nlev=nlev, c=c, bwd=bwd)
        in_specs = [
            pl.BlockSpec((nb, rs, key_w), lambda b, n, chunk=chunk: (b, chunk(n), cq // key_w)),
            pl.BlockSpec((nb, rs, key_w), lambda b, n, chunk=chunk: (b, chunk(n), ck // key_w)),
            pl.BlockSpec((nb, rs, width), lambda b, n, chunk=chunk: (b, chunk(n), cv // width)),
            pl.BlockSpec((nb, rs, LANES), lambda b, n, chunk=chunk: (b, chunk(n), clr // LANES)),
            pl.BlockSpec((LANES, key_w), lambda b, n: (0, 0)),
            pl.BlockSpec((1, key_w), lambda b, n: (0, 0)),
            pl.BlockSpec(w_np.shape, lambda b, n: (0, 0)),
            pl.BlockSpec(m_np.shape, lambda b, n: (0, 0, 0)),
        ]
        in_arrays = [proj, proj, proj, proj, wgk_pad[d], b_gk[d].reshape(1, key_w),
                     jnp.asarray(w_np, BF16), jnp.asarray(m_np)]
        o_fwd = _mixer_call(kern, "gla_mixer", proj, in_arrays, in_specs, chunk, nb, rs, c, width,
                            (heads, dv, dk), o_fwd, gate_col, norm_row)
    return o_fwd


def _retention_kernel(*refs, heads, dk, dv, c, bwd):
    (q_ref, k_ref, v_ref, cos_ref, sin_ref, dl_ref), fin, (o_ref, state_ref, o_scr) = \
        _split_refs(refs, 6, bwd)

    @pl.when(pl.program_id(1) == 0)
    def _():
        state_ref[...] = jnp.zeros_like(state_ref)

    ii = lax.broadcasted_iota(jnp.int32, (c, c), 0).astype(F32)
    jj = lax.broadcasted_iota(jnp.int32, (c, c), 1).astype(F32)
    rel = (jj - ii) if bwd else (ii - jj)
    t_col = lax.broadcasted_iota(jnp.int32, (c, 1), 0).astype(F32)
    since = ((c - 1.0) - t_col) if bwd else t_col
    log_gamma = _log_sigmoid(dl_ref[...]) * LOG2E
    cos, sin = cos_ref[...], sin_ref[...]
    half = dk // 2
    for h in range(heads):
        ks = slice(h * dk, (h + 1) * dk)
        vs = slice(h * dv, (h + 1) * dv)
        lg = log_gamma[:, h:h + 1]
        dmat = jnp.where(rel >= 0, jnp.exp2(lg * jnp.maximum(rel, 0.0)), 0.0)
        xi = jnp.exp2(lg * (since + 1.0))
        zeta = jnp.exp2(lg * ((c - 1.0) - since))
        for bb in range(q_ref.shape[0]):
            qh, kh = q_ref[bb, :, ks], k_ref[bb, :, ks]
            qh = (qh * cos + pltpu.roll(qh, half, 1) * sin) * (dk ** -0.5)
            kh = kh * cos + pltpu.roll(kh, half, 1) * sin
            vb = v_ref[bb, :, vs].astype(BF16)
            st = state_ref[bb * heads + h]
            scores = _dot_nt(qh.astype(BF16), kh.astype(BF16)) * dmat
            o = _dot(scores.astype(BF16), vb) + _dot_nt(qh.astype(BF16), st.astype(BF16)) * xi
            state_ref[bb * heads + h] = st * jnp.exp2(lg * c) + _dot_tn(vb, (kh * zeta).astype(BF16))
            o_scr[bb, :, vs] = o
    _finish(o_scr, *fin, o_ref, heads, dv, True, bwd)


def retention_mixer(proj, col, cos_t, sin_t, decay_logit_pad, n_ctx, width, gate_col, norm_g):
    bsz, l, _ = proj.shape
    heads = D_HEADS
    key_w = width // 2
    dk, dv = key_w // heads, width // heads
    c = _rows_per_step(n_ctx, l)
    cq, ck, cv = col
    norm_row = jnp.tile(norm_g, heads).reshape(1, width)
    nb = _batch_per_step(bsz, 2)
    o_fwd = None
    for bwd in (False, True):
        chunk = _chunk_order(n_ctx // c, l // c, bwd)
        kern = functools.partial(_retention_kernel, heads=heads, dk=dk, dv=dv, c=c, bwd=bwd)
        in_specs = [
            pl.BlockSpec((nb, c, key_w), lambda b, n, chunk=chunk: (b, chunk(n), cq // key_w)),
            pl.BlockSpec((nb, c, key_w), lambda b, n, chunk=chunk: (b, chunk(n), ck // key_w)),
            pl.BlockSpec((nb, c, width), lambda b, n, chunk=chunk: (b, chunk(n), cv // width)),
            pl.BlockSpec((c, dk), lambda b, n, chunk=chunk: (chunk(n), 0)),
            pl.BlockSpec((c, dk), lambda b, n, chunk=chunk: (chunk(n), 0)),
            pl.BlockSpec((1, LANES), lambda b, n: (0, 0)),
        ]
        in_arrays = [proj, proj, proj, cos_t, sin_t, decay_logit_pad[int(bwd)]]
        o_fwd = _mixer_call(kern, "retention_mixer", proj, in_arrays, in_specs, chunk, nb, c, c, width,
                            (heads, dv, dk), o_fwd, gate_col, norm_row)
    return o_fwd


def _s5_kernel(u_ref, lam_ref, bt_ref, cm_ref, y_ref,
               toep_ref, win_r_ref, win_i_ref, wout_ref, cl_ref, s_r_ref, s_i_ref, xp_ref, *,
               bsz, n_chunks_ctx, n_chunks):
    t_len, hc, half = S5_CHUNK, C_GROUP, C_STATE
    lane = lax.broadcasted_iota(jnp.int32, (1, LANES), 1)
    lo = lane < half
    sgn = jnp.where(lo, -1.0, 1.0)
    tau = lax.broadcasted_iota(jnp.int32, (t_len, 1), 0)
    u = u_ref[0]
    rows = u.shape[0]
    y_acc = jnp.zeros((rows, t_len * hc), F32)

    def cmul(ar, ai, br, bi):
        return ar * br - ai * bi, ar * bi + ai * br

    def expand(x1, pa, x2, pb, out_ref):
        for t in range(t_len):
            blk = (x1 * jnp.broadcast_to(pa[t:t + 1], (hc, LANES))
                   + x2 * jnp.broadcast_to(pb[t:t + 1], (hc, LANES)))
            out_ref[t * hc:(t + 1) * hc, :] = blk.astype(out_ref.dtype)

    for d in range(2):
        lam_re = jnp.minimum(lam_ref[d, 0, 0:1], C_MAX_RE)
        lam_im = lam_ref[d, 0, 1:2]
        dt = jnp.exp(lam_ref[d, 0, 2:3])
        mag = jnp.exp(lam_re * dt)
        lb_r, lb_i = mag * jnp.cos(lam_im * dt), mag * jnp.sin(lam_im * dt)
        den = lam_re * lam_re + lam_im * lam_im
        nr, ni = lb_r - 1.0, lb_i
        cf_r, cf_i = (nr * lam_re + ni * lam_im) / den, (ni * lam_re - nr * lam_im) / den
        bt_r, bt_i = bt_ref[d, 0, 0], bt_ref[d, 0, 1]
        bb_r, bb_i = cmul(cf_r, cf_i, bt_r, bt_i)
        c_r, c_i = cm_ref[d, 0, 0], cm_ref[d, 0, 1]

        p_r, p_i = jnp.ones((t_len, LANES), F32), jnp.zeros((t_len, LANES), F32)
        q_r, q_i = p_r, p_i
        sq_r, sq_i = lb_r, lb_i
        for bit in range(int(math.log2(t_len))):
            sel = ((tau >> bit) & 1) == 1
            p_r, p_i = cmul(p_r, p_i, jnp.where(sel, sq_r, 1.0), jnp.where(sel, sq_i, 0.0))
            selq = (((t_len - 1 - tau) >> bit) & 1) == 1
            q_r, q_i = cmul(q_r, q_i, jnp.where(selq, sq_r, 1.0), jnp.where(selq, sq_i, 0.0))
            sq_r, sq_i = cmul(sq_r, sq_i, sq_r, sq_i)
        lc_r, lc_i = sq_r, sq_i
        if d == 0:
            toep_p, in_p = (p_r, p_i), (q_r, q_i)
            out_p = cmul(p_r, p_i, lb_r, lb_i)
        else:
            toep_p, in_p = (q_r, q_i), (p_r, p_i)
            out_p = cmul(q_r, q_i, lb_r, lb_i)

        tp_r, tp_i = toep_p
        expand(c_r, jnp.where(lo, tp_r, tp_i), sgn * c_i, jnp.where(lo, tp_i, tp_r), cl_ref)
        bbs = jnp.where(lo, bb_r, -bb_i)
        kt = lax.dot_general(bbs, cl_ref[...], (((1,), (1,)), ((), ())),
                             precision=lax.Precision.HIGHEST, preferred_element_type=F32)
        width = t_len * hc
        glane = lax.broadcasted_iota(jnp.int32, (hc, width), 1)
        per_tile = LANES // hc
        for m in range(per_tile):
            if d == 0:
                base = kt if m == 0 else jnp.where(glane >= hc * m, pltpu.roll(kt, hc * m, 1), 0.0)
            else:
                base = kt if m == 0 else jnp.where(glane < width - hc * m,
                                                   pltpu.roll(kt, width - hc * m, 1), 0.0)
            base = base.astype(BF16)
            for a in range(t_len // per_tile):
                off = a * LANES
                if d == 0:
                    j = a * per_tile + m
                    if off:
                        toep_ref[d, j * hc:(j + 1) * hc, :off] = jnp.zeros((hc, off), BF16)
                    toep_ref[d, j * hc:(j + 1) * hc, off:] = base[:, :width - off]
                else:
                    j = t_len - 1 - (a * per_tile + m)
                    if off:
                        toep_ref[d, j * hc:(j + 1) * hc, width - off:] = jnp.zeros((hc, off), BF16)
                    toep_ref[d, j * hc:(j + 1) * hc, :width - off] = base[:, off:]

        ip_r, ip_i = in_p
        expand(bb_r, ip_r, -bb_i, ip_i, win_r_ref)
        expand(bb_i, ip_r, bb_r, ip_i, win_i_ref)
        op_r, op_i = out_p
        expand(c_r, jnp.where(lo, op_r, -op_i), c_i, jnp.where(lo, -op_i, -op_r), wout_ref)

        s_r_ref[...] = _dot(u, win_r_ref[...])
        s_i_ref[...] = _dot(u, win_i_ref[...])
        if d == 0:
            order = list(range(n_chunks))
        else:
            order = list(range(n_chunks_ctx - 1, -1, -1)) + list(range(n_chunks - 1, n_chunks_ctx - 1, -1))
        x_r, x_i = jnp.zeros((bsz, LANES), F32), jnp.zeros((bsz, LANES), F32)
        for n in order:
            rs = slice(n * bsz, (n + 1) * bsz)
            xp_ref[rs, :] = jnp.where(lo, x_r, x_i)
            nx_r, nx_i = cmul(lc_r, lc_i, x_r, x_i)
            x_r, x_i = nx_r + s_r_ref[rs, :], nx_i + s_i_ref[rs, :]

        nblk = width // MXU_COLS
        cols = []
        for ib in range(nblk):
            acc = None
            for jb in (range(ib + 1) if d == 0 else range(ib, nblk)):
                term = _dot(u[:, jb * MXU_COLS:(jb + 1) * MXU_COLS],
                            toep_ref[d, jb * MXU_COLS:(jb + 1) * MXU_COLS,
                                     ib * MXU_COLS:(ib + 1) * MXU_COLS])
                acc = term if acc is None else acc + term
            cols.append(acc)
        y_acc = (y_acc + jnp.concatenate(cols, axis=1)
                 + _dot_nt(xp_ref[...].astype(BF16), wout_ref[...]))
    y_ref[0] = y_acc


def s5_core(ug, lam_pk, bt_pk, cm_pk, bsz, n_chunks_ctx, n_chunks):
    groups, rows, width = ug.shape
    kern = functools.partial(_s5_kernel, bsz=bsz, n_chunks_ctx=n_chunks_ctx, n_chunks=n_chunks)
    return pl.pallas_call(
        kern,
        grid=(groups,),
        in_specs=[
            pl.BlockSpec((1, rows, width), lambda g: (g, 0, 0)),
            pl.BlockSpec((2, 1, SUBLANES, LANES), lambda g: (0, g, 0, 0)),
            pl.BlockSpec((2, 1, 2, C_GROUP, LANES), lambda g: (0, g, 0, 0, 0)),
            pl.BlockSpec((2, 1, 2, C_GROUP, LANES), lambda g: (0, g, 0, 0, 0)),
        ],
        out_specs=pl.BlockSpec((1, rows, width), lambda g: (g, 0, 0)),
        out_shape=jax.ShapeDtypeStruct((groups, rows, width), F32),
        scratch_shapes=[
            pltpu.VMEM((2, width, width), BF16),
            pltpu.VMEM((width, LANES), BF16),
            pltpu.VMEM((width, LANES), BF16),
            pltpu.VMEM((width, LANES), BF16),
            pltpu.VMEM((width, LANES), F32),
            pltpu.VMEM((rows, LANES), F32),
            pltpu.VMEM((rows, LANES), F32),
            pltpu.VMEM((rows, LANES), F32),
        ],
        compiler_params=_cparams(("parallel",)),
        name="s5_core",
    )(ug, lam_pk, bt_pk, cm_pk)


def _s5_post_kernel(y_ref, u_ref, gate_ref, d_ref, w_ref, b_ref, o_ref):
    y = y_ref[0] + d_ref[...] * u_ref[0]
    z = jax.nn.gelu(y)
    t = _dot(z.astype(BF16), w_ref[...]) + b_ref[...]
    o_ref[0] = (z * _sigmoid(t) * _silu(gate_ref[0])).astype(BF16)


def s5_post(y, proj, u_col, gate_col, d_skip, w_glu, b_glu):
    bsz, l, width = y.shape
    tr = ROW_TILE
    return pl.pallas_call(
        _s5_post_kernel,
        grid=(bsz, l // tr),
        in_specs=[
            pl.BlockSpec((1, tr, width), lambda b, i: (b, i, 0)),
            pl.BlockSpec((1, tr, width), lambda b, i: (b, i, u_col // width)),
            pl.BlockSpec((1, tr, width), lambda b, i: (b, i, gate_col // width)),
            pl.BlockSpec((1, width), lambda b, i: (0, 0)),
            pl.BlockSpec((width, width), lambda b, i: (0, 0)),
            pl.BlockSpec((1, width), lambda b, i: (0, 0)),
        ],
        out_specs=pl.BlockSpec((1, tr, width), lambda b, i: (b, i, 0)),
        out_shape=jax.ShapeDtypeStruct((bsz, l, width), BF16),
        compiler_params=_cparams(("parallel", "parallel")),
        name="s5_post",
    )(y, proj, proj, d_skip.reshape(1, width), w_glu.astype(BF16), b_glu.reshape(1, width))


def _outproj_kernel(oa_ref, ob_ref, oc_ref, od_ref, w_ref, h_ref, gl_ref, gc_ref, o_ref, *,
                    n_ctx, tm, bw):
    acc = _dot(oa_ref[0], w_ref[0, 0:bw, :].astype(BF16))
    acc = acc + _dot(ob_ref[0], w_ref[0, bw:2 * bw, :].astype(BF16))
    acc = acc + _dot(oc_ref[0], w_ref[0, 2 * bw:3 * bw, :].astype(BF16))
    acc = acc + _dot(od_ref[0], w_ref[0, 3 * bw:4 * bw, :].astype(BF16))
    row = pl.program_id(1) * tm + lax.broadcasted_iota(jnp.int32, (tm, 1), 0)
    gate = jnp.where(row < n_ctx, gc_ref[0], gl_ref[0])
    o_ref[0] = h_ref[0] + gate * acc


def out_projection(o_parts, w_out, h, mods_flat, layer, n_ctx):
    bsz, l, d = h.shape
    bw = o_parts[0].shape[-1]
    tm = l if l <= 2048 else _pick_tile(l, (l // 4, l // 8, l // 16))
    tn = _pick_tile(d, (512, 256, 128))
    kern = functools.partial(_outproj_kernel, n_ctx=n_ctx, tm=tm, bw=bw)
    o_spec = pl.BlockSpec((1, tm, bw), lambda b, i, j: (b, i, 0))
    return pl.pallas_call(
        kern,
        grid=(bsz, l // tm, d // tn),
        in_specs=[
            o_spec, o_spec, o_spec, o_spec,
            pl.BlockSpec((1, 4 * bw, tn), lambda b, i, j: (layer, 0, j)),
            pl.BlockSpec((1, tm, tn), lambda b, i, j: (b, i, j)),
            pl.BlockSpec((1, 1, tn), lambda b, i, j: ((layer * SUBLANES + b) * 3 + 2, 0, j)),
            pl.BlockSpec((1, 1, tn), lambda b, i, j: ((layer * SUBLANES + bsz) * 3 + 2, 0, j)),
        ],
        out_specs=pl.BlockSpec((1, tm, tn), lambda b, i, j: (b, i, j)),
        out_shape=jax.ShapeDtypeStruct((bsz, l, d), F32),
        compiler_params=_cparams(("parallel", "parallel", "parallel")),
        name="out_projection",
    )(*o_parts, w_out, h, mods_flat, mods_flat)


def _final_norm_kernel(x_ref, g_ref, o_ref):
    x = x_ref[0]
    o_ref[0] = x * lax.rsqrt(jnp.mean(x * x, axis=-1, keepdims=True) + EPS) * g_ref[...]


def final_norm(h, g, n_ctx):
    bsz, l, d = h.shape
    tr = ROW_TILE
    skip = n_ctx // tr
    return pl.pallas_call(
        _final_norm_kernel,
        grid=(bsz, (l - n_ctx) // tr),
        in_specs=[
            pl.BlockSpec((1, tr, d), lambda b, i: (b, i + skip, 0)),
            pl.BlockSpec((1, d), lambda b, i: (0, 0)),
        ],
        out_specs=pl.BlockSpec((1, tr, d), lambda b, i: (b, i, 0)),
        out_shape=jax.ShapeDtypeStruct((bsz, l - n_ctx, d), F32),
        compiler_params=_cparams(("parallel", "parallel")),
        name="final_norm",
    )(h, g.reshape(1, d))


def _rope_tables(rows, n_ctx, dk):
    quarter = dk // 4
    freqs = ROPE_BASE ** (-jnp.arange(quarter, dtype=F32) / quarter)
    t = jnp.arange(rows * GRID_W)
    r = (t // GRID_W).astype(F32)
    col = (t % GRID_W).astype(F32)
    ang = jnp.concatenate([r[:, None] * freqs, col[:, None] * freqs], axis=-1)
    ang = jnp.concatenate([jnp.zeros((n_ctx, dk // 2), F32), ang], axis=0)
    cos, sin = jnp.cos(ang), jnp.sin(ang)
    return jnp.concatenate([cos, cos], axis=-1), jnp.concatenate([-sin, sin], axis=-1)


def _dup(x):
    return jnp.concatenate([x, x], axis=-1)


def mixer_layer(xn, h, mods_flat, layer, n_ctx, w_in_perm, hgrn_lb_logits, hgrn_norm_g, gla_w_gk,
                gla_b_gk, gla_norm_g, s5_lam_re, s5_lam_im, s5_log_dt, s5_b_re, s5_b_im, s5_c_re,
                s5_c_im, s5_d, s5_w_glu, s5_b_glu, ret_decay_logit, ret_norm_g, w_out, rope):
    bsz, l, d = h.shape
    bw = d // 4
    kw = bw // 2
    rank = B_GATE_RANK

    names = ("a_q", "a_ff", "a_fb", "a_i", "a_g", "b_q", "b_k", "b_v", "b_g", "c_u", "c_g",
             "d_q", "d_k", "d_v", "d_g", "b_lr")
    widths = (bw, bw, bw, bw, bw, kw, kw, bw, bw, bw, bw, kw, kw, bw, bw, LANES)
    col = dict(zip(names, np.concatenate([[0], np.cumsum(widths)[:-1]]).tolist()))

    proj = in_projection(xn.reshape(bsz * l, d), w_in_perm, layer).reshape(bsz, l, -1)

    o_a = hgrn_mixer(proj, (col["a_q"], col["a_ff"], col["a_i"]), hgrn_lb_logits, layer, n_ctx,
                     col["a_g"], hgrn_norm_g[layer])

    wgk = gla_w_gk[layer].astype(BF16)
    wgk_pad = jnp.zeros((2, LANES, kw), BF16)
    wgk_pad = wgk_pad.at[0, :rank].set(wgk[0]).at[1, rank:2 * rank].set(wgk[1])
    o_b = gla_mixer(proj, (col["b_q"], col["b_k"], col["b_v"], col["b_lr"]), wgk_pad,
                    gla_b_gk[layer], n_ctx, bw, col["b_g"], gla_norm_g[layer])

    groups = bw // C_GROUP
    nck = l // S5_CHUNK
    u = proj[:, :, col["c_u"]:col["c_u"] + bw].astype(BF16)
    ug = u.reshape(bsz, nck, S5_CHUNK, groups, C_GROUP).transpose(3, 1, 0, 2, 4)
    ug = ug.reshape(groups, nck * bsz, S5_CHUNK * C_GROUP)
    dt_row = jnp.broadcast_to(s5_log_dt[layer][..., None], (2, groups, C_STATE))
    lam_pk = jnp.stack([_dup(s5_lam_re[layer]), _dup(s5_lam_im[layer]), _dup(dt_row)], axis=2)
    lam_pk = jnp.pad(lam_pk, ((0, 0), (0, 0), (0, SUBLANES - 3), (0, 0)))
    bt_pk = jnp.stack([_dup(jnp.swapaxes(s5_b_re[layer], -1, -2)),
                       _dup(jnp.swapaxes(s5_b_im[layer], -1, -2))], axis=2)
    cm_pk = jnp.stack([_dup(s5_c_re[layer]), _dup(s5_c_im[layer])], axis=2)
    yg = s5_core(ug, lam_pk, bt_pk, cm_pk, bsz, n_ctx // S5_CHUNK, nck)
    y = yg.reshape(groups, nck, bsz, S5_CHUNK, C_GROUP).transpose(2, 1, 3, 0, 4).reshape(bsz, l, bw)
    o_c = s5_post(y, proj, col["c_u"], col["c_g"], s5_d[layer], s5_w_glu[layer], s5_b_glu[layer])

    dl = jnp.pad(ret_decay_logit[layer], ((0, 0), (0, LANES - D_HEADS))).reshape(2, 1, LANES)
    o_d = retention_mixer(proj, (col["d_q"], col["d_k"], col["d_v"]), rope[0], rope[1], dl, n_ctx, bw,
                          col["d_g"], ret_norm_g[layer])

    return out_projection((o_a, o_b, o_c, o_d), w_out, h, mods_flat, layer, n_ctx)


def kernel(x, c, ctx, c_ctx, norm_g, w_ada, b_ada, w_in, hgrn_lb_logits, hgrn_norm_g, gla_w_gk,
           gla_b_gk, gla_norm_g, s5_lam_re, s5_lam_im, s5_log_dt, s5_b_re, s5_b_im, s5_c_re, s5_c_im,
           s5_d, s5_w_glu, s5_b_glu, ret_decay_logit, ret_norm_g, w_out, final_norm_g):
    bsz, seq, d = x.shape
    n_ctx = ctx.shape[1]
    depth = w_in.shape[0]
    assert bsz < SUBLANES and n_ctx % ROW_TILE == 0 and seq % ROW_TILE == 0

    cvec = jnp.concatenate([c, c_ctx[None], jnp.zeros((SUBLANES - bsz - 1, d), F32)], axis=0)
    mods = ada_modulation(cvec, w_ada, b_ada)
    mods_flat = mods.reshape(depth * SUBLANES * 3, 1, d)
    rope = _rope_tables(seq // GRID_W, n_ctx, (d // 8) // D_HEADS)

    w_in_perm = permute_in_weights(w_in)
    h = None
    for layer in range(depth):
        if layer == 0:
            xn, h = prenorm_first(ctx, x, norm_g[0], mods_flat, n_ctx)
        else:
            xn = prenorm(h, norm_g[layer], mods_flat, layer, n_ctx)
        h = mixer_layer(xn, h, mods_flat, layer, n_ctx, w_in_perm, hgrn_lb_logits, hgrn_norm_g,
                        gla_w_gk, gla_b_gk, gla_norm_g, s5_lam_re, s5_lam_im, s5_log_dt, s5_b_re,
                        s5_b_im, s5_c_re, s5_c_im, s5_d, s5_w_glu, s5_b_glu, ret_decay_logit,
                        ret_norm_g, w_out, rope)
    return final_norm(h, final_norm_g, n_ctx)
```

```python
import functools
import math

import numpy as np
import jax
import jax.numpy as jnp
from jax import lax
from jax.experimental import pallas as pl
from jax.experimental.pallas import tpu as pltpu

F32 = jnp.float32
BF16 = jnp.bfloat16

EPS = 1e-6
A_HEAD_DIM = 128
A_MIN_FORGET = 1e-6
B_HEADS = 4
B_GATE_RANK = 16
B_GATE_NORM = 16.0
C_GROUP = 16
C_STATE = 64
C_MAX_RE = -1e-4
D_HEADS = 4
GRID_W = 64
ROPE_BASE = 10000.0

LANES = 128
SUBLANES = 8
BF16_ROWS = 16
VMEM_LIMIT = 56 * 1024 * 1024

MXU_COLS = 256
IN_PROJ_TN = 6 * MXU_COLS

CHUNK = 64
GLA_CHUNK = 256
LOG2E = math.log2(math.e)
SHORT_SPAN = 86.0
S5_CHUNK = 64
ROW_TILE = 256


def _cparams(sem):
    return pltpu.CompilerParams(dimension_semantics=sem, vmem_limit_bytes=VMEM_LIMIT)


def _dot(a, b):
    return jnp.dot(a, b, preferred_element_type=F32)


def _dot_nt(a, b):
    return lax.dot_general(a, b, (((1,), (1,)), ((), ())), preferred_element_type=F32)


def _dot_tn(a, b):
    return lax.dot_general(a, b, (((0,), (0,)), ((), ())), preferred_element_type=F32)


def _sigmoid(x):
    return 1.0 / (1.0 + jnp.exp(-x))


def _silu(x):
    return x * _sigmoid(x)


def _log_sigmoid(x):
    return jnp.minimum(x, 0.0) - jnp.log(1.0 + jnp.exp(-jnp.abs(x)))


def _ada_kernel(c_ref, w_ref, b_ref, o_ref):
    cv = _silu(c_ref[...]).astype(BF16)
    o_ref[0] = _dot(cv, w_ref[0].astype(BF16)) + b_ref[0]


def ada_modulation(cvec, w_ada, b_ada):
    depth, d, n3 = w_ada.shape
    tn = 512
    return pl.pallas_call(
        _ada_kernel,
        grid=(depth, n3 // tn),
        in_specs=[
            pl.BlockSpec((SUBLANES, d), lambda l, j: (0, 0)),
            pl.BlockSpec((1, d, tn), lambda l, j: (l, 0, j)),
            pl.BlockSpec((1, 1, tn), lambda l, j: (l, 0, j)),
        ],
        out_specs=pl.BlockSpec((1, SUBLANES, tn), lambda l, j: (l, 0, j)),
        out_shape=jax.ShapeDtypeStruct((depth, SUBLANES, n3), F32),
        compiler_params=_cparams(("parallel", "parallel")),
        name="ada_modulation",
    )(cvec, w_ada, b_ada.reshape(depth, 1, n3))


def _prenorm_kernel(x_ref, g_ref, sh_ref, sc_ref, o_ref):
    x = x_ref[0]
    y = x * lax.rsqrt(jnp.mean(x * x, axis=-1, keepdims=True) + EPS) * g_ref[...]
    o_ref[0] = (y * (1.0 + sc_ref[0]) + sh_ref[0]).astype(BF16)


def prenorm(h, g, mods_flat, layer, n_ctx):
    bsz, l, d = h.shape
    tr = ROW_TILE
    nct = n_ctx // tr

    def mod_map(part):
        def f(b, i):
            row = jnp.where(i < nct, bsz, b)
            return ((layer * SUBLANES + row) * 3 + part, 0, 0)
        return f

    return pl.pallas_call(
        _prenorm_kernel,
        grid=(bsz, l // tr),
        in_specs=[
            pl.BlockSpec((1, tr, d), lambda b, i: (b, i, 0)),
            pl.BlockSpec((1, d), lambda b, i: (0, 0)),
            pl.BlockSpec((1, 1, d), mod_map(0)),
            pl.BlockSpec((1, 1, d), mod_map(1)),
        ],
        out_specs=pl.BlockSpec((1, tr, d), lambda b, i: (b, i, 0)),
        out_shape=jax.ShapeDtypeStruct((bsz, l, d), BF16),
        compiler_params=_cparams(("parallel", "parallel")),
        name="prenorm",
    )(h, g.reshape(1, d), mods_flat, mods_flat)


def _prenorm_first_kernel(c_ref, x_ref, g_ref, sh_ref, sc_ref, o_ref, h_ref, *, nct):
    x = jnp.where(pl.program_id(1) < nct, c_ref[0], x_ref[0])
    h_ref[0] = x
    y = x * lax.rsqrt(jnp.mean(x * x, axis=-1, keepdims=True) + EPS) * g_ref[...]
    o_ref[0] = (y * (1.0 + sc_ref[0]) + sh_ref[0]).astype(BF16)


def prenorm_first(ctx, x, g, mods_flat, n_ctx):
    bsz, seq, d = x.shape
    l = n_ctx + seq
    tr = ROW_TILE
    nct = n_ctx // tr

    def mod_map(part):
        def f(b, i):
            row = jnp.where(i < nct, bsz, b)
            return (row * 3 + part, 0, 0)
        return f

    row_spec = pl.BlockSpec((1, tr, d), lambda b, i: (b, i, 0))
    return pl.pallas_call(
        functools.partial(_prenorm_first_kernel, nct=nct),
        grid=(bsz, l // tr),
        in_specs=[
            pl.BlockSpec((1, tr, d), lambda b, i: (b, jnp.minimum(i, nct - 1), 0)),
            pl.BlockSpec((1, tr, d), lambda b, i: (b, jnp.maximum(i - nct, 0), 0)),
            pl.BlockSpec((1, d), lambda b, i: (0, 0)),
            pl.BlockSpec((1, 1, d), mod_map(0)),
            pl.BlockSpec((1, 1, d), mod_map(1)),
        ],
        out_specs=[row_spec, row_spec],
        out_shape=[jax.ShapeDtypeStruct((bsz, l, d), BF16), jax.ShapeDtypeStruct((bsz, l, d), F32)],
        compiler_params=_cparams(("parallel", "arbitrary")),
        name="prenorm_first",
    )(ctx, x, g.reshape(1, d), mods_flat, mods_flat)


def _wprep_kernel(a_ref, b_ref, o_ref, *, first_shifted, tail_block, lr_w):
    i = pl.program_id(1)
    tr = a_ref.shape[1]

    @pl.when(i < first_shifted)
    def _():
        o_ref[0] = a_ref[0].astype(BF16)

    @pl.when(jnp.logical_and(i >= first_shifted, i < tail_block))
    def _():
        o_ref[0, :tr - lr_w] = a_ref[0, lr_w:].astype(BF16)
        o_ref[0, tr - lr_w:] = b_ref[0].astype(BF16)

    @pl.when(i == tail_block)
    def _():
        o_ref[0, :lr_w] = b_ref[0].astype(BF16)
        o_ref[0, lr_w:] = jnp.zeros((tr - lr_w, a_ref.shape[2]), BF16)

    @pl.when(i > tail_block)
    def _():
        o_ref[0] = jnp.zeros(o_ref.shape[1:], BF16)


def permute_in_weights(w_in):
    depth, d, n = w_in.shape
    bw, lr_w = d // 4, 2 * B_GATE_RANK
    lr0 = 5 * bw + 2 * (bw // 2) + bw
    tr = 4 * LANES
    assert (n - lr_w) % tr == 0 and lr0 % tr == 0 and tr % lr_w == 0
    w_t = jnp.swapaxes(w_in, 1, 2)
    first_shifted, tail_block = lr0 // tr, (n - lr_w) // tr
    per = tr // lr_w
    n_out = -(-(tail_block + 1) * tr // IN_PROJ_TN) * IN_PROJ_TN

    def a_map(l, i):
        return (l, jnp.minimum(i, tail_block - 1), 0)

    def b_map(l, i):
        return (l, jnp.where(i == tail_block, lr0 // lr_w,
                             jnp.minimum(i + 1, tail_block) * per), 0)

    return pl.pallas_call(
        functools.partial(_wprep_kernel, first_shifted=first_shifted, tail_block=tail_block, lr_w=lr_w),
        grid=(depth, n_out // tr),
        in_specs=[pl.BlockSpec((1, tr, d), a_map), pl.BlockSpec((1, lr_w, d), b_map)],
        out_specs=pl.BlockSpec((1, tr, d), lambda l, i: (l, i, 0)),
        out_shape=jax.ShapeDtypeStruct((depth, n_out, d), BF16),
        compiler_params=_cparams(("parallel", "parallel")),
        name="permute_in_weights",
    )(w_t, w_t)


def _matmul_kernel(x_ref, wt_ref, o_ref):
    o_ref[...] = _dot_nt(x_ref[...], wt_ref[0])


def _pick_tile(n, candidates):
    for c in candidates:
        if n % c == 0:
            return c
    return n


def in_projection(xn, w_all, layer):
    m, d = xn.shape
    n = w_all.shape[1]
    tm = _pick_tile(m, (1024, 512, 256, 128, 64, 32, 16))
    tn = IN_PROJ_TN
    return pl.pallas_call(
        _matmul_kernel,
        grid=(m // tm, n // tn),
        in_specs=[
            pl.BlockSpec((tm, d), lambda i, j: (i, 0), pipeline_mode=pl.Buffered(1)),
            pl.BlockSpec((1, tn, d), lambda i, j: (layer, j, 0)),
        ],
        out_specs=pl.BlockSpec((tm, tn), lambda i, j: (i, j)),
        out_shape=jax.ShapeDtypeStruct((m, n), F32),
        compiler_params=_cparams(("parallel", "parallel")),
        name="in_projection",
    )(xn, w_all)


def _decay_constants(c, bwd):
    nlev = int(math.log2(c))
    w = np.zeros((nlev + 2, c, c), np.float32)
    masks = np.zeros((nlev, c, c), np.float32)
    for lev in range(nlev):
        s = 1 << lev
        for r in range(c):
            pos = r % (2 * s)
            mid = r - pos + s
            if pos >= s:
                w[lev, r, mid:r + 1] = 1.0
            else:
                w[lev, r, r + 1:mid] = 1.0
        for i in range(c):
            for j in range(c):
                if i // (2 * s) == j // (2 * s) and i % (2 * s) >= s and j % (2 * s) < s:
                    masks[lev, i, j] = 1.0
    for r in range(c):
        w[nlev, r, :r + 1] = 1.0
        w[nlev + 1, r, r + 1:] = 1.0
    half = c // 2
    ii, jj = np.meshgrid(np.arange(c), np.arange(c), indexing="ij")
    diag = ((ii // half == jj // half) & (jj <= ii)).astype(np.float32)
    causal = (jj <= ii).astype(np.float32)
    masks = np.concatenate([masks, diag[None], causal[None]], axis=0)
    w = w.reshape((nlev + 2) * c, c)
    w = np.concatenate([w, np.ones((BF16_ROWS, c), np.float32)], axis=0)
    if bwd:
        w = np.concatenate([w[:-BF16_ROWS].reshape(nlev + 2, c, c)[:, ::-1, ::-1].reshape(-1, c),
                            w[-BF16_ROWS:]], axis=0)
        masks = masks[:, ::-1, ::-1]
    return np.ascontiguousarray(w), np.ascontiguousarray(masks), nlev


def _split2(x):
    hi = x.astype(BF16)
    return hi, (x - hi.astype(F32)).astype(BF16)


def _gated_core(qs, ks_, vs_, gs, w_ref, mask_ref, o_scr, state_ref, heads, dk, dv, nlev, c, bwd,
                single_route):
    half = c // 2
    nb = len(qs)
    w_in = w_ref[nlev * c:(nlev + 1) * c, :]
    first_a, first_b = (half - 1, c - 1) if bwd else (0, half)
    top_ref, exit_row = (half, 0) if bwd else (half - 1, c - 1)
    g_parts, b_ins, short, tiny = [], [], None, None
    for g in gs:
        sum_a = jnp.sum(g[:half], axis=0, keepdims=True)
        sum_b = jnp.sum(g[half:], axis=0, keepdims=True)
        ok = jnp.min(jnp.minimum(sum_a, sum_b)) >= -SHORT_SPAN
        ok1 = jnp.min(sum_a + sum_b) >= -SHORT_SPAN
        short = ok if short is None else jnp.logical_and(short, ok)
        tiny = ok1 if tiny is None else jnp.logical_and(tiny, ok1)
        parts = _split2(g)
        g_parts.append(parts)
        b_ins.append(_dot(w_in, parts[0]) + _dot(w_in, parts[1]))
    entry_row = c - 1 if bwd else 0

    def head_update(bb, h, scores, e_in, e_out, e_tot, extra):
        ks = slice(h * dk, (h + 1) * dk)
        vs = slice(h * dv, (h + 1) * dv)
        qh, kh, vh = qs[bb][:, ks], ks_[bb][:, ks], vs_[bb][:, vs]
        vb = vh.astype(BF16)
        st = state_ref[bb * heads + h]
        o = _dot_nt((qh * e_in).astype(BF16), st.astype(BF16)) + _dot(scores.astype(BF16), vb)
        if extra is not None:
            o = o + extra * vh
        state_ref[bb * heads + h] = st * e_tot + _dot_tn(vb, (kh * e_out).astype(BF16))
        o_scr[bb, :, vs] = o

    def single_reference_route():
        for bb in range(nb):
            b_in = b_ins[bb]
            m0 = b_in[entry_row:entry_row + 1]
            fq = jnp.exp2(b_in - m0)
            fk = jnp.exp2(m0 - b_in)
            tot = b_in[exit_row:exit_row + 1]
            e_in_all = jnp.exp2(b_in)
            e_out_all = jnp.exp2(tot - b_in)
            e_tot_all = jnp.exp2(tot)
            for h in range(heads):
                ks = slice(h * dk, (h + 1) * dk)
                scores = mask_ref[nlev + 1] * _dot_nt((qs[bb][:, ks] * fq[:, ks]).astype(BF16),
                                                      (ks_[bb][:, ks] * fk[:, ks]).astype(BF16))
                head_update(bb, h, scores, e_in_all[:, ks], e_out_all[:, ks], e_tot_all[:, ks], None)

    two_reference = short
    if single_route:
        pl.when(tiny)(single_reference_route)
        two_reference = jnp.logical_and(short, jnp.logical_not(tiny))

    @pl.when(two_reference)
    def _():
        row = lax.broadcasted_iota(jnp.int32, (c, 1), 0)
        in_a = row < half
        later = in_a if bwd else jnp.logical_not(in_a)
        for bb in range(nb):
            b_in = b_ins[bb]
            m = jnp.where(in_a, b_in[first_a:first_a + 1], b_in[first_b:first_b + 1])
            fq = jnp.exp2(b_in - m)
            fk = jnp.exp2(m - b_in)
            r1 = b_in[top_ref:top_ref + 1]
            ft = jnp.exp2(jnp.where(later, b_in - r1, r1 - b_in))
            tot = b_in[exit_row:exit_row + 1]
            e_in_all = jnp.exp2(b_in)
            e_out_all = jnp.exp2(tot - b_in)
            e_tot_all = jnp.exp2(tot)
            for h in range(heads):
                ks = slice(h * dk, (h + 1) * dk)
                qh, kh = qs[bb][:, ks], ks_[bb][:, ks]
                scores = (mask_ref[nlev] * _dot_nt((qh * fq[:, ks]).astype(BF16),
                                                   (kh * fk[:, ks]).astype(BF16))
                          + mask_ref[nlev - 1] * _dot_nt((qh * ft[:, ks]).astype(BF16),
                                                         (kh * ft[:, ks]).astype(BF16)))
                head_update(bb, h, scores, e_in_all[:, ks], e_out_all[:, ks], e_tot_all[:, ks], None)

    @pl.when(jnp.logical_not(short))
    def _():
        w = w_ref[...]
        for bb in range(nb):
            parts = g_parts[bb]
            e_all = jnp.exp2(_dot(w, parts[0]) + _dot(w, parts[1]))
            for h in range(heads):
                ks = slice(h * dk, (h + 1) * dk)
                qh, kh = qs[bb][:, ks], ks_[bb][:, ks]
                scores = jnp.zeros((c, c), F32)
                for lev in range(nlev):
                    f = e_all[lev * c:(lev + 1) * c, ks]
                    scores = scores + mask_ref[lev] * _dot_nt((qh * f).astype(BF16), (kh * f).astype(BF16))
                head_update(bb, h, scores, e_all[nlev * c:(nlev + 1) * c, ks],
                            e_all[(nlev + 1) * c:(nlev + 2) * c, ks],
                            e_all[(nlev + 2) * c:(nlev + 2) * c + 1, ks],
                            jnp.sum(qh * kh, axis=-1, keepdims=True))


def _finish(o_scr, of_ref, gate_ref, ng_ref, o_ref, heads, hd, center, final):
    if not final:
        return
    for bb in range(o_scr.shape[0]):
        gs = _silu(gate_ref[bb])
        for h in range(heads):
            sl = slice(h * hd, (h + 1) * hd)
            x = o_scr[bb, :, sl] + of_ref[bb, :, sl]
            if center:
                x = x - jnp.mean(x, axis=-1, keepdims=True)
            y = x * lax.rsqrt(jnp.mean(x * x, axis=-1, keepdims=True) + EPS) * ng_ref[:, sl]
            o_ref[bb, :, sl] = (y * gs[:, sl]).astype(BF16)


def _split_refs(refs, n_in, bwd):
    ins = refs[:n_in]
    if bwd:
        return ins, refs[n_in:n_in + 3], refs[n_in + 3:]
    o_ref, state_ref = refs[n_in:]
    return ins, (None, None, None), (o_ref, state_ref, o_ref)


def _for_each_chunk(c, n_rows, bwd, body):
    cps = n_rows // c

    def step(s, carry):
        idx = (cps - 1 - s) if bwd else s
        body(pl.ds(pl.multiple_of(idx * c, c), c))
        return carry

    lax.fori_loop(0, cps, step, 0)


def _row_views(rows, o_ref, o_scr, fin, bwd):
    def at(r):
        return None if r is None else r.at[:, rows, :]
    of_ref, gate_ref, ng_ref = fin
    o_here = at(o_ref)
    return (o_scr if bwd else o_here), (at(of_ref), at(gate_ref), ng_ref), o_here


def _hgrn_kernel(*refs, layer, heads, dk, nlev, c, bwd):
    (q_ref, z_ref, v_ref, lbl_ref, w_ref, mask_ref), fin, (o_ref, state_ref, o_scr) = \
        _split_refs(refs, 6, bwd)

    @pl.when(pl.program_id(1) == 0)
    def _():
        state_ref[...] = jnp.zeros_like(state_ref)

    logits = lbl_ref[...]
    ex = jnp.exp(logits - jnp.max(logits, axis=0, keepdims=True))
    p = ex / jnp.sum(ex, axis=0, keepdims=True)
    lb = jnp.sum(p[:layer + 1], axis=0, keepdims=True) - p[0:1]
    nb = q_ref.shape[0]

    def chunk(rows):
        ks_, gs = [], []
        for bb in range(nb):
            z = z_ref[bb, rows, :]
            e = jnp.exp(-jnp.abs(z))
            s_big = 1.0 / (1.0 + e)
            s_small = e * s_big
            sig_pos = jnp.where(z >= 0, s_big, s_small)
            sig_neg = jnp.where(z >= 0, s_small, s_big)
            gs.append(jnp.log2(jnp.maximum(lb + (1.0 - lb) * sig_pos, A_MIN_FORGET)))
            ks_.append((1.0 - lb) * sig_neg)
        o_dst, fin_here, o_here = _row_views(rows, o_ref, o_scr, fin, bwd)
        _gated_core([q_ref[bb, rows, :] for bb in range(nb)], ks_,
                    [v_ref[bb, rows, :] for bb in range(nb)], gs,
                    w_ref, mask_ref, o_dst, state_ref, heads, dk, dk, nlev, c, bwd,
                    single_route=False)
        _finish(o_dst, *fin_here, o_here, heads, dk, False, bwd)

    _for_each_chunk(c, q_ref.shape[1], bwd, chunk)


def _gla_kernel(*refs, heads, dk, dv, nlev, c, bwd):
    (q_ref, k_ref, v_ref, lr_ref, wgk_ref, bgk_ref, w_ref, mask_ref), fin, (o_ref, state_ref, o_scr) = \
        _split_refs(refs, 8, bwd)

    @pl.when(pl.program_id(1) == 0)
    def _():
        state_ref[...] = jnp.zeros_like(state_ref)

    nb = q_ref.shape[0]

    def chunk(rows):
        qs, gs = [], []
        for bb in range(nb):
            logit = _dot(lr_ref[bb, rows, :].astype(BF16), wgk_ref[...]) + bgk_ref[...]
            t = logit * LOG2E
            gs.append((jnp.minimum(t, 0.0) - jnp.log2(1.0 + jnp.exp2(-jnp.abs(t)))) * (1.0 / B_GATE_NORM))
            qs.append(q_ref[bb, rows, :] * (dk ** -0.5))
        o_dst, fin_here, o_here = _row_views(rows, o_ref, o_scr, fin, bwd)
        _gated_core(qs, [k_ref[bb, rows, :] for bb in range(nb)],
                    [v_ref[bb, rows, :] for bb in range(nb)], gs,
                    w_ref, mask_ref, o_dst, state_ref, heads, dk, dv, nlev, c, bwd,
                    single_route=True)
        _finish(o_dst, *fin_here, o_here, heads, dv, False, bwd)

    _for_each_chunk(c, q_ref.shape[1], bwd, chunk)


def _chunk_order(n_chunks_ctx, n_chunks, bwd):
    def chunk(n):
        if not bwd:
            return n
        return jnp.where(n < n_chunks_ctx, n_chunks_ctx - 1 - n, n_chunks - 1 - n + n_chunks_ctx)
    return chunk


def _mixer_call(kern, name, proj, in_arrays, in_specs, chunk, nb, rs, c, width, state_shape,
                o_fwd, gate_col, norm_row):
    bsz, l, _ = proj.shape
    bwd = o_fwd is not None
    blk = pl.BlockSpec((nb, rs, width), lambda b, n: (b, chunk(n), 0))
    if bwd:
        in_arrays = in_arrays + [o_fwd, proj, norm_row]
        in_specs = in_specs + [
            blk,
            pl.BlockSpec((nb, rs, width), lambda b, n: (b, chunk(n), gate_col // width)),
            pl.BlockSpec((1, width), lambda b, n: (0, 0)),
        ]
    return pl.pallas_call(
        kern,
        grid=(bsz // nb, l // rs),
        in_specs=in_specs,
        out_specs=blk,
        out_shape=jax.ShapeDtypeStruct((bsz, l, width), BF16 if bwd else F32),
        scratch_shapes=[pltpu.VMEM((nb * state_shape[0],) + state_shape[1:], F32)]
        + ([pltpu.VMEM((nb, c, width), F32)] if bwd else []),
        compiler_params=_cparams(("parallel", "arbitrary")),
        name=name + ("_bwd" if bwd else "_fwd"),
    )(*in_arrays)


def _batch_per_step(bsz, most):
    return max(nb for nb in (1, 2, 4) if nb <= most and bsz % nb == 0)


def _rows_per_step(n_ctx, l):
    return _pick_tile(math.gcd(n_ctx, l - n_ctx), (256, 128, 64))


def hgrn_mixer(proj, col, lb_logits, layer, n_ctx, gate_col, norm_g):
    bsz, l, _ = proj.shape
    depth, _, width = lb_logits.shape
    c = CHUNK
    heads, dk = width // A_HEAD_DIM, A_HEAD_DIM
    cq, cf, ci = (x // width for x in col)
    norm_row = jnp.tile(norm_g, heads).reshape(1, width)
    nb = _batch_per_step(bsz, 4)
    rs = _rows_per_step(n_ctx, l)
    o_fwd = None
    for bwd in (False, True):
        w_np, m_np, nlev = _decay_constants(c, bwd)
        chunk = _chunk_order(n_ctx // rs, l // rs, bwd)
        d = int(bwd)
        kern = functools.partial(_hgrn_kernel, layer=layer, heads=heads, dk=dk, nlev=nlev, c=c, bwd=bwd)
        in_specs = [
            pl.BlockSpec((nb, rs, width), lambda b, n, chunk=chunk: (b, chunk(n), cq)),
            pl.BlockSpec((nb, rs, width), lambda b, n, chunk=chunk, d=d: (b, chunk(n), cf + d)),
            pl.BlockSpec((nb, rs, width), lambda b, n, chunk=chunk: (b, chunk(n), ci)),
            pl.BlockSpec((depth, width), lambda b, n: (0, 0)),
            pl.BlockSpec(w_np.shape, lambda b, n: (0, 0)),
            pl.BlockSpec(m_np.shape, lambda b, n: (0, 0, 0)),
        ]
        in_arrays = [proj, proj, proj, lb_logits[:, d], jnp.asarray(w_np, BF16), jnp.asarray(m_np)]
        o_fwd = _mixer_call(kern, "hgrn_mixer", proj, in_arrays, in_specs, chunk, nb, rs, c, width,
                            (heads, dk, dk), o_fwd, gate_col, norm_row)
    return o_fwd


def gla_mixer(proj, col, wgk_pad, b_gk, n_ctx, width, gate_col, norm_g):
    bsz, l, _ = proj.shape
    key_w = wgk_pad.shape[-1]
    heads = B_HEADS
    dk, dv = key_w // heads, width // heads
    cq, ck, cv, clr = col
    norm_row = jnp.tile(norm_g, heads).reshape(1, width)
    nb = _batch_per_step(bsz, 2)
    rs = _rows_per_step(n_ctx, l)
    c = min(GLA_CHUNK, rs)
    o_fwd = None
    for bwd in (False, True):
        w_np, m_np, nlev = _decay_constants(c, bwd)
        chunk = _chunk_order(n_ctx // rs, l // rs, bwd)
        d = int(bwd)
        kern = functools.partial(_gla_kernel, heads=heads, dk=dk, dv=dv, nlev=nlev, c=c, bwd=bwd)
        in_specs = [
            pl.BlockSpec((nb, rs, key_w), lambda b, n, chunk=chunk: (b, chunk(n), cq // key_w)),
            pl.BlockSpec((nb, rs, key_w), lambda b, n, chunk=chunk: (b, chunk(n), ck // key_w)),
            pl.BlockSpec((nb, rs, width), lambda b, n, chunk=chunk: (b, chunk(n), cv // width)),
            pl.BlockSpec((nb, rs, LANES), lambda b, n, chunk=chunk: (b, chunk(n), clr // LANES)),
            pl.BlockSpec((LANES, key_w), lambda b, n: (0, 0)),
            pl.BlockSpec((1, key_w), lambda b, n: (0, 0)),
            pl.BlockSpec(w_np.shape, lambda b, n: (0, 0)),
            pl.BlockSpec(m_np.shape, lambda b, n: (0, 0, 0)),
        ]
        in_arrays = [proj, proj, proj, proj, wgk_pad[d], b_gk[d].reshape(1, key_w),
                     jnp.asarray(w_np, BF16), jnp.asarray(m_np)]
        o_fwd = _mixer_call(kern, "gla_mixer", proj, in_arrays, in_specs, chunk, nb, rs, c, width,
                            (heads, dv, dk), o_fwd, gate_col, norm_row)
    return o_fwd


def _retention_kernel(*refs, heads, dk, dv, c, bwd):
    (q_ref, k_ref, v_ref, cos_ref, sin_ref, dl_ref), fin, (o_ref, state_ref, o_scr) = \
        _split_refs(refs, 6, bwd)

    @pl.when(pl.program_id(1) == 0)
    def _():
        state_ref[...] = jnp.zeros_like(state_ref)

    ii = lax.broadcasted_iota(jnp.int32, (c, c), 0).astype(F32)
    jj = lax.broadcasted_iota(jnp.int32, (c, c), 1).astype(F32)
    rel = (jj - ii) if bwd else (ii - jj)
    t_col = lax.broadcasted_iota(jnp.int32, (c, 1), 0).astype(F32)
    since = ((c - 1.0) - t_col) if bwd else t_col
    log_gamma = _log_sigmoid(dl_ref[...]) * LOG2E
    cos, sin = cos_ref[...], sin_ref[...]
    half = dk // 2
    for h in range(heads):
        ks = slice(h * dk, (h + 1) * dk)
        vs = slice(h * dv, (h + 1) * dv)
        lg = log_gamma[:, h:h + 1]
        dmat = jnp.where(rel >= 0, jnp.exp2(lg * jnp.maximum(rel, 0.0)), 0.0)
        xi = jnp.exp2(lg * (since + 1.0))
        zeta = jnp.exp2(lg * ((c - 1.0) - since))
        for bb in range(q_ref.shape[0]):
            qh, kh = q_ref[bb, :, ks], k_ref[bb, :, ks]
            qh = (qh * cos + pltpu.roll(qh, half, 1) * sin) * (dk ** -0.5)
            kh = kh * cos + pltpu.roll(kh, half, 1) * sin
            vb = v_ref[bb, :, vs].astype(BF16)
            st = state_ref[bb * heads + h]
            scores = _dot_nt(qh.astype(BF16), kh.astype(BF16)) * dmat
            o = _dot(scores.astype(BF16), vb) + _dot_nt(qh.astype(BF16), st.astype(BF16)) * xi
            state_ref[bb * heads + h] = st * jnp.exp2(lg * c) + _dot_tn(vb, (kh * zeta).astype(BF16))
            o_scr[bb, :, vs] = o
    _finish(o_scr, *fin, o_ref, heads, dv, True, bwd)


def retention_mixer(proj, col, cos_t, sin_t, decay_logit_pad, n_ctx, width, gate_col, norm_g):
    bsz, l, _ = proj.shape
    heads = D_HEADS
    key_w = width // 2
    dk, dv = key_w // heads, width // heads
    c = _rows_per_step(n_ctx, l)
    cq, ck, cv = col
    norm_row = jnp.tile(norm_g, heads).reshape(1, width)
    nb = _batch_per_step(bsz, 2)
    o_fwd = None
    for bwd in (False, True):
        chunk = _chunk_order(n_ctx // c, l // c, bwd)
        kern = functools.partial(_retention_kernel, heads=heads, dk=dk, dv=dv, c=c, bwd=bwd)
        in_specs = [
            pl.BlockSpec((nb, c, key_w), lambda b, n, chunk=chunk: (b, chunk(n), cq // key_w)),
            pl.BlockSpec((nb, c, key_w), lambda b, n, chunk=chunk: (b, chunk(n), ck // key_w)),
            pl.BlockSpec((nb, c, width), lambda b, n, chunk=chunk: (b, chunk(n), cv // width)),
            pl.BlockSpec((c, dk), lambda b, n, chunk=chunk: (chunk(n), 0)),
            pl.BlockSpec((c, dk), lambda b, n, chunk=chunk: (chunk(n), 0)),
            pl.BlockSpec((1, LANES), lambda b, n: (0, 0)),
        ]
        in_arrays = [proj, proj, proj, cos_t, sin_t, decay_logit_pad[int(bwd)]]
        o_fwd = _mixer_call(kern, "retention_mixer", proj, in_arrays, in_specs, chunk, nb, c, c, width,
                            (heads, dv, dk), o_fwd, gate_col, norm_row)
    return o_fwd


def _s5_kernel(u_ref, lam_ref, bt_ref, cm_ref, y_ref,
               toep_ref, win_r_ref, win_i_ref, wout_ref, cl_ref, s_r_ref, s_i_ref, xp_ref, *,
               bsz, n_chunks_ctx, n_chunks):
    t_len, hc, half = S5_CHUNK, C_GROUP, C_STATE
    lane = lax.broadcasted_iota(jnp.int32, (1, LANES), 1)
    lo = lane < half
    sgn = jnp.where(lo, -1.0, 1.0)
    tau = lax.broadcasted_iota(jnp.int32, (t_len, 1), 0)
    u = u_ref[0]
    rows = u.shape[0]
    y_acc = jnp.zeros((rows, t_len * hc), F32)

    def cmul(ar, ai, br, bi):
        return ar * br - ai * bi, ar * bi + ai * br

    def expand(x1, pa, x2, pb, out_ref):
        for t in range(t_len):
            blk = (x1 * jnp.broadcast_to(pa[t:t + 1], (hc, LANES))
                   + x2 * jnp.broadcast_to(pb[t:t + 1], (hc, LANES)))
            out_ref[t * hc:(t + 1) * hc, :] = blk.astype(out_ref.dtype)

    for d in range(2):
        lam_re = jnp.minimum(lam_ref[d, 0, 0:1], C_MAX_RE)
        lam_im = lam_ref[d, 0, 1:2]
        dt = jnp.exp(lam_ref[d, 0, 2:3])
        mag = jnp.exp(lam_re * dt)
        lb_r, lb_i = mag * jnp.cos(lam_im * dt), mag * jnp.sin(lam_im * dt)
        den = lam_re * lam_re + lam_im * lam_im
        nr, ni = lb_r - 1.0, lb_i
        cf_r, cf_i = (nr * lam_re + ni * lam_im) / den, (ni * lam_re - nr * lam_im) / den
        bt_r, bt_i = bt_ref[d, 0, 0], bt_ref[d, 0, 1]
        bb_r, bb_i = cmul(cf_r, cf_i, bt_r, bt_i)
        c_r, c_i = cm_ref[d, 0, 0], cm_ref[d, 0, 1]

        p_r, p_i = jnp.ones((t_len, LANES), F32), jnp.zeros((t_len, LANES), F32)
        q_r, q_i = p_r, p_i
        sq_r, sq_i = lb_r, lb_i
        for bit in range(int(math.log2(t_len))):
            sel = ((tau >> bit) & 1) == 1
            p_r, p_i = cmul(p_r, p_i, jnp.where(sel, sq_r, 1.0), jnp.where(sel, sq_i, 0.0))
            selq = (((t_len - 1 - tau) >> bit) & 1) == 1
            q_r, q_i = cmul(q_r, q_i, jnp.where(selq, sq_r, 1.0), jnp.where(selq, sq_i, 0.0))
            sq_r, sq_i = cmul(sq_r, sq_i, sq_r, sq_i)
        lc_r, lc_i = sq_r, sq_i
        if d == 0:
            toep_p, in_p = (p_r, p_i), (q_r, q_i)
            out_p = cmul(p_r, p_i, lb_r, lb_i)
        else:
            toep_p, in_p = (q_r, q_i), (p_r, p_i)
            out_p = cmul(q_r, q_i, lb_r, lb_i)

        tp_r, tp_i = toep_p
        expand(c_r, jnp.where(lo, tp_r, tp_i), sgn * c_i, jnp.where(lo, tp_i, tp_r), cl_ref)
        bbs = jnp.where(lo, bb_r, -bb_i)
        kt = lax.dot_general(bbs, cl_ref[...], (((1,), (1,)), ((), ())),
                             precision=lax.Precision.HIGHEST, preferred_element_type=F32)
        width = t_len * hc
        glane = lax.broadcasted_iota(jnp.int32, (hc, width), 1)
        per_tile = LANES // hc
        for m in range(per_tile):
            if d == 0:
                base = kt if m == 0 else jnp.where(glane >= hc * m, pltpu.roll(kt, hc * m, 1), 0.0)
            else:
                base = kt if m == 0 else jnp.where(glane < width - hc * m,
                                                   pltpu.roll(kt, width - hc * m, 1), 0.0)
            base = base.astype(BF16)
            for a in range(t_len // per_tile):
                off = a * LANES
                if d == 0:
                    j = a * per_tile + m
                    if off:
                        toep_ref[d, j * hc:(j + 1) * hc, :off] = jnp.zeros((hc, off), BF16)
                    toep_ref[d, j * hc:(j + 1) * hc, off:] = base[:, :width - off]
                else:
                    j = t_len - 1 - (a * per_tile + m)
                    if off:
                        toep_ref[d, j * hc:(j + 1) * hc, width - off:] = jnp.zeros((hc, off), BF16)
                    toep_ref[d, j * hc:(j + 1) * hc, :width - off] = base[:, off:]

        ip_r, ip_i = in_p
        expand(bb_r, ip_r, -bb_i, ip_i, win_r_ref)
        expand(bb_i, ip_r, bb_r, ip_i, win_i_ref)
        op_r, op_i = out_p
        expand(c_r, jnp.where(lo, op_r, -op_i), c_i, jnp.where(lo, -op_i, -op_r), wout_ref)

        s_r_ref[...] = _dot(u, win_r_ref[...])
        s_i_ref[...] = _dot(u, win_i_ref[...])
        if d == 0:
            order = list(range(n_chunks))
        else:
            order = list(range(n_chunks_ctx - 1, -1, -1)) + list(range(n_chunks - 1, n_chunks_ctx - 1, -1))
        x_r, x_i = jnp.zeros((bsz, LANES), F32), jnp.zeros((bsz, LANES), F32)
        for n in order:
            rs = slice(n * bsz, (n + 1) * bsz)
            xp_ref[rs, :] = jnp.where(lo, x_r, x_i)
            nx_r, nx_i = cmul(lc_r, lc_i, x_r, x_i)
            x_r, x_i = nx_r + s_r_ref[rs, :], nx_i + s_i_ref[rs, :]

        nblk = width // MXU_COLS
        cols = []
        for ib in range(nblk):
            acc = None
            for jb in (range(ib + 1) if d == 0 else range(ib, nblk)):
                term = _dot(u[:, jb * MXU_COLS:(jb + 1) * MXU_COLS],
                            toep_ref[d, jb * MXU_COLS:(jb + 1) * MXU_COLS,
                                     ib * MXU_COLS:(ib + 1) * MXU_COLS])
                acc = term if acc is None else acc + term
            cols.append(acc)
        y_acc = (y_acc + jnp.concatenate(cols, axis=1)
                 + _dot_nt(xp_ref[...].astype(BF16), wout_ref[...]))
    y_ref[0] = y_acc


def s5_core(ug, lam_pk, bt_pk, cm_pk, bsz, n_chunks_ctx, n_chunks):
    groups, rows, width = ug.shape
    kern = functools.partial(_s5_kernel, bsz=bsz, n_chunks_ctx=n_chunks_ctx, n_chunks=n_chunks)
    return pl.pallas_call(
        kern,
        grid=(groups,),
        in_specs=[
            pl.BlockSpec((1, rows, width), lambda g: (g, 0, 0)),
            pl.BlockSpec((2, 1, SUBLANES, LANES), lambda g: (0, g, 0, 0)),
            pl.BlockSpec((2, 1, 2, C_GROUP, LANES), lambda g: (0, g, 0, 0, 0)),
            pl.BlockSpec((2, 1, 2, C_GROUP, LANES), lambda g: (0, g, 0, 0, 0)),
        ],
        out_specs=pl.BlockSpec((1, rows, width), lambda g: (g, 0, 0)),
        out_shape=jax.ShapeDtypeStruct((groups, rows, width), F32),
        scratch_shapes=[
            pltpu.VMEM((2, width, width), BF16),
            pltpu.VMEM((width, LANES), BF16),
            pltpu.VMEM((width, LANES), BF16),
            pltpu.VMEM((width, LANES), BF16),
            pltpu.VMEM((width, LANES), F32),
            pltpu.VMEM((rows, LANES), F32),
            pltpu.VMEM((rows, LANES), F32),
            pltpu.VMEM((rows, LANES), F32),
        ],
        compiler_params=_cparams(("parallel",)),
        name="s5_core",
    )(ug, lam_pk, bt_pk, cm_pk)


def _s5_post_kernel(y_ref, u_ref, gate_ref, d_ref, w_ref, b_ref, o_ref):
    y = y_ref[0] + d_ref[...] * u_ref[0]
    z = jax.nn.gelu(y)
    t = _dot(z.astype(BF16), w_ref[...]) + b_ref[...]
    o_ref[0] = (z * _sigmoid(t) * _silu(gate_ref[0])).astype(BF16)


def s5_post(y, proj, u_col, gate_col, d_skip, w_glu, b_glu):
    bsz, l, width = y.shape
    tr = l // 8 if l % (8 * BF16_ROWS) == 0 and l // 8 > ROW_TILE else ROW_TILE
    return pl.pallas_call(
        _s5_post_kernel,
        grid=(bsz, l // tr),
        in_specs=[
            pl.BlockSpec((1, tr, width), lambda b, i: (b, i, 0)),
            pl.BlockSpec((1, tr, width), lambda b, i: (b, i, u_col // width)),
            pl.BlockSpec((1, tr, width), lambda b, i: (b, i, gate_col // width)),
            pl.BlockSpec((1, width), lambda b, i: (0, 0)),
            pl.BlockSpec((width, width), lambda b, i: (0, 0)),
            pl.BlockSpec((1, width), lambda b, i: (0, 0)),
        ],
        out_specs=pl.BlockSpec((1, tr, width), lambda b, i: (b, i, 0)),
        out_shape=jax.ShapeDtypeStruct((bsz, l, width), BF16),
        compiler_params=_cparams(("parallel", "parallel")),
        name="s5_post",
    )(y, proj, proj, d_skip.reshape(1, width), w_glu.astype(BF16), b_glu.reshape(1, width))


def _outproj_kernel(oa_ref, ob_ref, oc_ref, od_ref, w_ref, h_ref, gl_ref, gc_ref, o_ref, *,
                    n_ctx, tm, bw):
    acc = _dot(oa_ref[0], w_ref[0, 0:bw, :].astype(BF16))
    acc = acc + _dot(ob_ref[0], w_ref[0, bw:2 * bw, :].astype(BF16))
    acc = acc + _dot(oc_ref[0], w_ref[0, 2 * bw:3 * bw, :].astype(BF16))
    acc = acc + _dot(od_ref[0], w_ref[0, 3 * bw:4 * bw, :].astype(BF16))
    row = pl.program_id(1) * tm + lax.broadcasted_iota(jnp.int32, (tm, 1), 0)
    gate = jnp.where(row < n_ctx, gc_ref[0], gl_ref[0])
    o_ref[0] = h_ref[0] + gate * acc


def out_projection(o_parts, w_out, h, mods_flat, layer, n_ctx):
    bsz, l, d = h.shape
    bw = o_parts[0].shape[-1]
    tm = l if l <= 2048 else _pick_tile(l, (l // 4, l // 8, l // 16))
    tn = _pick_tile(d, (512, 256, 128))
    kern = functools.partial(_outproj_kernel, n_ctx=n_ctx, tm=tm, bw=bw)
    o_spec = pl.BlockSpec((1, tm, bw), lambda b, i, j: (b, i, 0))
    return pl.pallas_call(
        kern,
        grid=(bsz, l // tm, d // tn),
        in_specs=[
            o_spec, o_spec, o_spec, o_spec,
            pl.BlockSpec((1, 4 * bw, tn), lambda b, i, j: (layer, 0, j)),
            pl.BlockSpec((1, tm, tn), lambda b, i, j: (b, i, j)),
            pl.BlockSpec((1, 1, tn), lambda b, i, j: ((layer * SUBLANES + b) * 3 + 2, 0, j)),
            pl.BlockSpec((1, 1, tn), lambda b, i, j: ((layer * SUBLANES + bsz) * 3 + 2, 0, j)),
        ],
        out_specs=pl.BlockSpec((1, tm, tn), lambda b, i, j: (b, i, j)),
        out_shape=jax.ShapeDtypeStruct((bsz, l, d), F32),
        compiler_params=_cparams(("parallel", "parallel", "parallel")),
        name="out_projection",
    )(*o_parts, w_out, h, mods_flat, mods_flat)


def _final_norm_kernel(x_ref, g_ref, o_ref):
    x = x_ref[0]
    o_ref[0] = x * lax.rsqrt(jnp.mean(x * x, axis=-1, keepdims=True) + EPS) * g_ref[...]


def final_norm(h, g, n_ctx):
    bsz, l, d = h.shape
    tr = ROW_TILE
    skip = n_ctx // tr
    return pl.pallas_call(
        _final_norm_kernel,
        grid=(bsz, (l - n_ctx) // tr),
        in_specs=[
            pl.BlockSpec((1, tr, d), lambda b, i: (b, i + skip, 0)),
            pl.BlockSpec((1, d), lambda b, i: (0, 0)),
        ],
        out_specs=pl.BlockSpec((1, tr, d), lambda b, i: (b, i, 0)),
        out_shape=jax.ShapeDtypeStruct((bsz, l - n_ctx, d), F32),
        compiler_params=_cparams(("parallel", "parallel")),
        name="final_norm",
    )(h, g.reshape(1, d))


def _rope_tables(rows, n_ctx, dk):
    quarter = dk // 4
    freqs = ROPE_BASE ** (-jnp.arange(quarter, dtype=F32) / quarter)
    t = jnp.arange(rows * GRID_W)
    r = (t // GRID_W).astype(F32)
    col = (t % GRID_W).astype(F32)
    ang = jnp.concatenate([r[:, None] * freqs, col[:, None] * freqs], axis=-1)
    ang = jnp.concatenate([jnp.zeros((n_ctx, dk // 2), F32), ang], axis=0)
    cos, sin = jnp.cos(ang), jnp.sin(ang)
    return jnp.concatenate([cos, cos], axis=-1), jnp.concatenate([-sin, sin], axis=-1)


def _dup(x):
    return jnp.concatenate([x, x], axis=-1)


def mixer_layer(xn, h, mods_flat, layer, n_ctx, w_in_perm, hgrn_lb_logits, hgrn_norm_g, gla_w_gk,
                gla_b_gk, gla_norm_g, s5_lam_re, s5_lam_im, s5_log_dt, s5_b_re, s5_b_im, s5_c_re,
                s5_c_im, s5_d, s5_w_glu, s5_b_glu, ret_decay_logit, ret_norm_g, w_out, rope):
    bsz, l, d = h.shape
    bw = d // 4
    kw = bw // 2
    rank = B_GATE_RANK

    names = ("a_q", "a_ff", "a_fb", "a_i", "a_g", "b_q", "b_k", "b_v", "b_g", "c_u", "c_g",
             "d_q", "d_k", "d_v", "d_g", "b_lr")
    widths = (bw, bw, bw, bw, bw, kw, kw, bw, bw, bw, bw, kw, kw, bw, bw, LANES)
    col = dict(zip(names, np.concatenate([[0], np.cumsum(widths)[:-1]]).tolist()))

    proj = in_projection(xn.reshape(bsz * l, d), w_in_perm, layer).reshape(bsz, l, -1)

    o_a = hgrn_mixer(proj, (col["a_q"], col["a_ff"], col["a_i"]), hgrn_lb_logits, layer, n_ctx,
                     col["a_g"], hgrn_norm_g[layer])

    wgk = gla_w_gk[layer].astype(BF16)
    wgk_pad = jnp.zeros((2, LANES, kw), BF16)
    wgk_pad = wgk_pad.at[0, :rank].set(wgk[0]).at[1, rank:2 * rank].set(wgk[1])
    o_b = gla_mixer(proj, (col["b_q"], col["b_k"], col["b_v"], col["b_lr"]), wgk_pad,
                    gla_b_gk[layer], n_ctx, bw, col["b_g"], gla_norm_g[layer])

    groups = bw // C_GROUP
    nck = l // S5_CHUNK
    u = proj[:, :, col["c_u"]:col["c_u"] + bw].astype(BF16)
    ug = u.reshape(bsz, nck, S5_CHUNK, groups, C_GROUP).transpose(3, 1, 0, 2, 4)
    ug = ug.reshape(groups, nck * bsz, S5_CHUNK * C_GROUP)
    dt_row = jnp.broadcast_to(s5_log_dt[layer][..., None], (2, groups, C_STATE))
    lam_pk = jnp.stack([_dup(s5_lam_re[layer]), _dup(s5_lam_im[layer]), _dup(dt_row)], axis=2)
    lam_pk = jnp.pad(lam_pk, ((0, 0), (0, 0), (0, SUBLANES - 3), (0, 0)))
    bt_pk = jnp.stack([_dup(jnp.swapaxes(s5_b_re[layer], -1, -2)),
                       _dup(jnp.swapaxes(s5_b_im[layer], -1, -2))], axis=2)
    cm_pk = jnp.stack([_dup(s5_c_re[layer]), _dup(s5_c_im[layer])], axis=2)
    yg = s5_core(ug, lam_pk, bt_pk, cm_pk, bsz, n_ctx // S5_CHUNK, nck)
    y = yg.reshape(groups, nck, bsz, S5_CHUNK, C_GROUP).transpose(2, 1, 3, 0, 4).reshape(bsz, l, bw)
    o_c = s5_post(y, proj, col["c_u"], col["c_g"], s5_d[layer], s5_w_glu[layer], s5_b_glu[layer])

    dl = jnp.pad(ret_decay_logit[layer], ((0, 0), (0, LANES - D_HEADS))).reshape(2, 1, LANES)
    o_d = retention_mixer(proj, (col["d_q"], col["d_k"], col["d_v"]), rope[0], rope[1], dl, n_ctx, bw,
                          col["d_g"], ret_norm_g[layer])

    return out_projection((o_a, o_b, o_c, o_d), w_out, h, mods_flat, layer, n_ctx)


def kernel(x, c, ctx, c_ctx, norm_g, w_ada, b_ada, w_in, hgrn_lb_logits, hgrn_norm_g, gla_w_gk,
           gla_b_gk, gla_norm_g, s5_lam_re, s5_lam_im, s5_log_dt, s5_b_re, s5_b_im, s5_c_re, s5_c_im,
           s5_d, s5_w_glu, s5_b_glu, ret_decay_logit, ret_norm_g, w_out, final_norm_g):
    bsz, seq, d = x.shape
    n_ctx = ctx.shape[1]
    depth = w_in.shape[0]
    assert bsz < SUBLANES and n_ctx % ROW_TILE == 0 and seq % ROW_TILE == 0

    cvec = jnp.concatenate([c, c_ctx[None], jnp.zeros((SUBLANES - bsz - 1, d), F32)], axis=0)
    mods = ada_modulation(cvec, w_ada, b_ada)
    mods_flat = mods.reshape(depth * SUBLANES * 3, 1, d)
    rope = _rope_tables(seq // GRID_W, n_ctx, (d // 8) // D_HEADS)

    w_in_perm = permute_in_weights(w_in)
    h = None
    for layer in range(depth):
        if layer == 0:
            xn, h = prenorm_first(ctx, x, norm_g[0], mods_flat, n_ctx)
        else:
            xn = prenorm(h, norm_g[layer], mods_flat, layer, n_ctx)
        h = mixer_layer(xn, h, mods_flat, layer, n_ctx, w_in_perm, hgrn_lb_logits, hgrn_norm_g,
                        gla_w_gk, gla_b_gk, gla_norm_g, s5_lam_re, s5_lam_im, s5_log_dt, s5_b_re,
                        s5_b_im, s5_c_re, s5_c_im, s5_d, s5_w_glu, s5_b_glu, ret_decay_logit,
                        ret_norm_g, w_out, rope)
    return final_norm(h, final_norm_g, n_ctx)
```

```python
import functools
import math

import numpy as np
import jax
import jax.numpy as jnp
from jax import lax
from jax.experimental import pallas as pl
from jax.experimental.pallas import tpu as pltpu

F32 = jnp.float32
BF16 = jnp.bfloat16

EPS = 1e-6
A_HEAD_DIM = 128
A_MIN_FORGET = 1e-6
B_HEADS = 4
B_GATE_RANK = 16
B_GATE_NORM = 16.0
C_GROUP = 16
C_STATE = 64
C_MAX_RE = -1e-4
D_HEADS = 4
GRID_W = 64
ROPE_BASE = 10000.0

LANES = 128
SUBLANES = 8
BF16_ROWS = 16
VMEM_LIMIT = 56 * 1024 * 1024

MXU_COLS = 256
IN_PROJ_TN = 6 * MXU_COLS

CHUNK = 64
GLA_CHUNK = 256
LOG2E = math.log2(math.e)
SHORT_SPAN = 86.0
S5_CHUNK = 64
ROW_TILE = 256


def _cparams(sem):
    return pltpu.CompilerParams(dimension_semantics=sem, vmem_limit_bytes=VMEM_LIMIT)


def _dot(a, b):
    return jnp.dot(a, b, preferred_element_type=F32)


def _dot_nt(a, b):
    return lax.dot_general(a, b, (((1,), (1,)), ((), ())), preferred_element_type=F32)


def _dot_tn(a, b):
    return lax.dot_general(a, b, (((0,), (0,)), ((), ())), preferred_element_type=F32)


def _sigmoid(x):
    return 1.0 / (1.0 + jnp.exp(-x))


def _silu(x):
    return x * _sigmoid(x)


def _log_sigmoid(x):
    return jnp.minimum(x, 0.0) - jnp.log(1.0 + jnp.exp(-jnp.abs(x)))


def _ada_kernel(c_ref, w_ref, b_ref, o_ref):
    cv = _silu(c_ref[...]).astype(BF16)
    o_ref[0] = _dot(cv, w_ref[0].astype(BF16)) + b_ref[0]


def ada_modulation(cvec, w_ada, b_ada):
    depth, d, n3 = w_ada.shape
    tn = 512
    return pl.pallas_call(
        _ada_kernel,
        grid=(depth, n3 // tn),
        in_specs=[
            pl.BlockSpec((SUBLANES, d), lambda l, j: (0, 0)),
            pl.BlockSpec((1, d, tn), lambda l, j: (l, 0, j)),
            pl.BlockSpec((1, 1, tn), lambda l, j: (l, 0, j)),
        ],
        out_specs=pl.BlockSpec((1, SUBLANES, tn), lambda l, j: (l, 0, j)),
        out_shape=jax.ShapeDtypeStruct((depth, SUBLANES, n3), F32),
        compiler_params=_cparams(("parallel", "parallel")),
        name="ada_modulation",
    )(cvec, w_ada, b_ada.reshape(depth, 1, n3))


def _prenorm_kernel(x_ref, g_ref, sh_ref, sc_ref, o_ref):
    x = x_ref[0]
    y = x * lax.rsqrt(jnp.mean(x * x, axis=-1, keepdims=True) + EPS) * g_ref[...]
    o_ref[0] = (y * (1.0 + sc_ref[0]) + sh_ref[0]).astype(BF16)


def prenorm(h, g, mods_flat, layer, n_ctx):
    bsz, l, d = h.shape
    tr = ROW_TILE
    nct = n_ctx // tr

    def mod_map(part):
        def f(b, i):
            row = jnp.where(i < nct, bsz, b)
            return ((layer * SUBLANES + row) * 3 + part, 0, 0)
        return f

    return pl.pallas_call(
        _prenorm_kernel,
        grid=(bsz, l // tr),
        in_specs=[
            pl.BlockSpec((1, tr, d), lambda b, i: (b, i, 0)),
            pl.BlockSpec((1, d), lambda b, i: (0, 0)),
            pl.BlockSpec((1, 1, d), mod_map(0)),
            pl.BlockSpec((1, 1, d), mod_map(1)),
        ],
        out_specs=pl.BlockSpec((1, tr, d), lambda b, i: (b, i, 0)),
        out_shape=jax.ShapeDtypeStruct((bsz, l, d), BF16),
        compiler_params=_cparams(("parallel", "parallel")),
        name="prenorm",
    )(h, g.reshape(1, d), mods_flat, mods_flat)


def _prenorm_first_kernel(c_ref, x_ref, g_ref, sh_ref, sc_ref, o_ref, h_ref, *, nct):
    x = jnp.where(pl.program_id(1) < nct, c_ref[0], x_ref[0])
    h_ref[0] = x
    y = x * lax.rsqrt(jnp.mean(x * x, axis=-1, keepdims=True) + EPS) * g_ref[...]
    o_ref[0] = (y * (1.0 + sc_ref[0]) + sh_ref[0]).astype(BF16)


def prenorm_first(ctx, x, g, mods_flat, n_ctx):
    bsz, seq, d = x.shape
    l = n_ctx + seq
    tr = ROW_TILE
    nct = n_ctx // tr

    def mod_map(part):
        def f(b, i):
            row = jnp.where(i < nct, bsz, b)
            return (row * 3 + part, 0, 0)
        return f

    row_spec = pl.BlockSpec((1, tr, d), lambda b, i: (b, i, 0))
    return pl.pallas_call(
        functools.partial(_prenorm_first_kernel, nct=nct),
        grid=(bsz, l // tr),
        in_specs=[
            pl.BlockSpec((1, tr, d), lambda b, i: (b, jnp.minimum(i, nct - 1), 0)),
            pl.BlockSpec((1, tr, d), lambda b, i: (b, jnp.maximum(i - nct, 0), 0)),
            pl.BlockSpec((1, d), lambda b, i: (0, 0)),
            pl.BlockSpec((1, 1, d), mod_map(0)),
            pl.BlockSpec((1, 1, d), mod_map(1)),
        ],
        out_specs=[row_spec, row_spec],
        out_shape=[jax.ShapeDtypeStruct((bsz, l, d), BF16), jax.ShapeDtypeStruct((bsz, l, d), F32)],
        compiler_params=_cparams(("parallel", "arbitrary")),
        name="prenorm_first",
    )(ctx, x, g.reshape(1, d), mods_flat, mods_flat)


def _wprep_kernel(a_ref, b_ref, o_ref, *, first_shifted, tail_block, lr_w):
    i = pl.program_id(1)
    tr = a_ref.shape[1]

    @pl.when(i < first_shifted)
    def _():
        o_ref[0] = a_ref[0].astype(BF16)

    @pl.when(jnp.logical_and(i >= first_shifted, i < tail_block))
    def _():
        o_ref[0, :tr - lr_w] = a_ref[0, lr_w:].astype(BF16)
        o_ref[0, tr - lr_w:] = b_ref[0].astype(BF16)

    @pl.when(i == tail_block)
    def _():
        o_ref[0, :lr_w] = b_ref[0].astype(BF16)
        o_ref[0, lr_w:] = jnp.zeros((tr - lr_w, a_ref.shape[2]), BF16)

    @pl.when(i > tail_block)
    def _():
        o_ref[0] = jnp.zeros(o_ref.shape[1:], BF16)


def permute_in_weights(w_in):
    depth, d, n = w_in.shape
    bw, lr_w = d // 4, 2 * B_GATE_RANK
    lr0 = 5 * bw + 2 * (bw // 2) + bw
    tr = 4 * LANES
    assert (n - lr_w) % tr == 0 and lr0 % tr == 0 and tr % lr_w == 0
    w_t = jnp.swapaxes(w_in, 1, 2)
    first_shifted, tail_block = lr0 // tr, (n - lr_w) // tr
    per = tr // lr_w
    n_out = -(-(tail_block + 1) * tr // IN_PROJ_TN) * IN_PROJ_TN

    def a_map(l, i):
        return (l, jnp.minimum(i, tail_block - 1), 0)

    def b_map(l, i):
        return (l, jnp.where(i == tail_block, lr0 // lr_w,
                             jnp.minimum(i + 1, tail_block) * per), 0)

    return pl.pallas_call(
        functools.partial(_wprep_kernel, first_shifted=first_shifted, tail_block=tail_block, lr_w=lr_w),
        grid=(depth, n_out // tr),
        in_specs=[pl.BlockSpec((1, tr, d), a_map), pl.BlockSpec((1, lr_w, d), b_map)],
        out_specs=pl.BlockSpec((1, tr, d), lambda l, i: (l, i, 0)),
        out_shape=jax.ShapeDtypeStruct((depth, n_out, d), BF16),
        compiler_params=_cparams(("parallel", "parallel")),
        name="permute_in_weights",
    )(w_t, w_t)


def _matmul_kernel(x_ref, wt_ref, o_ref):
    o_ref[...] = _dot_nt(x_ref[...], wt_ref[0])


def _pick_tile(n, candidates):
    for c in candidates:
        if n % c == 0:
            return c
    return n


def in_projection(xn, w_all, layer):
    m, d = xn.shape
    n = w_all.shape[1]
    tm = _pick_tile(m, (1024, 512, 256, 128, 64, 32, 16))
    tn = IN_PROJ_TN
    return pl.pallas_call(
        _matmul_kernel,
        grid=(m // tm, n // tn),
        in_specs=[
            pl.BlockSpec((tm, d), lambda i, j: (i, 0), pipeline_mode=pl.Buffered(1)),
            pl.BlockSpec((1, tn, d), lambda i, j: (layer, j, 0)),
        ],
        out_specs=pl.BlockSpec((tm, tn), lambda i, j: (i, j)),
        out_shape=jax.ShapeDtypeStruct((m, n), F32),
        compiler_params=_cparams(("parallel", "parallel")),
        name="in_projection",
    )(xn, w_all)


def _decay_constants(c, bwd):
    nlev = int(math.log2(c))
    w = np.zeros((nlev + 2, c, c), np.float32)
    masks = np.zeros((nlev, c, c), np.float32)
    for lev in range(nlev):
        s = 1 << lev
        for r in range(c):
            pos = r % (2 * s)
            mid = r - pos + s
            if pos >= s:
                w[lev, r, mid:r + 1] = 1.0
            else:
                w[lev, r, r + 1:mid] = 1.0
        for i in range(c):
            for j in range(c):
                if i // (2 * s) == j // (2 * s) and i % (2 * s) >= s and j % (2 * s) < s:
                    masks[lev, i, j] = 1.0
    for r in range(c):
        w[nlev, r, :r + 1] = 1.0
        w[nlev + 1, r, r + 1:] = 1.0
    half = c // 2
    ii, jj = np.meshgrid(np.arange(c), np.arange(c), indexing="ij")
    diag = ((ii // half == jj // half) & (jj <= ii)).astype(np.float32)
    causal = (jj <= ii).astype(np.float32)
    masks = np.concatenate([masks, diag[None], causal[None]], axis=0)
    w = w.reshape((nlev + 2) * c, c)
    w = np.concatenate([w, np.ones((BF16_ROWS, c), np.float32)], axis=0)
    if bwd:
        w = np.concatenate([w[:-BF16_ROWS].reshape(nlev + 2, c, c)[:, ::-1, ::-1].reshape(-1, c),
                            w[-BF16_ROWS:]], axis=0)
        masks = masks[:, ::-1, ::-1]
    return np.ascontiguousarray(w), np.ascontiguousarray(masks), nlev


def _split2(x):
    hi = x.astype(BF16)
    return hi, (x - hi.astype(F32)).astype(BF16)


def _gated_core(qs, ks_, vs_, gs, w_ref, mask_ref, o_scr, state_ref, heads, dk, dv, nlev, c, bwd,
                single_route):
    half = c // 2
    nb = len(qs)
    w_in = w_ref[nlev * c:(nlev + 1) * c, :]
    first_a, first_b = (half - 1, c - 1) if bwd else (0, half)
    top_ref, exit_row = (half, 0) if bwd else (half - 1, c - 1)
    g_parts, b_ins, short, tiny = [], [], None, None
    for g in gs:
        sum_a = jnp.sum(g[:half], axis=0, keepdims=True)
        sum_b = jnp.sum(g[half:], axis=0, keepdims=True)
        ok = jnp.min(jnp.minimum(sum_a, sum_b)) >= -SHORT_SPAN
        ok1 = jnp.min(sum_a + sum_b) >= -SHORT_SPAN
        short = ok if short is None else jnp.logical_and(short, ok)
        tiny = ok1 if tiny is None else jnp.logical_and(tiny, ok1)
        parts = _split2(g)
        g_parts.append(parts)
        b_ins.append(_dot(w_in, parts[0]) + _dot(w_in, parts[1]))
    entry_row = c - 1 if bwd else 0

    def head_update(bb, h, scores, e_in, e_out, e_tot, extra):
        ks = slice(h * dk, (h + 1) * dk)
        vs = slice(h * dv, (h + 1) * dv)
        qh, kh, vh = qs[bb][:, ks], ks_[bb][:, ks], vs_[bb][:, vs]
        vb = vh.astype(BF16)
        st = state_ref[bb * heads + h]
        o = _dot_nt((qh * e_in).astype(BF16), st.astype(BF16)) + _dot(scores.astype(BF16), vb)
        if extra is not None:
            o = o + extra * vh
        state_ref[bb * heads + h] = st * e_tot + _dot_tn(vb, (kh * e_out).astype(BF16))
        o_scr[bb, :, vs] = o

    def single_reference_route():
        for bb in range(nb):
            b_in = b_ins[bb]
            m0 = b_in[entry_row:entry_row + 1]
            fq = jnp.exp2(b_in - m0)
            fk = jnp.exp2(m0 - b_in)
            tot = b_in[exit_row:exit_row + 1]
            e_in_all = jnp.exp2(b_in)
            e_out_all = jnp.exp2(tot - b_in)
            e_tot_all = jnp.exp2(tot)
            for h in range(heads):
                ks = slice(h * dk, (h + 1) * dk)
                scores = mask_ref[nlev + 1] * _dot_nt((qs[bb][:, ks] * fq[:, ks]).astype(BF16),
                                                      (ks_[bb][:, ks] * fk[:, ks]).astype(BF16))
                head_update(bb, h, scores, e_in_all[:, ks], e_out_all[:, ks], e_tot_all[:, ks], None)

    two_reference = short
    if single_route:
        pl.when(tiny)(single_reference_route)
        two_reference = jnp.logical_and(short, jnp.logical_not(tiny))

    @pl.when(two_reference)
    def _():
        row = lax.broadcasted_iota(jnp.int32, (c, 1), 0)
        in_a = row < half
        later = in_a if bwd else jnp.logical_not(in_a)
        for bb in range(nb):
            b_in = b_ins[bb]
            m = jnp.where(in_a, b_in[first_a:first_a + 1], b_in[first_b:first_b + 1])
            fq = jnp.exp2(b_in - m)
            fk = jnp.exp2(m - b_in)
            r1 = b_in[top_ref:top_ref + 1]
            ft = jnp.exp2(jnp.where(later, b_in - r1, r1 - b_in))
            tot = b_in[exit_row:exit_row + 1]
            e_in_all = jnp.exp2(b_in)
            e_out_all = jnp.exp2(tot - b_in)
            e_tot_all = jnp.exp2(tot)
            for h in range(heads):
                ks = slice(h * dk, (h + 1) * dk)
                qh, kh = qs[bb][:, ks], ks_[bb][:, ks]
                scores = (mask_ref[nlev] * _dot_nt((qh * fq[:, ks]).astype(BF16),
                                                   (kh * fk[:, ks]).astype(BF16))
                          + mask_ref[nlev - 1] * _dot_nt((qh * ft[:, ks]).astype(BF16),
                                                         (kh * ft[:, ks]).astype(BF16)))
                head_update(bb, h, scores, e_in_all[:, ks], e_out_all[:, ks], e_tot_all[:, ks], None)

    @pl.when(jnp.logical_not(short))
    def _():
        w = w_ref[...]
        for bb in range(nb):
            parts = g_parts[bb]
            e_all = jnp.exp2(_dot(w, parts[0]) + _dot(w, parts[1]))
            for h in range(heads):
                ks = slice(h * dk, (h + 1) * dk)
                qh, kh = qs[bb][:, ks], ks_[bb][:, ks]
                scores = jnp.zeros((c, c), F32)
                for lev in range(nlev):
                    f = e_all[lev * c:(lev + 1) * c, ks]
                    scores = scores + mask_ref[lev] * _dot_nt((qh * f).astype(BF16), (kh * f).astype(BF16))
                head_update(bb, h, scores, e_all[nlev * c:(nlev + 1) * c, ks],
                            e_all[(nlev + 1) * c:(nlev + 2) * c, ks],
                            e_all[(nlev + 2) * c:(nlev + 2) * c + 1, ks],
                            jnp.sum(qh * kh, axis=-1, keepdims=True))


def _finish(o_scr, of_ref, gate_ref, ng_ref, o_ref, heads, hd, center, final):
    if not final:
        return
    for bb in range(o_scr.shape[0]):
        gs = _silu(gate_ref[bb])
        for h in range(heads):
            sl = slice(h * hd, (h + 1) * hd)
            x = o_scr[bb, :, sl] + of_ref[bb, :, sl]
            if center:
                x = x - jnp.mean(x, axis=-1, keepdims=True)
            y = x * lax.rsqrt(jnp.mean(x * x, axis=-1, keepdims=True) + EPS) * ng_ref[:, sl]
            o_ref[bb, :, sl] = (y * gs[:, sl]).astype(BF16)


def _split_refs(refs, n_in, bwd):
    ins = refs[:n_in]
    if bwd:
        return ins, refs[n_in:n_in + 3], refs[n_in + 3:]
    o_ref, state_ref = refs[n_in:]
    return ins, (None, None, None), (o_ref, state_ref, o_ref)


def _for_each_chunk(c, n_rows, bwd, body):
    cps = n_rows // c

    def step(s, carry):
        idx = (cps - 1 - s) if bwd else s
        body(pl.ds(pl.multiple_of(idx * c, c), c))
        return carry

    lax.fori_loop(0, cps, step, 0)


def _row_views(rows, o_ref, o_scr, fin, bwd):
    def at(r):
        return None if r is None else r.at[:, rows, :]
    of_ref, gate_ref, ng_ref = fin
    o_here = at(o_ref)
    return (o_scr if bwd else o_here), (at(of_ref), at(gate_ref), ng_ref), o_here


def _hgrn_kernel(*refs, layer, heads, dk, nlev, c, bwd):
    (q_ref, z_ref, v_ref, lbl_ref, w_ref, mask_ref), fin, (o_ref, state_ref, o_scr) = \
        _split_refs(refs, 6, bwd)

    @pl.when(pl.program_id(1) == 0)
    def _():
        state_ref[...] = jnp.zeros_like(state_ref)

    logits = lbl_ref[...]
    ex = jnp.exp(logits - jnp.max(logits, axis=0, keepdims=True))
    p = ex / jnp.sum(ex, axis=0, keepdims=True)
    lb = jnp.sum(p[:layer + 1], axis=0, keepdims=True) - p[0:1]
    nb = q_ref.shape[0]

    def chunk(rows):
        ks_, gs = [], []
        for bb in range(nb):
            z = z_ref[bb, rows, :]
            e = jnp.exp(-jnp.abs(z))
            s_big = 1.0 / (1.0 + e)
            s_small = e * s_big
            sig_pos = jnp.where(z >= 0, s_big, s_small)
            sig_neg = jnp.where(z >= 0, s_small, s_big)
            gs.append(jnp.log2(jnp.maximum(lb + (1.0 - lb) * sig_pos, A_MIN_FORGET)))
            ks_.append((1.0 - lb) * sig_neg)
        o_dst, fin_here, o_here = _row_views(rows, o_ref, o_scr, fin, bwd)
        _gated_core([q_ref[bb, rows, :] for bb in range(nb)], ks_,
                    [v_ref[bb, rows, :] for bb in range(nb)], gs,
                    w_ref, mask_ref, o_dst, state_ref, heads, dk, dk, nlev, c, bwd,
                    single_route=False)
        _finish(o_dst, *fin_here, o_here, heads, dk, False, bwd)

    _for_each_chunk(c, q_ref.shape[1], bwd, chunk)


def _gla_kernel(*refs, heads, dk, dv, nlev, c, bwd):
    (q_ref, k_ref, v_ref, lr_ref, wgk_ref, bgk_ref, w_ref, mask_ref), fin, (o_ref, state_ref, o_scr) = \
        _split_refs(refs, 8, bwd)

    @pl.when(pl.program_id(1) == 0)
    def _():
        state_ref[...] = jnp.zeros_like(state_ref)

    nb = q_ref.shape[0]

    def chunk(rows):
        qs, gs = [], []
        for bb in range(nb):
            logit = _dot(lr_ref[bb, rows, :].astype(BF16), wgk_ref[...]) + bgk_ref[...]
            t = logit * LOG2E
            gs.append((jnp.minimum(t, 0.0) - jnp.log2(1.0 + jnp.exp2(-jnp.abs(t)))) * (1.0 / B_GATE_NORM))
            qs.append(q_ref[bb, rows, :] * (dk ** -0.5))
        o_dst, fin_here, o_here = _row_views(rows, o_ref, o_scr, fin, bwd)
        _gated_core(qs, [k_ref[bb, rows, :] for bb in range(nb)],
                    [v_ref[bb, rows, :] for bb in range(nb)], gs,
                    w_ref, mask_ref, o_dst, state_ref, heads, dk, dv, nlev, c, bwd,
                    single_route=True)
        _finish(o_dst, *fin_here, o_here, heads, dv, False, bwd)

    _for_each_chunk(c, q_ref.shape[1], bwd, chunk)


def _chunk_order(n_chunks_ctx, n_chunks, bwd):
    def chunk(n):
        if not bwd:
            return n
        return jnp.where(n < n_chunks_ctx, n_chunks_ctx - 1 - n, n_chunks - 1 - n + n_chunks_ctx)
    return chunk


def _mixer_call(kern, name, proj, in_arrays, in_specs, chunk, nb, rs, c, width, state_shape,
                o_fwd, gate_col, norm_row):
    bsz, l, _ = proj.shape
    bwd = o_fwd is not None
    blk = pl.BlockSpec((nb, rs, width), lambda b, n: (b, chunk(n), 0))
    if bwd:
        in_arrays = in_arrays + [o_fwd, proj, norm_row]
        in_specs = in_specs + [
            blk,
            pl.BlockSpec((nb, rs, width), lambda b, n: (b, chunk(n), gate_col // width)),
            pl.BlockSpec((1, width), lambda b, n: (0, 0)),
        ]
    return pl.pallas_call(
        kern,
        grid=(bsz // nb, l // rs),
        in_specs=in_specs,
        out_specs=blk,
        out_shape=jax.ShapeDtypeStruct((bsz, l, width), BF16 if bwd else F32),
        scratch_shapes=[pltpu.VMEM((nb * state_shape[0],) + state_shape[1:], F32)]
        + ([pltpu.VMEM((nb, c, width), F32)] if bwd else []),
        compiler_params=_cparams(("parallel", "arbitrary")),
        name=name + ("_bwd" if bwd else "_fwd"),
    )(*in_arrays)


def _batch_per_step(bsz, most):
    return max(nb for nb in (1, 2, 4) if nb <= most and bsz % nb == 0)


def _rows_per_step(n_ctx, l):
    return _pick_tile(math.gcd(n_ctx, l - n_ctx), (256, 128, 64))


def hgrn_mixer(proj, col, lb_logits, layer, n_ctx, gate_col, norm_g):
    bsz, l, _ = proj.shape
    depth, _, width = lb_logits.shape
    c = CHUNK
    heads, dk = width // A_HEAD_DIM, A_HEAD_DIM
    cq, cf, ci = (x // width for x in col)
    norm_row = jnp.tile(norm_g, heads).reshape(1, width)
    nb = _batch_per_step(bsz, 4)
    rs = _rows_per_step(n_ctx, l)
    o_fwd = None
    for bwd in (False, True):
        w_np, m_np, nlev = _decay_constants(c, bwd)
        chunk = _chunk_order(n_ctx // rs, l // rs, bwd)
        d = int(bwd)
        kern = functools.partial(_hgrn_kernel, layer=layer, heads=heads, dk=dk, nlev=nlev, c=c, bwd=bwd)
        in_specs = [
            pl.BlockSpec((nb, rs, width), lambda b, n, chunk=chunk: (b, chunk(n), cq)),
            pl.BlockSpec((nb, rs, width), lambda b, n, chunk=chunk, d=d: (b, chunk(n), cf + d)),
            pl.BlockSpec((nb, rs, width), lambda b, n, chunk=chunk: (b, chunk(n), ci)),
            pl.BlockSpec((depth, width), lambda b, n: (0, 0)),
            pl.BlockSpec(w_np.shape, lambda b, n: (0, 0)),
            pl.BlockSpec(m_np.shape, lambda b, n: (0, 0, 0)),
        ]
        in_arrays = [proj, proj, proj, lb_logits[:, d], jnp.asarray(w_np, BF16), jnp.asarray(m_np)]
        o_fwd = _mixer_call(kern, "hgrn_mixer", proj, in_arrays, in_specs, chunk, nb, rs, c, width,
                            (heads, dk, dk), o_fwd, gate_col, norm_row)
    return o_fwd


def gla_mixer(proj, col, wgk_pad, b_gk, n_ctx, width, gate_col, norm_g):
    bsz, l, _ = proj.shape
    key_w = wgk_pad.shape[-1]
    heads = B_HEADS
    dk, dv = key_w // heads, width // heads
    cq, ck, cv, clr = col
    norm_row = jnp.tile(norm_g, heads).reshape(1, width)
    nb = _batch_per_step(bsz, 2)
    rs = _rows_per_step(n_ctx, l)
    c = min(GLA_CHUNK, rs)
    o_fwd = None
    for bwd in (False, True):
        w_np, m_np, nlev = _decay_constants(c, bwd)
        chunk = _chunk_order(n_ctx // rs, l // rs, bwd)
        d = int(bwd)
        kern = functools.partial(_gla_kernel, heads=heads, dk=dk, dv=dv, nlev=nlev, c=c, bwd=bwd)
        in_specs = [
            pl.BlockSpec((nb, rs, key_w), lambda b, n, chunk=chunk: (b, chunk(n), cq // key_w)),
            pl.BlockSpec((nb, rs, key_w), lambda b, n, chunk=chunk: (b, chunk(n), ck // key_w)),
            pl.BlockSpec((nb, rs, width), lambda b, n, chunk=chunk: (b, chunk(n), cv // width)),
            pl.BlockSpec((nb, rs, LANES), lambda b, n, chunk=chunk: (b, chunk(n), clr // LANES)),
            pl.BlockSpec((LANES, key_w), lambda b, n: (0, 0)),
            pl.BlockSpec((1, key_w), lambda b, n: (0, 0)),
            pl.BlockSpec(w_np.shape, lambda b, n: (0, 0)),
            pl.BlockSpec(m_np.shape, lambda b, n: (0, 0, 0)),
        ]
        in_arrays = [proj, proj, proj, proj, wgk_pad[d], b_gk[d].reshape(1, key_w),
                     jnp.asarray(w_np, BF16), jnp.asarray(m_np)]
        o_fwd = _mixer_call(kern, "gla_mixer", proj, in_arrays, in_specs, chunk, nb, rs, c, width,
                            (heads, dv, dk), o_fwd, gate_col, norm_row)
    return o_fwd


def _retention_kernel(*refs, heads, dk, dv, c, bwd):
    (q_ref, k_ref, v_ref, cos_ref, sin_ref, dl_ref), fin, (o_ref, state_ref, o_scr) = \
        _split_refs(refs, 6, bwd)

    @pl.when(pl.program_id(1) == 0)
    def _():
        state_ref[...] = jnp.zeros_like(state_ref)

    ii = lax.broadcasted_iota(jnp.int32, (c, c), 0).astype(F32)
    jj = lax.broadcasted_iota(jnp.int32, (c, c), 1).astype(F32)
    rel = (jj - ii) if bwd else (ii - jj)
    t_col = lax.broadcasted_iota(jnp.int32, (c, 1), 0).astype(F32)
    since = ((c - 1.0) - t_col) if bwd else t_col
    log_gamma = _log_sigmoid(dl_ref[...]) * LOG2E
    cos, sin = cos_ref[...], sin_ref[...]
    half = dk // 2
    for h in range(heads):
        ks = slice(h * dk, (h + 1) * dk)
        vs = slice(h * dv, (h + 1) * dv)
        lg = log_gamma[:, h:h + 1]
        dmat = jnp.where(rel >= 0, jnp.exp2(lg * jnp.maximum(rel, 0.0)), 0.0)
        xi = jnp.exp2(lg * (since + 1.0))
        zeta = jnp.exp2(lg * ((c - 1.0) - since))
        for bb in range(q_ref.shape[0]):
            qh, kh = q_ref[bb, :, ks], k_ref[bb, :, ks]
            qh = (qh * cos + pltpu.roll(qh, half, 1) * sin) * (dk ** -0.5)
            kh = kh * cos + pltpu.roll(kh, half, 1) * sin
            vb = v_ref[bb, :, vs].astype(BF16)
            st = state_ref[bb * heads + h]
            scores = _dot_nt(qh.astype(BF16), kh.astype(BF16)) * dmat
            o = _dot(scores.astype(BF16), vb) + _dot_nt(qh.astype(BF16), st.astype(BF16)) * xi
            state_ref[bb * heads + h] = st * jnp.exp2(lg * c) + _dot_tn(vb, (kh * zeta).astype(BF16))
            o_scr[bb, :, vs] = o
    _finish(o_scr, *fin, o_ref, heads, dv, True, bwd)


def retention_mixer(proj, col, cos_t, sin_t, decay_logit_pad, n_ctx, width, gate_col, norm_g):
    bsz, l, _ = proj.shape
    heads = D_HEADS
    key_w = width // 2
    dk, dv = key_w // heads, width // heads
    c = _rows_per_step(n_ctx, l)
    cq, ck, cv = col
    norm_row = jnp.tile(norm_g, heads).reshape(1, width)
    nb = _batch_per_step(bsz, 4)
    o_fwd = None
    for bwd in (False, True):
        chunk = _chunk_order(n_ctx // c, l // c, bwd)
        kern = functools.partial(_retention_kernel, heads=heads, dk=dk, dv=dv, c=c, bwd=bwd)
        in_specs = [
            pl.BlockSpec((nb, c, key_w), lambda b, n, chunk=chunk: (b, chunk(n), cq // key_w)),
            pl.BlockSpec((nb, c, key_w), lambda b, n, chunk=chunk: (b, chunk(n), ck // key_w)),
            pl.BlockSpec((nb, c, width), lambda b, n, chunk=chunk: (b, chunk(n), cv // width)),
            pl.BlockSpec((c, dk), lambda b, n, chunk=chunk: (chunk(n), 0)),
            pl.BlockSpec((c, dk), lambda b, n, chunk=chunk: (chunk(n), 0)),
            pl.BlockSpec((1, LANES), lambda b, n: (0, 0)),
        ]
        in_arrays = [proj, proj, proj, cos_t, sin_t, decay_logit_pad[int(bwd)]]
        o_fwd = _mixer_call(kern, "retention_mixer", proj, in_arrays, in_specs, chunk, nb, c, c, width,
                            (heads, dv, dk), o_fwd, gate_col, norm_row)
    return o_fwd


def _s5_kernel(u_ref, lam_ref, bt_ref, cm_ref, y_ref,
               toep_ref, win_r_ref, win_i_ref, wout_ref, cl_ref, s_r_ref, s_i_ref, xp_ref, *,
               bsz, n_chunks_ctx, n_chunks):
    t_len, hc, half = S5_CHUNK, C_GROUP, C_STATE
    lane = lax.broadcasted_iota(jnp.int32, (1, LANES), 1)
    lo = lane < half
    sgn = jnp.where(lo, -1.0, 1.0)
    tau = lax.broadcasted_iota(jnp.int32, (t_len, 1), 0)
    u = u_ref[0]
    rows = u.shape[0]
    y_acc = jnp.zeros((rows, t_len * hc), F32)

    def cmul(ar, ai, br, bi):
        return ar * br - ai * bi, ar * bi + ai * br

    def expand(x1, pa, x2, pb, out_ref):
        for t in range(t_len):
            blk = (x1 * jnp.broadcast_to(pa[t:t + 1], (hc, LANES))
                   + x2 * jnp.broadcast_to(pb[t:t + 1], (hc, LANES)))
            out_ref[t * hc:(t + 1) * hc, :] = blk.astype(out_ref.dtype)

    for d in range(2):
        lam_re = jnp.minimum(lam_ref[d, 0, 0:1], C_MAX_RE)
        lam_im = lam_ref[d, 0, 1:2]
        dt = jnp.exp(lam_ref[d, 0, 2:3])
        mag = jnp.exp(lam_re * dt)
        lb_r, lb_i = mag * jnp.cos(lam_im * dt), mag * jnp.sin(lam_im * dt)
        den = lam_re * lam_re + lam_im * lam_im
        nr, ni = lb_r - 1.0, lb_i
        cf_r, cf_i = (nr * lam_re + ni * lam_im) / den, (ni * lam_re - nr * lam_im) / den
        bt_r, bt_i = bt_ref[d, 0, 0], bt_ref[d, 0, 1]
        bb_r, bb_i = cmul(cf_r, cf_i, bt_r, bt_i)
        c_r, c_i = cm_ref[d, 0, 0], cm_ref[d, 0, 1]

        p_r, p_i = jnp.ones((t_len, LANES), F32), jnp.zeros((t_len, LANES), F32)
        q_r, q_i = p_r, p_i
        sq_r, sq_i = lb_r, lb_i
        for bit in range(int(math.log2(t_len))):
            sel = ((tau >> bit) & 1) == 1
            p_r, p_i = cmul(p_r, p_i, jnp.where(sel, sq_r, 1.0), jnp.where(sel, sq_i, 0.0))
            selq = (((t_len - 1 - tau) >> bit) & 1) == 1
            q_r, q_i = cmul(q_r, q_i, jnp.where(selq, sq_r, 1.0), jnp.where(selq, sq_i, 0.0))
            sq_r, sq_i = cmul(sq_r, sq_i, sq_r, sq_i)
        lc_r, lc_i = sq_r, sq_i
        if d == 0:
            toep_p, in_p = (p_r, p_i), (q_r, q_i)
            out_p = cmul(p_r, p_i, lb_r, lb_i)
        else:
            toep_p, in_p = (q_r, q_i), (p_r, p_i)
            out_p = cmul(q_r, q_i, lb_r, lb_i)

        tp_r, tp_i = toep_p
        expand(c_r, jnp.where(lo, tp_r, tp_i), sgn * c_i, jnp.where(lo, tp_i, tp_r), cl_ref)
        bbs = jnp.where(lo, bb_r, -bb_i)
        kt = lax.dot_general(bbs, cl_ref[...], (((1,), (1,)), ((), ())),
                             precision=lax.Precision.HIGHEST, preferred_element_type=F32)
        width = t_len * hc
        glane = lax.broadcasted_iota(jnp.int32, (hc, width), 1)
        per_tile = LANES // hc
        for m in range(per_tile):
            if d == 0:
                base = kt if m == 0 else jnp.where(glane >= hc * m, pltpu.roll(kt, hc * m, 1), 0.0)
            else:
                base = kt if m == 0 else jnp.where(glane < width - hc * m,
                                                   pltpu.roll(kt, width - hc * m, 1), 0.0)
            base = base.astype(BF16)
            for a in range(t_len // per_tile):
                off = a * LANES
                if d == 0:
                    j = a * per_tile + m
                    if off:
                        toep_ref[d, j * hc:(j + 1) * hc, :off] = jnp.zeros((hc, off), BF16)
                    toep_ref[d, j * hc:(j + 1) * hc, off:] = base[:, :width - off]
                else:
                    j = t_len - 1 - (a * per_tile + m)
                    if off:
                        toep_ref[d, j * hc:(j + 1) * hc, width - off:] = jnp.zeros((hc, off), BF16)
                    toep_ref[d, j * hc:(j + 1) * hc, :width - off] = base[:, off:]

        ip_r, ip_i = in_p
        expand(bb_r, ip_r, -bb_i, ip_i, win_r_ref)
        expand(bb_i, ip_r, bb_r, ip_i, win_i_ref)
        op_r, op_i = out_p
        expand(c_r, jnp.where(lo, op_r, -op_i), c_i, jnp.where(lo, -op_i, -op_r), wout_ref)

        s_r_ref[...] = _dot(u, win_r_ref[...])
        s_i_ref[...] = _dot(u, win_i_ref[...])
        if d == 0:
            order = list(range(n_chunks))
        else:
            order = list(range(n_chunks_ctx - 1, -1, -1)) + list(range(n_chunks - 1, n_chunks_ctx - 1, -1))
        x_r, x_i = jnp.zeros((bsz, LANES), F32), jnp.zeros((bsz, LANES), F32)
        for n in order:
            rs = slice(n * bsz, (n + 1) * bsz)
            xp_ref[rs, :] = jnp.where(lo, x_r, x_i)
            nx_r, nx_i = cmul(lc_r, lc_i, x_r, x_i)
            x_r, x_i = nx_r + s_r_ref[rs, :], nx_i + s_i_ref[rs, :]

        nblk = width // MXU_COLS
        cols = []
        for ib in range(nblk):
            acc = None
            for jb in (range(ib + 1) if d == 0 else range(ib, nblk)):
                term = _dot(u[:, jb * MXU_COLS:(jb + 1) * MXU_COLS],
                            toep_ref[d, jb * MXU_COLS:(jb + 1) * MXU_COLS,
                                     ib * MXU_COLS:(ib + 1) * MXU_COLS])
                acc = term if acc is None else acc + term
            cols.append(acc)
        y_acc = (y_acc + jnp.concatenate(cols, axis=1)
                 + _dot_nt(xp_ref[...].astype(BF16), wout_ref[...]))
    y_ref[0] = y_acc


def s5_core(ug, lam_pk, bt_pk, cm_pk, bsz, n_chunks_ctx, n_chunks):
    groups, rows, width = ug.shape
    kern = functools.partial(_s5_kernel, bsz=bsz, n_chunks_ctx=n_chunks_ctx, n_chunks=n_chunks)
    return pl.pallas_call(
        kern,
        grid=(groups,),
        in_specs=[
            pl.BlockSpec((1, rows, width), lambda g: (g, 0, 0)),
            pl.BlockSpec((2, 1, SUBLANES, LANES), lambda g: (0, g, 0, 0)),
            pl.BlockSpec((2, 1, 2, C_GROUP, LANES), lambda g: (0, g, 0, 0, 0)),
            pl.BlockSpec((2, 1, 2, C_GROUP, LANES), lambda g: (0, g, 0, 0, 0)),
        ],
        out_specs=pl.BlockSpec((1, rows, width), lambda g: (g, 0, 0)),
        out_shape=jax.ShapeDtypeStruct((groups, rows, width), F32),
        scratch_shapes=[
            pltpu.VMEM((2, width, width), BF16),
            pltpu.VMEM((width, LANES), BF16),
            pltpu.VMEM((width, LANES), BF16),
            pltpu.VMEM((width, LANES), BF16),
            pltpu.VMEM((width, LANES), F32),
            pltpu.VMEM((rows, LANES), F32),
            pltpu.VMEM((rows, LANES), F32),
            pltpu.VMEM((rows, LANES), F32),
        ],
        compiler_params=_cparams(("parallel",)),
        name="s5_core",
    )(ug, lam_pk, bt_pk, cm_pk)


def _s5_post_kernel(y_ref, u_ref, gate_ref, d_ref, w_ref, b_ref, o_ref):
    y = y_ref[0] + d_ref[...] * u_ref[0]
    z = jax.nn.gelu(y)
    t = _dot(z.astype(BF16), w_ref[...]) + b_ref[...]
    o_ref[0] = (z * _sigmoid(t) * _silu(gate_ref[0])).astype(BF16)


def s5_post(y, proj, u_col, gate_col, d_skip, w_glu, b_glu):
    bsz, l, width = y.shape
    tr = l // 8 if l % (8 * BF16_ROWS) == 0 and l // 8 > ROW_TILE else ROW_TILE
    return pl.pallas_call(
        _s5_post_kernel,
        grid=(bsz, l // tr),
        in_specs=[
            pl.BlockSpec((1, tr, width), lambda b, i: (b, i, 0)),
            pl.BlockSpec((1, tr, width), lambda b, i: (b, i, u_col // width)),
            pl.BlockSpec((1, tr, width), lambda b, i: (b, i, gate_col // width)),
            pl.BlockSpec((1, width), lambda b, i: (0, 0)),
            pl.BlockSpec((width, width), lambda b, i: (0, 0)),
            pl.BlockSpec((1, width), lambda b, i: (0, 0)),
        ],
        out_specs=pl.BlockSpec((1, tr, width), lambda b, i: (b, i, 0)),
        out_shape=jax.ShapeDtypeStruct((bsz, l, width), BF16),
        compiler_params=_cparams(("parallel", "parallel")),
        name="s5_post",
    )(y, proj, proj, d_skip.reshape(1, width), w_glu.astype(BF16), b_glu.reshape(1, width))


def _outproj_kernel(oa_ref, ob_ref, oc_ref, od_ref, w_ref, h_ref, gl_ref, gc_ref, o_ref, *,
                    n_ctx, tm, bw):
    acc = _dot(oa_ref[0], w_ref[0, 0:bw, :].astype(BF16))
    acc = acc + _dot(ob_ref[0], w_ref[0, bw:2 * bw, :].astype(BF16))
    acc = acc + _dot(oc_ref[0], w_ref[0, 2 * bw:3 * bw, :].astype(BF16))
    acc = acc + _dot(od_ref[0], w_ref[0, 3 * bw:4 * bw, :].astype(BF16))
    row = pl.program_id(1) * tm + lax.broadcasted_iota(jnp.int32, (tm, 1), 0)
    gate = jnp.where(row < n_ctx, gc_ref[0], gl_ref[0])
    o_ref[0] = h_ref[0] + gate * acc


def out_projection(o_parts, w_out, h, mods_flat, layer, n_ctx):
    bsz, l, d = h.shape
    bw = o_parts[0].shape[-1]
    tm = l if l <= 2048 else _pick_tile(l, (l // 4, l // 8, l // 16))
    tn = _pick_tile(d, (512, 256, 128))
    kern = functools.partial(_outproj_kernel, n_ctx=n_ctx, tm=tm, bw=bw)
    o_spec = pl.BlockSpec((1, tm, bw), lambda b, i, j: (b, i, 0))
    return pl.pallas_call(
        kern,
        grid=(bsz, l // tm, d // tn),
        in_specs=[
            o_spec, o_spec, o_spec, o_spec,
            pl.BlockSpec((1, 4 * bw, tn), lambda b, i, j: (layer, 0, j)),
            pl.BlockSpec((1, tm, tn), lambda b, i, j: (b, i, j)),
            pl.BlockSpec((1, 1, tn), lambda b, i, j: ((layer * SUBLANES + b) * 3 + 2, 0, j)),
            pl.BlockSpec((1, 1, tn), lambda b, i, j: ((layer * SUBLANES + bsz) * 3 + 2, 0, j)),
        ],
        out_specs=pl.BlockSpec((1, tm, tn), lambda b, i, j: (b, i, j)),
        out_shape=jax.ShapeDtypeStruct((bsz, l, d), F32),
        compiler_params=_cparams(("parallel", "parallel", "parallel")),
        name="out_projection",
    )(*o_parts, w_out, h, mods_flat, mods_flat)


def _final_norm_kernel(x_ref, g_ref, o_ref):
    x = x_ref[0]
    o_ref[0] = x * lax.rsqrt(jnp.mean(x * x, axis=-1, keepdims=True) + EPS) * g_ref[...]


def final_norm(h, g, n_ctx):
    bsz, l, d = h.shape
    tr = ROW_TILE
    skip = n_ctx // tr
    return pl.pallas_call(
        _final_norm_kernel,
        grid=(bsz, (l - n_ctx) // tr),
        in_specs=[
            pl.BlockSpec((1, tr, d), lambda b, i: (b, i + skip, 0)),
            pl.BlockSpec((1, d), lambda b, i: (0, 0)),
        ],
        out_specs=pl.BlockSpec((1, tr, d), lambda b, i: (b, i, 0)),
        out_shape=jax.ShapeDtypeStruct((bsz, l - n_ctx, d), F32),
        compiler_params=_cparams(("parallel", "parallel")),
        name="final_norm",
    )(h, g.reshape(1, d))


def _rope_tables(rows, n_ctx, dk):
    quarter = dk // 4
    freqs = ROPE_BASE ** (-jnp.arange(quarter, dtype=F32) / quarter)
    t = jnp.arange(rows * GRID_W)
    r = (t // GRID_W).astype(F32)
    col = (t % GRID_W).astype(F32)
    ang = jnp.concatenate([r[:, None] * freqs, col[:, None] * freqs], axis=-1)
    ang = jnp.concatenate([jnp.zeros((n_ctx, dk // 2), F32), ang], axis=0)
    cos, sin = jnp.cos(ang), jnp.sin(ang)
    return jnp.concatenate([cos, cos], axis=-1), jnp.concatenate([-sin, sin], axis=-1)


def _dup(x):
    return jnp.concatenate([x, x], axis=-1)


def mixer_layer(xn, h, mods_flat, layer, n_ctx, w_in_perm, hgrn_lb_logits, hgrn_norm_g, gla_w_gk,
                gla_b_gk, gla_norm_g, s5_lam_re, s5_lam_im, s5_log_dt, s5_b_re, s5_b_im, s5_c_re,
                s5_c_im, s5_d, s5_w_glu, s5_b_glu, ret_decay_logit, ret_norm_g, w_out, rope):
    bsz, l, d = h.shape
    bw = d // 4
    kw = bw // 2
    rank = B_GATE_RANK

    names = ("a_q", "a_ff", "a_fb", "a_i", "a_g", "b_q", "b_k", "b_v", "b_g", "c_u", "c_g",
             "d_q", "d_k", "d_v", "d_g", "b_lr")
    widths = (bw, bw, bw, bw, bw, kw, kw, bw, bw, bw, bw, kw, kw, bw, bw, LANES)
    col = dict(zip(names, np.concatenate([[0], np.cumsum(widths)[:-1]]).tolist()))

    proj = in_projection(xn.reshape(bsz * l, d), w_in_perm, layer).reshape(bsz, l, -1)

    o_a = hgrn_mixer(proj, (col["a_q"], col["a_ff"], col["a_i"]), hgrn_lb_logits, layer, n_ctx,
                     col["a_g"], hgrn_norm_g[layer])

    wgk = gla_w_gk[layer].astype(BF16)
    wgk_pad = jnp.zeros((2, LANES, kw), BF16)
    wgk_pad = wgk_pad.at[0, :rank].set(wgk[0]).at[1, rank:2 * rank].set(wgk[1])
    o_b = gla_mixer(proj, (col["b_q"], col["b_k"], col["b_v"], col["b_lr"]), wgk_pad,
                    gla_b_gk[layer], n_ctx, bw, col["b_g"], gla_norm_g[layer])

    groups = bw // C_GROUP
    nck = l // S5_CHUNK
    u = proj[:, :, col["c_u"]:col["c_u"] + bw].astype(BF16)
    ug = u.reshape(bsz, nck, S5_CHUNK, groups, C_GROUP).transpose(3, 1, 0, 2, 4)
    ug = ug.reshape(groups, nck * bsz, S5_CHUNK * C_GROUP)
    dt_row = jnp.broadcast_to(s5_log_dt[layer][..., None], (2, groups, C_STATE))
    lam_pk = jnp.stack([_dup(s5_lam_re[layer]), _dup(s5_lam_im[layer]), _dup(dt_row)], axis=2)
    lam_pk = jnp.pad(lam_pk, ((0, 0), (0, 0), (0, SUBLANES - 3), (0, 0)))
    bt_pk = jnp.stack([_dup(jnp.swapaxes(s5_b_re[layer], -1, -2)),
                       _dup(jnp.swapaxes(s5_b_im[layer], -1, -2))], axis=2)
    cm_pk = jnp.stack([_dup(s5_c_re[layer]), _dup(s5_c_im[layer])], axis=2)
    yg = s5_core(ug, lam_pk, bt_pk, cm_pk, bsz, n_ctx // S5_CHUNK, nck)
    y = yg.reshape(groups, nck, bsz, S5_CHUNK, C_GROUP).transpose(2, 1, 3, 0, 4).reshape(bsz, l, bw)
    o_c = s5_post(y, proj, col["c_u"], col["c_g"], s5_d[layer], s5_w_glu[layer], s5_b_glu[layer])

    dl = jnp.pad(ret_decay_logit[layer], ((0, 0), (0, LANES - D_HEADS))).reshape(2, 1, LANES)
    o_d = retention_mixer(proj, (col["d_q"], col["d_k"], col["d_v"]), rope[0], rope[1], dl, n_ctx, bw,
                          col["d_g"], ret_norm_g[layer])

    return out_projection((o_a, o_b, o_c, o_d), w_out, h, mods_flat, layer, n_ctx)


def kernel(x, c, ctx, c_ctx, norm_g, w_ada, b_ada, w_in, hgrn_lb_logits, hgrn_norm_g, gla_w_gk,
           gla_b_gk, gla_norm_g, s5_lam_re, s5_lam_im, s5_log_dt, s5_b_re, s5_b_im, s5_c_re, s5_c_im,
           s5_d, s5_w_glu, s5_b_glu, ret_decay_logit, ret_norm_g, w_out, final_norm_g):
    bsz, seq, d = x.shape
    n_ctx = ctx.shape[1]
    depth = w_in.shape[0]
    assert bsz < SUBLANES and n_ctx % ROW_TILE == 0 and seq % ROW_TILE == 0

    cvec = jnp.concatenate([c, c_ctx[None], jnp.zeros((SUBLANES - bsz - 1, d), F32)], axis=0)
    mods = ada_modulation(cvec, w_ada, b_ada)
    mods_flat = mods.reshape(depth * SUBLANES * 3, 1, d)
    rope = _rope_tables(seq // GRID_W, n_ctx, (d // 8) // D_HEADS)

    w_in_perm = permute_in_weights(w_in)
    h = None
    for layer in range(depth):
        if layer == 0:
            xn, h = prenorm_first(ctx, x, norm_g[0], mods_flat, n_ctx)
        else:
            xn = prenorm(h, norm_g[layer], mods_flat, layer, n_ctx)
        h = mixer_layer(xn, h, mods_flat, layer, n_ctx, w_in_perm, hgrn_lb_logits, hgrn_norm_g,
                        gla_w_gk, gla_b_gk, gla_norm_g, s5_lam_re, s5_lam_im, s5_log_dt, s5_b_re,
                        s5_b_im, s5_c_re, s5_c_im, s5_d, s5_w_glu, s5_b_glu, ret_decay_logit,
                        ret_norm_g, w_out, rope)
    return final_norm(h, final_norm_g, n_ctx)
```

```python
import functools
import math

import numpy as np
import jax
import jax.numpy as jnp
from jax import lax
from jax.experimental import pallas as pl
from jax.experimental.pallas import tpu as pltpu

F32 = jnp.float32
BF16 = jnp.bfloat16

EPS = 1e-6
A_HEAD_DIM = 128
A_MIN_FORGET = 1e-6
B_HEADS = 4
B_GATE_RANK = 16
B_GATE_NORM = 16.0
C_GROUP = 16
C_STATE = 64
C_MAX_RE = -1e-4
D_HEADS = 4
GRID_W = 64
ROPE_BASE = 10000.0

LANES = 128
SUBLANES = 8
BF16_ROWS = 16
VMEM_LIMIT = 56 * 1024 * 1024

MXU_COLS = 256
IN_PROJ_TN = 6 * MXU_COLS

CHUNK = 64
GLA_CHUNK = 256
LOG2E = math.log2(math.e)
SHORT_SPAN = 86.0
S5_CHUNK = 64
ROW_TILE = 256


def _cparams(sem):
    return pltpu.CompilerParams(dimension_semantics=sem, vmem_limit_bytes=VMEM_LIMIT)


def _dot(a, b):
    return jnp.dot(a, b, preferred_element_type=F32)


def _dot_nt(a, b):
    return lax.dot_general(a, b, (((1,), (1,)), ((), ())), preferred_element_type=F32)


def _dot_tn(a, b):
    return lax.dot_general(a, b, (((0,), (0,)), ((), ())), preferred_element_type=F32)


def _sigmoid(x):
    return 1.0 / (1.0 + jnp.exp(-x))


def _silu(x):
    return x * _sigmoid(x)


def _log_sigmoid(x):
    return jnp.minimum(x, 0.0) - jnp.log(1.0 + jnp.exp(-jnp.abs(x)))


def _ada_kernel(c_ref, w_ref, b_ref, o_ref):
    cv = _silu(c_ref[...]).astype(BF16)
    o_ref[0] = _dot(cv, w_ref[0].astype(BF16)) + b_ref[0]


def ada_modulation(cvec, w_ada, b_ada):
    depth, d, n3 = w_ada.shape
    tn = 512
    return pl.pallas_call(
        _ada_kernel,
        grid=(depth, n3 // tn),
        in_specs=[
            pl.BlockSpec((SUBLANES, d), lambda l, j: (0, 0)),
            pl.BlockSpec((1, d, tn), lambda l, j: (l, 0, j)),
            pl.BlockSpec((1, 1, tn), lambda l, j: (l, 0, j)),
        ],
        out_specs=pl.BlockSpec((1, SUBLANES, tn), lambda l, j: (l, 0, j)),
        out_shape=jax.ShapeDtypeStruct((depth, SUBLANES, n3), F32),
        compiler_params=_cparams(("parallel", "parallel")),
        name="ada_modulation",
    )(cvec, w_ada, b_ada.reshape(depth, 1, n3))


def _prenorm_kernel(x_ref, g_ref, sh_ref, sc_ref, o_ref):
    x = x_ref[0]
    y = x * lax.rsqrt(jnp.mean(x * x, axis=-1, keepdims=True) + EPS) * g_ref[...]
    o_ref[0] = (y * (1.0 + sc_ref[0]) + sh_ref[0]).astype(BF16)


def prenorm(h, g, mods_flat, layer, n_ctx):
    bsz, l, d = h.shape
    tr = ROW_TILE
    nct = n_ctx // tr

    def mod_map(part):
        def f(b, i):
            row = jnp.where(i < nct, bsz, b)
            return ((layer * SUBLANES + row) * 3 + part, 0, 0)
        return f

    return pl.pallas_call(
        _prenorm_kernel,
        grid=(bsz, l // tr),
        in_specs=[
            pl.BlockSpec((1, tr, d), lambda b, i: (b, i, 0)),
            pl.BlockSpec((1, d), lambda b, i: (0, 0)),
            pl.BlockSpec((1, 1, d), mod_map(0)),
            pl.BlockSpec((1, 1, d), mod_map(1)),
        ],
        out_specs=pl.BlockSpec((1, tr, d), lambda b, i: (b, i, 0)),
        out_shape=jax.ShapeDtypeStruct((bsz, l, d), BF16),
        compiler_params=_cparams(("parallel", "parallel")),
        name="prenorm",
    )(h, g.reshape(1, d), mods_flat, mods_flat)


def _prenorm_first_kernel(c_ref, x_ref, g_ref, sh_ref, sc_ref, o_ref, h_ref, *, nct):
    x = jnp.where(pl.program_id(1) < nct, c_ref[0], x_ref[0])
    h_ref[0] = x
    y = x * lax.rsqrt(jnp.mean(x * x, axis=-1, keepdims=True) + EPS) * g_ref[...]
    o_ref[0] = (y * (1.0 + sc_ref[0]) + sh_ref[0]).astype(BF16)


def prenorm_first(ctx, x, g, mods_flat, n_ctx):
    bsz, seq, d = x.shape
    l = n_ctx + seq
    tr = ROW_TILE
    nct = n_ctx // tr

    def mod_map(part):
        def f(b, i):
            row = jnp.where(i < nct, bsz, b)
            return (row * 3 + part, 0, 0)
        return f

    row_spec = pl.BlockSpec((1, tr, d), lambda b, i: (b, i, 0))
    return pl.pallas_call(
        functools.partial(_prenorm_first_kernel, nct=nct),
        grid=(bsz, l // tr),
        in_specs=[
            pl.BlockSpec((1, tr, d), lambda b, i: (b, jnp.minimum(i, nct - 1), 0)),
            pl.BlockSpec((1, tr, d), lambda b, i: (b, jnp.maximum(i - nct, 0), 0)),
            pl.BlockSpec((1, d), lambda b, i: (0, 0)),
            pl.BlockSpec((1, 1, d), mod_map(0)),
            pl.BlockSpec((1, 1, d), mod_map(1)),
        ],
        out_specs=[row_spec, row_spec],
        out_shape=[jax.ShapeDtypeStruct((bsz, l, d), BF16), jax.ShapeDtypeStruct((bsz, l, d), F32)],
        compiler_params=_cparams(("parallel", "arbitrary")),
        name="prenorm_first",
    )(ctx, x, g.reshape(1, d), mods_flat, mods_flat)


def _wprep_kernel(a_ref, b_ref, o_ref, *, first_shifted, tail_block, lr_w):
    i = pl.program_id(1)
    tr = a_ref.shape[1]

    @pl.when(i < first_shifted)
    def _():
        o_ref[0] = a_ref[0].astype(BF16)

    @pl.when(jnp.logical_and(i >= first_shifted, i < tail_block))
    def _():
        o_ref[0, :tr - lr_w] = a_ref[0, lr_w:].astype(BF16)
        o_ref[0, tr - lr_w:] = b_ref[0].astype(BF16)

    @pl.when(i == tail_block)
    def _():
        o_ref[0, :lr_w] = b_ref[0].astype(BF16)
        o_ref[0, lr_w:] = jnp.zeros((tr - lr_w, a_ref.shape[2]), BF16)

    @pl.when(i > tail_block)
    def _():
        o_ref[0] = jnp.zeros(o_ref.shape[1:], BF16)


def permute_in_weights(w_in):
    depth, d, n = w_in.shape
    bw, lr_w = d // 4, 2 * B_GATE_RANK
    lr0 = 5 * bw + 2 * (bw // 2) + bw
    tr = 4 * LANES
    assert (n - lr_w) % tr == 0 and lr0 % tr == 0 and tr % lr_w == 0
    w_t = jnp.swapaxes(w_in, 1, 2)
    first_shifted, tail_block = lr0 // tr, (n - lr_w) // tr
    per = tr // lr_w
    n_out = -(-(tail_block + 1) * tr // IN_PROJ_TN) * IN_PROJ_TN

    def a_map(l, i):
        return (l, jnp.minimum(i, tail_block - 1), 0)

    def b_map(l, i):
        return (l, jnp.where(i == tail_block, lr0 // lr_w,
                             jnp.minimum(i + 1, tail_block) * per), 0)

    return pl.pallas_call(
        functools.partial(_wprep_kernel, first_shifted=first_shifted, tail_block=tail_block, lr_w=lr_w),
        grid=(depth, n_out // tr),
        in_specs=[pl.BlockSpec((1, tr, d), a_map), pl.BlockSpec((1, lr_w, d), b_map)],
        out_specs=pl.BlockSpec((1, tr, d), lambda l, i: (l, i, 0)),
        out_shape=jax.ShapeDtypeStruct((depth, n_out, d), BF16),
        compiler_params=_cparams(("parallel", "parallel")),
        name="permute_in_weights",
    )(w_t, w_t)


def _matmul_kernel(x_ref, wt_ref, o_ref):
    o_ref[...] = _dot_nt(x_ref[...], wt_ref[0])


def _pick_tile(n, candidates):
    for c in candidates:
        if n % c == 0:
            return c
    return n


def in_projection(xn, w_all, layer):
    m, d = xn.shape
    n = w_all.shape[1]
    tm = _pick_tile(m, (1024, 512, 256, 128, 64, 32, 16))
    tn = IN_PROJ_TN
    return pl.pallas_call(
        _matmul_kernel,
        grid=(m // tm, n // tn),
        in_specs=[
            pl.BlockSpec((tm, d), lambda i, j: (i, 0), pipeline_mode=pl.Buffered(1)),
            pl.BlockSpec((1, tn, d), lambda i, j: (layer, j, 0)),
        ],
        out_specs=pl.BlockSpec((tm, tn), lambda i, j: (i, j)),
        out_shape=jax.ShapeDtypeStruct((m, n), F32),
        compiler_params=_cparams(("parallel", "parallel")),
        name="in_projection",
    )(xn, w_all)


def _decay_constants(c, bwd):
    nlev = int(math.log2(c))
    w = np.zeros((nlev + 2, c, c), np.float32)
    masks = np.zeros((nlev, c, c), np.float32)
    for lev in range(nlev):
        s = 1 << lev
        for r in range(c):
            pos = r % (2 * s)
            mid = r - pos + s
            if pos >= s:
                w[lev, r, mid:r + 1] = 1.0
            else:
                w[lev, r, r + 1:mid] = 1.0
        for i in range(c):
            for j in range(c):
                if i // (2 * s) == j // (2 * s) and i % (2 * s) >= s and j % (2 * s) < s:
                    masks[lev, i, j] = 1.0
    for r in range(c):
        w[nlev, r, :r + 1] = 1.0
        w[nlev + 1, r, r + 1:] = 1.0
    half = c // 2
    ii, jj = np.meshgrid(np.arange(c), np.arange(c), indexing="ij")
    diag = ((ii // half == jj // half) & (jj <= ii)).astype(np.float32)
    causal = (jj <= ii).astype(np.float32)
    masks = np.concatenate([masks, diag[None], causal[None]], axis=0)
    w = w.reshape((nlev + 2) * c, c)
    w = np.concatenate([w, np.ones((BF16_ROWS, c), np.float32)], axis=0)
    if bwd:
        w = np.concatenate([w[:-BF16_ROWS].reshape(nlev + 2, c, c)[:, ::-1, ::-1].reshape(-1, c),
                            w[-BF16_ROWS:]], axis=0)
        masks = masks[:, ::-1, ::-1]
    return np.ascontiguousarray(w), np.ascontiguousarray(masks), nlev


def _split2(x):
    hi = x.astype(BF16)
    return hi, (x - hi.astype(F32)).astype(BF16)


def _gated_core(qs, ks_, vs_, gs, w_ref, mask_ref, o_scr, state_ref, heads, dk, dv, nlev, c, bwd,
                single_route):
    half = c // 2
    nb = len(qs)
    w_in = w_ref[nlev * c:(nlev + 1) * c, :]
    first_a, first_b = (half - 1, c - 1) if bwd else (0, half)
    top_ref, exit_row = (half, 0) if bwd else (half - 1, c - 1)
    g_parts, b_ins, short, tiny = [], [], None, None
    for g in gs:
        sum_a = jnp.sum(g[:half], axis=0, keepdims=True)
        sum_b = jnp.sum(g[half:], axis=0, keepdims=True)
        ok = jnp.min(jnp.minimum(sum_a, sum_b)) >= -SHORT_SPAN
        ok1 = jnp.min(sum_a + sum_b) >= -SHORT_SPAN
        short = ok if short is None else jnp.logical_and(short, ok)
        tiny = ok1 if tiny is None else jnp.logical_and(tiny, ok1)
        parts = _split2(g)
        g_parts.append(parts)
        b_ins.append(_dot(w_in, parts[0]) + _dot(w_in, parts[1]))
    entry_row = c - 1 if bwd else 0

    def head_update(bb, h, scores, e_in, e_out, e_tot, extra):
        ks = slice(h * dk, (h + 1) * dk)
        vs = slice(h * dv, (h + 1) * dv)
        qh, kh, vh = qs[bb][:, ks], ks_[bb][:, ks], vs_[bb][:, vs]
        vb = vh.astype(BF16)
        st = state_ref[bb * heads + h]
        o = _dot_nt((qh * e_in).astype(BF16), st.astype(BF16)) + _dot(scores.astype(BF16), vb)
        if extra is not None:
            o = o + extra * vh
        state_ref[bb * heads + h] = st * e_tot + _dot_tn(vb, (kh * e_out).astype(BF16))
        o_scr[bb, :, vs] = o

    def single_reference_route():
        for bb in range(nb):
            b_in = b_ins[bb]
            m0 = b_in[entry_row:entry_row + 1]
            fq = jnp.exp2(b_in - m0)
            fk = jnp.exp2(m0 - b_in)
            tot = b_in[exit_row:exit_row + 1]
            e_in_all = jnp.exp2(b_in)
            e_out_all = jnp.exp2(tot - b_in)
            e_tot_all = jnp.exp2(tot)
            for h in range(heads):
                ks = slice(h * dk, (h + 1) * dk)
                scores = mask_ref[nlev + 1] * _dot_nt((qs[bb][:, ks] * fq[:, ks]).astype(BF16),
                                                      (ks_[bb][:, ks] * fk[:, ks]).astype(BF16))
                head_update(bb, h, scores, e_in_all[:, ks], e_out_all[:, ks], e_tot_all[:, ks], None)

    two_reference = short
    if single_route:
        pl.when(tiny)(single_reference_route)
        two_reference = jnp.logical_and(short, jnp.logical_not(tiny))

    @pl.when(two_reference)
    def _():
        row = lax.broadcasted_iota(jnp.int32, (c, 1), 0)
        in_a = row < half
        later = in_a if bwd else jnp.logical_not(in_a)
        for bb in range(nb):
            b_in = b_ins[bb]
            m = jnp.where(in_a, b_in[first_a:first_a + 1], b_in[first_b:first_b + 1])
            fq = jnp.exp2(b_in - m)
            fk = jnp.exp2(m - b_in)
            r1 = b_in[top_ref:top_ref + 1]
            ft = jnp.exp2(jnp.where(later, b_in - r1, r1 - b_in))
            tot = b_in[exit_row:exit_row + 1]
            e_in_all = jnp.exp2(b_in)
            e_out_all = jnp.exp2(tot - b_in)
            e_tot_all = jnp.exp2(tot)
            for h in range(heads):
                ks = slice(h * dk, (h + 1) * dk)
                qh, kh = qs[bb][:, ks], ks_[bb][:, ks]
                scores = (mask_ref[nlev] * _dot_nt((qh * fq[:, ks]).astype(BF16),
                                                   (kh * fk[:, ks]).astype(BF16))
                          + mask_ref[nlev - 1] * _dot_nt((qh * ft[:, ks]).astype(BF16),
                                                         (kh * ft[:, ks]).astype(BF16)))
                head_update(bb, h, scores, e_in_all[:, ks], e_out_all[:, ks], e_tot_all[:, ks], None)

    @pl.when(jnp.logical_not(short))
    def _():
        w = w_ref[...]
        for bb in range(nb):
            parts = g_parts[bb]
            e_all = jnp.exp2(_dot(w, parts[0]) + _dot(w, parts[1]))
            for h in range(heads):
                ks = slice(h * dk, (h + 1) * dk)
                qh, kh = qs[bb][:, ks], ks_[bb][:, ks]
                scores = jnp.zeros((c, c), F32)
                for lev in range(nlev):
                    f = e_all[lev * c:(lev + 1) * c, ks]
                    scores = scores + mask_ref[lev] * _dot_nt((qh * f).astype(BF16), (kh * f).astype(BF16))
                head_update(bb, h, scores, e_all[nlev * c:(nlev + 1) * c, ks],
                            e_all[(nlev + 1) * c:(nlev + 2) * c, ks],
                            e_all[(nlev + 2) * c:(nlev + 2) * c + 1, ks],
                            jnp.sum(qh * kh, axis=-1, keepdims=True))


def _finish(o_scr, of_ref, gate_ref, ng_ref, o_ref, heads, hd, center, final):
    if not final:
        return
    for bb in range(o_scr.shape[0]):
        gs = _silu(gate_ref[bb])
        for h in range(heads):
            sl = slice(h * hd, (h + 1) * hd)
            x = o_scr[bb, :, sl] + of_ref[bb, :, sl]
            if center:
                x = x - jnp.mean(x, axis=-1, keepdims=True)
            y = x * lax.rsqrt(jnp.mean(x * x, axis=-1, keepdims=True) + EPS) * ng_ref[:, sl]
            o_ref[bb, :, sl] = (y * gs[:, sl]).astype(BF16)


def _split_refs(refs, n_in, bwd):
    ins = refs[:n_in]
    if bwd:
        return ins, refs[n_in:n_in + 3], refs[n_in + 3:]
    o_ref, state_ref = refs[n_in:]
    return ins, (None, None, None), (o_ref, state_ref, o_ref)


class _Slab:
    def __init__(self, ref, bb, rows, scale=None):
        self.ref, self.bb, self.rows, self.scale = ref, bb, rows, scale

    def __getitem__(self, idx):
        x = self.ref[self.bb, self.rows, idx[1]]
        return x if self.scale is None else x * self.scale


def _for_each_chunk(c, n_rows, bwd, body):
    cps = n_rows // c

    def step(s, carry):
        idx = (cps - 1 - s) if bwd else s
        body(pl.ds(pl.multiple_of(idx * c, c), c))
        return carry

    lax.fori_loop(0, cps, step, 0)


def _row_views(rows, o_ref, o_scr, fin, bwd):
    def at(r):
        return None if r is None else r.at[:, rows, :]
    of_ref, gate_ref, ng_ref = fin
    o_here = at(o_ref)
    return (o_scr if bwd else o_here), (at(of_ref), at(gate_ref), ng_ref), o_here


def _hgrn_kernel(*refs, layer, heads, dk, nlev, c, bwd):
    (q_ref, z_ref, v_ref, lbl_ref, w_ref, mask_ref), fin, (o_ref, state_ref, o_scr) = \
        _split_refs(refs, 6, bwd)

    @pl.when(pl.program_id(1) == 0)
    def _():
        state_ref[...] = jnp.zeros_like(state_ref)

    logits = lbl_ref[...]
    ex = jnp.exp(logits - jnp.max(logits, axis=0, keepdims=True))
    p = ex / jnp.sum(ex, axis=0, keepdims=True)
    lb = jnp.sum(p[:layer + 1], axis=0, keepdims=True) - p[0:1]
    nb = q_ref.shape[0]

    def chunk(rows):
        ks_, gs = [], []
        for bb in range(nb):
            z = z_ref[bb, rows, :]
            e = jnp.exp(-jnp.abs(z))
            s_big = 1.0 / (1.0 + e)
            s_small = e * s_big
            sig_pos = jnp.where(z >= 0, s_big, s_small)
            sig_neg = jnp.where(z >= 0, s_small, s_big)
            gs.append(jnp.log2(jnp.maximum(lb + (1.0 - lb) * sig_pos, A_MIN_FORGET)))
            ks_.append((1.0 - lb) * sig_neg)
        o_dst, fin_here, o_here = _row_views(rows, o_ref, o_scr, fin, bwd)
        _gated_core([_Slab(q_ref, bb, rows) for bb in range(nb)], ks_,
                    [_Slab(v_ref, bb, rows) for bb in range(nb)], gs,
                    w_ref, mask_ref, o_dst, state_ref, heads, dk, dk, nlev, c, bwd,
                    single_route=False)
        _finish(o_dst, *fin_here, o_here, heads, dk, False, bwd)

    _for_each_chunk(c, q_ref.shape[1], bwd, chunk)


def _gla_kernel(*refs, heads, dk, dv, nlev, c, bwd):
    (q_ref, k_ref, v_ref, lr_ref, wgk_ref, bgk_ref, w_ref, mask_ref), fin, (o_ref, state_ref, o_scr) = \
        _split_refs(refs, 8, bwd)

    @pl.when(pl.program_id(1) == 0)
    def _():
        state_ref[...] = jnp.zeros_like(state_ref)

    nb = q_ref.shape[0]

    def chunk(rows):
        qs, gs = [], []
        for bb in range(nb):
            logit = _dot(lr_ref[bb, rows, :].astype(BF16), wgk_ref[...]) + bgk_ref[...]
            t = logit * LOG2E
            gs.append((jnp.minimum(t, 0.0) - jnp.log2(1.0 + jnp.exp2(-jnp.abs(t)))) * (1.0 / B_GATE_NORM))
            qs.append(_Slab(q_ref, bb, rows, dk ** -0.5))
        o_dst, fin_here, o_here = _row_views(rows, o_ref, o_scr, fin, bwd)
        _gated_core(qs, [_Slab(k_ref, bb, rows) for bb in range(nb)],
                    [_Slab(v_ref, bb, rows) for bb in range(nb)], gs,
                    w_ref, mask_ref, o_dst, state_ref, heads, dk, dv, nlev, c, bwd,
                    single_route=True)
        _finish(o_dst, *fin_here, o_here, heads, dv, False, bwd)

    _for_each_chunk(c, q_ref.shape[1], bwd, chunk)


def _chunk_order(n_chunks_ctx, n_chunks, bwd):
    def chunk(n):
        if not bwd:
            return n
        return jnp.where(n < n_chunks_ctx, n_chunks_ctx - 1 - n, n_chunks - 1 - n + n_chunks_ctx)
    return chunk


def _mixer_call(kern, name, proj, in_arrays, in_specs, chunk, nb, rs, c, width, state_shape,
                o_fwd, gate_col, norm_row):
    bsz, l, _ = proj.shape
    bwd = o_fwd is not None
    blk = pl.BlockSpec((nb, rs, width), lambda b, n: (b, chunk(n), 0))
    if bwd:
        in_arrays = in_arrays + [o_fwd, proj, norm_row]
        in_specs = in_specs + [
            blk,
            pl.BlockSpec((nb, rs, width), lambda b, n: (b, chunk(n), gate_col // width)),
            pl.BlockSpec((1, width), lambda b, n: (0, 0)),
        ]
    return pl.pallas_call(
        kern,
        grid=(bsz // nb, l // rs),
        in_specs=in_specs,
        out_specs=blk,
        out_shape=jax.ShapeDtypeStruct((bsz, l, width), BF16 if bwd else F32),
        scratch_shapes=[pltpu.VMEM((nb * state_shape[0],) + state_shape[1:], F32)]
        + ([pltpu.VMEM((nb, c, width), F32)] if bwd else []),
        compiler_params=_cparams(("parallel", "arbitrary")),
        name=name + ("_bwd" if bwd else "_fwd"),
    )(*in_arrays)


def _batch_per_step(bsz, most):
    return max(nb for nb in (1, 2, 4) if nb <= most and bsz % nb == 0)


def _rows_per_step(n_ctx, l):
    return _pick_tile(math.gcd(n_ctx, l - n_ctx), (256, 128, 64))


def hgrn_mixer(proj, col, lb_logits, layer, n_ctx, gate_col, norm_g):
    bsz, l, _ = proj.shape
    depth, _, width = lb_logits.shape
    c = CHUNK
    heads, dk = width // A_HEAD_DIM, A_HEAD_DIM
    cq, cf, ci = (x // width for x in col)
    norm_row = jnp.tile(norm_g, heads).reshape(1, width)
    nb = _batch_per_step(bsz, 4)
    rs = _rows_per_step(n_ctx, l)
    o_fwd = None
    for bwd in (False, True):
        w_np, m_np, nlev = _decay_constants(c, bwd)
        chunk = _chunk_order(n_ctx // rs, l // rs, bwd)
        d = int(bwd)
        kern = functools.partial(_hgrn_kernel, layer=layer, heads=heads, dk=dk, nlev=nlev, c=c, bwd=bwd)
        in_specs = [
            pl.BlockSpec((nb, rs, width), lambda b, n, chunk=chunk: (b, chunk(n), cq)),
            pl.BlockSpec((nb, rs, width), lambda b, n, chunk=chunk, d=d: (b, chunk(n), cf + d)),
            pl.BlockSpec((nb, rs, width), lambda b, n, chunk=chunk: (b, chunk(n), ci)),
            pl.BlockSpec((depth, width), lambda b, n: (0, 0)),
            pl.BlockSpec(w_np.shape, lambda b, n: (0, 0)),
            pl.BlockSpec(m_np.shape, lambda b, n: (0, 0, 0)),
        ]
        in_arrays = [proj, proj, proj, lb_logits[:, d], jnp.asarray(w_np, BF16), jnp.asarray(m_np)]
        o_fwd = _mixer_call(kern, "hgrn_mixer", proj, in_arrays, in_specs, chunk, nb, rs, c, width,
                            (heads, dk, dk), o_fwd, gate_col, norm_row)
    return o_fwd


def gla_mixer(proj, col, wgk_pad, b_gk, n_ctx, width, gate_col, norm_g):
    bsz, l, _ = proj.shape
    key_w = wgk_pad.shape[-1]
    heads = B_HEADS
    dk, dv = key_w // heads, width // heads
    cq, ck, cv, clr = col
    norm_row = jnp.tile(norm_g, heads).reshape(1, width)
    nb = _batch_per_step(bsz, 2)
    rs = _rows_per_step(n_ctx, l)
    c = min(GLA_CHUNK, rs)
    o_fwd = None
    for bwd in (False, True):
        w_np, m_np, nlev = _decay_constants(c, bwd)
        chunk = _chunk_order(n_ctx // rs, l // rs, bwd)
        d = int(bwd)
        kern = functools.partial(_gla_kernel, heads=heads, dk=dk, dv=dv, nlev=nlev, c=c, bwd=bwd)
        in_specs = [
            pl.BlockSpec((nb, rs, key_w), lambda b, n, chunk=chunk: (b, chunk(n), cq // key_w)),
            pl.BlockSpec((nb, rs, key_w), lambda b, n, chunk=chunk: (b, chunk(n), ck // key_w)),
            pl.BlockSpec((nb, rs, width), lambda b, n, chunk=chunk: (b, chunk(n), cv // width)),
            pl.BlockSpec((nb, rs, LANES), lambda b, n, chunk=chunk: (b, chunk(n), clr // LANES)),
            pl.BlockSpec((LANES, key_w), lambda b, n: (0, 0)),
            pl.BlockSpec((1, key_w), lambda b, n: (0, 0)),
            pl.BlockSpec(w_np.shape, lambda b, n: (0, 0)),
            pl.BlockSpec(m_np.shape, lambda b, n: (0, 0, 0)),
        ]
        in_arrays = [proj, proj, proj, proj, wgk_pad[d], b_gk[d].reshape(1, key_w),
                     jnp.asarray(w_np, BF16), jnp.asarray(m_np)]
        o_fwd = _mixer_call(kern, "gla_mixer", proj, in_arrays, in_specs, chunk, nb, rs, c, width,
                            (heads, dv, dk), o_fwd, gate_col, norm_row)
    return o_fwd


def _retention_kernel(*refs, heads, dk, dv, c, bwd):
    (q_ref, k_ref, v_ref, cos_ref, sin_ref, dl_ref), fin, (o_ref, state_ref, o_scr) = \
        _split_refs(refs, 6, bwd)

    @pl.when(pl.program_id(1) == 0)
    def _():
        state_ref[...] = jnp.zeros_like(state_ref)

    ii = lax.broadcasted_iota(jnp.int32, (c, c), 0).astype(F32)
    jj = lax.broadcasted_iota(jnp.int32, (c, c), 1).astype(F32)
    rel = (jj - ii) if bwd else (ii - jj)
    t_col = lax.broadcasted_iota(jnp.int32, (c, 1), 0).astype(F32)
    since = ((c - 1.0) - t_col) if bwd else t_col
    log_gamma = _log_sigmoid(dl_ref[...]) * LOG2E
    cos, sin = cos_ref[...], sin_ref[...]
    half = dk // 2
    for h in range(heads):
        ks = slice(h * dk, (h + 1) * dk)
        vs = slice(h * dv, (h + 1) * dv)
        lg = log_gamma[:, h:h + 1]
        dmat = jnp.where(rel >= 0, jnp.exp2(lg * jnp.maximum(rel, 0.0)), 0.0)
        xi = jnp.exp2(lg * (since + 1.0))
        zeta = jnp.exp2(lg * ((c - 1.0) - since))
        for bb in range(q_ref.shape[0]):
            qh, kh = q_ref[bb, :, ks], k_ref[bb, :, ks]
            qh = (qh * cos + pltpu.roll(qh, half, 1) * sin) * (dk ** -0.5)
            kh = kh * cos + pltpu.roll(kh, half, 1) * sin
            vb = v_ref[bb, :, vs].astype(BF16)
            st = state_ref[bb * heads + h]
            scores = _dot_nt(qh.astype(BF16), kh.astype(BF16)) * dmat
            o = _dot(scores.astype(BF16), vb) + _dot_nt(qh.astype(BF16), st.astype(BF16)) * xi
            state_ref[bb * heads + h] = st * jnp.exp2(lg * c) + _dot_tn(vb, (kh * zeta).astype(BF16))
            o_scr[bb, :, vs] = o
    _finish(o_scr, *fin, o_ref, heads, dv, True, bwd)


def retention_mixer(proj, col, cos_t, sin_t, decay_logit_pad, n_ctx, width, gate_col, norm_g):
    bsz, l, _ = proj.shape
    heads = D_HEADS
    key_w = width // 2
    dk, dv = key_w // heads, width // heads
    c = _rows_per_step(n_ctx, l)
    cq, ck, cv = col
    norm_row = jnp.tile(norm_g, heads).reshape(1, width)
    nb = _batch_per_step(bsz, 4)
    o_fwd = None
    for bwd in (False, True):
        chunk = _chunk_order(n_ctx // c, l // c, bwd)
        kern = functools.partial(_retention_kernel, heads=heads, dk=dk, dv=dv, c=c, bwd=bwd)
        in_specs = [
            pl.BlockSpec((nb, c, key_w), lambda b, n, chunk=chunk: (b, chunk(n), cq // key_w)),
            pl.BlockSpec((nb, c, key_w), lambda b, n, chunk=chunk: (b, chunk(n), ck // key_w)),
            pl.BlockSpec((nb, c, width), lambda b, n, chunk=chunk: (b, chunk(n), cv // width)),
            pl.BlockSpec((c, dk), lambda b, n, chunk=chunk: (chunk(n), 0)),
            pl.BlockSpec((c, dk), lambda b, n, chunk=chunk: (chunk(n), 0)),
            pl.BlockSpec((1, LANES), lambda b, n: (0, 0)),
        ]
        in_arrays = [proj, proj, proj, cos_t, sin_t, decay_logit_pad[int(bwd)]]
        o_fwd = _mixer_call(kern, "retention_mixer", proj, in_arrays, in_specs, chunk, nb, c, c, width,
                            (heads, dv, dk), o_fwd, gate_col, norm_row)
    return o_fwd


def _s5_kernel(u_ref, lam_ref, bt_ref, cm_ref, y_ref,
               toep_ref, win_r_ref, win_i_ref, wout_ref, cl_ref, s_r_ref, s_i_ref, xp_ref, *,
               bsz, n_chunks_ctx, n_chunks):
    t_len, hc, half = S5_CHUNK, C_GROUP, C_STATE
    lane = lax.broadcasted_iota(jnp.int32, (1, LANES), 1)
    lo = lane < half
    sgn = jnp.where(lo, -1.0, 1.0)
    tau = lax.broadcasted_iota(jnp.int32, (t_len, 1), 0)
    u = u_ref[0]
    rows = u.shape[0]
    y_acc = jnp.zeros((rows, t_len * hc), F32)

    def cmul(ar, ai, br, bi):
        return ar * br - ai * bi, ar * bi + ai * br

    def expand(x1, pa, x2, pb, out_ref):
        for t in range(t_len):
            blk = (x1 * jnp.broadcast_to(pa[t:t + 1], (hc, LANES))
                   + x2 * jnp.broadcast_to(pb[t:t + 1], (hc, LANES)))
            out_ref[t * hc:(t + 1) * hc, :] = blk.astype(out_ref.dtype)

    for d in range(2):
        lam_re = jnp.minimum(lam_ref[d, 0, 0:1], C_MAX_RE)
        lam_im = lam_ref[d, 0, 1:2]
        dt = jnp.exp(lam_ref[d, 0, 2:3])
        mag = jnp.exp(lam_re * dt)
        lb_r, lb_i = mag * jnp.cos(lam_im * dt), mag * jnp.sin(lam_im * dt)
        den = lam_re * lam_re + lam_im * lam_im
        nr, ni = lb_r - 1.0, lb_i
        cf_r, cf_i = (nr * lam_re + ni * lam_im) / den, (ni * lam_re - nr * lam_im) / den
        bt_r, bt_i = bt_ref[d, 0, 0], bt_ref[d, 0, 1]
        bb_r, bb_i = cmul(cf_r, cf_i, bt_r, bt_i)
        c_r, c_i = cm_ref[d, 0, 0], cm_ref[d, 0, 1]

        p_r, p_i = jnp.ones((t_len, LANES), F32), jnp.zeros((t_len, LANES), F32)
        q_r, q_i = p_r, p_i
        sq_r, sq_i = lb_r, lb_i
        for bit in range(int(math.log2(t_len))):
            sel = ((tau >> bit) & 1) == 1
            p_r, p_i = cmul(p_r, p_i, jnp.where(sel, sq_r, 1.0), jnp.where(sel, sq_i, 0.0))
            selq = (((t_len - 1 - tau) >> bit) & 1) == 1
            q_r, q_i = cmul(q_r, q_i, jnp.where(selq, sq_r, 1.0), jnp.where(selq, sq_i, 0.0))
            sq_r, sq_i = cmul(sq_r, sq_i, sq_r, sq_i)
        lc_r, lc_i = sq_r, sq_i
        if d == 0:
            toep_p, in_p = (p_r, p_i), (q_r, q_i)
            out_p = cmul(p_r, p_i, lb_r, lb_i)
        else:
            toep_p, in_p = (q_r, q_i), (p_r, p_i)
            out_p = cmul(q_r, q_i, lb_r, lb_i)

        tp_r, tp_i = toep_p
        expand(c_r, jnp.where(lo, tp_r, tp_i), sgn * c_i, jnp.where(lo, tp_i, tp_r), cl_ref)
        bbs = jnp.where(lo, bb_r, -bb_i)
        kt = lax.dot_general(bbs, cl_ref[...], (((1,), (1,)), ((), ())),
                             precision=lax.Precision.HIGHEST, preferred_element_type=F32)
        width = t_len * hc
        glane = lax.broadcasted_iota(jnp.int32, (hc, width), 1)
        per_tile = LANES // hc
        for m in range(per_tile):
            if d == 0:
                base = kt if m == 0 else jnp.where(glane >= hc * m, pltpu.roll(kt, hc * m, 1), 0.0)
            else:
                base = kt if m == 0 else jnp.where(glane < width - hc * m,
                                                   pltpu.roll(kt, width - hc * m, 1), 0.0)
            base = base.astype(BF16)
            for a in range(t_len // per_tile):
                off = a * LANES
                if d == 0:
                    j = a * per_tile + m
                    if off:
                        toep_ref[d, j * hc:(j + 1) * hc, :off] = jnp.zeros((hc, off), BF16)
                    toep_ref[d, j * hc:(j + 1) * hc, off:] = base[:, :width - off]
                else:
                    j = t_len - 1 - (a * per_tile + m)
                    if off:
                        toep_ref[d, j * hc:(j + 1) * hc, width - off:] = jnp.zeros((hc, off), BF16)
                    toep_ref[d, j * hc:(j + 1) * hc, :width - off] = base[:, off:]

        ip_r, ip_i = in_p
        expand(bb_r, ip_r, -bb_i, ip_i, win_r_ref)
        expand(bb_i, ip_r, bb_r, ip_i, win_i_ref)
        op_r, op_i = out_p
        expand(c_r, jnp.where(lo, op_r, -op_i), c_i, jnp.where(lo, -op_i, -op_r), wout_ref)

        s_r_ref[...] = _dot(u, win_r_ref[...])
        s_i_ref[...] = _dot(u, win_i_ref[...])
        if d == 0:
            order = list(range(n_chunks))
        else:
            order = list(range(n_chunks_ctx - 1, -1, -1)) + list(range(n_chunks - 1, n_chunks_ctx - 1, -1))
        x_r, x_i = jnp.zeros((bsz, LANES), F32), jnp.zeros((bsz, LANES), F32)
        for n in order:
            rs = slice(n * bsz, (n + 1) * bsz)
            xp_ref[rs, :] = jnp.where(lo, x_r, x_i)
            nx_r, nx_i = cmul(lc_r, lc_i, x_r, x_i)
            x_r, x_i = nx_r + s_r_ref[rs, :], nx_i + s_i_ref[rs, :]

        nblk = width // MXU_COLS
        cols = []
        for ib in range(nblk):
            acc = None
            for jb in (range(ib + 1) if d == 0 else range(ib, nblk)):
                term = _dot(u[:, jb * MXU_COLS:(jb + 1) * MXU_COLS],
                            toep_ref[d, jb * MXU_COLS:(jb + 1) * MXU_COLS,
                                     ib * MXU_COLS:(ib + 1) * MXU_COLS])
                acc = term if acc is None else acc + term
            cols.append(acc)
        y_acc = (y_acc + jnp.concatenate(cols, axis=1)
                 + _dot_nt(xp_ref[...].astype(BF16), wout_ref[...]))
    y_ref[0] = y_acc


def s5_core(ug, lam_pk, bt_pk, cm_pk, bsz, n_chunks_ctx, n_chunks):
    groups, rows, width = ug.shape
    kern = functools.partial(_s5_kernel, bsz=bsz, n_chunks_ctx=n_chunks_ctx, n_chunks=n_chunks)
    return pl.pallas_call(
        kern,
        grid=(groups,),
        in_specs=[
            pl.BlockSpec((1, rows, width), lambda g: (g, 0, 0)),
            pl.BlockSpec((2, 1, SUBLANES, LANES), lambda g: (0, g, 0, 0)),
            pl.BlockSpec((2, 1, 2, C_GROUP, LANES), lambda g: (0, g, 0, 0, 0)),
            pl.BlockSpec((2, 1, 2, C_GROUP, LANES), lambda g: (0, g, 0, 0, 0)),
        ],
        out_specs=pl.BlockSpec((1, rows, width), lambda g: (g, 0, 0)),
        out_shape=jax.ShapeDtypeStruct((groups, rows, width), F32),
        scratch_shapes=[
            pltpu.VMEM((2, width, width), BF16),
            pltpu.VMEM((width, LANES), BF16),
            pltpu.VMEM((width, LANES), BF16),
            pltpu.VMEM((width, LANES), BF16),
            pltpu.VMEM((width, LANES), F32),
            pltpu.VMEM((rows, LANES), F32),
            pltpu.VMEM((rows, LANES), F32),
            pltpu.VMEM((rows, LANES), F32),
        ],
        compiler_params=_cparams(("parallel",)),
        name="s5_core",
    )(ug, lam_pk, bt_pk, cm_pk)


def _s5_post_kernel(y_ref, u_ref, gate_ref, d_ref, w_ref, b_ref, o_ref):
    y = y_ref[0] + d_ref[...] * u_ref[0]
    z = jax.nn.gelu(y)
    t = _dot(z.astype(BF16), w_ref[...]) + b_ref[...]
    o_ref[0] = (z * _sigmoid(t) * _silu(gate_ref[0])).astype(BF16)


def s5_post(y, proj, u_col, gate_col, d_skip, w_glu, b_glu):
    bsz, l, width = y.shape
    tr = l // 8 if l % (8 * BF16_ROWS) == 0 and l // 8 > ROW_TILE else ROW_TILE
    return pl.pallas_call(
        _s5_post_kernel,
        grid=(bsz, l // tr),
        in_specs=[
            pl.BlockSpec((1, tr, width), lambda b, i: (b, i, 0)),
            pl.BlockSpec((1, tr, width), lambda b, i: (b, i, u_col // width)),
            pl.BlockSpec((1, tr, width), lambda b, i: (b, i, gate_col // width)),
            pl.BlockSpec((1, width), lambda b, i: (0, 0)),
            pl.BlockSpec((width, width), lambda b, i: (0, 0)),
            pl.BlockSpec((1, width), lambda b, i: (0, 0)),
        ],
        out_specs=pl.BlockSpec((1, tr, width), lambda b, i: (b, i, 0)),
        out_shape=jax.ShapeDtypeStruct((bsz, l, width), BF16),
        compiler_params=_cparams(("parallel", "parallel")),
        name="s5_post",
    )(y, proj, proj, d_skip.reshape(1, width), w_glu.astype(BF16), b_glu.reshape(1, width))


def _outproj_kernel(oa_ref, ob_ref, oc_ref, od_ref, w_ref, h_ref, gl_ref, gc_ref, o_ref, *,
                    n_ctx, tm, bw):
    acc = _dot(oa_ref[0], w_ref[0, 0:bw, :].astype(BF16))
    acc = acc + _dot(ob_ref[0], w_ref[0, bw:2 * bw, :].astype(BF16))
    acc = acc + _dot(oc_ref[0], w_ref[0, 2 * bw:3 * bw, :].astype(BF16))
    acc = acc + _dot(od_ref[0], w_ref[0, 3 * bw:4 * bw, :].astype(BF16))
    row = pl.program_id(1) * tm + lax.broadcasted_iota(jnp.int32, (tm, 1), 0)
    gate = jnp.where(row < n_ctx, gc_ref[0], gl_ref[0])
    o_ref[0] = h_ref[0] + gate * acc


def out_projection(o_parts, w_out, h, mods_flat, layer, n_ctx):
    bsz, l, d = h.shape
    bw = o_parts[0].shape[-1]
    tm = l if l <= 2048 else _pick_tile(l, (l // 4, l // 8, l // 16))
    tn = _pick_tile(d, (512, 256, 128))
    kern = functools.partial(_outproj_kernel, n_ctx=n_ctx, tm=tm, bw=bw)
    o_spec = pl.BlockSpec((1, tm, bw), lambda b, i, j: (b, i, 0))
    return pl.pallas_call(
        kern,
        grid=(bsz, l // tm, d // tn),
        in_specs=[
            o_spec, o_spec, o_spec, o_spec,
            pl.BlockSpec((1, 4 * bw, tn), lambda b, i, j: (layer, 0, j)),
            pl.BlockSpec((1, tm, tn), lambda b, i, j: (b, i, j)),
            pl.BlockSpec((1, 1, tn), lambda b, i, j: ((layer * SUBLANES + b) * 3 + 2, 0, j)),
            pl.BlockSpec((1, 1, tn), lambda b, i, j: ((layer * SUBLANES + bsz) * 3 + 2, 0, j)),
        ],
        out_specs=pl.BlockSpec((1, tm, tn), lambda b, i, j: (b, i, j)),
        out_shape=jax.ShapeDtypeStruct((bsz, l, d), F32),
        compiler_params=_cparams(("parallel", "parallel", "parallel")),
        name="out_projection",
    )(*o_parts, w_out, h, mods_flat, mods_flat)


def _final_norm_kernel(x_ref, g_ref, o_ref):
    x = x_ref[0]
    o_ref[0] = x * lax.rsqrt(jnp.mean(x * x, axis=-1, keepdims=True) + EPS) * g_ref[...]


def final_norm(h, g, n_ctx):
    bsz, l, d = h.shape
    tr = ROW_TILE
    skip = n_ctx // tr
    return pl.pallas_call(
        _final_norm_kernel,
        grid=(bsz, (l - n_ctx) // tr),
        in_specs=[
            pl.BlockSpec((1, tr, d), lambda b, i: (b, i + skip, 0)),
            pl.BlockSpec((1, d), lambda b, i: (0, 0)),
        ],
        out_specs=pl.BlockSpec((1, tr, d), lambda b, i: (b, i, 0)),
        out_shape=jax.ShapeDtypeStruct((bsz, l - n_ctx, d), F32),
        compiler_params=_cparams(("parallel", "parallel")),
        name="final_norm",
    )(h, g.reshape(1, d))


def _rope_tables(rows, n_ctx, dk):
    quarter = dk // 4
    freqs = ROPE_BASE ** (-jnp.arange(quarter, dtype=F32) / quarter)
    t = jnp.arange(rows * GRID_W)
    r = (t // GRID_W).astype(F32)
    col = (t % GRID_W).astype(F32)
    ang = jnp.concatenate([r[:, None] * freqs, col[:, None] * freqs], axis=-1)
    ang = jnp.concatenate([jnp.zeros((n_ctx, dk // 2), F32), ang], axis=0)
    cos, sin = jnp.cos(ang), jnp.sin(ang)
    return jnp.concatenate([cos, cos], axis=-1), jnp.concatenate([-sin, sin], axis=-1)


def _dup(x):
    return jnp.concatenate([x, x], axis=-1)


def mixer_layer(xn, h, mods_flat, layer, n_ctx, w_in_perm, hgrn_lb_logits, hgrn_norm_g, gla_w_gk,
                gla_b_gk, gla_norm_g, s5_lam_re, s5_lam_im, s5_log_dt, s5_b_re, s5_b_im, s5_c_re,
                s5_c_im, s5_d, s5_w_glu, s5_b_glu, ret_decay_logit, ret_norm_g, w_out, rope):
    bsz, l, d = h.shape
    bw = d // 4
    kw = bw // 2
    rank = B_GATE_RANK

    names = ("a_q", "a_ff", "a_fb", "a_i", "a_g", "b_q", "b_k", "b_v", "b_g", "c_u", "c_g",
             "d_q", "d_k", "d_v", "d_g", "b_lr")
    widths = (bw, bw, bw, bw, bw, kw, kw, bw, bw, bw, bw, kw, kw, bw, bw, LANES)
    col = dict(zip(names, np.concatenate([[0], np.cumsum(widths)[:-1]]).tolist()))

    proj = in_projection(xn.reshape(bsz * l, d), w_in_perm, layer).reshape(bsz, l, -1)

    o_a = hgrn_mixer(proj, (col["a_q"], col["a_ff"], col["a_i"]), hgrn_lb_logits, layer, n_ctx,
                     col["a_g"], hgrn_norm_g[layer])

    wgk = gla_w_gk[layer].astype(BF16)
    wgk_pad = jnp.zeros((2, LANES, kw), BF16)
    wgk_pad = wgk_pad.at[0, :rank].set(wgk[0]).at[1, rank:2 * rank].set(wgk[1])
    o_b = gla_mixer(proj, (col["b_q"], col["b_k"], col["b_v"], col["b_lr"]), wgk_pad,
                    gla_b_gk[layer], n_ctx, bw, col["b_g"], gla_norm_g[layer])

    groups = bw // C_GROUP
    nck = l // S5_CHUNK
    u = proj[:, :, col["c_u"]:col["c_u"] + bw].astype(BF16)
    ug = u.reshape(bsz, nck, S5_CHUNK, groups, C_GROUP).transpose(3, 1, 0, 2, 4)
    ug = ug.reshape(groups, nck * bsz, S5_CHUNK * C_GROUP)
    dt_row = jnp.broadcast_to(s5_log_dt[layer][..., None], (2, groups, C_STATE))
    lam_pk = jnp.stack([_dup(s5_lam_re[layer]), _dup(s5_lam_im[layer]), _dup(dt_row)], axis=2)
    lam_pk = jnp.pad(lam_pk, ((0, 0), (0, 0), (0, SUBLANES - 3), (0, 0)))
    bt_pk = jnp.stack([_dup(jnp.swapaxes(s5_b_re[layer], -1, -2)),
                       _dup(jnp.swapaxes(s5_b_im[layer], -1, -2))], axis=2)
    cm_pk = jnp.stack([_dup(s5_c_re[layer]), _dup(s5_c_im[layer])], axis=2)
    yg = s5_core(ug, lam_pk, bt_pk, cm_pk, bsz, n_ctx // S5_CHUNK, nck)
    y = yg.reshape(groups, nck, bsz, S5_CHUNK, C_GROUP).transpose(2, 1, 3, 0, 4).reshape(bsz, l, bw)
    o_c = s5_post(y, proj, col["c_u"], col["c_g"], s5_d[layer], s5_w_glu[layer], s5_b_glu[layer])

    dl = jnp.pad(ret_decay_logit[layer], ((0, 0), (0, LANES - D_HEADS))).reshape(2, 1, LANES)
    o_d = retention_mixer(proj, (col["d_q"], col["d_k"], col["d_v"]), rope[0], rope[1], dl, n_ctx, bw,
                          col["d_g"], ret_norm_g[layer])

    return out_projection((o_a, o_b, o_c, o_d), w_out, h, mods_flat, layer, n_ctx)


def kernel(x, c, ctx, c_ctx, norm_g, w_ada, b_ada, w_in, hgrn_lb_logits, hgrn_norm_g, gla_w_gk,
           gla_b_gk, gla_norm_g, s5_lam_re, s5_lam_im, s5_log_dt, s5_b_re, s5_b_im, s5_c_re, s5_c_im,
           s5_d, s5_w_glu, s5_b_glu, ret_decay_logit, ret_norm_g, w_out, final_norm_g):
    bsz, seq, d = x.shape
    n_ctx = ctx.shape[1]
    depth = w_in.shape[0]
    assert bsz < SUBLANES and n_ctx % ROW_TILE == 0 and seq % ROW_TILE == 0

    cvec = jnp.concatenate([c, c_ctx[None], jnp.zeros((SUBLANES - bsz - 1, d), F32)], axis=0)
    mods = ada_modulation(cvec, w_ada, b_ada)
    mods_flat = mods.reshape(depth * SUBLANES * 3, 1, d)
    rope = _rope_tables(seq // GRID_W, n_ctx, (d // 8) // D_HEADS)

    w_in_perm = permute_in_weights(w_in)
    h = None
    for layer in range(depth):
        if layer == 0:
            xn, h = prenorm_first(ctx, x, norm_g[0], mods_flat, n_ctx)
        else:
            xn = prenorm(h, norm_g[layer], mods_flat, layer, n_ctx)
        h = mixer_layer(xn, h, mods_flat, layer, n_ctx, w_in_perm, hgrn_lb_logits, hgrn_norm_g,
                        gla_w_gk, gla_b_gk, gla_norm_g, s5_lam_re, s5_lam_im, s5_log_dt, s5_b_re,
                        s5_b_im, s5_c_re, s5_c_im, s5_d, s5_w_glu, s5_b_glu, ret_decay_logit,
                        ret_norm_g, w_out, rope)
    return final_norm(h, final_norm_g, n_ctx)
```
